```python
import math
import jax
import jax.numpy as jnp
from jax import lax
import numpy as np

D_MODEL = 1024
BATCH = 32
SEQ = 256
DEPTH = 2
DEC_BATCH = 4
DEC_SEQ = 1024
PAST_LEN = 512

F32 = jnp.float32
GRID_W = 64
N_EVEN = (DEPTH + 1) // 2
N_ODD = DEPTH // 2
EPS = 1e-6
NEG_INF = -1e30
ROPE_BASE = 10000.0
Q_BLOCK = 128

A_HEADS = D_MODEL // 128
A_KV_HEADS = A_HEADS // 4
A_GROUP = A_HEADS // A_KV_HEADS
A_HEAD_DIM = 64
A_WINDOW = 128
A_BLOCK = 128

B_HEADS = D_MODEL // 128
B_DK = 64
B_DV = 64
B_CONV = 3
B_CHUNK = 64

C_HEADS = D_MODEL // 64
C_NOPE = 64
C_ROPE = 32
C_QK = C_NOPE + C_ROPE
C_V = 64
C_Q_LORA = 3 * D_MODEL // 8
C_KV_LORA = D_MODEL // 4

N_EXPERTS = 32
TOP_K = 4
D_FF = D_MODEL
SWIGLU_LIMIT = 7.0
SWIGLU_ALPHA = 1.702

A_QW = A_HEADS * A_HEAD_DIM
A_KW = A_KV_HEADS * A_HEAD_DIM
B_KW = B_HEADS * B_DK
B_VW = B_HEADS * B_DV
B_CONV_CH = 2 * B_KW + B_VW
EVEN_SPLITS = (A_QW, A_KW, A_KW, B_KW, B_KW, B_VW, B_VW, B_HEADS, B_HEADS, B_HEADS, B_HEADS)
EVEN_IN = sum(EVEN_SPLITS)
EVEN_MIX = A_QW + B_VW
ODD_IN = C_Q_LORA + C_KV_LORA + C_ROPE
ODD_MIX = C_HEADS * C_V

kernel_name = "hybrid_prefix_diffusion_trunk_step"


def rms_norm(x, g):
    xf = x.astype(F32)
    y = xf * lax.rsqrt(jnp.mean(xf * xf, axis=-1, keepdims=True) + EPS)
    return (y * g.astype(F32)).astype(x.dtype)


def l2_normalize(x):
    xf = x.astype(F32)
    return xf * lax.rsqrt(jnp.sum(xf * xf, axis=-1, keepdims=True) + EPS)


def split_cols(x, sizes):
    return jnp.split(x, np.cumsum(sizes)[:-1].tolist(), axis=-1)


def adaln(cond, w_mod, b_mod):
    m = jax.nn.silu(cond) @ w_mod + b_mod
    return [p[..., None, :] for p in jnp.split(m, 6, axis=-1)]


def modulate(x, g, shift, scale):
    return rms_norm(x, g) * (1 + scale) + shift


def grid_positions(t_len):
    rows = t_len // GRID_W
    row = jnp.broadcast_to(jnp.arange(rows)[:, None], (rows, GRID_W)).reshape(-1)
    col = jnp.broadcast_to(jnp.arange(GRID_W)[None, :], (rows, GRID_W)).reshape(-1)
    return row, col


def _rotate(x, ang):
    h = x.shape[-1] // 2
    x1, x2 = x[..., :h], x[..., h:]
    cos, sin = jnp.cos(ang), jnp.sin(ang)
    return jnp.concatenate([x1 * cos - x2 * sin, x2 * cos + x1 * sin], axis=-1)


def axial_rope(x):
    t_len, d = x.shape[-2], x.shape[-1]
    half = d // 2
    quarter = half // 2
    row, col = grid_positions(t_len)
    inv = ROPE_BASE ** (-jnp.arange(quarter, dtype=F32) / quarter)
    xf = x.astype(F32)
    out = jnp.concatenate([_rotate(xf[..., :half], row[:, None].astype(F32) * inv),
                           _rotate(xf[..., half:], col[:, None].astype(F32) * inv)], axis=-1)
    return out.astype(x.dtype)


def rope_tail(x):
    return jnp.concatenate([x[..., :C_NOPE], axial_rope(x[..., C_NOPE:])], axis=-1)


def attend(q, keys, vals, masks, sink, scale):
    parts = []
    for k, m in zip(keys, masks):
        s = jnp.einsum('bhgqd,bhkd->bhgqk', q, k).astype(F32) * scale
        if m is not None:
            s = jnp.where(m, s, NEG_INF)
        parts.append(s)
    if sink is not None:
        b, h, g, nq = q.shape[:4]
        parts.append(jnp.broadcast_to(sink.astype(F32)[None, :, :, None, None], (b, h, g, nq, 1)))
    p = jax.nn.softmax(jnp.concatenate(parts, axis=-1), axis=-1)
    outs = []
    start = 0
    for v in vals:
        n = v.shape[-2]
        outs.append(jnp.einsum('bhgqk,bhkd->bhgqd', p[..., start:start + n].astype(v.dtype), v))
        start += n
    return sum(outs[1:], outs[0])


def to_query_blocks(q):
    b, h, g, t, d = q.shape
    return q.reshape(b, h, g, t // Q_BLOCK, Q_BLOCK, d).transpose(3, 0, 1, 2, 4, 5)


def from_query_blocks(o):
    nb, b, h, g, qb, d = o.shape
    return o.transpose(1, 2, 3, 0, 4, 5).reshape(b, h, g, nb * qb, d)


def dense_attention(q, keys, vals, sink, scale):
    masks = [None] * len(keys)
    out = lax.map(lambda qb: attend(qb, keys, vals, masks, sink, scale), to_query_blocks(q))
    return from_query_blocks(out)


def band_blocks(x):
    b, h, t, d = x.shape
    nb = t // A_BLOCK
    xp = jnp.pad(x, ((0, 0), (0, 0), (A_BLOCK, A_BLOCK), (0, 0))).reshape(b, h, nb + 2, A_BLOCK, d)
    xb = jnp.concatenate([xp[:, :, :-2], xp[:, :, 1:-1], xp[:, :, 2:]], axis=3)
    return xb.transpose(2, 0, 1, 3, 4)


def band_mask(t_len):
    nb = t_len // A_BLOCK
    blk = jnp.arange(nb)[:, None, None]
    qpos = blk * A_BLOCK + jnp.arange(A_BLOCK)[None, :, None]
    kpos = (blk - 1) * A_BLOCK + jnp.arange(3 * A_BLOCK)[None, None, :]
    return (jnp.abs(qpos - kpos) <= A_WINDOW) & (kpos >= 0) & (kpos < t_len)


def windowed_attention(q, k, v, k_ctx, v_ctx, sink, scale):
    xs = (to_query_blocks(q), band_blocks(k), band_blocks(v), band_mask(q.shape[3]))
    out = lax.map(lambda a: attend(a[0], [a[1], k_ctx], [a[2], v_ctx], [a[3], None], sink, scale), xs)
    return from_query_blocks(out)


def short_conv(x, w):
    c = x.shape[-1]
    y = lax.conv_general_dilated(x, w[:, None, :].astype(x.dtype), window_strides=(1,),
                                 padding=[(B_CONV // 2, B_CONV // 2)],
                                 dimension_numbers=('NWC', 'WIO', 'NWC'), feature_group_count=c)
    return jax.nn.silu(y)


def gated_delta_chunked(q, k, v, g, beta, s0):
    b, t, h, _ = q.shape
    dv = v.shape[-1]
    c = B_CHUNK
    n = t // c

    def chunks(x):
        return x.reshape(b, n, c, h, -1).transpose(0, 3, 1, 2, 4)

    qc, kc, vc = chunks(q), chunks(k), chunks(v)
    gc = jnp.cumsum(chunks(g[..., None])[..., 0], axis=-1)
    bc = chunks(beta[..., None])
    kb, vb = kc * bc, vc * bc
    idx = jnp.arange(c)
    incl = idx[:, None] >= idx[None, :]
    strict = idx[:, None] > idx[None, :]
    decay = jnp.exp(jnp.where(incl, gc[..., :, None] - gc[..., None, :], NEG_INF))
    m = jnp.einsum('bhncd,bhnsd->bhncs', kb, kc) * jnp.where(strict, decay, 0.0)
    a = m + jnp.eye(c, dtype=m.dtype)
    u = lax.linalg.triangular_solve(a, vb, left_side=True, lower=True, unit_diagonal=True)
    w = lax.linalg.triangular_solve(a, kb * jnp.exp(gc)[..., None], left_side=True, lower=True,
                                    unit_diagonal=True)
    aqk = jnp.einsum('bhncd,bhnsd->bhncs', qc, kc) * decay

    def step(s, xs):
        qn, kn, un, wn, gn, an = xs
        delta = un - jnp.einsum('bhcd,bhde->bhce', wn, s)
        o = (jnp.einsum('bhcd,bhde->bhce', qn * jnp.exp(gn)[..., None], s)
             + jnp.einsum('bhcs,bhse->bhce', an, delta))
        g_last = gn[..., -1]
        s = (s * jnp.exp(g_last)[..., None, None]
             + jnp.einsum('bhcd,bhce->bhde', kn * jnp.exp(g_last[..., None] - gn)[..., None], delta))
        return s, o

    xs = tuple(jnp.moveaxis(x, 2, 0) for x in (qc, kc, u, w, gc, aqk))
    s_fin, o = lax.scan(step, s0.astype(F32), xs)
    return o.transpose(1, 0, 3, 2, 4).reshape(b, t, h, dv), s_fin


def deltanet_mixer(bq, bk, bv, bz, a_f, b_f, a_b, b_b, conv_w, a_log, dt_bias, o_norm, s0_f, s0_b):
    bsz, t, _ = bq.shape
    qkv = short_conv(jnp.concatenate([bq, bk, bv], axis=-1), conv_w)
    q, k, v = split_cols(qkv, (B_KW, B_KW, B_VW))
    q = l2_normalize(q.reshape(bsz, t, B_HEADS, B_DK)) * (B_DK ** -0.5)
    k = l2_normalize(k.reshape(bsz, t, B_HEADS, B_DK))
    v = v.reshape(bsz, t, B_HEADS, B_DV).astype(F32)

    def log_decay(a, i):
        return -jnp.exp(a_log[i].astype(F32)) * jax.nn.softplus(a.astype(F32) + dt_bias[i].astype(F32))

    def flip(x):
        return jnp.flip(x, axis=1)

    o_f, s_f = gated_delta_chunked(q, k, v, log_decay(a_f, 0), jax.nn.sigmoid(b_f.astype(F32)), s0_f)
    o_b, s_b = gated_delta_chunked(flip(q), flip(k), flip(v), flip(log_decay(a_b, 1)),
                                   flip(jax.nn.sigmoid(b_b.astype(F32))), s0_b)
    o = o_f + flip(o_b)
    o = rms_norm(o, o_norm) * jax.nn.silu(bz.reshape(bsz, t, B_HEADS, B_DV).astype(F32))
    return o.reshape(bsz, t, B_VW).astype(bq.dtype), s_f.astype(bq.dtype), s_b.astype(bq.dtype)


def even_mixer(h, ctx, w_in, w_out, a_qnorm, a_knorm, a_sink, b_conv, b_alog, b_dtbias, b_onorm):
    bsz, t, _ = h.shape
    aq, ak, av, bq, bk, bv, bz, a_f, b_f, a_b, b_b = split_cols(h @ w_in, EVEN_SPLITS)

    def heads(x, n):
        return x.reshape(bsz, t, n, -1).transpose(0, 2, 1, 3)

    q = rms_norm(heads(aq, A_HEADS), a_qnorm)
    k = rms_norm(heads(ak, A_KV_HEADS), a_knorm)
    v = heads(av, A_KV_HEADS)
    sink = a_sink.reshape(A_KV_HEADS, A_GROUP)
    scale = A_HEAD_DIM ** -0.5
    if ctx is None:
        s0_f = jnp.zeros((bsz, B_HEADS, B_DK, B_DV), F32)
        s0_b = s0_f
        qg = q.reshape(bsz, A_KV_HEADS, A_GROUP, t, A_HEAD_DIM)
        oa = dense_attention(qg, [k], [v], sink, scale)
    else:
        k_ctx, v_ctx, s0_f, s0_b = ctx
        q, k = axial_rope(q), axial_rope(k)
        qg = q.reshape(bsz, A_KV_HEADS, A_GROUP, t, A_HEAD_DIM)
        oa = windowed_attention(qg, k, v, k_ctx, v_ctx, sink, scale)
    oa = oa.reshape(bsz, A_HEADS, t, A_HEAD_DIM).transpose(0, 2, 1, 3).reshape(bsz, t, A_QW)
    ob, s_f, s_b = deltanet_mixer(bq, bk, bv, bz, a_f, b_f, a_b, b_b, b_conv, b_alog, b_dtbias, b_onorm,
                                  s0_f, s0_b)
    out = jnp.concatenate([oa.astype(h.dtype), ob], axis=-1) @ w_out
    return out, (k, v, s_f, s_b)


def mla_keys_values(ckv, krope, w_ukv, knorm, with_rope):
    bsz, t, _ = ckv.shape
    kv = (ckv @ w_ukv).reshape(bsz, t, C_HEADS, C_NOPE + C_V)
    k = jnp.concatenate([kv[..., :C_NOPE],
                         jnp.broadcast_to(krope[:, :, None, :], (bsz, t, C_HEADS, C_ROPE))], axis=-1)
    k = rms_norm(k, knorm).transpose(0, 2, 1, 3)
    if with_rope:
        k = rope_tail(k)
    return k, kv[..., C_NOPE:].transpose(0, 2, 1, 3)


def odd_mixer(h, ctx, w_in, q_lora_norm, kv_lora_norm, w_uq, w_ukv, qnorm, knorm, w_out):
    bsz, t, _ = h.shape
    cq, ckv, krope = split_cols(h @ w_in, (C_Q_LORA, C_KV_LORA, C_ROPE))
    ckv = rms_norm(ckv, kv_lora_norm)
    q = (rms_norm(cq, q_lora_norm) @ w_uq).reshape(bsz, t, C_HEADS, C_QK)
    q = rms_norm(q, qnorm).transpose(0, 2, 1, 3)
    scale = C_QK ** -0.5
    if ctx is None:
        k, v = mla_keys_values(ckv, krope, w_ukv, knorm, False)
        keys, vals = [k], [v]
    else:
        q = rope_tail(q)
        k, v = mla_keys_values(ckv, krope, w_ukv, knorm, True)
        k_ctx, v_ctx = mla_keys_values(ctx[0], ctx[1], w_ukv, knorm, False)
        keys, vals = [k, k_ctx], [v, v_ctx]
    o = dense_attention(q[:, :, None], keys, vals, None, scale)[:, :, 0]
    o = o.transpose(0, 2, 1, 3).reshape(bsz, t, ODD_MIX)
    return o @ w_out, (ckv, krope)


def moe(x, w_router, b_router, w_gu, b_gu, w_down, b_down):
    logits = (x @ w_router + b_router).astype(F32)
    top_vals, top_idx = lax.top_k(logits, TOP_K)
    gates = jax.nn.softmax(top_vals, axis=-1)
    combine = jnp.sum(jax.nn.one_hot(top_idx, N_EXPERTS, dtype=F32) * gates[..., None], axis=1)
    out = jnp.zeros(x.shape, F32)
    for e in range(N_EXPERTS):
        gu = x @ w_gu[e] + b_gu[e]
        gate = jnp.minimum(gu[..., :D_FF], SWIGLU_LIMIT)
        up = jnp.clip(gu[..., D_FF:], -SWIGLU_LIMIT, SWIGLU_LIMIT)
        y = ((up + 1) * gate * jax.nn.sigmoid(SWIGLU_ALPHA * gate)) @ w_down[e] + b_down[e]
        out = out + combine[:, e:e + 1] * y.astype(F32)
    return out.astype(x.dtype)


def setup_inputs(seed: int = 0) -> dict:
    key = jax.random.key(seed)
    keys = iter(jax.random.split(key, 48))

    def nrm(shape, scale=1.0):
        return scale * jax.random.normal(next(keys), shape, F32)

    def gain(shape):
        return 1.0 + 0.1 * jax.random.normal(next(keys), shape, F32)

    def unif(shape, lo, hi):
        return jax.random.uniform(next(keys), shape, F32, lo, hi)

    dt = jnp.exp(unif((N_EVEN, 2, B_HEADS), math.log(1e-3), math.log(1e-1)))
    return {
        "x_prompt": nrm((BATCH, SEQ, D_MODEL)),
        "x_sample": nrm((DEC_BATCH, DEC_SEQ, D_MODEL)),
        "c": nrm((DEC_BATCH, D_MODEL)),
        "cache_a_k": nrm((DEC_BATCH, N_EVEN, A_KV_HEADS, PAST_LEN, A_HEAD_DIM)),
        "cache_a_v": nrm((DEC_BATCH, N_EVEN, A_KV_HEADS, PAST_LEN, A_HEAD_DIM)),
        "state_b_fwd": nrm((DEC_BATCH, N_EVEN, B_HEADS, B_DK, B_DV), B_DK ** -0.5),
        "state_b_bwd": nrm((DEC_BATCH, N_EVEN, B_HEADS, B_DK, B_DV), B_DK ** -0.5),
        "cache_c_ckv": nrm((DEC_BATCH, N_ODD, PAST_LEN, C_KV_LORA)),
        "cache_c_krope": nrm((DEC_BATCH, N_ODD, PAST_LEN, C_ROPE)),
        "c_ctx": nrm((D_MODEL,)),
        "w_mod": nrm((DEPTH, D_MODEL, 6 * D_MODEL), 0.5 * D_MODEL ** -0.5),
        "b_mod": nrm((DEPTH, 6 * D_MODEL), 0.02),
        "norm_mix": gain((DEPTH, D_MODEL)),
        "norm_ffn": gain((DEPTH, D_MODEL)),
        "e_w_in": nrm((N_EVEN, D_MODEL, EVEN_IN), D_MODEL ** -0.5),
        "e_w_out": nrm((N_EVEN, EVEN_MIX, D_MODEL), EVEN_MIX ** -0.5),
        "e_a_qnorm": gain((N_EVEN, A_HEAD_DIM)),
        "e_a_knorm": gain((N_EVEN, A_HEAD_DIM)),
        "e_a_sink": nrm((N_EVEN, A_HEADS)),
        "e_b_conv": nrm((N_EVEN, B_CONV, B_CONV_CH), B_CONV ** -0.5),
        "e_b_alog": jnp.log(unif((N_EVEN, 2, B_HEADS), 1.0, 16.0)),
        "e_b_dtbias": dt + jnp.log(-jnp.expm1(-dt)),
        "e_b_onorm": gain((N_EVEN, B_DV)),
        "o_w_in": nrm((N_ODD, D_MODEL, ODD_IN), D_MODEL ** -0.5),
        "o_q_lora_norm": gain((N_ODD, C_Q_LORA)),
        "o_kv_lora_norm": gain((N_ODD, C_KV_LORA)),
        "o_w_uq": nrm((N_ODD, C_Q_LORA, C_HEADS * C_QK), C_Q_LORA ** -0.5),
        "o_w_ukv": nrm((N_ODD, C_KV_LORA, C_HEADS * (C_NOPE + C_V)), C_KV_LORA ** -0.5),
        "o_qnorm": gain((N_ODD, C_QK)),
        "o_knorm": gain((N_ODD, C_QK)),
        "o_w_out": nrm((N_ODD, ODD_MIX, D_MODEL), ODD_MIX ** -0.5),
        "moe_w_router": nrm((DEPTH, D_MODEL, N_EXPERTS), D_MODEL ** -0.5),
        "moe_b_router": nrm((DEPTH, N_EXPERTS), 0.01),
        "moe_w_gu": nrm((DEPTH, N_EXPERTS, D_MODEL, 2 * D_FF), D_MODEL ** -0.5),
        "moe_b_gu": nrm((DEPTH, N_EXPERTS, 2 * D_FF), 0.02),
        "moe_w_down": nrm((DEPTH, N_EXPERTS, D_FF, D_MODEL), D_FF ** -0.5),
        "moe_b_down": nrm((DEPTH, N_EXPERTS, D_MODEL), 0.02),
    }


def reference(x_prompt, x_sample, c, cache_a_k, cache_a_v, state_b_fwd, state_b_bwd, cache_c_ckv,
              cache_c_krope, c_ctx, w_mod, b_mod, norm_mix, norm_ffn, e_w_in, e_w_out, e_a_qnorm,
              e_a_knorm, e_a_sink, e_b_conv, e_b_alog, e_b_dtbias, e_b_onorm, o_w_in, o_q_lora_norm,
              o_kv_lora_norm, o_w_uq, o_w_ukv, o_qnorm, o_knorm, o_w_out, moe_w_router, moe_b_router,
              moe_w_gu, moe_b_gu, moe_w_down, moe_b_down):
    xp, xs = x_prompt, x_sample
    new_a_k, new_a_v, new_b_fwd, new_b_bwd, new_c_ckv, new_c_krope = [], [], [], [], [], []
    for layer in range(DEPTH):
        sh1p, sc1p, g1p, sh2p, sc2p, g2p = adaln(c_ctx, w_mod[layer], b_mod[layer])
        sh1s, sc1s, g1s, sh2s, sc2s, g2s = adaln(c, w_mod[layer], b_mod[layer])
        hp = modulate(xp, norm_mix[layer], sh1p, sc1p)
        hs = modulate(xs, norm_mix[layer], sh1s, sc1s)
        i = layer // 2
        if layer % 2 == 0:
            ep = (e_w_in[i], e_w_out[i], e_a_qnorm[i], e_a_knorm[i], e_a_sink[i], e_b_conv[i],
                  e_b_alog[i], e_b_dtbias[i], e_b_onorm[i])
            mp, (k_c, v_c, s_f, s_b) = even_mixer(hp, None, *ep)
            ms, _ = even_mixer(hs, (cache_a_k[:, i], cache_a_v[:, i], state_b_fwd[:, i],
                                    state_b_bwd[:, i]), *ep)
            new_a_k.append(k_c)
            new_a_v.append(v_c)
            new_b_fwd.append(s_f)
            new_b_bwd.append(s_b)
        else:
            op = (o_w_in[i], o_q_lora_norm[i], o_kv_lora_norm[i], o_w_uq[i], o_w_ukv[i], o_qnorm[i],
                  o_knorm[i], o_w_out[i])
            mp, (ckv_c, kr_c) = odd_mixer(hp, None, *op)
            ms, _ = odd_mixer(hs, (cache_c_ckv[:, i], cache_c_krope[:, i]), *op)
            new_c_ckv.append(ckv_c)
            new_c_krope.append(kr_c)
        xp = xp + g1p * mp
        xs = xs + g1s * ms
        hp = modulate(xp, norm_ffn[layer], sh2p, sc2p)
        hs = modulate(xs, norm_ffn[layer], sh2s, sc2s)
        n_p = hp.shape[0] * hp.shape[1]
        f = moe(jnp.concatenate([hp.reshape(-1, D_MODEL), hs.reshape(-1, D_MODEL)], axis=0),
                moe_w_router[layer], moe_b_router[layer], moe_w_gu[layer], moe_b_gu[layer],
                moe_w_down[layer], moe_b_down[layer])
        xp = xp + g2p * f[:n_p].reshape(xp.shape)
        xs = xs + g2s * f[n_p:].reshape(xs.shape)
    return (xp, xs, jnp.stack(new_a_k, axis=1), jnp.stack(new_a_v, axis=1),
            jnp.stack(new_b_fwd, axis=1), jnp.stack(new_b_bwd, axis=1),
            jnp.stack(new_c_ckv, axis=1), jnp.stack(new_c_krope, axis=1))
```

```python
import functools
import math

import numpy as np
import jax
import jax.numpy as jnp
from jax import lax
from jax.experimental import pallas as pl
from jax.experimental.pallas import tpu as pltpu

F32 = jnp.float32
BF16 = jnp.bfloat16
HIGHEST = lax.Precision.HIGHEST

EPS = 1e-6
NEG_INF = -1e30
ROPE_BASE = 10000.0
GRID_W = 64
N_GROUPS = 8

A_HEADS, A_KV_HEADS, A_GROUP, A_HEAD_DIM, A_WINDOW, A_BLOCK = 8, 2, 4, 64, 128, 128
B_HEADS, B_DK, B_DV, B_CHUNK = 8, 64, 64, 64
C_HEADS, C_NOPE, C_ROPE, C_V, C_Q_LORA, C_KV_LORA = 16, 64, 32, 64, 384, 256
C_QK = C_NOPE + C_ROPE
N_EXPERTS, TOP_K = 32, 4
SWIGLU_LIMIT, SWIGLU_ALPHA = 7.0, 1.702

VMEM_LIMIT = 56 * 1024 * 1024


def _params(sem, vmem=None):
    return pltpu.CompilerParams(dimension_semantics=sem, vmem_limit_bytes=vmem)


def _bdot(a, b):
    return jnp.dot(a.astype(BF16), b.astype(BF16), preferred_element_type=F32)


def _bdot_nt(a, b):
    return lax.dot_general(a.astype(BF16), b.astype(BF16), (((1,), (1,)), ((), ())),
                           preferred_element_type=F32)


def _bdot_tn(a, b):
    return lax.dot_general(a.astype(BF16), b.astype(BF16), (((0,), (0,)), ((), ())),
                           preferred_element_type=F32)


def _hdot(a, b):
    return jnp.dot(a, b, preferred_element_type=F32, precision=HIGHEST)


def _hdot_nt(a, b):
    return lax.dot_general(a, b, (((1,), (1,)), ((), ())), preferred_element_type=F32,
                           precision=HIGHEST)


def _rms(x, gain):
    return x * lax.rsqrt(jnp.mean(x * x, axis=-1, keepdims=True) + EPS) * gain


def _silu(x):
    return x * jax.nn.sigmoid(x)


def _group_of_tile(tm, n_p, t_s):
    def group(i):
        r = i * tm
        return jnp.where(r < n_p, 0, 1 + (r - n_p) // t_s)
    return group


def _adaln_kernel(cond_ref, w_ref, b_ref, o_ref):
    o_ref[...] = _bdot(_silu(cond_ref[...]), w_ref[...]) + b_ref[...]


def _adaln(cond, w_mod, b_mod):
    depth, d, _ = w_mod.shape
    return pl.pallas_call(
        _adaln_kernel,
        grid=(depth, 6),
        in_specs=[pl.BlockSpec((N_GROUPS, d), lambda l, j: (0, 0)),
                  pl.BlockSpec((None, d, d), lambda l, j: (l, 0, j)),
                  pl.BlockSpec((None, 1, d), lambda l, j: (l, 0, j))],
        out_specs=pl.BlockSpec((None, None, N_GROUPS, d), lambda l, j: (l, j, 0, 0)),
        out_shape=jax.ShapeDtypeStruct((depth, 6, N_GROUPS, d), F32),
        compiler_params=_params(("arbitrary", "arbitrary")),
        name="adaln",
    )(cond, w_mod, b_mod.reshape(depth, 1, 6 * d))


def _modulate_kernel(x_ref, g_ref, sc_ref, sh_ref, o_ref):
    y = _rms(x_ref[...], g_ref[...])
    o_ref[...] = (y * (1 + sc_ref[...]) + sh_ref[...]).astype(o_ref.dtype)


def _modulate(x, gain, scale, shift, n_p, t_s, tm=512):
    n, d = x.shape
    group = _group_of_tile(tm, n_p, t_s)
    return pl.pallas_call(
        _modulate_kernel,
        grid=(n // tm,),
        in_specs=[pl.BlockSpec((tm, d), lambda i: (i, 0)),
                  pl.BlockSpec((1, d), lambda i: (0, 0)),
                  pl.BlockSpec((None, 1, d), lambda i: (group(i), 0, 0)),
                  pl.BlockSpec((None, 1, d), lambda i: (group(i), 0, 0))],
        out_specs=pl.BlockSpec((tm, d), lambda i: (i, 0)),
        out_shape=jax.ShapeDtypeStruct((n, d), BF16),
        compiler_params=_params(("arbitrary",)),
        name="modulate",
    )(x, gain.reshape(1, d), scale.reshape(N_GROUPS, 1, d), shift.reshape(N_GROUPS, 1, d))


def _mm_kernel(x_ref, w_ref, o_ref, wb_ref):
    @pl.when(pl.program_id(1) == 0)
    def _():
        wb_ref[...] = w_ref[...].astype(BF16)
    o_ref[...] = jnp.dot(x_ref[...], wb_ref[...], preferred_element_type=F32).astype(o_ref.dtype)


def _matmul(x, w3, layer, col0_blocks, tn, n_out, out_dtype=F32, tm=512, name="matmul"):
    n, k = x.shape
    return pl.pallas_call(
        _mm_kernel,
        grid=(n_out // tn, n // tm),
        in_specs=[pl.BlockSpec((tm, k), lambda j, i: (i, 0)),
                  pl.BlockSpec((None, k, tn), lambda j, i: (layer, 0, col0_blocks + j))],
        out_specs=pl.BlockSpec((tm, tn), lambda j, i: (i, j)),
        out_shape=jax.ShapeDtypeStruct((n, n_out), out_dtype),
        scratch_shapes=[pltpu.VMEM((k, tn), BF16)],
        compiler_params=_params(("arbitrary", "arbitrary"), VMEM_LIMIT),
        name=name,
    )(x, w3)


def _mm_res_kernel(x_ref, w_ref, res_ref, gate_ref, o_ref, wb_ref):
    @pl.when(pl.program_id(0) == 0)
    def _():
        wb_ref[...] = w_ref[...].astype(BF16)
    y = jnp.dot(x_ref[...], wb_ref[...], preferred_element_type=F32)
    o_ref[...] = res_ref[...] + gate_ref[...] * y


def _matmul_residual(mix, w3, layer, res, gate, n_p, t_s, tm=512):
    n, k = mix.shape
    d = res.shape[1]
    group = _group_of_tile(tm, n_p, t_s)
    return pl.pallas_call(
        _mm_res_kernel,
        grid=(n // tm,),
        in_specs=[pl.BlockSpec((tm, k), lambda i: (i, 0)),
                  pl.BlockSpec((None, k, d), lambda i: (layer, 0, 0)),
                  pl.BlockSpec((tm, d), lambda i: (i, 0)),
                  pl.BlockSpec((None, 1, d), lambda i: (group(i), 0, 0))],
        out_specs=pl.BlockSpec((tm, d), lambda i: (i, 0)),
        out_shape=jax.ShapeDtypeStruct((n, d), F32),
        scratch_shapes=[pltpu.VMEM((k, d), BF16)],
        compiler_params=_params(("arbitrary",), VMEM_LIMIT),
        name="out_proj_residual",
    )(mix, w3, res, gate.reshape(N_GROUPS, 1, d))


def _rope_tables(t_len, d):
    half, quarter = d // 2, d // 4
    pos = np.arange(t_len)
    row, col = pos // GRID_W, pos % GRID_W
    inv = ROPE_BASE ** (-np.arange(quarter, dtype=np.float64) / quarter)
    ang_r = row[:, None] * inv[None, :]
    ang_c = col[:, None] * inv[None, :]
    cos = np.concatenate([np.cos(ang_r), np.cos(ang_r), np.cos(ang_c), np.cos(ang_c)], axis=1)
    sin = np.concatenate([-np.sin(ang_r), np.sin(ang_r), -np.sin(ang_c), np.sin(ang_c)], axis=1)
    return jnp.asarray(cos, F32), jnp.asarray(sin, F32)


def _swap_pairs(x):
    q = x.shape[-1] // 4
    return jnp.concatenate([x[:, q:2 * q], x[:, :q], x[:, 3 * q:], x[:, 2 * q:3 * q]], axis=-1)


def _rope(x, cos, sin):
    return x * cos + _swap_pairs(x) * sin


def _softmax_attend(parts, sink):
    m = parts[0][0].max(axis=-1, keepdims=True)
    for s, _ in parts[1:]:
        m = jnp.maximum(m, s.max(axis=-1, keepdims=True))
    if sink is not None:
        m = jnp.maximum(m, sink)
    den = jnp.exp(sink - m) if sink is not None else 0.0
    acc = None
    for s, v in parts:
        p = jnp.exp(s - m)
        den = den + p.sum(axis=-1, keepdims=True)
        o = _bdot(p, v)
        acc = o if acc is None else acc + o
    return acc / den


def _attn_a_prompt_kernel(sink_ref, q_ref, kv_ref, gq_ref, gk_ref, o_ref, kn_ref):
    scale = A_HEAD_DIM ** -0.5
    q = q_ref[...]
    kv = kv_ref[...]
    outs = []
    kns = []
    for hk in range(A_KV_HEADS):
        k = _rms(kv[:, hk * A_HEAD_DIM:(hk + 1) * A_HEAD_DIM], gk_ref[...])
        v = kv[:, (A_KV_HEADS + hk) * A_HEAD_DIM:(A_KV_HEADS + hk + 1) * A_HEAD_DIM]
        kns.append(k)
        for g in range(A_GROUP):
            h = hk * A_GROUP + g
            qh = _rms(q[:, h * A_HEAD_DIM:(h + 1) * A_HEAD_DIM], gq_ref[...])
            s = _bdot_nt(qh, k) * scale
            outs.append(_softmax_attend([(s, v)], sink_ref[h]))
    o_ref[...] = jnp.concatenate(outs, axis=-1).astype(o_ref.dtype)
    kn_ref[...] = jnp.concatenate(kns, axis=-1)


def _attn_a_prompt(proj, n_seq, t, sink, gq, gk):
    qw = A_HEADS * A_HEAD_DIM
    kvw = 2 * A_KV_HEADS * A_HEAD_DIM
    return pl.pallas_call(
        _attn_a_prompt_kernel,
        grid=(n_seq,),
        in_specs=[pl.BlockSpec(memory_space=pltpu.SMEM),
                  pl.BlockSpec((t, qw), lambda b: (b, 0)),
                  pl.BlockSpec((t, kvw), lambda b: (b, qw // kvw)),
                  pl.BlockSpec((1, A_HEAD_DIM), lambda b: (0, 0)),
                  pl.BlockSpec((1, A_HEAD_DIM), lambda b: (0, 0))],
        out_specs=[pl.BlockSpec((t, qw), lambda b: (b, 0)),
                   pl.BlockSpec((t, A_KV_HEADS * A_HEAD_DIM), lambda b: (b, 0))],
        out_shape=[jax.ShapeDtypeStruct((n_seq * t, qw), BF16),
                   jax.ShapeDtypeStruct((n_seq * t, A_KV_HEADS * A_HEAD_DIM), F32)],
        compiler_params=_params(("arbitrary",)),
        name="attn_a_context",
    )(sink, proj, proj, gq.reshape(1, -1), gk.reshape(1, -1))


def _attn_a_sample_kernel(sink_ref, q_ref, kv_ref, kc_ref, vc_ref, gq_ref, gk_ref, cos_ref, sin_ref,
                          o_ref, *, t):
    scale = A_HEAD_DIM ** -0.5
    i = pl.program_id(1)
    win = 3 * A_BLOCK
    q0 = pl.multiple_of(i * A_BLOCK, A_BLOCK)
    k0 = pl.multiple_of(jnp.clip((i - 1) * A_BLOCK, 0, t - win), A_BLOCK)
    q = q_ref[...]
    kv = kv_ref[pl.ds(k0, win), :]
    cq, sq = cos_ref[pl.ds(q0, A_BLOCK), :], sin_ref[pl.ds(q0, A_BLOCK), :]
    ck, sk = cos_ref[pl.ds(k0, win), :], sin_ref[pl.ds(k0, win), :]
    qpos = q0 + lax.broadcasted_iota(jnp.int32, (A_BLOCK, win), 0)
    kpos = k0 + lax.broadcasted_iota(jnp.int32, (A_BLOCK, win), 1)
    mask = jnp.abs(qpos - kpos) <= A_WINDOW
    outs = []
    for hk in range(A_KV_HEADS):
        k = _rope(_rms(kv[:, hk * A_HEAD_DIM:(hk + 1) * A_HEAD_DIM], gk_ref[...]), ck, sk)
        v = kv[:, (A_KV_HEADS + hk) * A_HEAD_DIM:(A_KV_HEADS + hk + 1) * A_HEAD_DIM]
        kc = kc_ref[hk]
        vc = vc_ref[hk]
        for g in range(A_GROUP):
            h = hk * A_GROUP + g
            qh = _rope(_rms(q[:, h * A_HEAD_DIM:(h + 1) * A_HEAD_DIM], gq_ref[...]), cq, sq)
            s1 = jnp.where(mask, _bdot_nt(qh, k) * scale, NEG_INF)
            s2 = _bdot_nt(qh, kc) * scale
            outs.append(_softmax_attend([(s1, v), (s2, vc)], sink_ref[h]))
    o_ref[...] = jnp.concatenate(outs, axis=-1).astype(o_ref.dtype)


def _attn_a_sample(proj, row0, n_seq, t, k_ctx, v_ctx, sink, gq, gk):
    qw = A_HEADS * A_HEAD_DIM
    kvw = 2 * A_KV_HEADS * A_HEAD_DIM
    nqb = t // A_BLOCK
    cos, sin = _rope_tables(t, A_HEAD_DIM)
    past = k_ctx.shape[2]
    return pl.pallas_call(
        functools.partial(_attn_a_sample_kernel, t=t),
        grid=(n_seq, nqb),
        in_specs=[pl.BlockSpec(memory_space=pltpu.SMEM),
                  pl.BlockSpec((A_BLOCK, qw), lambda b, i: (row0 // A_BLOCK + b * nqb + i, 0)),
                  pl.BlockSpec((t, kvw), lambda b, i: (row0 // t + b, qw // kvw)),
                  pl.BlockSpec((None, A_KV_HEADS, past, A_HEAD_DIM), lambda b, i: (b, 0, 0, 0)),
                  pl.BlockSpec((None, A_KV_HEADS, past, A_HEAD_DIM), lambda b, i: (b, 0, 0, 0)),
                  pl.BlockSpec((1, A_HEAD_DIM), lambda b, i: (0, 0)),
                  pl.BlockSpec((1, A_HEAD_DIM), lambda b, i: (0, 0)),
                  pl.BlockSpec((t, A_HEAD_DIM), lambda b, i: (0, 0)),
                  pl.BlockSpec((t, A_HEAD_DIM), lambda b, i: (0, 0))],
        out_specs=pl.BlockSpec((A_BLOCK, qw), lambda b, i: (b * nqb + i, 0)),
        out_shape=jax.ShapeDtypeStruct((n_seq * t, qw), BF16),
        compiler_params=_params(("arbitrary", "arbitrary")),
        name="attn_a_latent",
    )(sink, proj, proj, k_ctx, v_ctx, gq.reshape(1, -1), gk.reshape(1, -1), cos, sin)


def _per_head_lanes(x, fn):
    lane = lax.broadcasted_iota(jnp.int32, x.shape, 1)
    lo = fn(x[:, :B_DK])
    hi = fn(x[:, B_DK:])
    return jnp.where(lane < B_DK, lo, hi)


def _conv_silu(x, w):
    t = x.shape[0]
    row = lax.broadcasted_iota(jnp.int32, x.shape, 0)
    prev = jnp.where(row == 0, 0.0, pltpu.roll(x, 1, 0))
    nxt = jnp.where(row == t - 1, 0.0, pltpu.roll(x, t - 1, 0))
    return _silu(prev * w[0:1, :] + x * w[1:2, :] + nxt * w[2:3, :])


def _delta_chunk(qh, kh, vh, g_col, g_row, beta, s, tri, tri_t, strict, bdiag, eye):
    c = qh.shape[0]
    gc_col = jnp.sum(jnp.where(tri, g_row, 0.0), axis=1, keepdims=True)
    gc_row = jnp.sum(jnp.where(tri_t, g_col, 0.0), axis=0, keepdims=True)
    g_tot = jnp.sum(g_row, axis=1, keepdims=True)
    decay = jnp.exp(jnp.where(tri, gc_col - gc_row, NEG_INF))
    kb = kh * beta
    vb = vh * beta
    qk = _hdot_nt(jnp.concatenate([kb, qh], axis=0), kh)
    m = qk[:c] * jnp.where(strict, decay, 0.0)
    aqk = qk[c:] * decay
    dg = jnp.where(bdiag, m, 0.0)
    off = m - dg
    n1 = -dg
    x = eye + n1
    n2 = _hdot(n1, n1)
    x = x + _hdot(x, n2)
    n4 = _hdot(n2, n2)
    x = x + _hdot(x, n4)
    n8 = _hdot(n4, n4)
    x = x + _hdot(x, n8)
    f = -_hdot(x, off)
    f2 = _hdot(f, f)
    y = x + _hdot(f, x)
    ainv = y + _hdot(f2, y)
    egc = jnp.exp(gc_col)
    u = _hdot(ainv, vb)
    w = _hdot(ainv, kb * egc)
    delta = u - _bdot(w, s)
    o = _bdot(qh * egc, s) + _bdot(aqk, delta)
    kd = kh * jnp.exp(g_tot - gc_col)
    s_new = s * jnp.exp(g_tot) + _bdot_tn(kd, delta)
    return o, s_new


def _deltanet_kernel(alog_ref, dtb_ref, q_ref, k_ref, v_ref, z_ref, cwq_ref, cwk_ref, cwv_ref,
                     tail_ref, tailt_ref, onorm_ref, s0f_ref, s0b_ref,
                     o_ref, sf_ref, sb_ref, qc_ref, kc_ref, vc_ref, oacc_ref, *, n_chunks):
    hp = pl.program_id(1)
    c = B_CHUNK

    def l2n(x):
        ss = _per_head_lanes(x * x, lambda a: jnp.sum(a, axis=-1, keepdims=True))
        return x * lax.rsqrt(ss + EPS)

    qc_ref[...] = l2n(_conv_silu(q_ref[...], cwq_ref[...])) * (B_DK ** -0.5)
    kc_ref[...] = l2n(_conv_silu(k_ref[...], cwk_ref[...]))
    vc_ref[...] = _conv_silu(v_ref[...], cwv_ref[...])
    oacc_ref[...] = jnp.zeros_like(oacc_ref)

    ii = lax.broadcasted_iota(jnp.int32, (c, c), 0)
    jj = lax.broadcasted_iota(jnp.int32, (c, c), 1)
    eye = (ii == jj).astype(F32)
    bdiag = (ii // 16) == (jj // 16)
    lower, upper = ii >= jj, ii <= jj
    slower, supper = ii > jj, ii < jj
    lane32 = lax.broadcasted_iota(jnp.int32, (c, 4 * B_HEADS), 1)

    def gates(chunk, d, head):
        tail = tail_ref[pl.ds(pl.multiple_of(chunk * c, c), c), :]
        ia = 2 * d * B_HEADS + head
        ib = ia + B_HEADS
        a_col = jnp.sum(jnp.where(lane32 == ia, tail, 0.0), axis=1, keepdims=True)
        b_col = jnp.sum(jnp.where(lane32 == ib, tail, 0.0), axis=1, keepdims=True)
        a_row = tailt_ref[chunk, pl.ds(ia, 1), :]
        na = -jnp.exp(alog_ref[d, head])
        bias = dtb_ref[d, head]
        g_col = na * jax.nn.softplus(a_col + bias)
        g_row = na * jax.nn.softplus(a_row + bias)
        return g_col, g_row, jax.nn.sigmoid(b_col)

    def body(j, carry):
        new = []
        for d in range(2):
            chunk = j if d == 0 else n_chunks - 1 - j
            r0 = pl.multiple_of(chunk * c, c)
            qb, kb_, vb_ = qc_ref[pl.ds(r0, c), :], kc_ref[pl.ds(r0, c), :], vc_ref[pl.ds(r0, c), :]
            tri, tri_t, strict = (lower, upper, slower) if d == 0 else (upper, lower, supper)
            outs = []
            for hh in range(2):
                g_col, g_row, beta = gates(chunk, d, 2 * hp + hh)
                sl = slice(hh * B_DK, (hh + 1) * B_DK)
                o, s_new = _delta_chunk(qb[:, sl], kb_[:, sl], vb_[:, sl], g_col, g_row, beta,
                                        carry[2 * d + hh], tri, tri_t, strict, bdiag, eye)
                outs.append(o)
                new.append(s_new)
            oacc_ref[pl.ds(r0, c), :] += jnp.concatenate(outs, axis=-1)
        return tuple(new)

    init = (s0f_ref[0], s0f_ref[1], s0b_ref[0], s0b_ref[1])
    fin = lax.fori_loop(0, n_chunks, body, init)
    sf_ref[0], sf_ref[1], sb_ref[0], sb_ref[1] = fin[0], fin[1], fin[2], fin[3]

    o = oacc_ref[...]
    ms = _per_head_lanes(o * o, lambda a: jnp.mean(a, axis=-1, keepdims=True))
    o_ref[...] = (o * lax.rsqrt(ms + EPS) * onorm_ref[...] * _silu(z_ref[...])).astype(o_ref.dtype)


def _deltanet(proj, tail, row0, n_seq, t, conv_w, a_log, dt_bias, o_norm, s0_f, s0_b):
    c = B_CHUNK
    n_chunks = t // c
    lw = 2 * B_DK
    col_q = (A_HEADS + 2 * A_KV_HEADS) * A_HEAD_DIM // lw
    nhp = B_HEADS // 2
    rows = tail[row0:row0 + n_seq * t]
    tail_t = rows.reshape(n_seq, n_chunks, c, 4 * B_HEADS).transpose(0, 1, 3, 2)
    onorm2 = jnp.concatenate([o_norm, o_norm]).reshape(1, lw)
    b0 = row0 // t
    seq_blk = lambda off: pl.BlockSpec((t, lw), lambda b, h: (b0 + b, col_q + off + h))
    cw_blk = lambda off: pl.BlockSpec((3, lw), lambda b, h: (0, off + h))
    st_blk = pl.BlockSpec((None, 2, B_DK, B_DV), lambda b, h: (b, h, 0, 0))
    return pl.pallas_call(
        functools.partial(_deltanet_kernel, n_chunks=n_chunks),
        grid=(n_seq, nhp),
        in_specs=[pl.BlockSpec(memory_space=pltpu.SMEM), pl.BlockSpec(memory_space=pltpu.SMEM),
                  seq_blk(0), seq_blk(nhp), seq_blk(2 * nhp), seq_blk(3 * nhp),
                  cw_blk(0), cw_blk(nhp), cw_blk(2 * nhp),
                  pl.BlockSpec((t, 4 * B_HEADS), lambda b, h: (b, 0)),
                  pl.BlockSpec((None, n_chunks, 4 * B_HEADS, c), lambda b, h: (b, 0, 0, 0)),
                  pl.BlockSpec((1, lw), lambda b, h: (0, 0)),
                  st_blk, st_blk],
        out_specs=[pl.BlockSpec((t, lw), lambda b, h: (b, h)), st_blk, st_blk],
        out_shape=[jax.ShapeDtypeStruct((n_seq * t, B_HEADS * B_DV), BF16),
                   jax.ShapeDtypeStruct((n_seq, B_HEADS, B_DK, B_DV), F32),
                   jax.ShapeDtypeStruct((n_seq, B_HEADS, B_DK, B_DV), F32)],
        scratch_shapes=[pltpu.VMEM((t, lw), F32)] * 4,
        compiler_params=_params(("arbitrary", "arbitrary")),
        name="deltanet",
    )(a_log, dt_bias, proj, proj, proj, proj, conv_w, conv_w, conv_w, rows, tail_t, onorm2, s0_f, s0_b)


def _lora_norm_kernel(p_ref, gq_ref, gkv_ref, cq_ref, ckv_ref, ckvb_ref):
    p = p_ref[...]
    cq_ref[...] = _rms(p[:, :C_Q_LORA], gq_ref[...]).astype(cq_ref.dtype)
    ckv = _rms(p[:, C_Q_LORA:C_Q_LORA + C_KV_LORA], gkv_ref[...])
    ckv_ref[...] = ckv
    ckvb_ref[...] = ckv.astype(ckvb_ref.dtype)


def _lora_norm(p1, gq, gkv, tm=512):
    n, w = p1.shape
    return pl.pallas_call(
        _lora_norm_kernel,
        grid=(n // tm,),
        in_specs=[pl.BlockSpec((tm, w), lambda i: (i, 0)),
                  pl.BlockSpec((1, C_Q_LORA), lambda i: (0, 0)),
                  pl.BlockSpec((1, C_KV_LORA), lambda i: (0, 0))],
        out_specs=[pl.BlockSpec((tm, C_Q_LORA), lambda i: (i, 0)),
                   pl.BlockSpec((tm, C_KV_LORA), lambda i: (i, 0)),
                   pl.BlockSpec((tm, C_KV_LORA), lambda i: (i, 0))],
        out_shape=[jax.ShapeDtypeStruct((n, C_Q_LORA), BF16),
                   jax.ShapeDtypeStruct((n, C_KV_LORA), F32),
                   jax.ShapeDtypeStruct((n, C_KV_LORA), BF16)],
        compiler_params=_params(("arbitrary",)),
        name="lora_norm",
    )(p1, gq.reshape(1, -1), gkv.reshape(1, -1))


def _mla_head_q(q, h, gq):
    qn, qr = q[:, h * C_QK:h * C_QK + C_NOPE], q[:, h * C_QK + C_NOPE:(h + 1) * C_QK]
    rn = lax.rsqrt((jnp.sum(qn * qn, axis=-1, keepdims=True)
                    + jnp.sum(qr * qr, axis=-1, keepdims=True)) / C_QK + EPS)
    return qn * rn * gq[:, :C_NOPE], qr * rn * gq[:, C_NOPE:]


def _mla_head_k(kv, kr, kr_ss, h, gk):
    kn = kv[:, h * (C_NOPE + C_V):h * (C_NOPE + C_V) + C_NOPE]
    v = kv[:, h * (C_NOPE + C_V) + C_NOPE:(h + 1) * (C_NOPE + C_V)]
    rn = lax.rsqrt((jnp.sum(kn * kn, axis=-1, keepdims=True) + kr_ss) / C_QK + EPS)
    return kn * rn * gk[:, :C_NOPE], kr * rn, v


def _attn_c_prompt_kernel(q_ref, kv_ref, p_ref, gq_ref, gk_ref, o_ref):
    scale = C_QK ** -0.5
    q = q_ref[...]
    kv = kv_ref[...]
    gq, gk = gq_ref[...], gk_ref[...]
    kr_raw = p_ref[...][:, C_Q_LORA + C_KV_LORA:]
    kr_ss = jnp.sum(kr_raw * kr_raw, axis=-1, keepdims=True)
    kr_g = kr_raw * gk[:, C_NOPE:]
    outs = []
    for h in range(C_HEADS):
        qn, qr = _mla_head_q(q, h, gq)
        kn, kr, v = _mla_head_k(kv, kr_g, kr_ss, h, gk)
        s = (_bdot_nt(qn, kn) + _bdot_nt(qr, kr)) * scale
        outs.append(_softmax_attend([(s, v)], None))
    o_ref[...] = jnp.concatenate(outs, axis=-1).astype(o_ref.dtype)


def _attn_c_prompt(q, kv, p1, n_seq, t, gq, gk):
    return pl.pallas_call(
        _attn_c_prompt_kernel,
        grid=(n_seq,),
        in_specs=[pl.BlockSpec((t, q.shape[1]), lambda b: (b, 0)),
                  pl.BlockSpec((t, kv.shape[1]), lambda b: (b, 0)),
                  pl.BlockSpec((t, p1.shape[1]), lambda b: (b, 0)),
                  pl.BlockSpec((1, C_QK), lambda b: (0, 0)),
                  pl.BlockSpec((1, C_QK), lambda b: (0, 0))],
        out_specs=pl.BlockSpec((t, C_HEADS * C_V), lambda b: (b, 0)),
        out_shape=jax.ShapeDtypeStruct((n_seq * t, C_HEADS * C_V), BF16),
        compiler_params=_params(("arbitrary",), VMEM_LIMIT),
        name="attn_c_context",
    )(q, kv, p1, gq.reshape(1, -1), gk.reshape(1, -1))


def _attn_c_sample_kernel(q_ref, kv_ref, p_ref, kvc_ref, krc_ref, gq_ref, gk_ref, cos_ref, sin_ref,
                          o_ref, *, tq):
    scale = C_QK ** -0.5
    i = pl.program_id(1)
    q0 = pl.multiple_of(i * tq, tq)
    q = q_ref[...]
    kv = kv_ref[...]
    kvc = kvc_ref[...]
    gq, gk = gq_ref[...], gk_ref[...]
    cos, sin = cos_ref[...], sin_ref[...]
    cq, sq = cos_ref[pl.ds(q0, tq), :], sin_ref[pl.ds(q0, tq), :]
    kr_raw = p_ref[...][:, C_Q_LORA + C_KV_LORA:]
    kr_ss = jnp.sum(kr_raw * kr_raw, axis=-1, keepdims=True)
    kr_g = _rope(kr_raw * gk[:, C_NOPE:], cos, sin)
    krc_raw = krc_ref[...]
    krc_ss = jnp.sum(krc_raw * krc_raw, axis=-1, keepdims=True)
    krc_g = krc_raw * gk[:, C_NOPE:]
    outs = []
    for h in range(C_HEADS):
        qn, qr = _mla_head_q(q, h, gq)
        qr = _rope(qr, cq, sq)
        kn, kr, v = _mla_head_k(kv, kr_g, kr_ss, h, gk)
        knc, krc, vc = _mla_head_k(kvc, krc_g, krc_ss, h, gk)
        s1 = (_bdot_nt(qn, kn) + _bdot_nt(qr, kr)) * scale
        s2 = (_bdot_nt(qn, knc) + _bdot_nt(qr, krc)) * scale
        outs.append(_softmax_attend([(s1, v), (s2, vc)], None))
    o_ref[...] = jnp.concatenate(outs, axis=-1).astype(o_ref.dtype)


def _attn_c_sample(q, kv, p1, row0, n_seq, t, kr_ctx, gq, gk, tq=256):
    n = p1.shape[0]
    past = kr_ctx.shape[1]
    nq = t // tq
    cos, sin = _rope_tables(t, C_ROPE)
    return pl.pallas_call(
        functools.partial(_attn_c_sample_kernel, tq=tq),
        grid=(n_seq, nq),
        in_specs=[pl.BlockSpec((tq, q.shape[1]), lambda b, i: (row0 // tq + b * nq + i, 0)),
                  pl.BlockSpec((t, kv.shape[1]), lambda b, i: (row0 // t + b, 0)),
                  pl.BlockSpec((t, p1.shape[1]), lambda b, i: (row0 // t + b, 0)),
                  pl.BlockSpec((past, kv.shape[1]), lambda b, i: (n // past + b, 0)),
                  pl.BlockSpec((None, past, C_ROPE), lambda b, i: (b, 0, 0)),
                  pl.BlockSpec((1, C_QK), lambda b, i: (0, 0)),
                  pl.BlockSpec((1, C_QK), lambda b, i: (0, 0)),
                  pl.BlockSpec((t, C_ROPE), lambda b, i: (0, 0)),
                  pl.BlockSpec((t, C_ROPE), lambda b, i: (0, 0))],
        out_specs=pl.BlockSpec((tq, C_HEADS * C_V), lambda b, i: (b * nq + i, 0)),
        out_shape=jax.ShapeDtypeStruct((n_seq * t, C_HEADS * C_V), BF16),
        compiler_params=_params(("arbitrary", "arbitrary"), VMEM_LIMIT),
        name="attn_c_latent",
    )(q, kv, p1, kv, kr_ctx, gq.reshape(1, -1), gk.reshape(1, -1), cos, sin)


def _router_kernel(x_ref, g_ref, sc_ref, sh_ref, wr_ref, br_ref, h_ref, comb_ref):
    h = _rms(x_ref[...], g_ref[...]) * (1 + sc_ref[...]) + sh_ref[...]
    h_ref[...] = h.astype(h_ref.dtype)
    logits = _hdot(h, wr_ref[...]) + br_ref[...]
    lane = lax.broadcasted_iota(jnp.int32, logits.shape, 1)
    work = logits
    sel = jnp.zeros(logits.shape, jnp.bool_)
    top = None
    for _ in range(TOP_K):
        m = work.max(axis=-1, keepdims=True)
        first = jnp.min(jnp.where(work == m, lane, N_EXPERTS), axis=-1, keepdims=True)
        pick = lane == first
        sel = jnp.logical_or(sel, pick)
        work = jnp.where(pick, -jnp.inf, work)
        top = m if top is None else top
    w = jnp.where(sel, jnp.exp(logits - top), 0.0)
    comb_ref[...] = w / w.sum(axis=-1, keepdims=True)


def _router(x, gain, scale, shift, w_router, b_router, layer, n_p, t_s, tm=512):
    n, d = x.shape
    e = w_router.shape[-1]
    group = _group_of_tile(tm, n_p, t_s)
    return pl.pallas_call(
        _router_kernel,
        grid=(n // tm,),
        in_specs=[pl.BlockSpec((tm, d), lambda i: (i, 0)),
                  pl.BlockSpec((1, d), lambda i: (0, 0)),
                  pl.BlockSpec((None, 1, d), lambda i: (group(i), 0, 0)),
                  pl.BlockSpec((None, 1, d), lambda i: (group(i), 0, 0)),
                  pl.BlockSpec((None, d, e), lambda i: (layer, 0, 0)),
                  pl.BlockSpec((None, 1, e), lambda i: (layer, 0, 0))],
        out_specs=[pl.BlockSpec((tm, d), lambda i: (i, 0)), pl.BlockSpec((tm, e), lambda i: (i, 0))],
        out_shape=[jax.ShapeDtypeStruct((n, d), BF16), jax.ShapeDtypeStruct((n, e), F32)],
        compiler_params=_params(("arbitrary",)),
        name="router",
    )(x, gain.reshape(1, d), scale.reshape(N_GROUPS, 1, d), shift.reshape(N_GROUPS, 1, d),
      w_router, b_router.reshape(-1, 1, e))


def _moe_dense_kernel(h_ref, comb_ref, wgu_ref, bgu_ref, wd_ref, bd_ref, x_ref, gate_ref, o_ref,
                      acc_ref, *, d_ff):
    e = pl.program_id(1)

    @pl.when(e == 0)
    def _():
        acc_ref[...] = jnp.zeros_like(acc_ref)

    gu = jnp.dot(h_ref[...], wgu_ref[...].astype(BF16), preferred_element_type=F32) + bgu_ref[...]
    gate = jnp.minimum(gu[:, :d_ff], SWIGLU_LIMIT)
    up = jnp.clip(gu[:, d_ff:], -SWIGLU_LIMIT, SWIGLU_LIMIT)
    act = (up + 1) * gate * jax.nn.sigmoid(SWIGLU_ALPHA * gate)
    y = jnp.dot(act.astype(BF16), wd_ref[...].astype(BF16), preferred_element_type=F32) + bd_ref[...]
    comb = comb_ref[...]
    lane = lax.broadcasted_iota(jnp.int32, comb.shape, 1)
    ce = jnp.sum(jnp.where(lane == e, comb, 0.0), axis=-1, keepdims=True)
    acc_ref[...] += ce * y

    @pl.when(e == pl.num_programs(1) - 1)
    def _():
        o_ref[...] = x_ref[...] + gate_ref[...] * acc_ref[...]


def _moe_dense(h, comb, x, gate, w_gu, b_gu, w_down, b_down, layer, n_p, t_s, tm=512):
    n, d = x.shape
    _, e, _, two_ff = w_gu.shape
    group = _group_of_tile(tm, n_p, t_s)
    return pl.pallas_call(
        functools.partial(_moe_dense_kernel, d_ff=two_ff // 2),
        grid=(n // tm, e),
        in_specs=[pl.BlockSpec((tm, d), lambda i, j: (i, 0)),
                  pl.BlockSpec((tm, e), lambda i, j: (i, 0)),
                  pl.BlockSpec((None, None, d, two_ff), lambda i, j: (layer, j, 0, 0)),
                  pl.BlockSpec((None, None, 1, two_ff), lambda i, j: (layer, j, 0, 0)),
                  pl.BlockSpec((None, None, two_ff // 2, d), lambda i, j: (layer, j, 0, 0)),
                  pl.BlockSpec((None, None, 1, d), lambda i, j: (layer, j, 0, 0)),
                  pl.BlockSpec((tm, d), lambda i, j: (i, 0)),
                  pl.BlockSpec((None, 1, d), lambda i, j: (group(i), 0, 0))],
        out_specs=pl.BlockSpec((tm, d), lambda i, j: (i, 0)),
        out_shape=jax.ShapeDtypeStruct((n, d), F32),
        scratch_shapes=[pltpu.VMEM((tm, d), F32)],
        compiler_params=_params(("arbitrary", "arbitrary"), VMEM_LIMIT),
        name="moe_dense",
    )(h, comb, w_gu, b_gu.reshape(b_gu.shape[0], e, 1, two_ff), w_down,
      b_down.reshape(b_down.shape[0], e, 1, d), x, gate.reshape(N_GROUPS, 1, d))


def kernel(x_prompt, x_sample, c, cache_a_k, cache_a_v, state_b_fwd, state_b_bwd, cache_c_ckv,
           cache_c_krope, c_ctx, w_mod, b_mod, norm_mix, norm_ffn, e_w_in, e_w_out, e_a_qnorm,
           e_a_knorm, e_a_sink, e_b_conv, e_b_alog, e_b_dtbias, e_b_onorm, o_w_in, o_q_lora_norm,
           o_kv_lora_norm, o_w_uq, o_w_ukv, o_qnorm, o_knorm, o_w_out, moe_w_router, moe_b_router,
           moe_w_gu, moe_b_gu, moe_w_down, moe_b_down):
    bp, tp, d = x_prompt.shape
    bs, ts, _ = x_sample.shape
    depth = w_mod.shape[0]
    n_p, n_s = bp * tp, bs * ts
    n = n_p + n_s
    assert bs + 1 <= N_GROUPS and ts % 512 == 0 and n_p % 1024 == 0

    x = jnp.concatenate([x_prompt.reshape(n_p, d), x_sample.reshape(n_s, d)], axis=0)
    cond = jnp.concatenate([c_ctx[None], c, jnp.zeros((N_GROUPS - 1 - bs, d), F32)], axis=0)
    mod = _adaln(cond, w_mod, b_mod)

    new_a_k, new_a_v, new_b_fwd, new_b_bwd, new_c_ckv, new_c_krope = [], [], [], [], [], []
    for layer in range(depth):
        sh1, sc1, g1, sh2, sc2, g2 = (mod[layer, j] for j in range(6))
        h = _modulate(x, norm_mix[layer], sc1, sh1, n_p, ts)
        i = layer // 2
        if layer % 2 == 0:
            main_w = (A_HEADS + 2 * A_KV_HEADS) * A_HEAD_DIM + 4 * B_HEADS * B_DK
            proj = _matmul(h, e_w_in, i, 0, main_w // 2, main_w, name="even_in_proj")
            tail = _matmul(h, e_w_in[i][None, :, main_w:], 0, 0, 4 * B_HEADS, 4 * B_HEADS,
                           name="even_gate_proj")
            oa_p, kn_p = _attn_a_prompt(proj, bp, tp, e_a_sink[i], e_a_qnorm[i], e_a_knorm[i])
            oa_s = _attn_a_sample(proj, n_p, bs, ts, cache_a_k[:, i], cache_a_v[:, i],
                                  e_a_sink[i], e_a_qnorm[i], e_a_knorm[i])
            zeros = jnp.zeros((bp, B_HEADS, B_DK, B_DV), F32)
            ob_p, s_f, s_b = _deltanet(proj, tail, 0, bp, tp, e_b_conv[i], e_b_alog[i],
                                       e_b_dtbias[i], e_b_onorm[i], zeros, zeros)
            ob_s, _, _ = _deltanet(proj, tail, n_p, bs, ts, e_b_conv[i], e_b_alog[i],
                                   e_b_dtbias[i], e_b_onorm[i], state_b_fwd[:, i], state_b_bwd[:, i])
            mix = jnp.concatenate([jnp.concatenate([oa_p, ob_p], axis=1),
                                   jnp.concatenate([oa_s, ob_s], axis=1)], axis=0)
            x = _matmul_residual(mix, e_w_out, i, x, g1, n_p, ts)
            kw = A_KV_HEADS * A_HEAD_DIM
            new_a_k.append(kn_p.reshape(bp, tp, A_KV_HEADS, A_HEAD_DIM).transpose(0, 2, 1, 3))
            v_p = proj[:n_p, A_HEADS * A_HEAD_DIM + kw:A_HEADS * A_HEAD_DIM + 2 * kw]
            new_a_v.append(v_p.reshape(bp, tp, A_KV_HEADS, A_HEAD_DIM).transpose(0, 2, 1, 3))
            new_b_fwd.append(s_f)
            new_b_bwd.append(s_b)
        else:
            p1 = _matmul(h, o_w_in, i, 0, o_w_in.shape[-1], o_w_in.shape[-1], name="odd_in_proj")
            cq, ckv, ckv_b = _lora_norm(p1, o_q_lora_norm[i], o_kv_lora_norm[i])
            q = _matmul(cq, o_w_uq, i, 0, o_w_uq.shape[-1] // 2, o_w_uq.shape[-1], name="odd_uq")
            ckv_all = jnp.concatenate([ckv_b, cache_c_ckv[:, i].reshape(-1, C_KV_LORA).astype(BF16)], 0)
            kv = _matmul(ckv_all, o_w_ukv, i, 0, o_w_ukv.shape[-1] // 2, o_w_ukv.shape[-1],
                         name="odd_ukv")
            o_p = _attn_c_prompt(q, kv, p1, bp, tp, o_qnorm[i], o_knorm[i])
            o_s = _attn_c_sample(q, kv, p1, n_p, bs, ts, cache_c_krope[:, i], o_qnorm[i], o_knorm[i])
            x = _matmul_residual(jnp.concatenate([o_p, o_s], axis=0), o_w_out, i, x, g1, n_p, ts)
            new_c_ckv.append(ckv[:n_p].reshape(bp, tp, C_KV_LORA))
            new_c_krope.append(p1[:n_p, C_Q_LORA + C_KV_LORA:].reshape(bp, tp, C_ROPE))
        h2, comb = _router(x, norm_ffn[layer], sc2, sh2, moe_w_router, moe_b_router, layer, n_p, ts)
        x = _moe_dense(h2, comb, x, g2, moe_w_gu, moe_b_gu, moe_w_down, moe_b_down, layer, n_p, ts)

    return (x[:n_p].reshape(bp, tp, d), x[n_p:].reshape(bs, ts, d),
            jnp.stack(new_a_k, axis=1), jnp.stack(new_a_v, axis=1),
            jnp.stack(new_b_fwd, axis=1), jnp.stack(new_b_bwd, axis=1),
            jnp.stack(new_c_ckv, axis=1), jnp.stack(new_c_krope, axis=1))
```

```python
import functools
import math

import numpy as np
import jax
import jax.numpy as jnp
from jax import lax
from jax.experimental import pallas as pl
from jax.experimental.pallas import tpu as pltpu

F32 = jnp.float32
BF16 = jnp.bfloat16
HIGHEST = lax.Precision.HIGHEST

EPS = 1e-6
NEG_INF = -1e30
ROPE_BASE = 10000.0
GRID_W = 64
N_GROUPS = 8

A_HEADS, A_KV_HEADS, A_GROUP, A_HEAD_DIM, A_WINDOW, A_BLOCK = 8, 2, 4, 64, 128, 128
B_HEADS, B_DK, B_DV, B_CHUNK = 8, 64, 64, 64
C_HEADS, C_NOPE, C_ROPE, C_V, C_Q_LORA, C_KV_LORA = 16, 64, 32, 64, 384, 256
C_QK = C_NOPE + C_ROPE
N_EXPERTS, TOP_K = 32, 4
SWIGLU_LIMIT, SWIGLU_ALPHA = 7.0, 1.702

VMEM_LIMIT = 56 * 1024 * 1024


def _params(sem, vmem=None):
    return pltpu.CompilerParams(dimension_semantics=sem, vmem_limit_bytes=vmem)


def _bdot(a, b):
    return jnp.dot(a.astype(BF16), b.astype(BF16), preferred_element_type=F32)


def _bdot_nt(a, b):
    return lax.dot_general(a.astype(BF16), b.astype(BF16), (((1,), (1,)), ((), ())),
                           preferred_element_type=F32)


def _bdot_tn(a, b):
    return lax.dot_general(a.astype(BF16), b.astype(BF16), (((0,), (0,)), ((), ())),
                           preferred_element_type=F32)


def _hdot(a, b):
    return jnp.dot(a, b, preferred_element_type=F32, precision=HIGHEST)


def _hdot_nt(a, b):
    return lax.dot_general(a, b, (((1,), (1,)), ((), ())), preferred_element_type=F32,
                           precision=HIGHEST)


def _rms(x, gain):
    return x * lax.rsqrt(jnp.mean(x * x, axis=-1, keepdims=True) + EPS) * gain


def _silu(x):
    return x * jax.nn.sigmoid(x)


def _group_of_tile(tm, n_p, t_s):
    def group(i):
        r = i * tm
        return jnp.where(r < n_p, 0, 1 + (r - n_p) // t_s)
    return group


def _adaln_kernel(cond_ref, w_ref, b_ref, o_ref):
    o_ref[...] = _bdot(_silu(cond_ref[...]), w_ref[...]) + b_ref[...]


def _adaln(cond, w_mod, b_mod):
    depth, d, _ = w_mod.shape
    return pl.pallas_call(
        _adaln_kernel,
        grid=(depth, 6),
        in_specs=[pl.BlockSpec((N_GROUPS, d), lambda l, j: (0, 0)),
                  pl.BlockSpec((None, d, d), lambda l, j: (l, 0, j)),
                  pl.BlockSpec((None, 1, d), lambda l, j: (l, 0, j))],
        out_specs=pl.BlockSpec((None, None, N_GROUPS, d), lambda l, j: (l, j, 0, 0)),
        out_shape=jax.ShapeDtypeStruct((depth, 6, N_GROUPS, d), F32),
        compiler_params=_params(("arbitrary", "arbitrary")),
        name="adaln",
    )(cond, w_mod, b_mod.reshape(depth, 1, 6 * d))


def _modulate_kernel(x_ref, g_ref, sc_ref, sh_ref, o_ref):
    y = _rms(x_ref[...], g_ref[...])
    o_ref[...] = (y * (1 + sc_ref[...]) + sh_ref[...]).astype(o_ref.dtype)


def _modulate(x, gain, scale, shift, n_p, t_s, tm=512):
    n, d = x.shape
    group = _group_of_tile(tm, n_p, t_s)
    return pl.pallas_call(
        _modulate_kernel,
        grid=(n // tm,),
        in_specs=[pl.BlockSpec((tm, d), lambda i: (i, 0)),
                  pl.BlockSpec((1, d), lambda i: (0, 0)),
                  pl.BlockSpec((None, 1, d), lambda i: (group(i), 0, 0)),
                  pl.BlockSpec((None, 1, d), lambda i: (group(i), 0, 0))],
        out_specs=pl.BlockSpec((tm, d), lambda i: (i, 0)),
        out_shape=jax.ShapeDtypeStruct((n, d), BF16),
        compiler_params=_params(("arbitrary",)),
        name="modulate",
    )(x, gain.reshape(1, d), scale.reshape(N_GROUPS, 1, d), shift.reshape(N_GROUPS, 1, d))


def _mm_kernel(x_ref, w_ref, o_ref, wb_ref):
    @pl.when(pl.program_id(1) == 0)
    def _():
        wb_ref[...] = w_ref[...].astype(BF16)
    o_ref[...] = jnp.dot(x_ref[...], wb_ref[...], preferred_element_type=F32).astype(o_ref.dtype)


def _matmul(x, w3, layer, col0_blocks, tn, n_out, out_dtype=F32, tm=512, name="matmul"):
    n, k = x.shape
    return pl.pallas_call(
        _mm_kernel,
        grid=(n_out // tn, n // tm),
        in_specs=[pl.BlockSpec((tm, k), lambda j, i: (i, 0)),
                  pl.BlockSpec((None, k, tn), lambda j, i: (layer, 0, col0_blocks + j))],
        out_specs=pl.BlockSpec((tm, tn), lambda j, i: (i, j)),
        out_shape=jax.ShapeDtypeStruct((n, n_out), out_dtype),
        scratch_shapes=[pltpu.VMEM((k, tn), BF16)],
        compiler_params=_params(("arbitrary", "arbitrary"), VMEM_LIMIT),
        name=name,
    )(x, w3)


def _mm_res_kernel(x_ref, w_ref, res_ref, gate_ref, o_ref, wb_ref):
    @pl.when(pl.program_id(0) == 0)
    def _():
        wb_ref[...] = w_ref[...].astype(BF16)
    y = jnp.dot(x_ref[...], wb_ref[...], preferred_element_type=F32)
    o_ref[...] = res_ref[...] + gate_ref[...] * y


def _matmul_residual(mix, w3, layer, res, gate, n_p, t_s, tm=512):
    n, k = mix.shape
    d = res.shape[1]
    group = _group_of_tile(tm, n_p, t_s)
    return pl.pallas_call(
        _mm_res_kernel,
        grid=(n // tm,),
        in_specs=[pl.BlockSpec((tm, k), lambda i: (i, 0)),
                  pl.BlockSpec((None, k, d), lambda i: (layer, 0, 0)),
                  pl.BlockSpec((tm, d), lambda i: (i, 0)),
                  pl.BlockSpec((None, 1, d), lambda i: (group(i), 0, 0))],
        out_specs=pl.BlockSpec((tm, d), lambda i: (i, 0)),
        out_shape=jax.ShapeDtypeStruct((n, d), F32),
        scratch_shapes=[pltpu.VMEM((k, d), BF16)],
        compiler_params=_params(("arbitrary",), VMEM_LIMIT),
        name="out_proj_residual",
    )(mix, w3, res, gate.reshape(N_GROUPS, 1, d))


def _rope_tables(t_len, d):
    half, quarter = d // 2, d // 4
    pos = np.arange(t_len)
    row, col = pos // GRID_W, pos % GRID_W
    inv = ROPE_BASE ** (-np.arange(quarter, dtype=np.float64) / quarter)
    ang_r = row[:, None] * inv[None, :]
    ang_c = col[:, None] * inv[None, :]
    cos = np.concatenate([np.cos(ang_r), np.cos(ang_r), np.cos(ang_c), np.cos(ang_c)], axis=1)
    sin = np.concatenate([-np.sin(ang_r), np.sin(ang_r), -np.sin(ang_c), np.sin(ang_c)], axis=1)
    return jnp.asarray(cos, F32), jnp.asarray(sin, F32)


def _swap_pairs(x):
    q = x.shape[-1] // 4
    return jnp.concatenate([x[:, q:2 * q], x[:, :q], x[:, 3 * q:], x[:, 2 * q:3 * q]], axis=-1)


def _rope(x, cos, sin):
    return x * cos + _swap_pairs(x) * sin


def _softmax_attend(parts, sink):
    m = parts[0][0].max(axis=-1, keepdims=True)
    for s, _ in parts[1:]:
        m = jnp.maximum(m, s.max(axis=-1, keepdims=True))
    if sink is not None:
        m = jnp.maximum(m, sink)
    den = jnp.exp(sink - m) if sink is not None else 0.0
    acc = None
    for s, v in parts:
        p = jnp.exp(s - m)
        den = den + p.sum(axis=-1, keepdims=True)
        o = _bdot(p, v)
        acc = o if acc is None else acc + o
    return acc / den


def _attn_a_prompt_kernel(sink_ref, q_ref, kv_ref, gq_ref, gk_ref, o_ref, kn_ref):
    scale = A_HEAD_DIM ** -0.5
    q = q_ref[...]
    kv = kv_ref[...]
    outs = []
    kns = []
    for hk in range(A_KV_HEADS):
        k = _rms(kv[:, hk * A_HEAD_DIM:(hk + 1) * A_HEAD_DIM], gk_ref[...])
        v = kv[:, (A_KV_HEADS + hk) * A_HEAD_DIM:(A_KV_HEADS + hk + 1) * A_HEAD_DIM]
        kns.append(k)
        for g in range(A_GROUP):
            h = hk * A_GROUP + g
            qh = _rms(q[:, h * A_HEAD_DIM:(h + 1) * A_HEAD_DIM], gq_ref[...])
            s = _bdot_nt(qh, k) * scale
            outs.append(_softmax_attend([(s, v)], sink_ref[h]))
    o_ref[...] = jnp.concatenate(outs, axis=-1).astype(o_ref.dtype)
    kn_ref[...] = jnp.concatenate(kns, axis=-1)


def _attn_a_prompt(proj, n_seq, t, sink, gq, gk):
    qw = A_HEADS * A_HEAD_DIM
    kvw = 2 * A_KV_HEADS * A_HEAD_DIM
    return pl.pallas_call(
        _attn_a_prompt_kernel,
        grid=(n_seq,),
        in_specs=[pl.BlockSpec(memory_space=pltpu.SMEM),
                  pl.BlockSpec((t, qw), lambda b: (b, 0)),
                  pl.BlockSpec((t, kvw), lambda b: (b, qw // kvw)),
                  pl.BlockSpec((1, A_HEAD_DIM), lambda b: (0, 0)),
                  pl.BlockSpec((1, A_HEAD_DIM), lambda b: (0, 0))],
        out_specs=[pl.BlockSpec((t, qw), lambda b: (b, 0)),
                   pl.BlockSpec((t, A_KV_HEADS * A_HEAD_DIM), lambda b: (b, 0))],
        out_shape=[jax.ShapeDtypeStruct((n_seq * t, qw), BF16),
                   jax.ShapeDtypeStruct((n_seq * t, A_KV_HEADS * A_HEAD_DIM), F32)],
        compiler_params=_params(("arbitrary",)),
        name="attn_a_context",
    )(sink, proj, proj, gq.reshape(1, -1), gk.reshape(1, -1))


def _attn_a_sample_kernel(sink_ref, q_ref, kv_ref, kc_ref, vc_ref, gq_ref, gk_ref, cos_ref, sin_ref,
                          o_ref, *, t):
    scale = A_HEAD_DIM ** -0.5
    i = pl.program_id(1)
    win = 3 * A_BLOCK
    q0 = pl.multiple_of(i * A_BLOCK, A_BLOCK)
    k0 = pl.multiple_of(jnp.clip((i - 1) * A_BLOCK, 0, t - win), A_BLOCK)
    q = q_ref[...]
    kv = kv_ref[pl.ds(k0, win), :]
    cq, sq = cos_ref[pl.ds(q0, A_BLOCK), :], sin_ref[pl.ds(q0, A_BLOCK), :]
    ck, sk = cos_ref[pl.ds(k0, win), :], sin_ref[pl.ds(k0, win), :]
    qpos = q0 + lax.broadcasted_iota(jnp.int32, (A_BLOCK, win), 0)
    kpos = k0 + lax.broadcasted_iota(jnp.int32, (A_BLOCK, win), 1)
    mask = jnp.abs(qpos - kpos) <= A_WINDOW
    outs = []
    for hk in range(A_KV_HEADS):
        k = _rope(_rms(kv[:, hk * A_HEAD_DIM:(hk + 1) * A_HEAD_DIM], gk_ref[...]), ck, sk)
        v = kv[:, (A_KV_HEADS + hk) * A_HEAD_DIM:(A_KV_HEADS + hk + 1) * A_HEAD_DIM]
        kc = kc_ref[hk]
        vc = vc_ref[hk]
        for g in range(A_GROUP):
            h = hk * A_GROUP + g
            qh = _rope(_rms(q[:, h * A_HEAD_DIM:(h + 1) * A_HEAD_DIM], gq_ref[...]), cq, sq)
            s1 = jnp.where(mask, _bdot_nt(qh, k) * scale, NEG_INF)
            s2 = _bdot_nt(qh, kc) * scale
            outs.append(_softmax_attend([(s1, v), (s2, vc)], sink_ref[h]))
    o_ref[...] = jnp.concatenate(outs, axis=-1).astype(o_ref.dtype)


def _attn_a_sample(proj, row0, n_seq, t, k_ctx, v_ctx, sink, gq, gk):
    qw = A_HEADS * A_HEAD_DIM
    kvw = 2 * A_KV_HEADS * A_HEAD_DIM
    nqb = t // A_BLOCK
    cos, sin = _rope_tables(t, A_HEAD_DIM)
    past = k_ctx.shape[2]
    return pl.pallas_call(
        functools.partial(_attn_a_sample_kernel, t=t),
        grid=(n_seq, nqb),
        in_specs=[pl.BlockSpec(memory_space=pltpu.SMEM),
                  pl.BlockSpec((A_BLOCK, qw), lambda b, i: (row0 // A_BLOCK + b * nqb + i, 0)),
                  pl.BlockSpec((t, kvw), lambda b, i: (row0 // t + b, qw // kvw)),
                  pl.BlockSpec((None, A_KV_HEADS, past, A_HEAD_DIM), lambda b, i: (b, 0, 0, 0)),
                  pl.BlockSpec((None, A_KV_HEADS, past, A_HEAD_DIM), lambda b, i: (b, 0, 0, 0)),
                  pl.BlockSpec((1, A_HEAD_DIM), lambda b, i: (0, 0)),
                  pl.BlockSpec((1, A_HEAD_DIM), lambda b, i: (0, 0)),
                  pl.BlockSpec((t, A_HEAD_DIM), lambda b, i: (0, 0)),
                  pl.BlockSpec((t, A_HEAD_DIM), lambda b, i: (0, 0))],
        out_specs=pl.BlockSpec((A_BLOCK, qw), lambda b, i: (b * nqb + i, 0)),
        out_shape=jax.ShapeDtypeStruct((n_seq * t, qw), BF16),
        compiler_params=_params(("arbitrary", "arbitrary")),
        name="attn_a_latent",
    )(sink, proj, proj, k_ctx, v_ctx, gq.reshape(1, -1), gk.reshape(1, -1), cos, sin)


def _per_head_lanes(x, fn):
    lane = lax.broadcasted_iota(jnp.int32, x.shape, 1)
    lo = fn(x[:, :B_DK])
    hi = fn(x[:, B_DK:])
    return jnp.where(lane < B_DK, lo, hi)


def _conv_silu(x, w):
    t = x.shape[0]
    row = lax.broadcasted_iota(jnp.int32, x.shape, 0)
    prev = jnp.where(row == 0, 0.0, pltpu.roll(x, 1, 0))
    nxt = jnp.where(row == t - 1, 0.0, pltpu.roll(x, t - 1, 0))
    return _silu(prev * w[0:1, :] + x * w[1:2, :] + nxt * w[2:3, :])


def _delta_prepare(qh, kh, vh, g_col, g_row, beta, tri, tri_t, strict, bdiag):
    c = qh.shape[0]
    gc_col = jnp.sum(jnp.where(tri, g_row, 0.0), axis=1, keepdims=True)
    gc_row = jnp.sum(jnp.where(tri_t, g_col, 0.0), axis=0, keepdims=True)
    g_tot = jnp.sum(g_row, axis=1, keepdims=True)
    decay = jnp.exp(jnp.where(tri, gc_col - gc_row, NEG_INF))
    kb = kh * beta
    vb = vh * beta
    qk = _bdot_nt(jnp.concatenate([kb, qh], axis=0), kh)
    m = qk[:c] * jnp.where(strict, decay, 0.0)
    aqk = qk[c:] * decay
    dg = jnp.where(bdiag, m, 0.0)
    off = m - dg
    n1 = -dg
    xs = n1
    n2 = _bdot(n1, n1)
    xs = xs + n2 + _bdot(xs, n2)
    n4 = _bdot(n2, n2)
    xs = xs + n4 + _bdot(xs, n4)
    n8 = _bdot(n4, n4)
    xs = xs + n8 + _bdot(xs, n8)
    f = -(off + _bdot(xs, off))
    f2 = _bdot(f, f)
    ys = xs + f + _bdot(f, xs)
    ts = ys + f2 + _bdot(f2, ys)
    egc = jnp.exp(gc_col)
    rhs = jnp.concatenate([vb, kb * egc], axis=-1)
    uw = rhs + _bdot(ts, rhs)
    e_tot = jnp.broadcast_to(jnp.exp(g_tot), qh.shape)
    return uw[:, :B_DV], uw[:, B_DV:], aqk, qh * egc, kh * jnp.exp(g_tot - gc_col), e_tot


def _deltanet_kernel(alog_ref, dtb_ref, q_ref, k_ref, v_ref, z_ref, cwq_ref, cwk_ref, cwv_ref,
                     tail_ref, tailt_ref, onorm_ref, s0f_ref, s0b_ref,
                     o_ref, sf_ref, sb_ref,
                     qc_ref, kc_ref, vc_ref, oacc_ref, u_ref, et_ref, w_ref, aqk_ref, qg_ref, kd_ref,
                     *, n_chunks):
    hp = pl.program_id(1)
    c = B_CHUNK

    def l2n(x):
        ss = _per_head_lanes(x * x, lambda a: jnp.sum(a, axis=-1, keepdims=True))
        return x * lax.rsqrt(ss + EPS)

    qc_ref[...] = l2n(_conv_silu(q_ref[...], cwq_ref[...])) * (B_DK ** -0.5)
    kc_ref[...] = l2n(_conv_silu(k_ref[...], cwk_ref[...]))
    vc_ref[...] = _conv_silu(v_ref[...], cwv_ref[...])
    oacc_ref[...] = jnp.zeros_like(oacc_ref)

    ii = lax.broadcasted_iota(jnp.int32, (c, c), 0)
    jj = lax.broadcasted_iota(jnp.int32, (c, c), 1)
    bdiag = (ii // 16) == (jj // 16)
    lower, upper = ii >= jj, ii <= jj
    slower, supper = ii > jj, ii < jj
    lane32 = lax.broadcasted_iota(jnp.int32, (c, 4 * B_HEADS), 1)

    def gates(chunk, d, head):
        tail = tail_ref[pl.ds(pl.multiple_of(chunk * c, c), c), :]
        ia = 2 * d * B_HEADS + head
        ib = ia + B_HEADS
        a_col = jnp.sum(jnp.where(lane32 == ia, tail, 0.0), axis=1, keepdims=True)
        b_col = jnp.sum(jnp.where(lane32 == ib, tail, 0.0), axis=1, keepdims=True)
        a_row = tailt_ref[chunk, pl.ds(ia, 1), :]
        na = -jnp.exp(alog_ref[d, head])
        bias = dtb_ref[d, head]
        g_col = na * jax.nn.softplus(a_col + bias)
        g_row = na * jax.nn.softplus(a_row + bias)
        return g_col, g_row, jax.nn.sigmoid(b_col)

    def prepare(chunk, carry):
        r0 = pl.multiple_of(chunk * c, c)
        qb, kb_, vb_ = qc_ref[pl.ds(r0, c), :], kc_ref[pl.ds(r0, c), :], vc_ref[pl.ds(r0, c), :]
        for d in range(2):
            tri, tri_t, strict = (lower, upper, slower) if d == 0 else (upper, lower, supper)
            parts = []
            for hh in range(2):
                g_col, g_row, beta = gates(chunk, d, 2 * hp + hh)
                sl = slice(hh * B_DK, (hh + 1) * B_DK)
                parts.append(_delta_prepare(qb[:, sl], kb_[:, sl], vb_[:, sl], g_col, g_row, beta,
                                            tri, tri_t, strict, bdiag))
            for ref, a, b in zip((u_ref, w_ref, aqk_ref, qg_ref, kd_ref, et_ref), parts[0], parts[1]):
                ref[d, pl.ds(r0, c), :] = jnp.concatenate([a, b], axis=-1).astype(ref.dtype)
        return carry

    lax.fori_loop(0, n_chunks, prepare, 0, unroll=2)

    def scan(j, carry):
        new = []
        for d in range(2):
            chunk = j if d == 0 else n_chunks - 1 - j
            r0 = pl.multiple_of(chunk * c, c)
            u, et = u_ref[d, pl.ds(r0, c), :], et_ref[d, pl.ds(r0, c), :]
            w, aqk = w_ref[d, pl.ds(r0, c), :], aqk_ref[d, pl.ds(r0, c), :]
            qg, kd = qg_ref[d, pl.ds(r0, c), :], kd_ref[d, pl.ds(r0, c), :]
            outs = []
            for hh in range(2):
                sl = slice(hh * B_DK, (hh + 1) * B_DK)
                s = carry[2 * d + hh]
                sb = s.astype(BF16)
                delta = u[:, sl] - _bdot(w[:, sl], sb)
                outs.append(_bdot(qg[:, sl], sb) + _bdot(aqk[:, sl], delta))
                new.append(s * et[:, sl] + _bdot_tn(kd[:, sl], delta))
            oacc_ref[pl.ds(r0, c), :] += jnp.concatenate(outs, axis=-1)
        return tuple(new)

    init = (s0f_ref[0], s0f_ref[1], s0b_ref[0], s0b_ref[1])
    fin = lax.fori_loop(0, n_chunks, scan, init)
    sf_ref[0], sf_ref[1], sb_ref[0], sb_ref[1] = fin[0], fin[1], fin[2], fin[3]

    o = oacc_ref[...]
    ms = _per_head_lanes(o * o, lambda a: jnp.mean(a, axis=-1, keepdims=True))
    o_ref[...] = (o * lax.rsqrt(ms + EPS) * onorm_ref[...] * _silu(z_ref[...])).astype(o_ref.dtype)


def _deltanet(proj, tail, row0, n_seq, t, conv_w, a_log, dt_bias, o_norm, s0_f, s0_b):
    c = B_CHUNK
    n_chunks = t // c
    lw = 2 * B_DK
    col_q = (A_HEADS + 2 * A_KV_HEADS) * A_HEAD_DIM // lw
    nhp = B_HEADS // 2
    rows = tail[row0:row0 + n_seq * t]
    tail_t = rows.reshape(n_seq, n_chunks, c, 4 * B_HEADS).transpose(0, 1, 3, 2)
    onorm2 = jnp.concatenate([o_norm, o_norm]).reshape(1, lw)
    b0 = row0 // t
    seq_blk = lambda off: pl.BlockSpec((t, lw), lambda b, h: (b0 + b, col_q + off + h))
    cw_blk = lambda off: pl.BlockSpec((3, lw), lambda b, h: (0, off + h))
    st_blk = pl.BlockSpec((None, 2, B_DK, B_DV), lambda b, h: (b, h, 0, 0))
    return pl.pallas_call(
        functools.partial(_deltanet_kernel, n_chunks=n_chunks),
        grid=(n_seq, nhp),
        in_specs=[pl.BlockSpec(memory_space=pltpu.SMEM), pl.BlockSpec(memory_space=pltpu.SMEM),
                  seq_blk(0), seq_blk(nhp), seq_blk(2 * nhp), seq_blk(3 * nhp),
                  cw_blk(0), cw_blk(nhp), cw_blk(2 * nhp),
                  pl.BlockSpec((t, 4 * B_HEADS), lambda b, h: (b, 0)),
                  pl.BlockSpec((None, n_chunks, 4 * B_HEADS, c), lambda b, h: (b, 0, 0, 0)),
                  pl.BlockSpec((1, lw), lambda b, h: (0, 0)),
                  st_blk, st_blk],
        out_specs=[pl.BlockSpec((t, lw), lambda b, h: (b, h)), st_blk, st_blk],
        out_shape=[jax.ShapeDtypeStruct((n_seq * t, B_HEADS * B_DV), BF16),
                   jax.ShapeDtypeStruct((n_seq, B_HEADS, B_DK, B_DV), F32),
                   jax.ShapeDtypeStruct((n_seq, B_HEADS, B_DK, B_DV), F32)],
        scratch_shapes=[pltpu.VMEM((t, lw), F32)] * 4 + [pltpu.VMEM((2, t, lw), F32)] * 2
        + [pltpu.VMEM((2, t, lw), BF16)] * 4,
        compiler_params=_params(("arbitrary", "arbitrary")),
        name="deltanet",
    )(a_log, dt_bias, proj, proj, proj, proj, conv_w, conv_w, conv_w, rows, tail_t, onorm2, s0_f, s0_b)


def _lora_norm_kernel(p_ref, gq_ref, gkv_ref, cq_ref, ckv_ref, ckvb_ref):
    p = p_ref[...]
    cq_ref[...] = _rms(p[:, :C_Q_LORA], gq_ref[...]).astype(cq_ref.dtype)
    ckv = _rms(p[:, C_Q_LORA:C_Q_LORA + C_KV_LORA], gkv_ref[...])
    ckv_ref[...] = ckv
    ckvb_ref[...] = ckv.astype(ckvb_ref.dtype)


def _lora_norm(p1, gq, gkv, tm=512):
    n, w = p1.shape
    return pl.pallas_call(
        _lora_norm_kernel,
        grid=(n // tm,),
        in_specs=[pl.BlockSpec((tm, w), lambda i: (i, 0)),
                  pl.BlockSpec((1, C_Q_LORA), lambda i: (0, 0)),
                  pl.BlockSpec((1, C_KV_LORA), lambda i: (0, 0))],
        out_specs=[pl.BlockSpec((tm, C_Q_LORA), lambda i: (i, 0)),
                   pl.BlockSpec((tm, C_KV_LORA), lambda i: (i, 0)),
                   pl.BlockSpec((tm, C_KV_LORA), lambda i: (i, 0))],
        out_shape=[jax.ShapeDtypeStruct((n, C_Q_LORA), BF16),
                   jax.ShapeDtypeStruct((n, C_KV_LORA), F32),
                   jax.ShapeDtypeStruct((n, C_KV_LORA), BF16)],
        compiler_params=_params(("arbitrary",)),
        name="lora_norm",
    )(p1, gq.reshape(1, -1), gkv.reshape(1, -1))


def _mla_head_q(q, h, gq):
    qn, qr = q[:, h * C_QK:h * C_QK + C_NOPE], q[:, h * C_QK + C_NOPE:(h + 1) * C_QK]
    rn = lax.rsqrt((jnp.sum(qn * qn, axis=-1, keepdims=True)
                    + jnp.sum(qr * qr, axis=-1, keepdims=True)) / C_QK + EPS)
    return qn * rn * gq[:, :C_NOPE], qr * rn * gq[:, C_NOPE:]


def _mla_head_k(kv, kr, kr_ss, h, gk):
    kn = kv[:, h * (C_NOPE + C_V):h * (C_NOPE + C_V) + C_NOPE]
    v = kv[:, h * (C_NOPE + C_V) + C_NOPE:(h + 1) * (C_NOPE + C_V)]
    rn = lax.rsqrt((jnp.sum(kn * kn, axis=-1, keepdims=True) + kr_ss) / C_QK + EPS)
    return kn * rn * gk[:, :C_NOPE], kr * rn, v


def _attn_c_prompt_kernel(q_ref, kv_ref, p_ref, gq_ref, gk_ref, o_ref):
    scale = C_QK ** -0.5
    q = q_ref[...]
    kv = kv_ref[...]
    gq, gk = gq_ref[...], gk_ref[...]
    kr_raw = p_ref[...][:, C_Q_LORA + C_KV_LORA:]
    kr_ss = jnp.sum(kr_raw * kr_raw, axis=-1, keepdims=True)
    kr_g = kr_raw * gk[:, C_NOPE:]
    outs = []
    for h in range(C_HEADS):
        qn, qr = _mla_head_q(q, h, gq)
        kn, kr, v = _mla_head_k(kv, kr_g, kr_ss, h, gk)
        s = (_bdot_nt(qn, kn) + _bdot_nt(qr, kr)) * scale
        outs.append(_softmax_attend([(s, v)], None))
    o_ref[...] = jnp.concatenate(outs, axis=-1).astype(o_ref.dtype)


def _attn_c_prompt(q, kv, p1, n_seq, t, gq, gk):
    return pl.pallas_call(
        _attn_c_prompt_kernel,
        grid=(n_seq,),
        in_specs=[pl.BlockSpec((t, q.shape[1]), lambda b: (b, 0)),
                  pl.BlockSpec((t, kv.shape[1]), lambda b: (b, 0)),
                  pl.BlockSpec((t, p1.shape[1]), lambda b: (b, 0)),
                  pl.BlockSpec((1, C_QK), lambda b: (0, 0)),
                  pl.BlockSpec((1, C_QK), lambda b: (0, 0))],
        out_specs=pl.BlockSpec((t, C_HEADS * C_V), lambda b: (b, 0)),
        out_shape=jax.ShapeDtypeStruct((n_seq * t, C_HEADS * C_V), BF16),
        compiler_params=_params(("arbitrary",), VMEM_LIMIT),
        name="attn_c_context",
    )(q, kv, p1, gq.reshape(1, -1), gk.reshape(1, -1))


def _attn_c_sample_kernel(q_ref, kv_ref, p_ref, kvc_ref, krc_ref, gq_ref, gk_ref, cos_ref, sin_ref,
                          o_ref, *, tq):
    scale = C_QK ** -0.5
    i = pl.program_id(1)
    q0 = pl.multiple_of(i * tq, tq)
    q = q_ref[...]
    kv = kv_ref[...]
    kvc = kvc_ref[...]
    gq, gk = gq_ref[...], gk_ref[...]
    cos, sin = cos_ref[...], sin_ref[...]
    cq, sq = cos_ref[pl.ds(q0, tq), :], sin_ref[pl.ds(q0, tq), :]
    kr_raw = p_ref[...][:, C_Q_LORA + C_KV_LORA:]
    kr_ss = jnp.sum(kr_raw * kr_raw, axis=-1, keepdims=True)
    kr_g = _rope(kr_raw * gk[:, C_NOPE:], cos, sin)
    krc_raw = krc_ref[...]
    krc_ss = jnp.sum(krc_raw * krc_raw, axis=-1, keepdims=True)
    krc_g = krc_raw * gk[:, C_NOPE:]
    outs = []
    for h in range(C_HEADS):
        qn, qr = _mla_head_q(q, h, gq)
        qr = _rope(qr, cq, sq)
        kn, kr, v = _mla_head_k(kv, kr_g, kr_ss, h, gk)
        knc, krc, vc = _mla_head_k(kvc, krc_g, krc_ss, h, gk)
        s1 = (_bdot_nt(qn, kn) + _bdot_nt(qr, kr)) * scale
        s2 = (_bdot_nt(qn, knc) + _bdot_nt(qr, krc)) * scale
        outs.append(_softmax_attend([(s1, v), (s2, vc)], None))
    o_ref[...] = jnp.concatenate(outs, axis=-1).astype(o_ref.dtype)


def _attn_c_sample(q, kv, p1, row0, n_seq, t, kr_ctx, gq, gk, tq=256):
    n = p1.shape[0]
    past = kr_ctx.shape[1]
    nq = t // tq
    cos, sin = _rope_tables(t, C_ROPE)
    return pl.pallas_call(
        functools.partial(_attn_c_sample_kernel, tq=tq),
        grid=(n_seq, nq),
        in_specs=[pl.BlockSpec((tq, q.shape[1]), lambda b, i: (row0 // tq + b * nq + i, 0)),
                  pl.BlockSpec((t, kv.shape[1]), lambda b, i: (row0 // t + b, 0)),
                  pl.BlockSpec((t, p1.shape[1]), lambda b, i: (row0 // t + b, 0)),
                  pl.BlockSpec((past, kv.shape[1]), lambda b, i: (n // past + b, 0)),
                  pl.BlockSpec((None, past, C_ROPE), lambda b, i: (b, 0, 0)),
                  pl.BlockSpec((1, C_QK), lambda b, i: (0, 0)),
                  pl.BlockSpec((1, C_QK), lambda b, i: (0, 0)),
                  pl.BlockSpec((t, C_ROPE), lambda b, i: (0, 0)),
                  pl.BlockSpec((t, C_ROPE), lambda b, i: (0, 0))],
        out_specs=pl.BlockSpec((tq, C_HEADS * C_V), lambda b, i: (b * nq + i, 0)),
        out_shape=jax.ShapeDtypeStruct((n_seq * t, C_HEADS * C_V), BF16),
        compiler_params=_params(("arbitrary", "arbitrary"), VMEM_LIMIT),
        name="attn_c_latent",
    )(q, kv, p1, kv, kr_ctx, gq.reshape(1, -1), gk.reshape(1, -1), cos, sin)


ROUTE_TILE = 256
META_LANES = 128


def _router_kernel(x_ref, g_ref, sc_ref, sh_ref, wr_ref, br_ref, h_ref, meta_ref, cnt_ref, run_ref):
    @pl.when(pl.program_id(0) == 0)
    def _():
        run_ref[...] = jnp.zeros_like(run_ref)

    h = _rms(x_ref[...], g_ref[...]) * (1 + sc_ref[...]) + sh_ref[...]
    h_ref[...] = h
    tm = h.shape[0]
    logits = _hdot(h, wr_ref[...]) + br_ref[...]
    lane = lax.broadcasted_iota(jnp.int32, logits.shape, 1)
    work = logits
    picks, tops, ids = [], [], []
    for _ in range(TOP_K):
        m = work.max(axis=-1, keepdims=True)
        first = jnp.min(jnp.where(work == m, lane, N_EXPERTS), axis=-1, keepdims=True)
        pick = lane == first
        picks.append(pick)
        tops.append(m)
        ids.append(first)
        work = jnp.where(pick, -jnp.inf, work)
    sel = sum(p.astype(F32) for p in picks)
    earlier = (lax.broadcasted_iota(jnp.int32, (tm, tm), 0)
               > lax.broadcasted_iota(jnp.int32, (tm, tm), 1)).astype(BF16)
    before = run_ref[...] + jnp.dot(earlier, sel.astype(BF16), preferred_element_type=F32)
    ws = [jnp.exp(t - tops[0]) for t in tops]
    den = sum(ws)
    mlane = lax.broadcasted_iota(jnp.int32, (tm, META_LANES), 1)
    meta = jnp.zeros((tm, META_LANES), F32)
    for k in range(TOP_K):
        rank = jnp.sum(jnp.where(picks[k], before, 0.0), axis=-1, keepdims=True)
        meta = jnp.where(mlane == k, ids[k].astype(F32), meta)
        meta = jnp.where(mlane == TOP_K + k, rank, meta)
        meta = jnp.where(mlane == 2 * TOP_K + k, ws[k] / den, meta)
    meta_ref[...] = meta
    run_ref[...] += jnp.sum(sel, axis=0, keepdims=True)
    cnt_ref[...] = run_ref[...]


def _router(x, gain, scale, shift, w_router, b_router, layer, n_p, t_s, tm=512):
    n, d = x.shape
    e = w_router.shape[-1]
    group = _group_of_tile(tm, n_p, t_s)
    return pl.pallas_call(
        _router_kernel,
        grid=(n // tm,),
        in_specs=[pl.BlockSpec((tm, d), lambda i: (i, 0)),
                  pl.BlockSpec((1, d), lambda i: (0, 0)),
                  pl.BlockSpec((None, 1, d), lambda i: (group(i), 0, 0)),
                  pl.BlockSpec((None, 1, d), lambda i: (group(i), 0, 0)),
                  pl.BlockSpec((None, d, e), lambda i: (layer, 0, 0)),
                  pl.BlockSpec((None, 1, e), lambda i: (layer, 0, 0))],
        out_specs=[pl.BlockSpec((tm, d), lambda i: (i, 0)),
                   pl.BlockSpec((tm, META_LANES), lambda i: (i, 0)),
                   pl.BlockSpec((1, e), lambda i: (0, 0))],
        out_shape=[jax.ShapeDtypeStruct((n, d), F32),
                   jax.ShapeDtypeStruct((n, META_LANES), F32),
                   jax.ShapeDtypeStruct((1, e), F32)],
        scratch_shapes=[pltpu.VMEM((1, e), F32)],
        compiler_params=_params(("arbitrary",)),
        name="router",
    )(x, gain.reshape(1, d), scale.reshape(N_GROUPS, 1, d), shift.reshape(N_GROUPS, 1, d),
      w_router, b_router.reshape(-1, 1, e))


def _route_plan(meta, cnt, n_tiles):
    ids = meta[:, :TOP_K].astype(jnp.int32)
    ranks = meta[:, TOP_K:2 * TOP_K].astype(jnp.int32)
    gates = meta[:, 2 * TOP_K:3 * TOP_K]
    counts = cnt[0].astype(jnp.int32)
    padded = (counts + ROUTE_TILE - 1) // ROUTE_TILE * ROUTE_TILE
    ends = jnp.cumsum(padded)
    offs = ends - padded
    pos = offs[ids] + ranks
    tile_start = jnp.arange(n_tiles, dtype=jnp.int32) * ROUTE_TILE
    n_valid = ends[-1] // ROUTE_TILE
    te = jnp.minimum(jnp.sum(ends[None, :] <= tile_start[:, None], axis=1), N_EXPERTS - 1)
    te = jnp.where(tile_start < ends[-1], te, te[jnp.maximum(n_valid - 1, 0)]).astype(jnp.int32)
    rows = jnp.clip((offs + counts)[te] - tile_start, 0, ROUTE_TILE).astype(jnp.int32)
    ragged = jnp.where(counts % ROUTE_TILE != 0, ends // ROUTE_TILE - 1, -1)
    tail = tile_start[n_tiles - N_EXPERTS:] // ROUTE_TILE
    fill = jnp.concatenate([ragged, jnp.where(tail >= n_valid, tail, -1)]).astype(jnp.int32)
    return pos, gates, te, rows, n_valid.reshape(1).astype(jnp.int32), fill


def _dispatch_kernel(fill_ref, pos_ref, h_ref, xs_ref, zero_ref, sem, zsem):
    tm = h_ref.shape[0]

    @pl.when(pl.program_id(0) == 0)
    def _():
        zero_ref[...] = jnp.zeros_like(zero_ref)

        def fill_copy(j):
            row0 = pl.multiple_of(fill_ref[j] * ROUTE_TILE, ROUTE_TILE)
            return pltpu.make_async_copy(zero_ref, xs_ref.at[pl.ds(row0, ROUTE_TILE), :], zsem)

        def start(j, carry):
            @pl.when(fill_ref[j] >= 0)
            def _():
                fill_copy(j).start()
            return carry

        def wait(j, carry):
            @pl.when(fill_ref[j] >= 0)
            def _():
                fill_copy(j).wait()
            return carry

        lax.fori_loop(0, fill_ref.shape[0], start, 0)
        lax.fori_loop(0, fill_ref.shape[0], wait, 0)

    def body(r, carry):
        for k in range(TOP_K):
            p = pos_ref[0, r * TOP_K + k]
            pltpu.make_async_copy(h_ref.at[pl.ds(r, 1), :], xs_ref.at[pl.ds(p, 1), :], sem).start()
        return carry

    lax.fori_loop(0, tm, body, 0, unroll=4)
    for _ in range(TOP_K):
        pltpu.make_async_copy(h_ref, h_ref, sem).wait()


def _dispatch(h, pos, fill, n_slots, tm=512):
    n, d = h.shape
    grid_spec = pltpu.PrefetchScalarGridSpec(
        num_scalar_prefetch=1,
        grid=(n // tm,),
        in_specs=[pl.BlockSpec((None, 1, tm * TOP_K), lambda i, fl: (i, 0, 0), memory_space=pltpu.SMEM),
                  pl.BlockSpec((tm, d), lambda i, fl: (i, 0))],
        out_specs=pl.BlockSpec(memory_space=pl.ANY),
        scratch_shapes=[pltpu.VMEM((ROUTE_TILE, d), h.dtype), pltpu.SemaphoreType.DMA(()),
                        pltpu.SemaphoreType.DMA(())])
    return pl.pallas_call(
        _dispatch_kernel,
        grid_spec=grid_spec,
        out_shape=jax.ShapeDtypeStruct((n_slots, d), h.dtype),
        compiler_params=_params(("arbitrary",)),
        name="moe_dispatch",
    )(fill, pos.reshape(n // tm, 1, tm * TOP_K), h)


def _experts_kernel(te_ref, rows_ref, nv_ref, x_ref, wgu_ref, bgu_ref, wd_ref, bd_ref, y_ref,
                    wgub_ref, wdb_ref, *, d_ff):
    i = pl.program_id(0)
    valid = i < nv_ref[0]
    fresh = jnp.logical_or(i == 0, te_ref[i] != te_ref[jnp.maximum(i - 1, 0)])

    @pl.when(jnp.logical_and(valid, fresh))
    def _():
        wgub_ref[...] = wgu_ref[...].astype(BF16)
        wdb_ref[...] = wd_ref[...].astype(BF16)

    @pl.when(valid)
    def _():
        row = lax.broadcasted_iota(jnp.int32, x_ref.shape, 0)
        x = jnp.where(row < rows_ref[i], x_ref[...], 0.0).astype(BF16)
        gu = jnp.dot(x, wgub_ref[...], preferred_element_type=F32) + bgu_ref[...]
        gate = jnp.minimum(gu[:, :d_ff], SWIGLU_LIMIT)
        up = jnp.clip(gu[:, d_ff:], -SWIGLU_LIMIT, SWIGLU_LIMIT)
        act = (up + 1) * gate * jax.nn.sigmoid(SWIGLU_ALPHA * gate)
        y_ref[...] = jnp.dot(act.astype(BF16), wdb_ref[...], preferred_element_type=F32) + bd_ref[...]

    @pl.when(jnp.logical_not(valid))
    def _():
        y_ref[...] = jnp.zeros_like(y_ref)


def _experts(xs, te, rows, n_valid, w_gu, b_gu, w_down, b_down, layer):
    n_slots, d = xs.shape
    _, e, _, two_ff = w_gu.shape
    last = lambda i, nv: jnp.minimum(i, nv[0] - 1)
    grid_spec = pltpu.PrefetchScalarGridSpec(
        num_scalar_prefetch=3,
        grid=(n_slots // ROUTE_TILE,),
        in_specs=[pl.BlockSpec((ROUTE_TILE, d), lambda i, te, rw, nv: (last(i, nv), 0)),
                  pl.BlockSpec((None, None, d, two_ff), lambda i, te, rw, nv: (layer, te[i], 0, 0)),
                  pl.BlockSpec((None, None, 1, two_ff), lambda i, te, rw, nv: (layer, te[i], 0, 0)),
                  pl.BlockSpec((None, None, two_ff // 2, d), lambda i, te, rw, nv: (layer, te[i], 0, 0)),
                  pl.BlockSpec((None, None, 1, d), lambda i, te, rw, nv: (layer, te[i], 0, 0))],
        out_specs=pl.BlockSpec((ROUTE_TILE, d), lambda i, te, rw, nv: (i, 0)),
        scratch_shapes=[pltpu.VMEM((d, two_ff), BF16), pltpu.VMEM((two_ff // 2, d), BF16)])
    return pl.pallas_call(
        functools.partial(_experts_kernel, d_ff=two_ff // 2),
        grid_spec=grid_spec,
        out_shape=jax.ShapeDtypeStruct((n_slots, d), F32),
        compiler_params=_params(("arbitrary",), VMEM_LIMIT),
        name="moe_experts",
    )(te, rows, n_valid, xs, w_gu, b_gu.reshape(b_gu.shape[0], e, 1, two_ff), w_down,
      b_down.reshape(b_down.shape[0], e, 1, d))


def _combine_kernel(pos_ref, x_ref, gate_ref, gts_ref, y_ref, o_ref, buf_ref, sem):
    tm = x_ref.shape[0]

    def body(r, carry):
        for k in range(TOP_K):
            p = pos_ref[0, r * TOP_K + k]
            pltpu.make_async_copy(y_ref.at[pl.ds(p, 1), :], buf_ref.at[k, pl.ds(r, 1), :], sem).start()
        return carry

    lax.fori_loop(0, tm, body, 0, unroll=4)
    for k in range(TOP_K):
        pltpu.make_async_copy(buf_ref.at[k], buf_ref.at[k], sem).wait()
    g = gts_ref[...]
    acc = g[:, 0:1] * buf_ref[0]
    for k in range(1, TOP_K):
        acc = acc + g[:, k:k + 1] * buf_ref[k]
    o_ref[...] = x_ref[...] + gate_ref[...] * acc


def _combine(y, pos, gates, x, gate, n_p, t_s, tm=256):
    n, d = x.shape
    group = _group_of_tile(tm, n_p, t_s)
    return pl.pallas_call(
        _combine_kernel,
        grid=(n // tm,),
        in_specs=[pl.BlockSpec((None, 1, tm * TOP_K), lambda i: (i, 0, 0), memory_space=pltpu.SMEM),
                  pl.BlockSpec((tm, d), lambda i: (i, 0)),
                  pl.BlockSpec((None, 1, d), lambda i: (group(i), 0, 0)),
                  pl.BlockSpec((tm, TOP_K), lambda i: (i, 0)),
                  pl.BlockSpec(memory_space=pl.ANY)],
        out_specs=pl.BlockSpec((tm, d), lambda i: (i, 0)),
        out_shape=jax.ShapeDtypeStruct((n, d), F32),
        scratch_shapes=[pltpu.VMEM((TOP_K, tm, d), F32), pltpu.SemaphoreType.DMA(())],
        compiler_params=_params(("arbitrary",)),
        name="moe_combine",
    )(pos.reshape(n // tm, 1, tm * TOP_K), x, gate.reshape(N_GROUPS, 1, d), gates, y)


def _moe(x, gain, scale, shift, gate, w_router, b_router, w_gu, b_gu, w_down, b_down, layer, n_p, t_s):
    n = x.shape[0]
    n_slots = n * TOP_K + N_EXPERTS * ROUTE_TILE
    h, meta, cnt = _router(x, gain, scale, shift, w_router, b_router, layer, n_p, t_s)
    pos, gates, te, rows, n_valid, fill = _route_plan(meta, cnt, n_slots // ROUTE_TILE)
    xs = _dispatch(h, pos, fill, n_slots)
    y = _experts(xs, te, rows, n_valid, w_gu, b_gu, w_down, b_down, layer)
    return _combine(y, pos, gates, x, gate, n_p, t_s)


def kernel(x_prompt, x_sample, c, cache_a_k, cache_a_v, state_b_fwd, state_b_bwd, cache_c_ckv,
           cache_c_krope, c_ctx, w_mod, b_mod, norm_mix, norm_ffn, e_w_in, e_w_out, e_a_qnorm,
           e_a_knorm, e_a_sink, e_b_conv, e_b_alog, e_b_dtbias, e_b_onorm, o_w_in, o_q_lora_norm,
           o_kv_lora_norm, o_w_uq, o_w_ukv, o_qnorm, o_knorm, o_w_out, moe_w_router, moe_b_router,
           moe_w_gu, moe_b_gu, moe_w_down, moe_b_down):
    bp, tp, d = x_prompt.shape
    bs, ts, _ = x_sample.shape
    depth = w_mod.shape[0]
    n_p, n_s = bp * tp, bs * ts
    n = n_p + n_s
    assert bs + 1 <= N_GROUPS and ts % 512 == 0 and n_p % ts == 0

    x = jnp.concatenate([x_prompt.reshape(n_p, d), x_sample.reshape(n_s, d)], axis=0)
    cond = jnp.concatenate([c_ctx[None], c, jnp.zeros((N_GROUPS - 1 - bs, d), F32)], axis=0)
    mod = _adaln(cond, w_mod, b_mod)

    new_a_k, new_a_v, new_b_fwd, new_b_bwd, new_c_ckv, new_c_krope = [], [], [], [], [], []
    for layer in range(depth):
        sh1, sc1, g1, sh2, sc2, g2 = (mod[layer, j] for j in range(6))
        h = _modulate(x, norm_mix[layer], sc1, sh1, n_p, ts)
        i = layer // 2
        if layer % 2 == 0:
            main_w = (A_HEADS + 2 * A_KV_HEADS) * A_HEAD_DIM + 4 * B_HEADS * B_DK
            proj = _matmul(h, e_w_in, i, 0, main_w // 2, main_w, name="even_in_proj")
            tail = _matmul(h, e_w_in[i][None, :, main_w:], 0, 0, 4 * B_HEADS, 4 * B_HEADS,
                           name="even_gate_proj")
            oa_p, kn_p = _attn_a_prompt(proj, bp, tp, e_a_sink[i], e_a_qnorm[i], e_a_knorm[i])
            oa_s = _attn_a_sample(proj, n_p, bs, ts, cache_a_k[:, i], cache_a_v[:, i],
                                  e_a_sink[i], e_a_qnorm[i], e_a_knorm[i])
            zeros = jnp.zeros((bp, B_HEADS, B_DK, B_DV), F32)
            ob_p, s_f, s_b = _deltanet(proj, tail, 0, bp, tp, e_b_conv[i], e_b_alog[i],
                                       e_b_dtbias[i], e_b_onorm[i], zeros, zeros)
            ob_s, _, _ = _deltanet(proj, tail, n_p, bs, ts, e_b_conv[i], e_b_alog[i],
                                   e_b_dtbias[i], e_b_onorm[i], state_b_fwd[:, i], state_b_bwd[:, i])
            mix = jnp.concatenate([jnp.concatenate([oa_p, ob_p], axis=1),
                                   jnp.concatenate([oa_s, ob_s], axis=1)], axis=0)
            x = _matmul_residual(mix, e_w_out, i, x, g1, n_p, ts)
            kw = A_KV_HEADS * A_HEAD_DIM
            new_a_k.append(kn_p.reshape(bp, tp, A_KV_HEADS, A_HEAD_DIM).transpose(0, 2, 1, 3))
            v_p = proj[:n_p, A_HEADS * A_HEAD_DIM + kw:A_HEADS * A_HEAD_DIM + 2 * kw]
            new_a_v.append(v_p.reshape(bp, tp, A_KV_HEADS, A_HEAD_DIM).transpose(0, 2, 1, 3))
            new_b_fwd.append(s_f)
            new_b_bwd.append(s_b)
        else:
            p1 = _matmul(h, o_w_in, i, 0, o_w_in.shape[-1], o_w_in.shape[-1], name="odd_in_proj")
            cq, ckv, ckv_b = _lora_norm(p1, o_q_lora_norm[i], o_kv_lora_norm[i])
            q = _matmul(cq, o_w_uq, i, 0, o_w_uq.shape[-1] // 2, o_w_uq.shape[-1], name="odd_uq")
            ckv_all = jnp.concatenate([ckv_b, cache_c_ckv[:, i].reshape(-1, C_KV_LORA).astype(BF16)], 0)
            kv = _matmul(ckv_all, o_w_ukv, i, 0, o_w_ukv.shape[-1] // 2, o_w_ukv.shape[-1],
                         name="odd_ukv")
            o_p = _attn_c_prompt(q, kv, p1, bp, tp, o_qnorm[i], o_knorm[i])
            o_s = _attn_c_sample(q, kv, p1, n_p, bs, ts, cache_c_krope[:, i], o_qnorm[i], o_knorm[i])
            x = _matmul_residual(jnp.concatenate([o_p, o_s], axis=0), o_w_out, i, x, g1, n_p, ts)
            new_c_ckv.append(ckv[:n_p].reshape(bp, tp, C_KV_LORA))
            new_c_krope.append(p1[:n_p, C_Q_LORA + C_KV_LORA:].reshape(bp, tp, C_ROPE))
        x = _moe(x, norm_ffn[layer], sc2, sh2, g2, moe_w_router, moe_b_router, moe_w_gu, moe_b_gu,
                 moe_w_down, moe_b_down, layer, n_p, ts)

    return (x[:n_p].reshape(bp, tp, d), x[n_p:].reshape(bs, ts, d),
            jnp.stack(new_a_k, axis=1), jnp.stack(new_a_v, axis=1),
            jnp.stack(new_b_fwd, axis=1), jnp.stack(new_b_bwd, axis=1),
            jnp.stack(new_c_ckv, axis=1), jnp.stack(new_c_krope, axis=1))
```

```python
import functools
import math

import numpy as np
import jax
import jax.numpy as jnp
from jax import lax
from jax.experimental import pallas as pl
from jax.experimental.pallas import tpu as pltpu

F32 = jnp.float32
BF16 = jnp.bfloat16
HIGHEST = lax.Precision.HIGHEST

EPS = 1e-6
NEG_INF = -1e30
ROPE_BASE = 10000.0
GRID_W = 64
N_GROUPS = 8

A_HEADS, A_KV_HEADS, A_GROUP, A_HEAD_DIM, A_WINDOW, A_BLOCK = 8, 2, 4, 64, 128, 128
B_HEADS, B_DK, B_DV, B_CHUNK = 8, 64, 64, 64
C_HEADS, C_NOPE, C_ROPE, C_V, C_Q_LORA, C_KV_LORA = 16, 64, 32, 64, 384, 256
C_QK = C_NOPE + C_ROPE
N_EXPERTS, TOP_K = 32, 4
SWIGLU_LIMIT, SWIGLU_ALPHA = 7.0, 1.702

VMEM_LIMIT = 56 * 1024 * 1024


def _params(sem, vmem=None):
    return pltpu.CompilerParams(dimension_semantics=sem, vmem_limit_bytes=vmem)


def _bdot(a, b):
    return jnp.dot(a.astype(BF16), b.astype(BF16), preferred_element_type=F32)


def _bdot_nt(a, b):
    return lax.dot_general(a.astype(BF16), b.astype(BF16), (((1,), (1,)), ((), ())),
                           preferred_element_type=F32)


def _bdot_tn(a, b):
    return lax.dot_general(a.astype(BF16), b.astype(BF16), (((0,), (0,)), ((), ())),
                           preferred_element_type=F32)


def _hdot(a, b):
    return jnp.dot(a, b, preferred_element_type=F32, precision=HIGHEST)


def _hdot_nt(a, b):
    return lax.dot_general(a, b, (((1,), (1,)), ((), ())), preferred_element_type=F32,
                           precision=HIGHEST)


def _rms(x, gain):
    return x * lax.rsqrt(jnp.mean(x * x, axis=-1, keepdims=True) + EPS) * gain


def _silu(x):
    return x * jax.nn.sigmoid(x)


def _group_of_tile(tm, n_p, t_s):
    def group(i):
        r = i * tm
        return jnp.where(r < n_p, 0, 1 + (r - n_p) // t_s)
    return group


def _adaln_kernel(cond_ref, w_ref, b_ref, o_ref):
    o_ref[...] = _bdot(_silu(cond_ref[...]), w_ref[...]) + b_ref[...]


def _adaln(cond, w_mod, b_mod):
    depth, d, _ = w_mod.shape
    return pl.pallas_call(
        _adaln_kernel,
        grid=(depth, 6),
        in_specs=[pl.BlockSpec((N_GROUPS, d), lambda l, j: (0, 0)),
                  pl.BlockSpec((None, d, d), lambda l, j: (l, 0, j)),
                  pl.BlockSpec((None, 1, d), lambda l, j: (l, 0, j))],
        out_specs=pl.BlockSpec((None, None, N_GROUPS, d), lambda l, j: (l, j, 0, 0)),
        out_shape=jax.ShapeDtypeStruct((depth, 6, N_GROUPS, d), F32),
        compiler_params=_params(("arbitrary", "arbitrary")),
        name="adaln",
    )(cond, w_mod, b_mod.reshape(depth, 1, 6 * d))


def _modulate_kernel(x_ref, g_ref, sc_ref, sh_ref, o_ref):
    y = _rms(x_ref[...], g_ref[...])
    o_ref[...] = (y * (1 + sc_ref[...]) + sh_ref[...]).astype(o_ref.dtype)


def _modulate(x, gain, scale, shift, n_p, t_s, tm=512):
    n, d = x.shape
    group = _group_of_tile(tm, n_p, t_s)
    return pl.pallas_call(
        _modulate_kernel,
        grid=(n // tm,),
        in_specs=[pl.BlockSpec((tm, d), lambda i: (i, 0)),
                  pl.BlockSpec((1, d), lambda i: (0, 0)),
                  pl.BlockSpec((None, 1, d), lambda i: (group(i), 0, 0)),
                  pl.BlockSpec((None, 1, d), lambda i: (group(i), 0, 0))],
        out_specs=pl.BlockSpec((tm, d), lambda i: (i, 0)),
        out_shape=jax.ShapeDtypeStruct((n, d), BF16),
        compiler_params=_params(("arbitrary",)),
        name="modulate",
    )(x, gain.reshape(1, d), scale.reshape(N_GROUPS, 1, d), shift.reshape(N_GROUPS, 1, d))


def _mm_kernel(x_ref, w_ref, o_ref, wb_ref):
    @pl.when(pl.program_id(1) == 0)
    def _():
        wb_ref[...] = w_ref[...].astype(BF16)
    o_ref[...] = jnp.dot(x_ref[...], wb_ref[...], preferred_element_type=F32).astype(o_ref.dtype)


def _matmul(x, w3, layer, col0_blocks, tn, n_out, out_dtype=F32, tm=512, name="matmul"):
    n, k = x.shape
    return pl.pallas_call(
        _mm_kernel,
        grid=(n_out // tn, n // tm),
        in_specs=[pl.BlockSpec((tm, k), lambda j, i: (i, 0)),
                  pl.BlockSpec((None, k, tn), lambda j, i: (layer, 0, col0_blocks + j))],
        out_specs=pl.BlockSpec((tm, tn), lambda j, i: (i, j)),
        out_shape=jax.ShapeDtypeStruct((n, n_out), out_dtype),
        scratch_shapes=[pltpu.VMEM((k, tn), BF16)],
        compiler_params=_params(("arbitrary", "arbitrary"), VMEM_LIMIT),
        name=name,
    )(x, w3)


def _mm_res_kernel(x_ref, w_ref, res_ref, gate_ref, o_ref, wb_ref):
    @pl.when(pl.program_id(0) == 0)
    def _():
        wb_ref[...] = w_ref[...].astype(BF16)
    y = jnp.dot(x_ref[...], wb_ref[...], preferred_element_type=F32)
    o_ref[...] = res_ref[...] + gate_ref[...] * y


def _matmul_residual(mix, w3, layer, res, gate, n_p, t_s, tm=512):
    n, k = mix.shape
    d = res.shape[1]
    group = _group_of_tile(tm, n_p, t_s)
    return pl.pallas_call(
        _mm_res_kernel,
        grid=(n // tm,),
        in_specs=[pl.BlockSpec((tm, k), lambda i: (i, 0)),
                  pl.BlockSpec((None, k, d), lambda i: (layer, 0, 0)),
                  pl.BlockSpec((tm, d), lambda i: (i, 0)),
                  pl.BlockSpec((None, 1, d), lambda i: (group(i), 0, 0))],
        out_specs=pl.BlockSpec((tm, d), lambda i: (i, 0)),
        out_shape=jax.ShapeDtypeStruct((n, d), F32),
        scratch_shapes=[pltpu.VMEM((k, d), BF16)],
        compiler_params=_params(("arbitrary",), VMEM_LIMIT),
        name="out_proj_residual",
    )(mix, w3, res, gate.reshape(N_GROUPS, 1, d))


def _rope_tables(t_len, d):
    half, quarter = d // 2, d // 4
    pos = np.arange(t_len)
    row, col = pos // GRID_W, pos % GRID_W
    inv = ROPE_BASE ** (-np.arange(quarter, dtype=np.float64) / quarter)
    ang_r = row[:, None] * inv[None, :]
    ang_c = col[:, None] * inv[None, :]
    cos = np.concatenate([np.cos(ang_r), np.cos(ang_r), np.cos(ang_c), np.cos(ang_c)], axis=1)
    sin = np.concatenate([-np.sin(ang_r), np.sin(ang_r), -np.sin(ang_c), np.sin(ang_c)], axis=1)
    return jnp.asarray(cos, F32), jnp.asarray(sin, F32)


def _swap_pairs(x):
    q = x.shape[-1] // 4
    return jnp.concatenate([x[:, q:2 * q], x[:, :q], x[:, 3 * q:], x[:, 2 * q:3 * q]], axis=-1)


def _rope(x, cos, sin):
    return x * cos + _swap_pairs(x) * sin


def _softmax_attend(parts, sink):
    m = parts[0][0].max(axis=-1, keepdims=True)
    for s, _ in parts[1:]:
        m = jnp.maximum(m, s.max(axis=-1, keepdims=True))
    if sink is not None:
        m = jnp.maximum(m, sink)
    den = jnp.exp(sink - m) if sink is not None else 0.0
    acc = None
    for s, v in parts:
        p = jnp.exp(s - m)
        den = den + p.sum(axis=-1, keepdims=True)
        o = _bdot(p, v)
        acc = o if acc is None else acc + o
    return acc / den


def _attn_a_prompt_kernel(sink_ref, q_ref, kv_ref, gq_ref, gk_ref, o_ref, kn_ref):
    scale = A_HEAD_DIM ** -0.5
    q = q_ref[...]
    kv = kv_ref[...]
    outs = []
    kns = []
    for hk in range(A_KV_HEADS):
        k = _rms(kv[:, hk * A_HEAD_DIM:(hk + 1) * A_HEAD_DIM], gk_ref[...])
        v = kv[:, (A_KV_HEADS + hk) * A_HEAD_DIM:(A_KV_HEADS + hk + 1) * A_HEAD_DIM]
        kns.append(k)
        for g in range(A_GROUP):
            h = hk * A_GROUP + g
            qh = _rms(q[:, h * A_HEAD_DIM:(h + 1) * A_HEAD_DIM], gq_ref[...])
            s = _bdot_nt(qh, k) * scale
            outs.append(_softmax_attend([(s, v)], sink_ref[h]))
    o_ref[...] = jnp.concatenate(outs, axis=-1).astype(o_ref.dtype)
    kn_ref[...] = jnp.concatenate(kns, axis=-1)


def _attn_a_prompt(proj, n_seq, t, sink, gq, gk):
    qw = A_HEADS * A_HEAD_DIM
    kvw = 2 * A_KV_HEADS * A_HEAD_DIM
    return pl.pallas_call(
        _attn_a_prompt_kernel,
        grid=(n_seq,),
        in_specs=[pl.BlockSpec(memory_space=pltpu.SMEM),
                  pl.BlockSpec((t, qw), lambda b: (b, 0)),
                  pl.BlockSpec((t, kvw), lambda b: (b, qw // kvw)),
                  pl.BlockSpec((1, A_HEAD_DIM), lambda b: (0, 0)),
                  pl.BlockSpec((1, A_HEAD_DIM), lambda b: (0, 0))],
        out_specs=[pl.BlockSpec((t, qw), lambda b: (b, 0)),
                   pl.BlockSpec((t, A_KV_HEADS * A_HEAD_DIM), lambda b: (b, 0))],
        out_shape=[jax.ShapeDtypeStruct((n_seq * t, qw), BF16),
                   jax.ShapeDtypeStruct((n_seq * t, A_KV_HEADS * A_HEAD_DIM), F32)],
        compiler_params=_params(("arbitrary",)),
        name="attn_a_context",
    )(sink, proj, proj, gq.reshape(1, -1), gk.reshape(1, -1))


def _attn_a_sample_kernel(sink_ref, q_ref, kv_ref, kc_ref, vc_ref, gq_ref, gk_ref, cos_ref, sin_ref,
                          o_ref, *, t):
    scale = A_HEAD_DIM ** -0.5
    i = pl.program_id(1)
    win = 3 * A_BLOCK
    q0 = pl.multiple_of(i * A_BLOCK, A_BLOCK)
    k0 = pl.multiple_of(jnp.clip((i - 1) * A_BLOCK, 0, t - win), A_BLOCK)
    q = q_ref[...]
    kv = kv_ref[pl.ds(k0, win), :]
    cq, sq = cos_ref[pl.ds(q0, A_BLOCK), :], sin_ref[pl.ds(q0, A_BLOCK), :]
    ck, sk = cos_ref[pl.ds(k0, win), :], sin_ref[pl.ds(k0, win), :]
    qpos = q0 + lax.broadcasted_iota(jnp.int32, (A_BLOCK, win), 0)
    kpos = k0 + lax.broadcasted_iota(jnp.int32, (A_BLOCK, win), 1)
    mask = jnp.abs(qpos - kpos) <= A_WINDOW
    outs = []
    for hk in range(A_KV_HEADS):
        k = _rope(_rms(kv[:, hk * A_HEAD_DIM:(hk + 1) * A_HEAD_DIM], gk_ref[...]), ck, sk)
        v = kv[:, (A_KV_HEADS + hk) * A_HEAD_DIM:(A_KV_HEADS + hk + 1) * A_HEAD_DIM]
        kc = kc_ref[hk]
        vc = vc_ref[hk]
        for g in range(A_GROUP):
            h = hk * A_GROUP + g
            qh = _rope(_rms(q[:, h * A_HEAD_DIM:(h + 1) * A_HEAD_DIM], gq_ref[...]), cq, sq)
            s1 = jnp.where(mask, _bdot_nt(qh, k) * scale, NEG_INF)
            s2 = _bdot_nt(qh, kc) * scale
            outs.append(_softmax_attend([(s1, v), (s2, vc)], sink_ref[h]))
    o_ref[...] = jnp.concatenate(outs, axis=-1).astype(o_ref.dtype)


def _attn_a_sample(proj, row0, n_seq, t, k_ctx, v_ctx, sink, gq, gk):
    qw = A_HEADS * A_HEAD_DIM
    kvw = 2 * A_KV_HEADS * A_HEAD_DIM
    nqb = t // A_BLOCK
    cos, sin = _rope_tables(t, A_HEAD_DIM)
    past = k_ctx.shape[2]
    return pl.pallas_call(
        functools.partial(_attn_a_sample_kernel, t=t),
        grid=(n_seq, nqb),
        in_specs=[pl.BlockSpec(memory_space=pltpu.SMEM),
                  pl.BlockSpec((A_BLOCK, qw), lambda b, i: (row0 // A_BLOCK + b * nqb + i, 0)),
                  pl.BlockSpec((t, kvw), lambda b, i: (row0 // t + b, qw // kvw)),
                  pl.BlockSpec((None, A_KV_HEADS, past, A_HEAD_DIM), lambda b, i: (b, 0, 0, 0)),
                  pl.BlockSpec((None, A_KV_HEADS, past, A_HEAD_DIM), lambda b, i: (b, 0, 0, 0)),
                  pl.BlockSpec((1, A_HEAD_DIM), lambda b, i: (0, 0)),
                  pl.BlockSpec((1, A_HEAD_DIM), lambda b, i: (0, 0)),
                  pl.BlockSpec((t, A_HEAD_DIM), lambda b, i: (0, 0)),
                  pl.BlockSpec((t, A_HEAD_DIM), lambda b, i: (0, 0))],
        out_specs=pl.BlockSpec((A_BLOCK, qw), lambda b, i: (b * nqb + i, 0)),
        out_shape=jax.ShapeDtypeStruct((n_seq * t, qw), BF16),
        compiler_params=_params(("arbitrary", "arbitrary")),
        name="attn_a_latent",
    )(sink, proj, proj, k_ctx, v_ctx, gq.reshape(1, -1), gk.reshape(1, -1), cos, sin)


def _per_head_lanes(x, fn):
    lane = lax.broadcasted_iota(jnp.int32, x.shape, 1)
    lo = fn(x[:, :B_DK])
    hi = fn(x[:, B_DK:])
    return jnp.where(lane < B_DK, lo, hi)


def _conv_silu(x, w):
    t = x.shape[0]
    row = lax.broadcasted_iota(jnp.int32, x.shape, 0)
    prev = jnp.where(row == 0, 0.0, pltpu.roll(x, 1, 0))
    nxt = jnp.where(row == t - 1, 0.0, pltpu.roll(x, t - 1, 0))
    return _silu(prev * w[0:1, :] + x * w[1:2, :] + nxt * w[2:3, :])


def _delta_prepare(q, k, v, g_col, g_row, beta):
    c = B_CHUNK
    r = q.shape[0]
    ii = lax.broadcasted_iota(jnp.int32, (r, r), 0)
    jj = lax.broadcasted_iota(jnp.int32, (r, r), 1)
    same = (ii // c) == (jj // c)
    ahead = jnp.where(ii < r // 2, ii - jj, jj - ii)
    tri = jnp.where(same, ahead, -1) >= 0
    tri_t = jnp.where(same, ahead, 1) <= 0
    bdiag = (ii // 16) == (jj // 16)
    gc_col = jnp.sum(jnp.where(tri, g_row, 0.0), axis=1, keepdims=True)
    gc_row = jnp.sum(jnp.where(tri_t, g_col, 0.0), axis=0, keepdims=True)
    g_tot = jnp.sum(jnp.where(same, g_row, 0.0), axis=1, keepdims=True)
    decay = jnp.exp(jnp.where(tri, gc_col - gc_row, NEG_INF))
    kb = k * beta
    qk = _bdot_nt(jnp.concatenate([kb, q], axis=0), k)
    m = qk[:r] * jnp.where(ii == jj, 0.0, decay)
    aqk = qk[r:] * decay
    dg = jnp.where(bdiag, m, 0.0)
    off = m - dg
    n1 = -dg
    xs = n1
    n2 = _bdot(n1, n1)
    xs = xs + n2 + _bdot(xs, n2)
    n4 = _bdot(n2, n2)
    xs = xs + n4 + _bdot(xs, n4)
    n8 = _bdot(n4, n4)
    xs = xs + n8 + _bdot(xs, n8)
    f = -(off + _bdot(xs, off))
    f2 = _bdot(f, f)
    ys = xs + f + _bdot(f, xs)
    ts = ys + f2 + _bdot(f2, ys)
    egc = jnp.exp(gc_col)
    rhs = jnp.concatenate([v * beta, kb * egc], axis=-1)
    uw = rhs + _bdot(ts, rhs)

    def block_diag(x):
        return jnp.where(same, jnp.concatenate([x] * (r // c), axis=-1), 0.0)

    kd = k * jnp.exp(g_tot - gc_col)
    kd_t = jnp.concatenate([kd, jnp.zeros_like(kd)], axis=-1).T[:c]
    kd_t = jnp.where(same, jnp.concatenate([kd_t] * (r // c), axis=0), 0.0)
    e_tot = jnp.broadcast_to(jnp.exp(g_tot), v.shape)
    return uw[:, :B_DV], e_tot, block_diag(uw[:, B_DV:]), aqk, block_diag(q * egc), kd_t


def _deltanet_kernel(alog_ref, dtb_ref, q_ref, k_ref, v_ref, z_ref, cwq_ref, cwk_ref, cwv_ref,
                     tail_ref, tailt_ref, onorm_ref, s0f_ref, s0b_ref,
                     o_ref, sf_ref, sb_ref,
                     qc_ref, kc_ref, vc_ref, oacc_ref, u_ref, et_ref, w_ref, aqk_ref, qg_ref, kdt_ref,
                     *, n_chunks):
    hp = pl.program_id(1)
    c = B_CHUNK

    def l2n(x):
        ss = _per_head_lanes(x * x, lambda a: jnp.sum(a, axis=-1, keepdims=True))
        return x * lax.rsqrt(ss + EPS)

    qc_ref[...] = l2n(_conv_silu(q_ref[...], cwq_ref[...])) * (B_DK ** -0.5)
    kc_ref[...] = l2n(_conv_silu(k_ref[...], cwk_ref[...]))
    vc_ref[...] = _conv_silu(v_ref[...], cwv_ref[...])
    oacc_ref[...] = jnp.zeros_like(oacc_ref)

    lane32 = lax.broadcasted_iota(jnp.int32, (c, 4 * B_HEADS), 1)

    def gates(chunk, d, head):
        tail = tail_ref[pl.ds(pl.multiple_of(chunk * c, c), c), :]
        ia = 2 * d * B_HEADS + head
        ib = ia + B_HEADS
        a_col = jnp.sum(jnp.where(lane32 == ia, tail, 0.0), axis=1, keepdims=True)
        b_col = jnp.sum(jnp.where(lane32 == ib, tail, 0.0), axis=1, keepdims=True)
        a_row = tailt_ref[chunk, pl.ds(ia, 1), :]
        na = -jnp.exp(alog_ref[d, head])
        bias = dtb_ref[d, head]
        g_col = na * jax.nn.softplus(a_col + bias)
        g_row = na * jax.nn.softplus(a_row + bias)
        return g_col, g_row, jax.nn.sigmoid(b_col)

    def stacked(ref, chunks):
        parts = []
        for chunk in chunks:
            x = ref[pl.ds(pl.multiple_of(chunk * c, c), c), :]
            parts += [x[:, :B_DK], x[:, B_DK:]]
        return jnp.concatenate(parts, axis=0)

    def prepare(j, carry):
        chunks = (j, n_chunks - 1 - j)
        gs = [gates(chunks[d], d, 2 * hp + hh) for d in range(2) for hh in range(2)]
        g_col = jnp.concatenate([g[0] for g in gs], axis=0)
        g_row = jnp.concatenate([g[1] for g in gs], axis=1)
        beta = jnp.concatenate([g[2] for g in gs], axis=0)
        outs = _delta_prepare(stacked(qc_ref, chunks), stacked(kc_ref, chunks), stacked(vc_ref, chunks),
                              g_col, g_row, beta)
        for ref, x in zip((u_ref, et_ref, w_ref, aqk_ref, qg_ref, kdt_ref), outs):
            ref[j] = x.astype(ref.dtype)
        return carry

    lax.fori_loop(0, n_chunks, prepare, 0, unroll=2)

    def scan(j, s):
        sb = s.astype(BF16)
        delta = u_ref[j] - jnp.dot(w_ref[j], sb, preferred_element_type=F32)
        db = delta.astype(BF16)
        o = (jnp.dot(qg_ref[j], sb, preferred_element_type=F32)
             + jnp.dot(aqk_ref[j], db, preferred_element_type=F32))
        for d, chunk in enumerate((j, n_chunks - 1 - j)):
            rows = pl.ds(pl.multiple_of(chunk * c, c), c)
            oacc_ref[rows, :] += jnp.concatenate([o[2 * d * c:(2 * d + 1) * c],
                                                  o[(2 * d + 1) * c:(2 * d + 2) * c]], axis=-1)
        return s * et_ref[j] + jnp.dot(kdt_ref[j], db, preferred_element_type=F32)

    init = jnp.concatenate([s0f_ref[0], s0f_ref[1], s0b_ref[0], s0b_ref[1]], axis=0)
    fin = lax.fori_loop(0, n_chunks, scan, init)
    sf_ref[0], sf_ref[1], sb_ref[0], sb_ref[1] = (fin[i * B_DK:(i + 1) * B_DK] for i in range(4))

    o = oacc_ref[...]
    ms = _per_head_lanes(o * o, lambda a: jnp.mean(a, axis=-1, keepdims=True))
    o_ref[...] = (o * lax.rsqrt(ms + EPS) * onorm_ref[...] * _silu(z_ref[...])).astype(o_ref.dtype)


def _deltanet(proj, tail, row0, n_seq, t, conv_w, a_log, dt_bias, o_norm, s0_f, s0_b):
    c = B_CHUNK
    n_chunks = t // c
    lw = 2 * B_DK
    col_q = (A_HEADS + 2 * A_KV_HEADS) * A_HEAD_DIM // lw
    nhp = B_HEADS // 2
    rows = tail[row0:row0 + n_seq * t]
    tail_t = rows.reshape(n_seq, n_chunks, c, 4 * B_HEADS).transpose(0, 1, 3, 2)
    onorm2 = jnp.concatenate([o_norm, o_norm]).reshape(1, lw)
    b0 = row0 // t
    seq_blk = lambda off: pl.BlockSpec((t, lw), lambda b, h: (b0 + b, col_q + off + h))
    cw_blk = lambda off: pl.BlockSpec((3, lw), lambda b, h: (0, off + h))
    st_blk = pl.BlockSpec((None, 2, B_DK, B_DV), lambda b, h: (b, h, 0, 0))
    return pl.pallas_call(
        functools.partial(_deltanet_kernel, n_chunks=n_chunks),
        grid=(n_seq, nhp),
        in_specs=[pl.BlockSpec(memory_space=pltpu.SMEM), pl.BlockSpec(memory_space=pltpu.SMEM),
                  seq_blk(0), seq_blk(nhp), seq_blk(2 * nhp), seq_blk(3 * nhp),
                  cw_blk(0), cw_blk(nhp), cw_blk(2 * nhp),
                  pl.BlockSpec((t, 4 * B_HEADS), lambda b, h: (b, 0)),
                  pl.BlockSpec((None, n_chunks, 4 * B_HEADS, c), lambda b, h: (b, 0, 0, 0)),
                  pl.BlockSpec((1, lw), lambda b, h: (0, 0)),
                  st_blk, st_blk],
        out_specs=[pl.BlockSpec((t, lw), lambda b, h: (b, h)), st_blk, st_blk],
        out_shape=[jax.ShapeDtypeStruct((n_seq * t, B_HEADS * B_DV), BF16),
                   jax.ShapeDtypeStruct((n_seq, B_HEADS, B_DK, B_DV), F32),
                   jax.ShapeDtypeStruct((n_seq, B_HEADS, B_DK, B_DV), F32)],
        scratch_shapes=[pltpu.VMEM((t, lw), F32)] * 4 + [pltpu.VMEM((n_chunks, 4 * c, B_DV), F32)] * 2
        + [pltpu.VMEM((n_chunks, 4 * c, 4 * c), BF16)] * 4,
        compiler_params=_params(("arbitrary", "arbitrary")),
        name="deltanet",
    )(a_log, dt_bias, proj, proj, proj, proj, conv_w, conv_w, conv_w, rows, tail_t, onorm2, s0_f, s0_b)


def _lora_norm_kernel(p_ref, gq_ref, gkv_ref, cq_ref, ckv_ref, ckvb_ref):
    p = p_ref[...]
    cq_ref[...] = _rms(p[:, :C_Q_LORA], gq_ref[...]).astype(cq_ref.dtype)
    ckv = _rms(p[:, C_Q_LORA:C_Q_LORA + C_KV_LORA], gkv_ref[...])
    ckv_ref[...] = ckv
    ckvb_ref[...] = ckv.astype(ckvb_ref.dtype)


def _lora_norm(p1, gq, gkv, tm=512):
    n, w = p1.shape
    return pl.pallas_call(
        _lora_norm_kernel,
        grid=(n // tm,),
        in_specs=[pl.BlockSpec((tm, w), lambda i: (i, 0)),
                  pl.BlockSpec((1, C_Q_LORA), lambda i: (0, 0)),
                  pl.BlockSpec((1, C_KV_LORA), lambda i: (0, 0))],
        out_specs=[pl.BlockSpec((tm, C_Q_LORA), lambda i: (i, 0)),
                   pl.BlockSpec((tm, C_KV_LORA), lambda i: (i, 0)),
                   pl.BlockSpec((tm, C_KV_LORA), lambda i: (i, 0))],
        out_shape=[jax.ShapeDtypeStruct((n, C_Q_LORA), BF16),
                   jax.ShapeDtypeStruct((n, C_KV_LORA), F32),
                   jax.ShapeDtypeStruct((n, C_KV_LORA), BF16)],
        compiler_params=_params(("arbitrary",)),
        name="lora_norm",
    )(p1, gq.reshape(1, -1), gkv.reshape(1, -1))


def _mla_head_q(q, h, gq):
    qn, qr = q[:, h * C_QK:h * C_QK + C_NOPE], q[:, h * C_QK + C_NOPE:(h + 1) * C_QK]
    rn = lax.rsqrt((jnp.sum(qn * qn, axis=-1, keepdims=True)
                    + jnp.sum(qr * qr, axis=-1, keepdims=True)) / C_QK + EPS)
    return qn * rn * gq[:, :C_NOPE], qr * rn * gq[:, C_NOPE:]


def _mla_head_k(kv, kr, kr_ss, h, gk):
    kn = kv[:, h * (C_NOPE + C_V):h * (C_NOPE + C_V) + C_NOPE]
    v = kv[:, h * (C_NOPE + C_V) + C_NOPE:(h + 1) * (C_NOPE + C_V)]
    rn = lax.rsqrt((jnp.sum(kn * kn, axis=-1, keepdims=True) + kr_ss) / C_QK + EPS)
    return kn * rn * gk[:, :C_NOPE], kr * rn, v


def _attn_c_prompt_kernel(q_ref, kv_ref, p_ref, gq_ref, gk_ref, o_ref):
    scale = C_QK ** -0.5
    q = q_ref[...]
    kv = kv_ref[...]
    gq, gk = gq_ref[...], gk_ref[...]
    kr_raw = p_ref[...][:, C_Q_LORA + C_KV_LORA:]
    kr_ss = jnp.sum(kr_raw * kr_raw, axis=-1, keepdims=True)
    kr_g = kr_raw * gk[:, C_NOPE:]
    outs = []
    for h in range(C_HEADS):
        qn, qr = _mla_head_q(q, h, gq)
        kn, kr, v = _mla_head_k(kv, kr_g, kr_ss, h, gk)
        s = (_bdot_nt(qn, kn) + _bdot_nt(qr, kr)) * scale
        outs.append(_softmax_attend([(s, v)], None))
    o_ref[...] = jnp.concatenate(outs, axis=-1).astype(o_ref.dtype)


def _attn_c_prompt(q, kv, p1, n_seq, t, gq, gk):
    return pl.pallas_call(
        _attn_c_prompt_kernel,
        grid=(n_seq,),
        in_specs=[pl.BlockSpec((t, q.shape[1]), lambda b: (b, 0)),
                  pl.BlockSpec((t, kv.shape[1]), lambda b: (b, 0)),
                  pl.BlockSpec((t, p1.shape[1]), lambda b: (b, 0)),
                  pl.BlockSpec((1, C_QK), lambda b: (0, 0)),
                  pl.BlockSpec((1, C_QK), lambda b: (0, 0))],
        out_specs=pl.BlockSpec((t, C_HEADS * C_V), lambda b: (b, 0)),
        out_shape=jax.ShapeDtypeStruct((n_seq * t, C_HEADS * C_V), BF16),
        compiler_params=_params(("arbitrary",), VMEM_LIMIT),
        name="attn_c_context",
    )(q, kv, p1, gq.reshape(1, -1), gk.reshape(1, -1))


def _attn_c_sample_kernel(q_ref, kv_ref, p_ref, kvc_ref, krc_ref, gq_ref, gk_ref, cos_ref, sin_ref,
                          o_ref, *, tq):
    scale = C_QK ** -0.5
    i = pl.program_id(1)
    q0 = pl.multiple_of(i * tq, tq)
    q = q_ref[...]
    kv = kv_ref[...]
    kvc = kvc_ref[...]
    gq, gk = gq_ref[...], gk_ref[...]
    cos, sin = cos_ref[...], sin_ref[...]
    cq, sq = cos_ref[pl.ds(q0, tq), :], sin_ref[pl.ds(q0, tq), :]
    kr_raw = p_ref[...][:, C_Q_LORA + C_KV_LORA:]
    kr_ss = jnp.sum(kr_raw * kr_raw, axis=-1, keepdims=True)
    kr_g = _rope(kr_raw * gk[:, C_NOPE:], cos, sin)
    krc_raw = krc_ref[...]
    krc_ss = jnp.sum(krc_raw * krc_raw, axis=-1, keepdims=True)
    krc_g = krc_raw * gk[:, C_NOPE:]
    outs = []
    for h in range(C_HEADS):
        qn, qr = _mla_head_q(q, h, gq)
        qr = _rope(qr, cq, sq)
        kn, kr, v = _mla_head_k(kv, kr_g, kr_ss, h, gk)
        knc, krc, vc = _mla_head_k(kvc, krc_g, krc_ss, h, gk)
        s1 = (_bdot_nt(qn, kn) + _bdot_nt(qr, kr)) * scale
        s2 = (_bdot_nt(qn, knc) + _bdot_nt(qr, krc)) * scale
        outs.append(_softmax_attend([(s1, v), (s2, vc)], None))
    o_ref[...] = jnp.concatenate(outs, axis=-1).astype(o_ref.dtype)


def _attn_c_sample(q, kv, p1, row0, n_seq, t, kr_ctx, gq, gk, tq=256):
    n = p1.shape[0]
    past = kr_ctx.shape[1]
    nq = t // tq
    cos, sin = _rope_tables(t, C_ROPE)
    return pl.pallas_call(
        functools.partial(_attn_c_sample_kernel, tq=tq),
        grid=(n_seq, nq),
        in_specs=[pl.BlockSpec((tq, q.shape[1]), lambda b, i: (row0 // tq + b * nq + i, 0)),
                  pl.BlockSpec((t, kv.shape[1]), lambda b, i: (row0 // t + b, 0)),
                  pl.BlockSpec((t, p1.shape[1]), lambda b, i: (row0 // t + b, 0)),
                  pl.BlockSpec((past, kv.shape[1]), lambda b, i: (n // past + b, 0)),
                  pl.BlockSpec((None, past, C_ROPE), lambda b, i: (b, 0, 0)),
                  pl.BlockSpec((1, C_QK), lambda b, i: (0, 0)),
                  pl.BlockSpec((1, C_QK), lambda b, i: (0, 0)),
                  pl.BlockSpec((t, C_ROPE), lambda b, i: (0, 0)),
                  pl.BlockSpec((t, C_ROPE), lambda b, i: (0, 0))],
        out_specs=pl.BlockSpec((tq, C_HEADS * C_V), lambda b, i: (b * nq + i, 0)),
        out_shape=jax.ShapeDtypeStruct((n_seq * t, C_HEADS * C_V), BF16),
        compiler_params=_params(("arbitrary", "arbitrary"), VMEM_LIMIT),
        name="attn_c_latent",
    )(q, kv, p1, kv, kr_ctx, gq.reshape(1, -1), gk.reshape(1, -1), cos, sin)


ROUTE_TILE = 256
META_LANES = 128


def _router_kernel(x_ref, g_ref, sc_ref, sh_ref, wr_ref, br_ref, h_ref, meta_ref, cnt_ref, run_ref):
    @pl.when(pl.program_id(0) == 0)
    def _():
        run_ref[...] = jnp.zeros_like(run_ref)

    h = _rms(x_ref[...], g_ref[...]) * (1 + sc_ref[...]) + sh_ref[...]
    h_ref[...] = h
    tm = h.shape[0]
    logits = _hdot(h, wr_ref[...]) + br_ref[...]
    lane = lax.broadcasted_iota(jnp.int32, logits.shape, 1)
    work = logits
    picks, tops, ids = [], [], []
    for _ in range(TOP_K):
        m = work.max(axis=-1, keepdims=True)
        first = jnp.min(jnp.where(work == m, lane, N_EXPERTS), axis=-1, keepdims=True)
        pick = lane == first
        picks.append(pick)
        tops.append(m)
        ids.append(first)
        work = jnp.where(pick, -jnp.inf, work)
    sel = sum(p.astype(F32) for p in picks)
    earlier = (lax.broadcasted_iota(jnp.int32, (tm, tm), 0)
               > lax.broadcasted_iota(jnp.int32, (tm, tm), 1)).astype(BF16)
    before = run_ref[...] + jnp.dot(earlier, sel.astype(BF16), preferred_element_type=F32)
    ws = [jnp.exp(t - tops[0]) for t in tops]
    den = sum(ws)
    mlane = lax.broadcasted_iota(jnp.int32, (tm, META_LANES), 1)
    meta = jnp.zeros((tm, META_LANES), F32)
    for k in range(TOP_K):
        rank = jnp.sum(jnp.where(picks[k], before, 0.0), axis=-1, keepdims=True)
        meta = jnp.where(mlane == k, ids[k].astype(F32), meta)
        meta = jnp.where(mlane == TOP_K + k, rank, meta)
        meta = jnp.where(mlane == 2 * TOP_K + k, ws[k] / den, meta)
    meta_ref[...] = meta
    run_ref[...] += jnp.sum(sel, axis=0, keepdims=True)
    cnt_ref[...] = run_ref[...]


def _router(x, gain, scale, shift, w_router, b_router, layer, n_p, t_s, tm=512):
    n, d = x.shape
    e = w_router.shape[-1]
    group = _group_of_tile(tm, n_p, t_s)
    return pl.pallas_call(
        _router_kernel,
        grid=(n // tm,),
        in_specs=[pl.BlockSpec((tm, d), lambda i: (i, 0)),
                  pl.BlockSpec((1, d), lambda i: (0, 0)),
                  pl.BlockSpec((None, 1, d), lambda i: (group(i), 0, 0)),
                  pl.BlockSpec((None, 1, d), lambda i: (group(i), 0, 0)),
                  pl.BlockSpec((None, d, e), lambda i: (layer, 0, 0)),
                  pl.BlockSpec((None, 1, e), lambda i: (layer, 0, 0))],
        out_specs=[pl.BlockSpec((tm, d), lambda i: (i, 0)),
                   pl.BlockSpec((tm, META_LANES), lambda i: (i, 0)),
                   pl.BlockSpec((1, e), lambda i: (0, 0))],
        out_shape=[jax.ShapeDtypeStruct((n, d), F32),
                   jax.ShapeDtypeStruct((n, META_LANES), F32),
                   jax.ShapeDtypeStruct((1, e), F32)],
        scratch_shapes=[pltpu.VMEM((1, e), F32)],
        compiler_params=_params(("arbitrary",)),
        name="router",
    )(x, gain.reshape(1, d), scale.reshape(N_GROUPS, 1, d), shift.reshape(N_GROUPS, 1, d),
      w_router, b_router.reshape(-1, 1, e))


def _route_plan(meta, cnt, n_tiles):
    ids = meta[:, :TOP_K].astype(jnp.int32)
    ranks = meta[:, TOP_K:2 * TOP_K].astype(jnp.int32)
    gates = meta[:, 2 * TOP_K:3 * TOP_K]
    counts = cnt[0].astype(jnp.int32)
    padded = (counts + ROUTE_TILE - 1) // ROUTE_TILE * ROUTE_TILE
    ends = jnp.cumsum(padded)
    offs = ends - padded
    pos = offs[ids] + ranks
    tile_start = jnp.arange(n_tiles, dtype=jnp.int32) * ROUTE_TILE
    n_valid = ends[-1] // ROUTE_TILE
    te = jnp.minimum(jnp.sum(ends[None, :] <= tile_start[:, None], axis=1), N_EXPERTS - 1)
    te = jnp.where(tile_start < ends[-1], te, te[jnp.maximum(n_valid - 1, 0)]).astype(jnp.int32)
    rows = jnp.clip((offs + counts)[te] - tile_start, 0, ROUTE_TILE).astype(jnp.int32)
    ragged = jnp.where(counts % ROUTE_TILE != 0, ends // ROUTE_TILE - 1, -1)
    tail = tile_start[n_tiles - N_EXPERTS:] // ROUTE_TILE
    fill = jnp.concatenate([ragged, jnp.where(tail >= n_valid, tail, -1)]).astype(jnp.int32)
    return pos, gates, te, rows, n_valid.reshape(1).astype(jnp.int32), fill


def _dispatch_kernel(fill_ref, pos_ref, h_ref, xs_ref, zero_ref, sem, zsem):
    tm = h_ref.shape[0]

    @pl.when(pl.program_id(0) == 0)
    def _():
        zero_ref[...] = jnp.zeros_like(zero_ref)

        def fill_copy(j):
            row0 = pl.multiple_of(fill_ref[j] * ROUTE_TILE, ROUTE_TILE)
            return pltpu.make_async_copy(zero_ref, xs_ref.at[pl.ds(row0, ROUTE_TILE), :], zsem)

        def start(j, carry):
            @pl.when(fill_ref[j] >= 0)
            def _():
                fill_copy(j).start()
            return carry

        def wait(j, carry):
            @pl.when(fill_ref[j] >= 0)
            def _():
                fill_copy(j).wait()
            return carry

        lax.fori_loop(0, fill_ref.shape[0], start, 0)
        lax.fori_loop(0, fill_ref.shape[0], wait, 0)

    def body(r, carry):
        for k in range(TOP_K):
            p = pos_ref[0, r * TOP_K + k]
            pltpu.make_async_copy(h_ref.at[pl.ds(r, 1), :], xs_ref.at[pl.ds(p, 1), :], sem).start()
        return carry

    lax.fori_loop(0, tm, body, 0, unroll=4)
    for _ in range(TOP_K):
        pltpu.make_async_copy(h_ref, h_ref, sem).wait()


def _dispatch(h, pos, fill, n_slots, tm=512):
    n, d = h.shape
    grid_spec = pltpu.PrefetchScalarGridSpec(
        num_scalar_prefetch=1,
        grid=(n // tm,),
        in_specs=[pl.BlockSpec((None, 1, tm * TOP_K), lambda i, fl: (i, 0, 0), memory_space=pltpu.SMEM),
                  pl.BlockSpec((tm, d), lambda i, fl: (i, 0))],
        out_specs=pl.BlockSpec(memory_space=pl.ANY),
        scratch_shapes=[pltpu.VMEM((ROUTE_TILE, d), h.dtype), pltpu.SemaphoreType.DMA(()),
                        pltpu.SemaphoreType.DMA(())])
    return pl.pallas_call(
        _dispatch_kernel,
        grid_spec=grid_spec,
        out_shape=jax.ShapeDtypeStruct((n_slots, d), h.dtype),
        compiler_params=_params(("arbitrary",)),
        name="moe_dispatch",
    )(fill, pos.reshape(n // tm, 1, tm * TOP_K), h)


def _experts_kernel(te_ref, rows_ref, nv_ref, x_ref, wgu_ref, bgu_ref, wd_ref, bd_ref, y_ref,
                    wgub_ref, wdb_ref, *, d_ff):
    i = pl.program_id(0)
    valid = i < nv_ref[0]
    fresh = jnp.logical_or(i == 0, te_ref[i] != te_ref[jnp.maximum(i - 1, 0)])

    @pl.when(jnp.logical_and(valid, fresh))
    def _():
        wgub_ref[...] = wgu_ref[...].astype(BF16)
        wdb_ref[...] = wd_ref[...].astype(BF16)

    @pl.when(valid)
    def _():
        row = lax.broadcasted_iota(jnp.int32, x_ref.shape, 0)
        x = jnp.where(row < rows_ref[i], x_ref[...], 0.0).astype(BF16)
        gu = jnp.dot(x, wgub_ref[...], preferred_element_type=F32) + bgu_ref[...]
        gate = jnp.minimum(gu[:, :d_ff], SWIGLU_LIMIT)
        up = jnp.clip(gu[:, d_ff:], -SWIGLU_LIMIT, SWIGLU_LIMIT)
        act = (up + 1) * gate * jax.nn.sigmoid(SWIGLU_ALPHA * gate)
        y_ref[...] = jnp.dot(act.astype(BF16), wdb_ref[...], preferred_element_type=F32) + bd_ref[...]

    @pl.when(jnp.logical_not(valid))
    def _():
        y_ref[...] = jnp.zeros_like(y_ref)


def _experts(xs, te, rows, n_valid, w_gu, b_gu, w_down, b_down, layer):
    n_slots, d = xs.shape
    _, e, _, two_ff = w_gu.shape
    last = lambda i, nv: jnp.minimum(i, nv[0] - 1)
    grid_spec = pltpu.PrefetchScalarGridSpec(
        num_scalar_prefetch=3,
        grid=(n_slots // ROUTE_TILE,),
        in_specs=[pl.BlockSpec((ROUTE_TILE, d), lambda i, te, rw, nv: (last(i, nv), 0)),
                  pl.BlockSpec((None, None, d, two_ff), lambda i, te, rw, nv: (layer, te[i], 0, 0)),
                  pl.BlockSpec((None, None, 1, two_ff), lambda i, te, rw, nv: (layer, te[i], 0, 0)),
                  pl.BlockSpec((None, None, two_ff // 2, d), lambda i, te, rw, nv: (layer, te[i], 0, 0)),
                  pl.BlockSpec((None, None, 1, d), lambda i, te, rw, nv: (layer, te[i], 0, 0))],
        out_specs=pl.BlockSpec((ROUTE_TILE, d), lambda i, te, rw, nv: (i, 0)),
        scratch_shapes=[pltpu.VMEM((d, two_ff), BF16), pltpu.VMEM((two_ff // 2, d), BF16)])
    return pl.pallas_call(
        functools.partial(_experts_kernel, d_ff=two_ff // 2),
        grid_spec=grid_spec,
        out_shape=jax.ShapeDtypeStruct((n_slots, d), F32),
        compiler_params=_params(("arbitrary",), VMEM_LIMIT),
        name="moe_experts",
    )(te, rows, n_valid, xs, w_gu, b_gu.reshape(b_gu.shape[0], e, 1, two_ff), w_down,
      b_down.reshape(b_down.shape[0], e, 1, d))


def _combine_kernel(pos_ref, x_ref, gate_ref, gts_ref, y_ref, o_ref, buf_ref, sem):
    tm = x_ref.shape[0]

    def body(r, carry):
        for k in range(TOP_K):
            p = pos_ref[0, r * TOP_K + k]
            pltpu.make_async_copy(y_ref.at[pl.ds(p, 1), :], buf_ref.at[k, pl.ds(r, 1), :], sem).start()
        return carry

    lax.fori_loop(0, tm, body, 0, unroll=4)
    for k in range(TOP_K):
        pltpu.make_async_copy(buf_ref.at[k], buf_ref.at[k], sem).wait()
    g = gts_ref[...]
    acc = g[:, 0:1] * buf_ref[0]
    for k in range(1, TOP_K):
        acc = acc + g[:, k:k + 1] * buf_ref[k]
    o_ref[...] = x_ref[...] + gate_ref[...] * acc


def _combine(y, pos, gates, x, gate, n_p, t_s, tm=256):
    n, d = x.shape
    group = _group_of_tile(tm, n_p, t_s)
    return pl.pallas_call(
        _combine_kernel,
        grid=(n // tm,),
        in_specs=[pl.BlockSpec((None, 1, tm * TOP_K), lambda i: (i, 0, 0), memory_space=pltpu.SMEM),
                  pl.BlockSpec((tm, d), lambda i: (i, 0)),
                  pl.BlockSpec((None, 1, d), lambda i: (group(i), 0, 0)),
                  pl.BlockSpec((tm, TOP_K), lambda i: (i, 0)),
                  pl.BlockSpec(memory_space=pl.ANY)],
        out_specs=pl.BlockSpec((tm, d), lambda i: (i, 0)),
        out_shape=jax.ShapeDtypeStruct((n, d), F32),
        scratch_shapes=[pltpu.VMEM((TOP_K, tm, d), F32), pltpu.SemaphoreType.DMA(())],
        compiler_params=_params(("arbitrary",)),
        name="moe_combine",
    )(pos.reshape(n // tm, 1, tm * TOP_K), x, gate.reshape(N_GROUPS, 1, d), gates, y)


def _moe(x, gain, scale, shift, gate, w_router, b_router, w_gu, b_gu, w_down, b_down, layer, n_p, t_s):
    n = x.shape[0]
    n_slots = n * TOP_K + N_EXPERTS * ROUTE_TILE
    h, meta, cnt = _router(x, gain, scale, shift, w_router, b_router, layer, n_p, t_s)
    pos, gates, te, rows, n_valid, fill = _route_plan(meta, cnt, n_slots // ROUTE_TILE)
    xs = _dispatch(h, pos, fill, n_slots)
    y = _experts(xs, te, rows, n_valid, w_gu, b_gu, w_down, b_down, layer)
    return _combine(y, pos, gates, x, gate, n_p, t_s)


def kernel(x_prompt, x_sample, c, cache_a_k, cache_a_v, state_b_fwd, state_b_bwd, cache_c_ckv,
           cache_c_krope, c_ctx, w_mod, b_mod, norm_mix, norm_ffn, e_w_in, e_w_out, e_a_qnorm,
           e_a_knorm, e_a_sink, e_b_conv, e_b_alog, e_b_dtbias, e_b_onorm, o_w_in, o_q_lora_norm,
           o_kv_lora_norm, o_w_uq, o_w_ukv, o_qnorm, o_knorm, o_w_out, moe_w_router, moe_b_router,
           moe_w_gu, moe_b_gu, moe_w_down, moe_b_down):
    bp, tp, d = x_prompt.shape
    bs, ts, _ = x_sample.shape
    depth = w_mod.shape[0]
    n_p, n_s = bp * tp, bs * ts
    n = n_p + n_s
    assert bs + 1 <= N_GROUPS and ts % 512 == 0 and n_p % ts == 0

    x = jnp.concatenate([x_prompt.reshape(n_p, d), x_sample.reshape(n_s, d)], axis=0)
    cond = jnp.concatenate([c_ctx[None], c, jnp.zeros((N_GROUPS - 1 - bs, d), F32)], axis=0)
    mod = _adaln(cond, w_mod, b_mod)

    new_a_k, new_a_v, new_b_fwd, new_b_bwd, new_c_ckv, new_c_krope = [], [], [], [], [], []
    for layer in range(depth):
        sh1, sc1, g1, sh2, sc2, g2 = (mod[layer, j] for j in range(6))
        h = _modulate(x, norm_mix[layer], sc1, sh1, n_p, ts)
        i = layer // 2
        if layer % 2 == 0:
            main_w = (A_HEADS + 2 * A_KV_HEADS) * A_HEAD_DIM + 4 * B_HEADS * B_DK
            proj = _matmul(h, e_w_in, i, 0, main_w // 2, main_w, name="even_in_proj")
            tail = _matmul(h, e_w_in[i][None, :, main_w:], 0, 0, 4 * B_HEADS, 4 * B_HEADS,
                           name="even_gate_proj")
            oa_p, kn_p = _attn_a_prompt(proj, bp, tp, e_a_sink[i], e_a_qnorm[i], e_a_knorm[i])
            oa_s = _attn_a_sample(proj, n_p, bs, ts, cache_a_k[:, i], cache_a_v[:, i],
                                  e_a_sink[i], e_a_qnorm[i], e_a_knorm[i])
            zeros = jnp.zeros((bp, B_HEADS, B_DK, B_DV), F32)
            ob_p, s_f, s_b = _deltanet(proj, tail, 0, bp, tp, e_b_conv[i], e_b_alog[i],
                                       e_b_dtbias[i], e_b_onorm[i], zeros, zeros)
            ob_s, _, _ = _deltanet(proj, tail, n_p, bs, ts, e_b_conv[i], e_b_alog[i],
                                   e_b_dtbias[i], e_b_onorm[i], state_b_fwd[:, i], state_b_bwd[:, i])
            mix = jnp.concatenate([jnp.concatenate([oa_p, ob_p], axis=1),
                                   jnp.concatenate([oa_s, ob_s], axis=1)], axis=0)
            x = _matmul_residual(mix, e_w_out, i, x, g1, n_p, ts)
            kw = A_KV_HEADS * A_HEAD_DIM
            new_a_k.append(kn_p.reshape(bp, tp, A_KV_HEADS, A_HEAD_DIM).transpose(0, 2, 1, 3))
            v_p = proj[:n_p, A_HEADS * A_HEAD_DIM + kw:A_HEADS * A_HEAD_DIM + 2 * kw]
            new_a_v.append(v_p.reshape(bp, tp, A_KV_HEADS, A_HEAD_DIM).transpose(0, 2, 1, 3))
            new_b_fwd.append(s_f)
            new_b_bwd.append(s_b)
        else:
            p1 = _matmul(h, o_w_in, i, 0, o_w_in.shape[-1], o_w_in.shape[-1], name="odd_in_proj")
            cq, ckv, ckv_b = _lora_norm(p1, o_q_lora_norm[i], o_kv_lora_norm[i])
            q = _matmul(cq, o_w_uq, i, 0, o_w_uq.shape[-1] // 2, o_w_uq.shape[-1], name="odd_uq")
            ckv_all = jnp.concatenate([ckv_b, cache_c_ckv[:, i].reshape(-1, C_KV_LORA).astype(BF16)], 0)
            kv = _matmul(ckv_all, o_w_ukv, i, 0, o_w_ukv.shape[-1] // 2, o_w_ukv.shape[-1],
                         name="odd_ukv")
            o_p = _attn_c_prompt(q, kv, p1, bp, tp, o_qnorm[i], o_knorm[i])
            o_s = _attn_c_sample(q, kv, p1, n_p, bs, ts, cache_c_krope[:, i], o_qnorm[i], o_knorm[i])
            x = _matmul_residual(jnp.concatenate([o_p, o_s], axis=0), o_w_out, i, x, g1, n_p, ts)
            new_c_ckv.append(ckv[:n_p].reshape(bp, tp, C_KV_LORA))
            new_c_krope.append(p1[:n_p, C_Q_LORA + C_KV_LORA:].reshape(bp, tp, C_ROPE))
        x = _moe(x, norm_ffn[layer], sc2, sh2, g2, moe_w_router, moe_b_router, moe_w_gu, moe_b_gu,
                 moe_w_down, moe_b_down, layer, n_p, ts)

    return (x[:n_p].reshape(bp, tp, d), x[n_p:].reshape(bs, ts, d),
            jnp.stack(new_a_k, axis=1), jnp.stack(new_a_v, axis=1),
            jnp.stack(new_b_fwd, axis=1), jnp.stack(new_b_bwd, axis=1),
            jnp.stack(new_c_ckv, axis=1), jnp.stack(new_c_krope, axis=1))
```

```python
import functools
import math

import numpy as np
import jax
import jax.numpy as jnp
from jax import lax
from jax.experimental import pallas as pl
from jax.experimental.pallas import tpu as pltpu

F32 = jnp.float32
BF16 = jnp.bfloat16
HIGHEST = lax.Precision.HIGHEST

EPS = 1e-6
NEG_INF = -1e30
ROPE_BASE = 10000.0
GRID_W = 64
N_GROUPS = 8

A_HEADS, A_KV_HEADS, A_GROUP, A_HEAD_DIM, A_WINDOW, A_BLOCK = 8, 2, 4, 64, 128, 128
B_HEADS, B_DK, B_DV, B_CHUNK = 8, 64, 64, 64
C_HEADS, C_NOPE, C_ROPE, C_V, C_Q_LORA, C_KV_LORA = 16, 64, 32, 64, 384, 256
C_QK = C_NOPE + C_ROPE
N_EXPERTS, TOP_K = 32, 4
SWIGLU_LIMIT, SWIGLU_ALPHA = 7.0, 1.702

VMEM_LIMIT = 56 * 1024 * 1024


def _params(sem, vmem=None):
    return pltpu.CompilerParams(dimension_semantics=sem, vmem_limit_bytes=vmem)


def _bdot(a, b):
    return jnp.dot(a.astype(BF16), b.astype(BF16), preferred_element_type=F32)


def _bdot_nt(a, b):
    return lax.dot_general(a.astype(BF16), b.astype(BF16), (((1,), (1,)), ((), ())),
                           preferred_element_type=F32)


def _bdot_tn(a, b):
    return lax.dot_general(a.astype(BF16), b.astype(BF16), (((0,), (0,)), ((), ())),
                           preferred_element_type=F32)


def _hdot(a, b):
    return jnp.dot(a, b, preferred_element_type=F32, precision=HIGHEST)


def _hdot_nt(a, b):
    return lax.dot_general(a, b, (((1,), (1,)), ((), ())), preferred_element_type=F32,
                           precision=HIGHEST)


def _rms(x, gain):
    return x * lax.rsqrt(jnp.mean(x * x, axis=-1, keepdims=True) + EPS) * gain


def _silu(x):
    return x * jax.nn.sigmoid(x)


def _group_of_tile(tm, n_p, t_s):
    def group(i):
        r = i * tm
        return jnp.where(r < n_p, 0, 1 + (r - n_p) // t_s)
    return group


def _adaln_kernel(cond_ref, w_ref, b_ref, o_ref):
    o_ref[...] = _bdot(_silu(cond_ref[...]), w_ref[...]) + b_ref[...]


def _adaln(cond, w_mod, b_mod):
    depth, d, _ = w_mod.shape
    return pl.pallas_call(
        _adaln_kernel,
        grid=(depth, 6),
        in_specs=[pl.BlockSpec((N_GROUPS, d), lambda l, j: (0, 0)),
                  pl.BlockSpec((None, d, d), lambda l, j: (l, 0, j)),
                  pl.BlockSpec((None, 1, d), lambda l, j: (l, 0, j))],
        out_specs=pl.BlockSpec((None, None, N_GROUPS, d), lambda l, j: (l, j, 0, 0)),
        out_shape=jax.ShapeDtypeStruct((depth, 6, N_GROUPS, d), F32),
        compiler_params=_params(("arbitrary", "arbitrary")),
        name="adaln",
    )(cond, w_mod, b_mod.reshape(depth, 1, 6 * d))


def _modulate_kernel(x_ref, g_ref, sc_ref, sh_ref, o_ref):
    y = _rms(x_ref[...], g_ref[...])
    o_ref[...] = (y * (1 + sc_ref[...]) + sh_ref[...]).astype(o_ref.dtype)


def _modulate(x, gain, scale, shift, n_p, t_s, tm=512):
    n, d = x.shape
    group = _group_of_tile(tm, n_p, t_s)
    return pl.pallas_call(
        _modulate_kernel,
        grid=(n // tm,),
        in_specs=[pl.BlockSpec((tm, d), lambda i: (i, 0)),
                  pl.BlockSpec((1, d), lambda i: (0, 0)),
                  pl.BlockSpec((None, 1, d), lambda i: (group(i), 0, 0)),
                  pl.BlockSpec((None, 1, d), lambda i: (group(i), 0, 0))],
        out_specs=pl.BlockSpec((tm, d), lambda i: (i, 0)),
        out_shape=jax.ShapeDtypeStruct((n, d), BF16),
        compiler_params=_params(("arbitrary",)),
        name="modulate",
    )(x, gain.reshape(1, d), scale.reshape(N_GROUPS, 1, d), shift.reshape(N_GROUPS, 1, d))


def _mm_kernel(x_ref, w_ref, o_ref, wb_ref):
    @pl.when(pl.program_id(1) == 0)
    def _():
        wb_ref[...] = w_ref[...].astype(BF16)
    o_ref[...] = jnp.dot(x_ref[...], wb_ref[...], preferred_element_type=F32).astype(o_ref.dtype)


def _matmul(x, w3, layer, col0_blocks, tn, n_out, out_dtype=F32, tm=512, name="matmul"):
    n, k = x.shape
    return pl.pallas_call(
        _mm_kernel,
        grid=(n_out // tn, n // tm),
        in_specs=[pl.BlockSpec((tm, k), lambda j, i: (i, 0)),
                  pl.BlockSpec((None, k, tn), lambda j, i: (layer, 0, col0_blocks + j))],
        out_specs=pl.BlockSpec((tm, tn), lambda j, i: (i, j)),
        out_shape=jax.ShapeDtypeStruct((n, n_out), out_dtype),
        scratch_shapes=[pltpu.VMEM((k, tn), BF16)],
        compiler_params=_params(("arbitrary", "arbitrary"), VMEM_LIMIT),
        name=name,
    )(x, w3)


def _mm_res_kernel(x_ref, w_ref, res_ref, gate_ref, o_ref, wb_ref):
    @pl.when(pl.program_id(0) == 0)
    def _():
        wb_ref[...] = w_ref[...].astype(BF16)
    y = jnp.dot(x_ref[...], wb_ref[...], preferred_element_type=F32)
    o_ref[...] = res_ref[...] + gate_ref[...] * y


def _matmul_residual(mix, w3, layer, res, gate, n_p, t_s, tm=512):
    n, k = mix.shape
    d = res.shape[1]
    group = _group_of_tile(tm, n_p, t_s)
    return pl.pallas_call(
        _mm_res_kernel,
        grid=(n // tm,),
        in_specs=[pl.BlockSpec((tm, k), lambda i: (i, 0)),
                  pl.BlockSpec((None, k, d), lambda i: (layer, 0, 0)),
                  pl.BlockSpec((tm, d), lambda i: (i, 0)),
                  pl.BlockSpec((None, 1, d), lambda i: (group(i), 0, 0))],
        out_specs=pl.BlockSpec((tm, d), lambda i: (i, 0)),
        out_shape=jax.ShapeDtypeStruct((n, d), F32),
        scratch_shapes=[pltpu.VMEM((k, d), BF16)],
        compiler_params=_params(("arbitrary",), VMEM_LIMIT),
        name="out_proj_residual",
    )(mix, w3, res, gate.reshape(N_GROUPS, 1, d))


def _rope_tables(t_len, d):
    half, quarter = d // 2, d // 4
    pos = np.arange(t_len)
    row, col = pos // GRID_W, pos % GRID_W
    inv = ROPE_BASE ** (-np.arange(quarter, dtype=np.float64) / quarter)
    ang_r = row[:, None] * inv[None, :]
    ang_c = col[:, None] * inv[None, :]
    cos = np.concatenate([np.cos(ang_r), np.cos(ang_r), np.cos(ang_c), np.cos(ang_c)], axis=1)
    sin = np.concatenate([-np.sin(ang_r), np.sin(ang_r), -np.sin(ang_c), np.sin(ang_c)], axis=1)
    return jnp.asarray(cos, F32), jnp.asarray(sin, F32)


def _swap_pairs(x):
    q = x.shape[-1] // 4
    return jnp.concatenate([x[:, q:2 * q], x[:, :q], x[:, 3 * q:], x[:, 2 * q:3 * q]], axis=-1)


def _rope(x, cos, sin):
    return x * cos + _swap_pairs(x) * sin


def _softmax_attend(parts, sink):
    m = parts[0][0].max(axis=-1, keepdims=True)
    for s, _ in parts[1:]:
        m = jnp.maximum(m, s.max(axis=-1, keepdims=True))
    if sink is not None:
        m = jnp.maximum(m, sink)
    den = jnp.exp(sink - m) if sink is not None else 0.0
    acc = None
    for s, v in parts:
        p = jnp.exp(s - m)
        den = den + p.sum(axis=-1, keepdims=True)
        o = _bdot(p, v)
        acc = o if acc is None else acc + o
    return acc / den


def _attn_a_prompt_kernel(sink_ref, q_ref, kv_ref, gq_ref, gk_ref, o_ref, kn_ref):
    scale = A_HEAD_DIM ** -0.5
    q = q_ref[...]
    kv = kv_ref[...]
    outs = []
    kns = []
    for hk in range(A_KV_HEADS):
        k = _rms(kv[:, hk * A_HEAD_DIM:(hk + 1) * A_HEAD_DIM], gk_ref[...])
        v = kv[:, (A_KV_HEADS + hk) * A_HEAD_DIM:(A_KV_HEADS + hk + 1) * A_HEAD_DIM]
        kns.append(k)
        for g in range(A_GROUP):
            h = hk * A_GROUP + g
            qh = _rms(q[:, h * A_HEAD_DIM:(h + 1) * A_HEAD_DIM], gq_ref[...])
            s = _bdot_nt(qh, k) * scale
            outs.append(_softmax_attend([(s, v)], sink_ref[h]))
    o_ref[...] = jnp.concatenate(outs, axis=-1).astype(o_ref.dtype)
    kn_ref[...] = jnp.concatenate(kns, axis=-1)


def _attn_a_prompt(proj, n_seq, t, sink, gq, gk):
    qw = A_HEADS * A_HEAD_DIM
    kvw = 2 * A_KV_HEADS * A_HEAD_DIM
    return pl.pallas_call(
        _attn_a_prompt_kernel,
        grid=(n_seq,),
        in_specs=[pl.BlockSpec(memory_space=pltpu.SMEM),
                  pl.BlockSpec((t, qw), lambda b: (b, 0)),
                  pl.BlockSpec((t, kvw), lambda b: (b, qw // kvw)),
                  pl.BlockSpec((1, A_HEAD_DIM), lambda b: (0, 0)),
                  pl.BlockSpec((1, A_HEAD_DIM), lambda b: (0, 0))],
        out_specs=[pl.BlockSpec((t, qw), lambda b: (b, 0)),
                   pl.BlockSpec((t, A_KV_HEADS * A_HEAD_DIM), lambda b: (b, 0))],
        out_shape=[jax.ShapeDtypeStruct((n_seq * t, qw), BF16),
                   jax.ShapeDtypeStruct((n_seq * t, A_KV_HEADS * A_HEAD_DIM), F32)],
        compiler_params=_params(("arbitrary",)),
        name="attn_a_context",
    )(sink, proj, proj, gq.reshape(1, -1), gk.reshape(1, -1))


def _attn_a_sample_kernel(sink_ref, q_ref, kv_ref, kc_ref, vc_ref, gq_ref, gk_ref, cos_ref, sin_ref,
                          o_ref, *, t):
    scale = A_HEAD_DIM ** -0.5
    i = pl.program_id(1)
    win = 3 * A_BLOCK
    q0 = pl.multiple_of(i * A_BLOCK, A_BLOCK)
    k0 = pl.multiple_of(jnp.clip((i - 1) * A_BLOCK, 0, t - win), A_BLOCK)
    q = q_ref[...]
    kv = kv_ref[pl.ds(k0, win), :]
    cq, sq = cos_ref[pl.ds(q0, A_BLOCK), :], sin_ref[pl.ds(q0, A_BLOCK), :]
    ck, sk = cos_ref[pl.ds(k0, win), :], sin_ref[pl.ds(k0, win), :]
    qpos = q0 + lax.broadcasted_iota(jnp.int32, (A_BLOCK, win), 0)
    kpos = k0 + lax.broadcasted_iota(jnp.int32, (A_BLOCK, win), 1)
    mask = jnp.abs(qpos - kpos) <= A_WINDOW
    outs = []
    for hk in range(A_KV_HEADS):
        k = _rope(_rms(kv[:, hk * A_HEAD_DIM:(hk + 1) * A_HEAD_DIM], gk_ref[...]), ck, sk)
        v = kv[:, (A_KV_HEADS + hk) * A_HEAD_DIM:(A_KV_HEADS + hk + 1) * A_HEAD_DIM]
        kc = kc_ref[hk]
        vc = vc_ref[hk]
        for g in range(A_GROUP):
            h = hk * A_GROUP + g
            qh = _rope(_rms(q[:, h * A_HEAD_DIM:(h + 1) * A_HEAD_DIM], gq_ref[...]), cq, sq)
            s1 = jnp.where(mask, _bdot_nt(qh, k) * scale, NEG_INF)
            s2 = _bdot_nt(qh, kc) * scale
            outs.append(_softmax_attend([(s1, v), (s2, vc)], sink_ref[h]))
    o_ref[...] = jnp.concatenate(outs, axis=-1).astype(o_ref.dtype)


def _attn_a_sample(proj, row0, n_seq, t, k_ctx, v_ctx, sink, gq, gk):
    qw = A_HEADS * A_HEAD_DIM
    kvw = 2 * A_KV_HEADS * A_HEAD_DIM
    nqb = t // A_BLOCK
    cos, sin = _rope_tables(t, A_HEAD_DIM)
    past = k_ctx.shape[2]
    return pl.pallas_call(
        functools.partial(_attn_a_sample_kernel, t=t),
        grid=(n_seq, nqb),
        in_specs=[pl.BlockSpec(memory_space=pltpu.SMEM),
                  pl.BlockSpec((A_BLOCK, qw), lambda b, i: (row0 // A_BLOCK + b * nqb + i, 0)),
                  pl.BlockSpec((t, kvw), lambda b, i: (row0 // t + b, qw // kvw)),
                  pl.BlockSpec((None, A_KV_HEADS, past, A_HEAD_DIM), lambda b, i: (b, 0, 0, 0)),
                  pl.BlockSpec((None, A_KV_HEADS, past, A_HEAD_DIM), lambda b, i: (b, 0, 0, 0)),
                  pl.BlockSpec((1, A_HEAD_DIM), lambda b, i: (0, 0)),
                  pl.BlockSpec((1, A_HEAD_DIM), lambda b, i: (0, 0)),
                  pl.BlockSpec((t, A_HEAD_DIM), lambda b, i: (0, 0)),
                  pl.BlockSpec((t, A_HEAD_DIM), lambda b, i: (0, 0))],
        out_specs=pl.BlockSpec((A_BLOCK, qw), lambda b, i: (b * nqb + i, 0)),
        out_shape=jax.ShapeDtypeStruct((n_seq * t, qw), BF16),
        compiler_params=_params(("arbitrary", "arbitrary")),
        name="attn_a_latent",
    )(sink, proj, proj, k_ctx, v_ctx, gq.reshape(1, -1), gk.reshape(1, -1), cos, sin)


def _per_head_lanes(x, fn):
    lane = lax.broadcasted_iota(jnp.int32, x.shape, 1)
    lo = fn(x[:, :B_DK])
    hi = fn(x[:, B_DK:])
    return jnp.where(lane < B_DK, lo, hi)


def _conv_silu(x, w):
    t = x.shape[0]
    row = lax.broadcasted_iota(jnp.int32, x.shape, 0)
    prev = jnp.where(row == 0, 0.0, pltpu.roll(x, 1, 0))
    nxt = jnp.where(row == t - 1, 0.0, pltpu.roll(x, t - 1, 0))
    return _silu(prev * w[0:1, :] + x * w[1:2, :] + nxt * w[2:3, :])


M_SAME, M_TRI, M_TRI_T, M_STRICT, M_BDIAG = range(5)


def _delta_masks(r):
    c = B_CHUNK
    ii = lax.broadcasted_iota(jnp.int32, (r, r), 0)
    jj = lax.broadcasted_iota(jnp.int32, (r, r), 1)
    same = (ii // c) == (jj // c)
    ahead = jnp.where(ii < r // 2, ii - jj, jj - ii)
    tri = jnp.where(same, ahead, -1) >= 0
    tri_t = jnp.where(same, ahead, 1) <= 0
    strict = jnp.where(same, ahead, -1) > 0
    bdiag = (ii // 16) == (jj // 16)
    return [x.astype(F32) for x in (same, tri, tri_t, strict, bdiag)]


def _delta_prepare(q, k, v, g_col, g_row, beta, mask_ref):
    c = B_CHUNK
    r = q.shape[0]
    dot = functools.partial(jnp.dot, preferred_element_type=F32)
    gc_col = jnp.sum(mask_ref[M_TRI] * g_row, axis=1, keepdims=True)
    gc_row = jnp.sum(mask_ref[M_TRI_T] * g_col, axis=0, keepdims=True)
    g_tot = jnp.sum(mask_ref[M_SAME] * g_row, axis=1, keepdims=True)
    ex = jnp.exp((gc_col - gc_row) * mask_ref[M_TRI])
    kb = k * beta
    qk = _bdot_nt(jnp.concatenate([kb, q], axis=0), k)
    m = qk[:r] * (ex * mask_ref[M_STRICT])
    aqk = qk[r:] * (ex * mask_ref[M_TRI])
    dg = m * mask_ref[M_BDIAG]
    off = m - dg
    n1 = -dg
    n1b = n1.astype(BF16)
    n2 = dot(n1b, n1b)
    n2b = n2.astype(BF16)
    t = dot(jnp.concatenate([n1b, n2b], axis=0), n2b)
    xs = n1 + n2 + t[:r]
    n4 = t[r:]
    n4b = n4.astype(BF16)
    t = dot(jnp.concatenate([xs.astype(BF16), n4b], axis=0), n4b)
    xs = xs + n4 + t[:r]
    n8 = t[r:]
    xs = xs + n8 + dot(xs.astype(BF16), n8.astype(BF16))
    xsb = xs.astype(BF16)
    f = -(off + dot(xsb, off.astype(BF16)))
    fb = f.astype(BF16)
    t = dot(fb, jnp.concatenate([xsb, fb], axis=1))
    ys = xs + f + t[:, :r]
    f2 = t[:, r:]
    ts = ys + f2 + dot(f2.astype(BF16), ys.astype(BF16))
    egc = jnp.exp(gc_col)
    rhs = jnp.concatenate([v * beta, kb * egc], axis=-1)
    uw = rhs + _bdot(ts, rhs)

    def block_diag(x):
        return jnp.concatenate([x] * (r // c), axis=-1) * mask_ref[M_SAME]

    kd = k * jnp.exp(g_tot - gc_col)
    kd_t = jnp.concatenate([kd, jnp.zeros_like(kd)], axis=-1).T[:c]
    kd_t = jnp.concatenate([kd_t] * (r // c), axis=0) * mask_ref[M_SAME]
    e_tot = jnp.broadcast_to(jnp.exp(g_tot), v.shape)
    return uw[:, :B_DV], e_tot, block_diag(uw[:, B_DV:]), aqk, block_diag(q * egc), kd_t


def _deltanet_kernel(alog_ref, dtb_ref, q_ref, k_ref, v_ref, z_ref, cwq_ref, cwk_ref, cwv_ref,
                     tail_ref, tailt_ref, onorm_ref, s0f_ref, s0b_ref,
                     o_ref, sf_ref, sb_ref,
                     qc_ref, kc_ref, vc_ref, oacc_ref, u_ref, et_ref, w_ref, aqk_ref, qg_ref, kdt_ref,
                     mask_ref, *, n_chunks):
    hp = pl.program_id(1)
    c = B_CHUNK

    @pl.when(jnp.logical_and(pl.program_id(0) == 0, hp == 0))
    def _():
        for i, x in enumerate(_delta_masks(4 * c)):
            mask_ref[i] = x

    def l2n(x):
        ss = _per_head_lanes(x * x, lambda a: jnp.sum(a, axis=-1, keepdims=True))
        return x * lax.rsqrt(ss + EPS)

    qc_ref[...] = l2n(_conv_silu(q_ref[...], cwq_ref[...])) * (B_DK ** -0.5)
    kc_ref[...] = l2n(_conv_silu(k_ref[...], cwk_ref[...]))
    vc_ref[...] = _conv_silu(v_ref[...], cwv_ref[...])
    oacc_ref[...] = jnp.zeros_like(oacc_ref)

    lane32 = lax.broadcasted_iota(jnp.int32, (c, 4 * B_HEADS), 1)

    def gates(chunk, d, head):
        tail = tail_ref[pl.ds(pl.multiple_of(chunk * c, c), c), :]
        ia = 2 * d * B_HEADS + head
        ib = ia + B_HEADS
        a_col = jnp.sum(jnp.where(lane32 == ia, tail, 0.0), axis=1, keepdims=True)
        b_col = jnp.sum(jnp.where(lane32 == ib, tail, 0.0), axis=1, keepdims=True)
        a_row = tailt_ref[chunk, pl.ds(ia, 1), :]
        na = -jnp.exp(alog_ref[d, head])
        bias = dtb_ref[d, head]
        g_col = na * jax.nn.softplus(a_col + bias)
        g_row = na * jax.nn.softplus(a_row + bias)
        return g_col, g_row, jax.nn.sigmoid(b_col)

    def stacked(ref, chunks):
        parts = []
        for chunk in chunks:
            x = ref[pl.ds(pl.multiple_of(chunk * c, c), c), :]
            parts += [x[:, :B_DK], x[:, B_DK:]]
        return jnp.concatenate(parts, axis=0)

    def prepare(j, carry):
        chunks = (j, n_chunks - 1 - j)
        gs = [gates(chunks[d], d, 2 * hp + hh) for d in range(2) for hh in range(2)]
        g_col = jnp.concatenate([g[0] for g in gs], axis=0)
        g_row = jnp.concatenate([g[1] for g in gs], axis=1)
        beta = jnp.concatenate([g[2] for g in gs], axis=0)
        outs = _delta_prepare(stacked(qc_ref, chunks), stacked(kc_ref, chunks), stacked(vc_ref, chunks),
                              g_col, g_row, beta, mask_ref)
        for ref, x in zip((u_ref, et_ref, w_ref, aqk_ref, qg_ref, kdt_ref), outs):
            ref[j] = x.astype(ref.dtype)
        return carry

    lax.fori_loop(0, n_chunks, prepare, 0, unroll=2)

    def scan(j, s):
        sb = s.astype(BF16)
        delta = u_ref[j] - jnp.dot(w_ref[j], sb, preferred_element_type=F32)
        db = delta.astype(BF16)
        o = (jnp.dot(qg_ref[j], sb, preferred_element_type=F32)
             + jnp.dot(aqk_ref[j], db, preferred_element_type=F32))
        for d, chunk in enumerate((j, n_chunks - 1 - j)):
            rows = pl.ds(pl.multiple_of(chunk * c, c), c)
            oacc_ref[rows, :] += jnp.concatenate([o[2 * d * c:(2 * d + 1) * c],
                                                  o[(2 * d + 1) * c:(2 * d + 2) * c]], axis=-1)
        return s * et_ref[j] + jnp.dot(kdt_ref[j], db, preferred_element_type=F32)

    init = jnp.concatenate([s0f_ref[0], s0f_ref[1], s0b_ref[0], s0b_ref[1]], axis=0)
    fin = lax.fori_loop(0, n_chunks, scan, init)
    sf_ref[0], sf_ref[1], sb_ref[0], sb_ref[1] = (fin[i * B_DK:(i + 1) * B_DK] for i in range(4))

    o = oacc_ref[...]
    ms = _per_head_lanes(o * o, lambda a: jnp.mean(a, axis=-1, keepdims=True))
    o_ref[...] = (o * lax.rsqrt(ms + EPS) * onorm_ref[...] * _silu(z_ref[...])).astype(o_ref.dtype)


def _deltanet(proj, tail, row0, n_seq, t, conv_w, a_log, dt_bias, o_norm, s0_f, s0_b):
    c = B_CHUNK
    n_chunks = t // c
    lw = 2 * B_DK
    col_q = (A_HEADS + 2 * A_KV_HEADS) * A_HEAD_DIM // lw
    nhp = B_HEADS // 2
    rows = tail[row0:row0 + n_seq * t]
    tail_t = rows.reshape(n_seq, n_chunks, c, 4 * B_HEADS).transpose(0, 1, 3, 2)
    onorm2 = jnp.concatenate([o_norm, o_norm]).reshape(1, lw)
    b0 = row0 // t
    seq_blk = lambda off: pl.BlockSpec((t, lw), lambda b, h: (b0 + b, col_q + off + h))
    cw_blk = lambda off: pl.BlockSpec((3, lw), lambda b, h: (0, off + h))
    st_blk = pl.BlockSpec((None, 2, B_DK, B_DV), lambda b, h: (b, h, 0, 0))
    return pl.pallas_call(
        functools.partial(_deltanet_kernel, n_chunks=n_chunks),
        grid=(n_seq, nhp),
        in_specs=[pl.BlockSpec(memory_space=pltpu.SMEM), pl.BlockSpec(memory_space=pltpu.SMEM),
                  seq_blk(0), seq_blk(nhp), seq_blk(2 * nhp), seq_blk(3 * nhp),
                  cw_blk(0), cw_blk(nhp), cw_blk(2 * nhp),
                  pl.BlockSpec((t, 4 * B_HEADS), lambda b, h: (b, 0)),
                  pl.BlockSpec((None, n_chunks, 4 * B_HEADS, c), lambda b, h: (b, 0, 0, 0)),
                  pl.BlockSpec((1, lw), lambda b, h: (0, 0)),
                  st_blk, st_blk],
        out_specs=[pl.BlockSpec((t, lw), lambda b, h: (b, h)), st_blk, st_blk],
        out_shape=[jax.ShapeDtypeStruct((n_seq * t, B_HEADS * B_DV), BF16),
                   jax.ShapeDtypeStruct((n_seq, B_HEADS, B_DK, B_DV), F32),
                   jax.ShapeDtypeStruct((n_seq, B_HEADS, B_DK, B_DV), F32)],
        scratch_shapes=[pltpu.VMEM((t, lw), F32)] * 4 + [pltpu.VMEM((n_chunks, 4 * c, B_DV), F32)] * 2
        + [pltpu.VMEM((n_chunks, 4 * c, 4 * c), BF16)] * 4 + [pltpu.VMEM((5, 4 * c, 4 * c), F32)],
        compiler_params=_params(("arbitrary", "arbitrary")),
        name="deltanet",
    )(a_log, dt_bias, proj, proj, proj, proj, conv_w, conv_w, conv_w, rows, tail_t, onorm2, s0_f, s0_b)


def _lora_norm_kernel(p_ref, gq_ref, gkv_ref, cq_ref, ckv_ref, ckvb_ref):
    p = p_ref[...]
    cq_ref[...] = _rms(p[:, :C_Q_LORA], gq_ref[...]).astype(cq_ref.dtype)
    ckv = _rms(p[:, C_Q_LORA:C_Q_LORA + C_KV_LORA], gkv_ref[...])
    ckv_ref[...] = ckv
    ckvb_ref[...] = ckv.astype(ckvb_ref.dtype)


def _lora_norm(p1, gq, gkv, tm=512):
    n, w = p1.shape
    return pl.pallas_call(
        _lora_norm_kernel,
        grid=(n // tm,),
        in_specs=[pl.BlockSpec((tm, w), lambda i: (i, 0)),
                  pl.BlockSpec((1, C_Q_LORA), lambda i: (0, 0)),
                  pl.BlockSpec((1, C_KV_LORA), lambda i: (0, 0))],
        out_specs=[pl.BlockSpec((tm, C_Q_LORA), lambda i: (i, 0)),
                   pl.BlockSpec((tm, C_KV_LORA), lambda i: (i, 0)),
                   pl.BlockSpec((tm, C_KV_LORA), lambda i: (i, 0))],
        out_shape=[jax.ShapeDtypeStruct((n, C_Q_LORA), BF16),
                   jax.ShapeDtypeStruct((n, C_KV_LORA), F32),
                   jax.ShapeDtypeStruct((n, C_KV_LORA), BF16)],
        compiler_params=_params(("arbitrary",)),
        name="lora_norm",
    )(p1, gq.reshape(1, -1), gkv.reshape(1, -1))


def _mla_head_q(q, h, gq):
    qn, qr = q[:, h * C_QK:h * C_QK + C_NOPE], q[:, h * C_QK + C_NOPE:(h + 1) * C_QK]
    rn = lax.rsqrt((jnp.sum(qn * qn, axis=-1, keepdims=True)
                    + jnp.sum(qr * qr, axis=-1, keepdims=True)) / C_QK + EPS)
    return qn * rn * gq[:, :C_NOPE], qr * rn * gq[:, C_NOPE:]


def _mla_head_k(kv, kr, kr_ss, h, gk):
    kn = kv[:, h * (C_NOPE + C_V):h * (C_NOPE + C_V) + C_NOPE]
    v = kv[:, h * (C_NOPE + C_V) + C_NOPE:(h + 1) * (C_NOPE + C_V)]
    rn = lax.rsqrt((jnp.sum(kn * kn, axis=-1, keepdims=True) + kr_ss) / C_QK + EPS)
    return kn * rn * gk[:, :C_NOPE], kr * rn, v


def _attn_c_prompt_kernel(q_ref, kv_ref, p_ref, gq_ref, gk_ref, o_ref):
    scale = C_QK ** -0.5
    q = q_ref[...]
    kv = kv_ref[...]
    gq, gk = gq_ref[...], gk_ref[...]
    kr_raw = p_ref[...][:, C_Q_LORA + C_KV_LORA:]
    kr_ss = jnp.sum(kr_raw * kr_raw, axis=-1, keepdims=True)
    kr_g = kr_raw * gk[:, C_NOPE:]
    outs = []
    for h in range(C_HEADS):
        qn, qr = _mla_head_q(q, h, gq)
        kn, kr, v = _mla_head_k(kv, kr_g, kr_ss, h, gk)
        s = (_bdot_nt(qn, kn) + _bdot_nt(qr, kr)) * scale
        outs.append(_softmax_attend([(s, v)], None))
    o_ref[...] = jnp.concatenate(outs, axis=-1).astype(o_ref.dtype)


def _attn_c_prompt(q, kv, p1, n_seq, t, gq, gk):
    return pl.pallas_call(
        _attn_c_prompt_kernel,
        grid=(n_seq,),
        in_specs=[pl.BlockSpec((t, q.shape[1]), lambda b: (b, 0)),
                  pl.BlockSpec((t, kv.shape[1]), lambda b: (b, 0)),
                  pl.BlockSpec((t, p1.shape[1]), lambda b: (b, 0)),
                  pl.BlockSpec((1, C_QK), lambda b: (0, 0)),
                  pl.BlockSpec((1, C_QK), lambda b: (0, 0))],
        out_specs=pl.BlockSpec((t, C_HEADS * C_V), lambda b: (b, 0)),
        out_shape=jax.ShapeDtypeStruct((n_seq * t, C_HEADS * C_V), BF16),
        compiler_params=_params(("arbitrary",), VMEM_LIMIT),
        name="attn_c_context",
    )(q, kv, p1, gq.reshape(1, -1), gk.reshape(1, -1))


def _attn_c_sample_kernel(q_ref, kv_ref, p_ref, kvc_ref, krc_ref, gq_ref, gk_ref, cos_ref, sin_ref,
                          o_ref, *, tq):
    scale = C_QK ** -0.5
    i = pl.program_id(1)
    q0 = pl.multiple_of(i * tq, tq)
    q = q_ref[...]
    kv = kv_ref[...]
    kvc = kvc_ref[...]
    gq, gk = gq_ref[...], gk_ref[...]
    cos, sin = cos_ref[...], sin_ref[...]
    cq, sq = cos_ref[pl.ds(q0, tq), :], sin_ref[pl.ds(q0, tq), :]
    kr_raw = p_ref[...][:, C_Q_LORA + C_KV_LORA:]
    kr_ss = jnp.sum(kr_raw * kr_raw, axis=-1, keepdims=True)
    kr_g = _rope(kr_raw * gk[:, C_NOPE:], cos, sin)
    krc_raw = krc_ref[...]
    krc_ss = jnp.sum(krc_raw * krc_raw, axis=-1, keepdims=True)
    krc_g = krc_raw * gk[:, C_NOPE:]
    outs = []
    for h in range(C_HEADS):
        qn, qr = _mla_head_q(q, h, gq)
        qr = _rope(qr, cq, sq)
        kn, kr, v = _mla_head_k(kv, kr_g, kr_ss, h, gk)
        knc, krc, vc = _mla_head_k(kvc, krc_g, krc_ss, h, gk)
        s1 = (_bdot_nt(qn, kn) + _bdot_nt(qr, kr)) * scale
        s2 = (_bdot_nt(qn, knc) + _bdot_nt(qr, krc)) * scale
        outs.append(_softmax_attend([(s1, v), (s2, vc)], None))
    o_ref[...] = jnp.concatenate(outs, axis=-1).astype(o_ref.dtype)


def _attn_c_sample(q, kv, p1, row0, n_seq, t, kr_ctx, gq, gk, tq=256):
    n = p1.shape[0]
    past = kr_ctx.shape[1]
    nq = t // tq
    cos, sin = _rope_tables(t, C_ROPE)
    return pl.pallas_call(
        functools.partial(_attn_c_sample_kernel, tq=tq),
        grid=(n_seq, nq),
        in_specs=[pl.BlockSpec((tq, q.shape[1]), lambda b, i: (row0 // tq + b * nq + i, 0)),
                  pl.BlockSpec((t, kv.shape[1]), lambda b, i: (row0 // t + b, 0)),
                  pl.BlockSpec((t, p1.shape[1]), lambda b, i: (row0 // t + b, 0)),
                  pl.BlockSpec((past, kv.shape[1]), lambda b, i: (n // past + b, 0)),
                  pl.BlockSpec((None, past, C_ROPE), lambda b, i: (b, 0, 0)),
                  pl.BlockSpec((1, C_QK), lambda b, i: (0, 0)),
                  pl.BlockSpec((1, C_QK), lambda b, i: (0, 0)),
                  pl.BlockSpec((t, C_ROPE), lambda b, i: (0, 0)),
                  pl.BlockSpec((t, C_ROPE), lambda b, i: (0, 0))],
        out_specs=pl.BlockSpec((tq, C_HEADS * C_V), lambda b, i: (b * nq + i, 0)),
        out_shape=jax.ShapeDtypeStruct((n_seq * t, C_HEADS * C_V), BF16),
        compiler_params=_params(("arbitrary", "arbitrary"), VMEM_LIMIT),
        name="attn_c_latent",
    )(q, kv, p1, kv, kr_ctx, gq.reshape(1, -1), gk.reshape(1, -1), cos, sin)


ROUTE_TILE = 512
META_LANES = 128


def _router_kernel(x_ref, g_ref, sc_ref, sh_ref, wr_ref, br_ref, h_ref, meta_ref, cnt_ref, run_ref):
    @pl.when(pl.program_id(0) == 0)
    def _():
        run_ref[...] = jnp.zeros_like(run_ref)

    h = _rms(x_ref[...], g_ref[...]) * (1 + sc_ref[...]) + sh_ref[...]
    h_ref[...] = h
    tm = h.shape[0]
    logits = _hdot(h, wr_ref[...]) + br_ref[...]
    lane = lax.broadcasted_iota(jnp.int32, logits.shape, 1)
    work = logits
    picks, tops, ids = [], [], []
    for _ in range(TOP_K):
        m = work.max(axis=-1, keepdims=True)
        first = jnp.min(jnp.where(work == m, lane, N_EXPERTS), axis=-1, keepdims=True)
        pick = lane == first
        picks.append(pick)
        tops.append(m)
        ids.append(first)
        work = jnp.where(pick, -jnp.inf, work)
    sel = sum(p.astype(F32) for p in picks)
    earlier = (lax.broadcasted_iota(jnp.int32, (tm, tm), 0)
               > lax.broadcasted_iota(jnp.int32, (tm, tm), 1)).astype(BF16)
    before = run_ref[...] + jnp.dot(earlier, sel.astype(BF16), preferred_element_type=F32)
    ws = [jnp.exp(t - tops[0]) for t in tops]
    den = sum(ws)
    mlane = lax.broadcasted_iota(jnp.int32, (tm, META_LANES), 1)
    meta = jnp.zeros((tm, META_LANES), F32)
    for k in range(TOP_K):
        rank = jnp.sum(jnp.where(picks[k], before, 0.0), axis=-1, keepdims=True)
        meta = jnp.where(mlane == k, ids[k].astype(F32), meta)
        meta = jnp.where(mlane == TOP_K + k, rank, meta)
        meta = jnp.where(mlane == 2 * TOP_K + k, ws[k] / den, meta)
    meta_ref[...] = meta
    run_ref[...] += jnp.sum(sel, axis=0, keepdims=True)
    cnt_ref[...] = run_ref[...]


def _router(x, gain, scale, shift, w_router, b_router, layer, n_p, t_s, tm=512):
    n, d = x.shape
    e = w_router.shape[-1]
    group = _group_of_tile(tm, n_p, t_s)
    return pl.pallas_call(
        _router_kernel,
        grid=(n // tm,),
        in_specs=[pl.BlockSpec((tm, d), lambda i: (i, 0)),
                  pl.BlockSpec((1, d), lambda i: (0, 0)),
                  pl.BlockSpec((None, 1, d), lambda i: (group(i), 0, 0)),
                  pl.BlockSpec((None, 1, d), lambda i: (group(i), 0, 0)),
                  pl.BlockSpec((None, d, e), lambda i: (layer, 0, 0)),
                  pl.BlockSpec((None, 1, e), lambda i: (layer, 0, 0))],
        out_specs=[pl.BlockSpec((tm, d), lambda i: (i, 0)),
                   pl.BlockSpec((tm, META_LANES), lambda i: (i, 0)),
                   pl.BlockSpec((1, e), lambda i: (0, 0))],
        out_shape=[jax.ShapeDtypeStruct((n, d), F32),
                   jax.ShapeDtypeStruct((n, META_LANES), F32),
                   jax.ShapeDtypeStruct((1, e), F32)],
        scratch_shapes=[pltpu.VMEM((1, e), F32)],
        compiler_params=_params(("arbitrary",)),
        name="router",
    )(x, gain.reshape(1, d), scale.reshape(N_GROUPS, 1, d), shift.reshape(N_GROUPS, 1, d),
      w_router, b_router.reshape(-1, 1, e))


def _route_plan(meta, cnt, n_tiles):
    ids = meta[:, :TOP_K].astype(jnp.int32)
    ranks = meta[:, TOP_K:2 * TOP_K].astype(jnp.int32)
    gates = meta[:, 2 * TOP_K:3 * TOP_K]
    counts = cnt[0].astype(jnp.int32)
    padded = (counts + ROUTE_TILE - 1) // ROUTE_TILE * ROUTE_TILE
    ends = jnp.cumsum(padded)
    offs = ends - padded
    pos = offs[ids] + ranks
    tile_start = jnp.arange(n_tiles, dtype=jnp.int32) * ROUTE_TILE
    n_valid = ends[-1] // ROUTE_TILE
    te = jnp.minimum(jnp.sum(ends[None, :] <= tile_start[:, None], axis=1), N_EXPERTS - 1)
    te = jnp.where(tile_start < ends[-1], te, te[jnp.maximum(n_valid - 1, 0)]).astype(jnp.int32)
    rows = jnp.clip((offs + counts)[te] - tile_start, 0, ROUTE_TILE).astype(jnp.int32)
    ragged = jnp.where(counts % ROUTE_TILE != 0, ends // ROUTE_TILE - 1, -1)
    tail = tile_start[n_tiles - N_EXPERTS:] // ROUTE_TILE
    fill = jnp.concatenate([ragged, jnp.where(tail >= n_valid, tail, -1)]).astype(jnp.int32)
    return pos, gates, te, rows, n_valid.reshape(1).astype(jnp.int32), fill


def _dispatch_kernel(fill_ref, pos_ref, h_ref, xs_ref, zero_ref, sem, zsem):
    tm = h_ref.shape[0]

    @pl.when(pl.program_id(0) == 0)
    def _():
        zero_ref[...] = jnp.zeros_like(zero_ref)

        def fill_copy(j):
            row0 = pl.multiple_of(fill_ref[j] * ROUTE_TILE, ROUTE_TILE)
            return pltpu.make_async_copy(zero_ref, xs_ref.at[pl.ds(row0, ROUTE_TILE), :], zsem)

        def start(j, carry):
            @pl.when(fill_ref[j] >= 0)
            def _():
                fill_copy(j).start()
            return carry

        def wait(j, carry):
            @pl.when(fill_ref[j] >= 0)
            def _():
                fill_copy(j).wait()
            return carry

        lax.fori_loop(0, fill_ref.shape[0], start, 0)
        lax.fori_loop(0, fill_ref.shape[0], wait, 0)

    def body(r, carry):
        for k in range(TOP_K):
            p = pos_ref[0, r * TOP_K + k]
            pltpu.make_async_copy(h_ref.at[pl.ds(r, 1), :], xs_ref.at[pl.ds(p, 1), :], sem).start()
        return carry

    lax.fori_loop(0, tm, body, 0, unroll=4)
    for _ in range(TOP_K):
        pltpu.make_async_copy(h_ref, h_ref, sem).wait()


def _dispatch(h, pos, fill, n_slots, tm=512):
    n, d = h.shape
    grid_spec = pltpu.PrefetchScalarGridSpec(
        num_scalar_prefetch=1,
        grid=(n // tm,),
        in_specs=[pl.BlockSpec((None, 1, tm * TOP_K), lambda i, fl: (i, 0, 0), memory_space=pltpu.SMEM),
                  pl.BlockSpec((tm, d), lambda i, fl: (i, 0))],
        out_specs=pl.BlockSpec(memory_space=pl.ANY),
        scratch_shapes=[pltpu.VMEM((ROUTE_TILE, d), h.dtype), pltpu.SemaphoreType.DMA(()),
                        pltpu.SemaphoreType.DMA(())])
    return pl.pallas_call(
        _dispatch_kernel,
        grid_spec=grid_spec,
        out_shape=jax.ShapeDtypeStruct((n_slots, d), h.dtype),
        compiler_params=_params(("arbitrary",)),
        name="moe_dispatch",
    )(fill, pos.reshape(n // tm, 1, tm * TOP_K), h)


def _experts_kernel(te_ref, rows_ref, nv_ref, x_ref, wgu_ref, bgu_ref, wd_ref, bd_ref, y_ref,
                    wgub_ref, wdb_ref, *, d_ff):
    i = pl.program_id(0)
    valid = i < nv_ref[0]
    fresh = jnp.logical_or(i == 0, te_ref[i] != te_ref[jnp.maximum(i - 1, 0)])

    @pl.when(jnp.logical_and(valid, fresh))
    def _():
        wgub_ref[...] = wgu_ref[...].astype(BF16)
        wdb_ref[...] = wd_ref[...].astype(BF16)

    @pl.when(valid)
    def _():
        row = lax.broadcasted_iota(jnp.int32, x_ref.shape, 0)
        x = jnp.where(row < rows_ref[i], x_ref[...], 0.0).astype(BF16)
        gu = jnp.dot(x, wgub_ref[...], preferred_element_type=F32) + bgu_ref[...]
        gate = jnp.minimum(gu[:, :d_ff], SWIGLU_LIMIT)
        up = jnp.clip(gu[:, d_ff:], -SWIGLU_LIMIT, SWIGLU_LIMIT)
        act = (up + 1) * gate * jax.nn.sigmoid(SWIGLU_ALPHA * gate)
        y_ref[...] = jnp.dot(act.astype(BF16), wdb_ref[...], preferred_element_type=F32) + bd_ref[...]

    @pl.when(jnp.logical_not(valid))
    def _():
        y_ref[...] = jnp.zeros_like(y_ref)


def _experts(xs, te, rows, n_valid, w_gu, b_gu, w_down, b_down, layer):
    n_slots, d = xs.shape
    _, e, _, two_ff = w_gu.shape
    last = lambda i, nv: jnp.minimum(i, nv[0] - 1)
    grid_spec = pltpu.PrefetchScalarGridSpec(
        num_scalar_prefetch=3,
        grid=(n_slots // ROUTE_TILE,),
        in_specs=[pl.BlockSpec((ROUTE_TILE, d), lambda i, te, rw, nv: (last(i, nv), 0)),
                  pl.BlockSpec((None, None, d, two_ff), lambda i, te, rw, nv: (layer, te[i], 0, 0)),
                  pl.BlockSpec((None, None, 1, two_ff), lambda i, te, rw, nv: (layer, te[i], 0, 0)),
                  pl.BlockSpec((None, None, two_ff // 2, d), lambda i, te, rw, nv: (layer, te[i], 0, 0)),
                  pl.BlockSpec((None, None, 1, d), lambda i, te, rw, nv: (layer, te[i], 0, 0))],
        out_specs=pl.BlockSpec((ROUTE_TILE, d), lambda i, te, rw, nv: (i, 0)),
        scratch_shapes=[pltpu.VMEM((d, two_ff), BF16), pltpu.VMEM((two_ff // 2, d), BF16)])
    return pl.pallas_call(
        functools.partial(_experts_kernel, d_ff=two_ff // 2),
        grid_spec=grid_spec,
        out_shape=jax.ShapeDtypeStruct((n_slots, d), F32),
        compiler_params=_params(("arbitrary",), VMEM_LIMIT),
        name="moe_experts",
    )(te, rows, n_valid, xs, w_gu, b_gu.reshape(b_gu.shape[0], e, 1, two_ff), w_down,
      b_down.reshape(b_down.shape[0], e, 1, d))


def _combine_kernel(pos_ref, x_ref, gate_ref, gts_ref, y_ref, o_ref, buf_ref, sem):
    tm = x_ref.shape[0]

    def body(r, carry):
        for k in range(TOP_K):
            p = pos_ref[0, r * TOP_K + k]
            pltpu.make_async_copy(y_ref.at[pl.ds(p, 1), :], buf_ref.at[k, pl.ds(r, 1), :], sem).start()
        return carry

    lax.fori_loop(0, tm, body, 0, unroll=4)
    for k in range(TOP_K):
        pltpu.make_async_copy(buf_ref.at[k], buf_ref.at[k], sem).wait()
    g = gts_ref[...]
    acc = g[:, 0:1] * buf_ref[0]
    for k in range(1, TOP_K):
        acc = acc + g[:, k:k + 1] * buf_ref[k]
    o_ref[...] = x_ref[...] + gate_ref[...] * acc


def _combine(y, pos, gates, x, gate, n_p, t_s, tm=256):
    n, d = x.shape
    group = _group_of_tile(tm, n_p, t_s)
    return pl.pallas_call(
        _combine_kernel,
        grid=(n // tm,),
        in_specs=[pl.BlockSpec((None, 1, tm * TOP_K), lambda i: (i, 0, 0), memory_space=pltpu.SMEM),
                  pl.BlockSpec((tm, d), lambda i: (i, 0)),
                  pl.BlockSpec((None, 1, d), lambda i: (group(i), 0, 0)),
                  pl.BlockSpec((tm, TOP_K), lambda i: (i, 0)),
                  pl.BlockSpec(memory_space=pl.ANY)],
        out_specs=pl.BlockSpec((tm, d), lambda i: (i, 0)),
        out_shape=jax.ShapeDtypeStruct((n, d), F32),
        scratch_shapes=[pltpu.VMEM((TOP_K, tm, d), F32), pltpu.SemaphoreType.DMA(())],
        compiler_params=_params(("arbitrary",)),
        name="moe_combine",
    )(pos.reshape(n // tm, 1, tm * TOP_K), x, gate.reshape(N_GROUPS, 1, d), gates, y)


def _moe(x, gain, scale, shift, gate, w_router, b_router, w_gu, b_gu, w_down, b_down, layer, n_p, t_s):
    n = x.shape[0]
    n_slots = n * TOP_K + N_EXPERTS * ROUTE_TILE
    h, meta, cnt = _router(x, gain, scale, shift, w_router, b_router, layer, n_p, t_s)
    pos, gates, te, rows, n_valid, fill = _route_plan(meta, cnt, n_slots // ROUTE_TILE)
    xs = _dispatch(h, pos, fill, n_slots)
    y = _experts(xs, te, rows, n_valid, w_gu, b_gu, w_down, b_down, layer)
    return _combine(y, pos, gates, x, gate, n_p, t_s)


def kernel(x_prompt, x_sample, c, cache_a_k, cache_a_v, state_b_fwd, state_b_bwd, cache_c_ckv,
           cache_c_krope, c_ctx, w_mod, b_mod, norm_mix, norm_ffn, e_w_in, e_w_out, e_a_qnorm,
           e_a_knorm, e_a_sink, e_b_conv, e_b_alog, e_b_dtbias, e_b_onorm, o_w_in, o_q_lora_norm,
           o_kv_lora_norm, o_w_uq, o_w_ukv, o_qnorm, o_knorm, o_w_out, moe_w_router, moe_b_router,
           moe_w_gu, moe_b_gu, moe_w_down, moe_b_down):
    bp, tp, d = x_prompt.shape
    bs, ts, _ = x_sample.shape
    depth = w_mod.shape[0]
    n_p, n_s = bp * tp, bs * ts
    n = n_p + n_s
    assert bs + 1 <= N_GROUPS and ts % 512 == 0 and n_p % ts == 0

    x = jnp.concatenate([x_prompt.reshape(n_p, d), x_sample.reshape(n_s, d)], axis=0)
    cond = jnp.concatenate([c_ctx[None], c, jnp.zeros((N_GROUPS - 1 - bs, d), F32)], axis=0)
    mod = _adaln(cond, w_mod, b_mod)

    new_a_k, new_a_v, new_b_fwd, new_b_bwd, new_c_ckv, new_c_krope = [], [], [], [], [], []
    for layer in range(depth):
        sh1, sc1, g1, sh2, sc2, g2 = (mod[layer, j] for j in range(6))
        h = _modulate(x, norm_mix[layer], sc1, sh1, n_p, ts)
        i = layer // 2
        if layer % 2 == 0:
            main_w = (A_HEADS + 2 * A_KV_HEADS) * A_HEAD_DIM + 4 * B_HEADS * B_DK
            proj = _matmul(h, e_w_in, i, 0, main_w // 2, main_w, name="even_in_proj")
            tail = _matmul(h, e_w_in[i][None, :, main_w:], 0, 0, 4 * B_HEADS, 4 * B_HEADS,
                           name="even_gate_proj")
            oa_p, kn_p = _attn_a_prompt(proj, bp, tp, e_a_sink[i], e_a_qnorm[i], e_a_knorm[i])
            oa_s = _attn_a_sample(proj, n_p, bs, ts, cache_a_k[:, i], cache_a_v[:, i],
                                  e_a_sink[i], e_a_qnorm[i], e_a_knorm[i])
            zeros = jnp.zeros((bp, B_HEADS, B_DK, B_DV), F32)
            ob_p, s_f, s_b = _deltanet(proj, tail, 0, bp, tp, e_b_conv[i], e_b_alog[i],
                                       e_b_dtbias[i], e_b_onorm[i], zeros, zeros)
            ob_s, _, _ = _deltanet(proj, tail, n_p, bs, ts, e_b_conv[i], e_b_alog[i],
                                   e_b_dtbias[i], e_b_onorm[i], state_b_fwd[:, i], state_b_bwd[:, i])
            mix = jnp.concatenate([jnp.concatenate([oa_p, ob_p], axis=1),
                                   jnp.concatenate([oa_s, ob_s], axis=1)], axis=0)
            x = _matmul_residual(mix, e_w_out, i, x, g1, n_p, ts)
            kw = A_KV_HEADS * A_HEAD_DIM
            new_a_k.append(kn_p.reshape(bp, tp, A_KV_HEADS, A_HEAD_DIM).transpose(0, 2, 1, 3))
            v_p = proj[:n_p, A_HEADS * A_HEAD_DIM + kw:A_HEADS * A_HEAD_DIM + 2 * kw]
            new_a_v.append(v_p.reshape(bp, tp, A_KV_HEADS, A_HEAD_DIM).transpose(0, 2, 1, 3))
            new_b_fwd.append(s_f)
            new_b_bwd.append(s_b)
        else:
            p1 = _matmul(h, o_w_in, i, 0, o_w_in.shape[-1], o_w_in.shape[-1], name="odd_in_proj")
            cq, ckv, ckv_b = _lora_norm(p1, o_q_lora_norm[i], o_kv_lora_norm[i])
            q = _matmul(cq, o_w_uq, i, 0, o_w_uq.shape[-1] // 2, o_w_uq.shape[-1], name="odd_uq")
            ckv_all = jnp.concatenate([ckv_b, cache_c_ckv[:, i].reshape(-1, C_KV_LORA).astype(BF16)], 0)
            kv = _matmul(ckv_all, o_w_ukv, i, 0, o_w_ukv.shape[-1] // 2, o_w_ukv.shape[-1],
                         name="odd_ukv")
            o_p = _attn_c_prompt(q, kv, p1, bp, tp, o_qnorm[i], o_knorm[i])
            o_s = _attn_c_sample(q, kv, p1, n_p, bs, ts, cache_c_krope[:, i], o_qnorm[i], o_knorm[i])
            x = _matmul_residual(jnp.concatenate([o_p, o_s], axis=0), o_w_out, i, x, g1, n_p, ts)
            new_c_ckv.append(ckv[:n_p].reshape(bp, tp, C_KV_LORA))
            new_c_krope.append(p1[:n_p, C_Q_LORA + C_KV_LORA:].reshape(bp, tp, C_ROPE))
        x = _moe(x, norm_ffn[layer], sc2, sh2, g2, moe_w_router, moe_b_router, moe_w_gu, moe_b_gu,
                 moe_w_down, moe_b_down, layer, n_p, ts)

    return (x[:n_p].reshape(bp, tp, d), x[n_p:].reshape(bs, ts, d),
            jnp.stack(new_a_k, axis=1), jnp.stack(new_a_v, axis=1),
            jnp.stack(new_b_fwd, axis=1), jnp.stack(new_b_bwd, axis=1),
            jnp.stack(new_c_ckv, axis=1), jnp.stack(new_c_krope, axis=1))
```

```python
import functools
import math

import numpy as np
import jax
import jax.numpy as jnp
from jax import lax
from jax.experimental import pallas as pl
from jax.experimental.pallas import tpu as pltpu

F32 = jnp.float32
BF16 = jnp.bfloat16
HIGHEST = lax.Precision.HIGHEST

EPS = 1e-6
NEG_INF = -1e30
ROPE_BASE = 10000.0
GRID_W = 64
N_GROUPS = 8

A_HEADS, A_KV_HEADS, A_GROUP, A_HEAD_DIM, A_WINDOW, A_BLOCK = 8, 2, 4, 64, 128, 128
B_HEADS, B_DK, B_DV, B_CHUNK = 8, 64, 64, 64
C_HEADS, C_NOPE, C_ROPE, C_V, C_Q_LORA, C_KV_LORA = 16, 64, 32, 64, 384, 256
C_QK = C_NOPE + C_ROPE
N_EXPERTS, TOP_K = 32, 4
SWIGLU_LIMIT, SWIGLU_ALPHA = 7.0, 1.702

VMEM_LIMIT = 56 * 1024 * 1024


def _params(sem, vmem=None):
    return pltpu.CompilerParams(dimension_semantics=sem, vmem_limit_bytes=vmem)


def _bdot(a, b):
    return jnp.dot(a.astype(BF16), b.astype(BF16), preferred_element_type=F32)


def _bdot_nt(a, b):
    return lax.dot_general(a.astype(BF16), b.astype(BF16), (((1,), (1,)), ((), ())),
                           preferred_element_type=F32)


def _bdot_tn(a, b):
    return lax.dot_general(a.astype(BF16), b.astype(BF16), (((0,), (0,)), ((), ())),
                           preferred_element_type=F32)


def _hdot(a, b):
    return jnp.dot(a, b, preferred_element_type=F32, precision=HIGHEST)


def _hdot_nt(a, b):
    return lax.dot_general(a, b, (((1,), (1,)), ((), ())), preferred_element_type=F32,
                           precision=HIGHEST)


def _rms(x, gain):
    return x * lax.rsqrt(jnp.mean(x * x, axis=-1, keepdims=True) + EPS) * gain


def _silu(x):
    return x * jax.nn.sigmoid(x)


def _group_of_tile(tm, n_p, t_s):
    def group(i):
        r = i * tm
        return jnp.where(r < n_p, 0, 1 + (r - n_p) // t_s)
    return group


def _adaln_kernel(cond_ref, w_ref, b_ref, o_ref):
    o_ref[...] = _bdot(_silu(cond_ref[...]), w_ref[...]) + b_ref[...]


def _adaln(cond, w_mod, b_mod):
    depth, d, _ = w_mod.shape
    return pl.pallas_call(
        _adaln_kernel,
        grid=(depth, 6),
        in_specs=[pl.BlockSpec((N_GROUPS, d), lambda l, j: (0, 0)),
                  pl.BlockSpec((None, d, d), lambda l, j: (l, 0, j)),
                  pl.BlockSpec((None, 1, d), lambda l, j: (l, 0, j))],
        out_specs=pl.BlockSpec((None, None, N_GROUPS, d), lambda l, j: (l, j, 0, 0)),
        out_shape=jax.ShapeDtypeStruct((depth, 6, N_GROUPS, d), F32),
        compiler_params=_params(("arbitrary", "arbitrary")),
        name="adaln",
    )(cond, w_mod, b_mod.reshape(depth, 1, 6 * d))


def _modulate_kernel(x_ref, g_ref, sc_ref, sh_ref, o_ref):
    y = _rms(x_ref[...], g_ref[...])
    o_ref[...] = (y * (1 + sc_ref[...]) + sh_ref[...]).astype(o_ref.dtype)


def _modulate(x, gain, scale, shift, n_p, t_s, tm=512):
    n, d = x.shape
    group = _group_of_tile(tm, n_p, t_s)
    return pl.pallas_call(
        _modulate_kernel,
        grid=(n // tm,),
        in_specs=[pl.BlockSpec((tm, d), lambda i: (i, 0)),
                  pl.BlockSpec((1, d), lambda i: (0, 0)),
                  pl.BlockSpec((None, 1, d), lambda i: (group(i), 0, 0)),
                  pl.BlockSpec((None, 1, d), lambda i: (group(i), 0, 0))],
        out_specs=pl.BlockSpec((tm, d), lambda i: (i, 0)),
        out_shape=jax.ShapeDtypeStruct((n, d), BF16),
        compiler_params=_params(("arbitrary",)),
        name="modulate",
    )(x, gain.reshape(1, d), scale.reshape(N_GROUPS, 1, d), shift.reshape(N_GROUPS, 1, d))


def _mm_kernel(x_ref, w_ref, o_ref, wb_ref):
    @pl.when(pl.program_id(1) == 0)
    def _():
        wb_ref[...] = w_ref[...].astype(BF16)
    o_ref[...] = jnp.dot(x_ref[...], wb_ref[...], preferred_element_type=F32).astype(o_ref.dtype)


def _matmul(x, w3, layer, col0_blocks, tn, n_out, out_dtype=F32, tm=512, name="matmul"):
    n, k = x.shape
    return pl.pallas_call(
        _mm_kernel,
        grid=(n_out // tn, n // tm),
        in_specs=[pl.BlockSpec((tm, k), lambda j, i: (i, 0)),
                  pl.BlockSpec((None, k, tn), lambda j, i: (layer, 0, col0_blocks + j))],
        out_specs=pl.BlockSpec((tm, tn), lambda j, i: (i, j)),
        out_shape=jax.ShapeDtypeStruct((n, n_out), out_dtype),
        scratch_shapes=[pltpu.VMEM((k, tn), BF16)],
        compiler_params=_params(("arbitrary", "arbitrary"), VMEM_LIMIT),
        name=name,
    )(x, w3)


def _mm_res_kernel(x_ref, w_ref, res_ref, gate_ref, o_ref, wb_ref):
    @pl.when(pl.program_id(0) == 0)
    def _():
        wb_ref[...] = w_ref[...].astype(BF16)
    y = jnp.dot(x_ref[...], wb_ref[...], preferred_element_type=F32)
    o_ref[...] = res_ref[...] + gate_ref[...] * y


def _matmul_residual(mix, w3, layer, res, gate, n_p, t_s, tm=512):
    n, k = mix.shape
    d = res.shape[1]
    group = _group_of_tile(tm, n_p, t_s)
    return pl.pallas_call(
        _mm_res_kernel,
        grid=(n // tm,),
        in_specs=[pl.BlockSpec((tm, k), lambda i: (i, 0)),
                  pl.BlockSpec((None, k, d), lambda i: (layer, 0, 0)),
                  pl.BlockSpec((tm, d), lambda i: (i, 0)),
                  pl.BlockSpec((None, 1, d), lambda i: (group(i), 0, 0))],
        out_specs=pl.BlockSpec((tm, d), lambda i: (i, 0)),
        out_shape=jax.ShapeDtypeStruct((n, d), F32),
        scratch_shapes=[pltpu.VMEM((k, d), BF16)],
        compiler_params=_params(("arbitrary",), VMEM_LIMIT),
        name="out_proj_residual",
    )(mix, w3, res, gate.reshape(N_GROUPS, 1, d))


def _rope_tables(t_len, d):
    half, quarter = d // 2, d // 4
    pos = np.arange(t_len)
    row, col = pos // GRID_W, pos % GRID_W
    inv = ROPE_BASE ** (-np.arange(quarter, dtype=np.float64) / quarter)
    ang_r = row[:, None] * inv[None, :]
    ang_c = col[:, None] * inv[None, :]
    cos = np.concatenate([np.cos(ang_r), np.cos(ang_r), np.cos(ang_c), np.cos(ang_c)], axis=1)
    sin = np.concatenate([-np.sin(ang_r), np.sin(ang_r), -np.sin(ang_c), np.sin(ang_c)], axis=1)
    return jnp.asarray(cos, F32), jnp.asarray(sin, F32)


def _swap_pairs(x):
    q = x.shape[-1] // 4
    return jnp.concatenate([x[:, q:2 * q], x[:, :q], x[:, 3 * q:], x[:, 2 * q:3 * q]], axis=-1)


def _rope(x, cos, sin):
    return x * cos + _swap_pairs(x) * sin


def _softmax_attend(parts, sink):
    m = parts[0][0].max(axis=-1, keepdims=True)
    for s, _ in parts[1:]:
        m = jnp.maximum(m, s.max(axis=-1, keepdims=True))
    if sink is not None:
        m = jnp.maximum(m, sink)
    den = jnp.exp(sink - m) if sink is not None else 0.0
    acc = None
    for s, v in parts:
        p = jnp.exp(s - m)
        den = den + p.sum(axis=-1, keepdims=True)
        o = _bdot(p, v)
        acc = o if acc is None else acc + o
    return acc / den


def _group_sinks(sink_ref, hk, rows):
    head = lax.broadcasted_iota(jnp.int32, (A_GROUP * rows, 1), 0) // rows
    col = jnp.full((A_GROUP * rows, 1), sink_ref[hk * A_GROUP], F32)
    for g in range(1, A_GROUP):
        col = jnp.where(head == g, sink_ref[hk * A_GROUP + g], col)
    return col


def _attn_a_prompt_kernel(sink_ref, q_ref, kv_ref, gq_ref, gk_ref, o_ref, kn_ref):
    scale = A_HEAD_DIM ** -0.5
    q = q_ref[...]
    kv = kv_ref[...]
    t = q.shape[0]
    outs = []
    kns = []
    for hk in range(A_KV_HEADS):
        k = _rms(kv[:, hk * A_HEAD_DIM:(hk + 1) * A_HEAD_DIM], gk_ref[...])
        v = kv[:, (A_KV_HEADS + hk) * A_HEAD_DIM:(A_KV_HEADS + hk + 1) * A_HEAD_DIM]
        kns.append(k)
        qg = jnp.concatenate([_rms(q[:, h * A_HEAD_DIM:(h + 1) * A_HEAD_DIM], gq_ref[...])
                              for h in range(hk * A_GROUP, (hk + 1) * A_GROUP)], axis=0)
        s = _bdot_nt(qg, k) * scale
        o = _softmax_attend([(s, v)], _group_sinks(sink_ref, hk, t))
        outs += [o[g * t:(g + 1) * t] for g in range(A_GROUP)]
    o_ref[...] = jnp.concatenate(outs, axis=-1).astype(o_ref.dtype)
    kn_ref[...] = jnp.concatenate(kns, axis=-1)


def _attn_a_prompt(proj, n_seq, t, sink, gq, gk):
    qw = A_HEADS * A_HEAD_DIM
    kvw = 2 * A_KV_HEADS * A_HEAD_DIM
    return pl.pallas_call(
        _attn_a_prompt_kernel,
        grid=(n_seq,),
        in_specs=[pl.BlockSpec(memory_space=pltpu.SMEM),
                  pl.BlockSpec((t, qw), lambda b: (b, 0)),
                  pl.BlockSpec((t, kvw), lambda b: (b, qw // kvw)),
                  pl.BlockSpec((1, A_HEAD_DIM), lambda b: (0, 0)),
                  pl.BlockSpec((1, A_HEAD_DIM), lambda b: (0, 0))],
        out_specs=[pl.BlockSpec((t, qw), lambda b: (b, 0)),
                   pl.BlockSpec((t, A_KV_HEADS * A_HEAD_DIM), lambda b: (b, 0))],
        out_shape=[jax.ShapeDtypeStruct((n_seq * t, qw), BF16),
                   jax.ShapeDtypeStruct((n_seq * t, A_KV_HEADS * A_HEAD_DIM), F32)],
        compiler_params=_params(("arbitrary",)),
        name="attn_a_context",
    )(sink, proj, proj, gq.reshape(1, -1), gk.reshape(1, -1))


def _attn_a_sample_kernel(sink_ref, q_ref, kv_ref, kc_ref, vc_ref, gq_ref, gk_ref, cos_ref, sin_ref,
                          o_ref, *, t):
    scale = A_HEAD_DIM ** -0.5
    i = pl.program_id(1)
    win = 3 * A_BLOCK
    q0 = pl.multiple_of(i * A_BLOCK, A_BLOCK)
    k0 = pl.multiple_of(jnp.clip((i - 1) * A_BLOCK, 0, t - win), A_BLOCK)
    q = q_ref[...]
    kv = kv_ref[pl.ds(k0, win), :]
    cq, sq = cos_ref[pl.ds(q0, A_BLOCK), :], sin_ref[pl.ds(q0, A_BLOCK), :]
    ck, sk = cos_ref[pl.ds(k0, win), :], sin_ref[pl.ds(k0, win), :]
    rows = A_GROUP * A_BLOCK
    qpos = q0 + lax.broadcasted_iota(jnp.int32, (rows, win), 0) % A_BLOCK
    kpos = k0 + lax.broadcasted_iota(jnp.int32, (rows, win), 1)
    mask = jnp.abs(qpos - kpos) <= A_WINDOW
    outs = []
    for hk in range(A_KV_HEADS):
        k = _rope(_rms(kv[:, hk * A_HEAD_DIM:(hk + 1) * A_HEAD_DIM], gk_ref[...]), ck, sk)
        v = kv[:, (A_KV_HEADS + hk) * A_HEAD_DIM:(A_KV_HEADS + hk + 1) * A_HEAD_DIM]
        qg = jnp.concatenate([_rope(_rms(q[:, h * A_HEAD_DIM:(h + 1) * A_HEAD_DIM], gq_ref[...]), cq, sq)
                              for h in range(hk * A_GROUP, (hk + 1) * A_GROUP)], axis=0)
        s1 = jnp.where(mask, _bdot_nt(qg, k) * scale, NEG_INF)
        s2 = _bdot_nt(qg, kc_ref[hk]) * scale
        o = _softmax_attend([(s1, v), (s2, vc_ref[hk])], _group_sinks(sink_ref, hk, A_BLOCK))
        outs += [o[g * A_BLOCK:(g + 1) * A_BLOCK] for g in range(A_GROUP)]
    o_ref[...] = jnp.concatenate(outs, axis=-1).astype(o_ref.dtype)


def _attn_a_sample(proj, row0, n_seq, t, k_ctx, v_ctx, sink, gq, gk):
    qw = A_HEADS * A_HEAD_DIM
    kvw = 2 * A_KV_HEADS * A_HEAD_DIM
    nqb = t // A_BLOCK
    cos, sin = _rope_tables(t, A_HEAD_DIM)
    past = k_ctx.shape[2]
    return pl.pallas_call(
        functools.partial(_attn_a_sample_kernel, t=t),
        grid=(n_seq, nqb),
        in_specs=[pl.BlockSpec(memory_space=pltpu.SMEM),
                  pl.BlockSpec((A_BLOCK, qw), lambda b, i: (row0 // A_BLOCK + b * nqb + i, 0)),
                  pl.BlockSpec((t, kvw), lambda b, i: (row0 // t + b, qw // kvw)),
                  pl.BlockSpec((None, A_KV_HEADS, past, A_HEAD_DIM), lambda b, i: (b, 0, 0, 0)),
                  pl.BlockSpec((None, A_KV_HEADS, past, A_HEAD_DIM), lambda b, i: (b, 0, 0, 0)),
                  pl.BlockSpec((1, A_HEAD_DIM), lambda b, i: (0, 0)),
                  pl.BlockSpec((1, A_HEAD_DIM), lambda b, i: (0, 0)),
                  pl.BlockSpec((t, A_HEAD_DIM), lambda b, i: (0, 0)),
                  pl.BlockSpec((t, A_HEAD_DIM), lambda b, i: (0, 0))],
        out_specs=pl.BlockSpec((A_BLOCK, qw), lambda b, i: (b * nqb + i, 0)),
        out_shape=jax.ShapeDtypeStruct((n_seq * t, qw), BF16),
        compiler_params=_params(("arbitrary", "arbitrary")),
        name="attn_a_latent",
    )(sink, proj, proj, k_ctx, v_ctx, gq.reshape(1, -1), gk.reshape(1, -1), cos, sin)


def _per_head_lanes(x, fn):
    lane = lax.broadcasted_iota(jnp.int32, x.shape, 1)
    lo = fn(x[:, :B_DK])
    hi = fn(x[:, B_DK:])
    return jnp.where(lane < B_DK, lo, hi)


def _conv_silu(x, w):
    t = x.shape[0]
    row = lax.broadcasted_iota(jnp.int32, x.shape, 0)
    prev = jnp.where(row == 0, 0.0, pltpu.roll(x, 1, 0))
    nxt = jnp.where(row == t - 1, 0.0, pltpu.roll(x, t - 1, 0))
    return _silu(prev * w[0:1, :] + x * w[1:2, :] + nxt * w[2:3, :])


M_SAME, M_TRI, M_TRI_T, M_STRICT, M_BDIAG = range(5)


def _delta_masks(r):
    c = B_CHUNK
    ii = lax.broadcasted_iota(jnp.int32, (r, r), 0)
    jj = lax.broadcasted_iota(jnp.int32, (r, r), 1)
    same = (ii // c) == (jj // c)
    ahead = jnp.where(ii < r // 2, ii - jj, jj - ii)
    tri = jnp.where(same, ahead, -1) >= 0
    tri_t = jnp.where(same, ahead, 1) <= 0
    strict = jnp.where(same, ahead, -1) > 0
    bdiag = (ii // 16) == (jj // 16)
    return [x.astype(F32) for x in (same, tri, tri_t, strict, bdiag)]


def _delta_prepare(q, k, v, g_col, g_row, beta, mask_ref):
    c = B_CHUNK
    r = q.shape[0]
    dot = functools.partial(jnp.dot, preferred_element_type=F32)
    gc_col = jnp.sum(mask_ref[M_TRI] * g_row, axis=1, keepdims=True)
    gc_row = jnp.sum(mask_ref[M_TRI_T] * g_col, axis=0, keepdims=True)
    g_tot = jnp.sum(mask_ref[M_SAME] * g_row, axis=1, keepdims=True)
    ex = jnp.exp((gc_col - gc_row) * mask_ref[M_TRI])
    kb = k * beta
    qk = _bdot_nt(jnp.concatenate([kb, q], axis=0), k)
    m = qk[:r] * (ex * mask_ref[M_STRICT])
    aqk = qk[r:] * (ex * mask_ref[M_TRI])
    dg = m * mask_ref[M_BDIAG]
    off = m - dg
    n1 = -dg
    n1b = n1.astype(BF16)
    n2 = dot(n1b, n1b)
    n2b = n2.astype(BF16)
    t = dot(jnp.concatenate([n1b, n2b], axis=0), n2b)
    xs = n1 + n2 + t[:r]
    n4 = t[r:]
    n4b = n4.astype(BF16)
    t = dot(jnp.concatenate([xs.astype(BF16), n4b], axis=0), n4b)
    xs = xs + n4 + t[:r]
    n8 = t[r:]
    xs = xs + n8 + dot(xs.astype(BF16), n8.astype(BF16))
    xsb = xs.astype(BF16)
    f = -(off + dot(xsb, off.astype(BF16)))
    fb = f.astype(BF16)
    t = dot(fb, jnp.concatenate([xsb, fb], axis=1))
    ys = xs + f + t[:, :r]
    f2 = t[:, r:]
    ts = ys + f2 + dot(f2.astype(BF16), ys.astype(BF16))
    egc = jnp.exp(gc_col)
    rhs = jnp.concatenate([v * beta, kb * egc], axis=-1)
    uw = rhs + _bdot(ts, rhs)

    def block_diag(x):
        return jnp.concatenate([x] * (r // c), axis=-1) * mask_ref[M_SAME]

    kd = k * jnp.exp(g_tot - gc_col)
    kd_t = jnp.concatenate([kd, jnp.zeros_like(kd)], axis=-1).T[:c]
    kd_t = jnp.concatenate([kd_t] * (r // c), axis=0) * mask_ref[M_SAME]
    e_tot = jnp.broadcast_to(jnp.exp(g_tot), v.shape)
    return uw[:, :B_DV], e_tot, block_diag(uw[:, B_DV:]), aqk, block_diag(q * egc), kd_t


def _deltanet_kernel(alog_ref, dtb_ref, q_ref, k_ref, v_ref, z_ref, cwq_ref, cwk_ref, cwv_ref,
                     tail_ref, tailt_ref, onorm_ref, s0f_ref, s0b_ref,
                     o_ref, sf_ref, sb_ref,
                     qc_ref, kc_ref, vc_ref, oacc_ref, u_ref, et_ref, w_ref, aqk_ref, qg_ref, kdt_ref,
                     mask_ref, *, n_chunks):
    hp = pl.program_id(1)
    c = B_CHUNK

    @pl.when(jnp.logical_and(pl.program_id(0) == 0, hp == 0))
    def _():
        for i, x in enumerate(_delta_masks(4 * c)):
            mask_ref[i] = x

    def l2n(x):
        ss = _per_head_lanes(x * x, lambda a: jnp.sum(a, axis=-1, keepdims=True))
        return x * lax.rsqrt(ss + EPS)

    qc_ref[...] = l2n(_conv_silu(q_ref[...], cwq_ref[...])) * (B_DK ** -0.5)
    kc_ref[...] = l2n(_conv_silu(k_ref[...], cwk_ref[...]))
    vc_ref[...] = _conv_silu(v_ref[...], cwv_ref[...])
    oacc_ref[...] = jnp.zeros_like(oacc_ref)

    lane32 = lax.broadcasted_iota(jnp.int32, (c, 4 * B_HEADS), 1)

    def gates(chunk, d, head):
        tail = tail_ref[pl.ds(pl.multiple_of(chunk * c, c), c), :]
        ia = 2 * d * B_HEADS + head
        ib = ia + B_HEADS
        a_col = jnp.sum(jnp.where(lane32 == ia, tail, 0.0), axis=1, keepdims=True)
        b_col = jnp.sum(jnp.where(lane32 == ib, tail, 0.0), axis=1, keepdims=True)
        a_row = tailt_ref[chunk, pl.ds(ia, 1), :]
        na = -jnp.exp(alog_ref[d, head])
        bias = dtb_ref[d, head]
        g_col = na * jax.nn.softplus(a_col + bias)
        g_row = na * jax.nn.softplus(a_row + bias)
        return g_col, g_row, jax.nn.sigmoid(b_col)

    def stacked(ref, chunks):
        parts = []
        for chunk in chunks:
            x = ref[pl.ds(pl.multiple_of(chunk * c, c), c), :]
            parts += [x[:, :B_DK], x[:, B_DK:]]
        return jnp.concatenate(parts, axis=0)

    def prepare(j, carry):
        chunks = (j, n_chunks - 1 - j)
        gs = [gates(chunks[d], d, 2 * hp + hh) for d in range(2) for hh in range(2)]
        g_col = jnp.concatenate([g[0] for g in gs], axis=0)
        g_row = jnp.concatenate([g[1] for g in gs], axis=1)
        beta = jnp.concatenate([g[2] for g in gs], axis=0)
        outs = _delta_prepare(stacked(qc_ref, chunks), stacked(kc_ref, chunks), stacked(vc_ref, chunks),
                              g_col, g_row, beta, mask_ref)
        for ref, x in zip((u_ref, et_ref, w_ref, aqk_ref, qg_ref, kdt_ref), outs):
            ref[j] = x.astype(ref.dtype)
        return carry

    lax.fori_loop(0, n_chunks, prepare, 0, unroll=4)

    def scan(j, s):
        sb = s.astype(BF16)
        delta = u_ref[j] - jnp.dot(w_ref[j], sb, preferred_element_type=F32)
        db = delta.astype(BF16)
        o = (jnp.dot(qg_ref[j], sb, preferred_element_type=F32)
             + jnp.dot(aqk_ref[j], db, preferred_element_type=F32))
        for d, chunk in enumerate((j, n_chunks - 1 - j)):
            rows = pl.ds(pl.multiple_of(chunk * c, c), c)
            oacc_ref[rows, :] += jnp.concatenate([o[2 * d * c:(2 * d + 1) * c],
                                                  o[(2 * d + 1) * c:(2 * d + 2) * c]], axis=-1)
        return s * et_ref[j] + jnp.dot(kdt_ref[j], db, preferred_element_type=F32)

    init = jnp.concatenate([s0f_ref[0], s0f_ref[1], s0b_ref[0], s0b_ref[1]], axis=0)
    fin = lax.fori_loop(0, n_chunks, scan, init)
    sf_ref[0], sf_ref[1], sb_ref[0], sb_ref[1] = (fin[i * B_DK:(i + 1) * B_DK] for i in range(4))

    o = oacc_ref[...]
    ms = _per_head_lanes(o * o, lambda a: jnp.mean(a, axis=-1, keepdims=True))
    o_ref[...] = (o * lax.rsqrt(ms + EPS) * onorm_ref[...] * _silu(z_ref[...])).astype(o_ref.dtype)


def _deltanet(proj, tail, row0, n_seq, t, conv_w, a_log, dt_bias, o_norm, s0_f, s0_b):
    c = B_CHUNK
    n_chunks = t // c
    lw = 2 * B_DK
    col_q = (A_HEADS + 2 * A_KV_HEADS) * A_HEAD_DIM // lw
    nhp = B_HEADS // 2
    rows = tail[row0:row0 + n_seq * t]
    tail_t = rows.reshape(n_seq, n_chunks, c, 4 * B_HEADS).transpose(0, 1, 3, 2)
    onorm2 = jnp.concatenate([o_norm, o_norm]).reshape(1, lw)
    b0 = row0 // t
    seq_blk = lambda off: pl.BlockSpec((t, lw), lambda b, h: (b0 + b, col_q + off + h))
    cw_blk = lambda off: pl.BlockSpec((3, lw), lambda b, h: (0, off + h))
    st_blk = pl.BlockSpec((None, 2, B_DK, B_DV), lambda b, h: (b, h, 0, 0))
    return pl.pallas_call(
        functools.partial(_deltanet_kernel, n_chunks=n_chunks),
        grid=(n_seq, nhp),
        in_specs=[pl.BlockSpec(memory_space=pltpu.SMEM), pl.BlockSpec(memory_space=pltpu.SMEM),
                  seq_blk(0), seq_blk(nhp), seq_blk(2 * nhp), seq_blk(3 * nhp),
                  cw_blk(0), cw_blk(nhp), cw_blk(2 * nhp),
                  pl.BlockSpec((t, 4 * B_HEADS), lambda b, h: (b, 0)),
                  pl.BlockSpec((None, n_chunks, 4 * B_HEADS, c), lambda b, h: (b, 0, 0, 0)),
                  pl.BlockSpec((1, lw), lambda b, h: (0, 0)),
                  st_blk, st_blk],
        out_specs=[pl.BlockSpec((t, lw), lambda b, h: (b, h)), st_blk, st_blk],
        out_shape=[jax.ShapeDtypeStruct((n_seq * t, B_HEADS * B_DV), BF16),
                   jax.ShapeDtypeStruct((n_seq, B_HEADS, B_DK, B_DV), F32),
                   jax.ShapeDtypeStruct((n_seq, B_HEADS, B_DK, B_DV), F32)],
        scratch_shapes=[pltpu.VMEM((t, lw), F32)] * 4 + [pltpu.VMEM((n_chunks, 4 * c, B_DV), F32)] * 2
        + [pltpu.VMEM((n_chunks, 4 * c, 4 * c), BF16)] * 4 + [pltpu.VMEM((5, 4 * c, 4 * c), F32)],
        compiler_params=_params(("arbitrary", "arbitrary")),
        name="deltanet",
    )(a_log, dt_bias, proj, proj, proj, proj, conv_w, conv_w, conv_w, rows, tail_t, onorm2, s0_f, s0_b)


def _lora_norm_kernel(p_ref, gq_ref, gkv_ref, cq_ref, ckv_ref, ckvb_ref):
    p = p_ref[...]
    cq_ref[...] = _rms(p[:, :C_Q_LORA], gq_ref[...]).astype(cq_ref.dtype)
    ckv = _rms(p[:, C_Q_LORA:C_Q_LORA + C_KV_LORA], gkv_ref[...])
    ckv_ref[...] = ckv
    ckvb_ref[...] = ckv.astype(ckvb_ref.dtype)


def _lora_norm(p1, gq, gkv, tm=512):
    n, w = p1.shape
    return pl.pallas_call(
        _lora_norm_kernel,
        grid=(n // tm,),
        in_specs=[pl.BlockSpec((tm, w), lambda i: (i, 0)),
                  pl.BlockSpec((1, C_Q_LORA), lambda i: (0, 0)),
                  pl.BlockSpec((1, C_KV_LORA), lambda i: (0, 0))],
        out_specs=[pl.BlockSpec((tm, C_Q_LORA), lambda i: (i, 0)),
                   pl.BlockSpec((tm, C_KV_LORA), lambda i: (i, 0)),
                   pl.BlockSpec((tm, C_KV_LORA), lambda i: (i, 0))],
        out_shape=[jax.ShapeDtypeStruct((n, C_Q_LORA), BF16),
                   jax.ShapeDtypeStruct((n, C_KV_LORA), F32),
                   jax.ShapeDtypeStruct((n, C_KV_LORA), BF16)],
        compiler_params=_params(("arbitrary",)),
        name="lora_norm",
    )(p1, gq.reshape(1, -1), gkv.reshape(1, -1))


def _mla_head_q(q, h, gq):
    qn, qr = q[:, h * C_QK:h * C_QK + C_NOPE], q[:, h * C_QK + C_NOPE:(h + 1) * C_QK]
    rn = lax.rsqrt((jnp.sum(qn * qn, axis=-1, keepdims=True)
                    + jnp.sum(qr * qr, axis=-1, keepdims=True)) / C_QK + EPS)
    return qn * rn * gq[:, :C_NOPE], qr * rn * gq[:, C_NOPE:]


def _mla_head_k(kv, kr, kr_ss, h, gk):
    kn = kv[:, h * (C_NOPE + C_V):h * (C_NOPE + C_V) + C_NOPE]
    v = kv[:, h * (C_NOPE + C_V) + C_NOPE:(h + 1) * (C_NOPE + C_V)]
    rn = lax.rsqrt((jnp.sum(kn * kn, axis=-1, keepdims=True) + kr_ss) / C_QK + EPS)
    return kn * rn * gk[:, :C_NOPE], kr * rn, v


C_STACK = 4


def _place(x, i, n):
    t, w = x.shape
    parts = ([jnp.zeros((t, i * w), x.dtype)] if i else []) + [x]
    if i < n - 1:
        parts.append(jnp.zeros((t, (n - 1 - i) * w), x.dtype))
    return jnp.concatenate(parts, axis=-1)


def _by_block(cols, lane, w):
    out = cols[-1]
    for i in reversed(range(len(cols) - 1)):
        out = jnp.where(lane < (i + 1) * w, cols[i], out)
    return out


def _attn_c_prompt_kernel(q_ref, kv_ref, p_ref, gq_ref, gk_ref, o_ref):
    scale = C_QK ** -0.5
    n = C_STACK
    q = q_ref[...]
    kv = kv_ref[...]
    gk = gk_ref[...]
    t = q.shape[0]
    kr_raw = p_ref[...][:, C_Q_LORA + C_KV_LORA:]
    kr_ss = jnp.sum(kr_raw * kr_raw, axis=-1, keepdims=True)
    kr_g = kr_raw * gk[:, C_NOPE:]
    lane_q = lax.broadcasted_iota(jnp.int32, (t, n * C_QK), 1)
    lane_o = lax.broadcasted_iota(jnp.int32, (t, n * C_V), 1)
    outs = []
    for grp in range(C_HEADS // n):
        qs = q[:, grp * n * C_QK:(grp + 1) * n * C_QK]
        rn = [lax.rsqrt(jnp.mean(qs[:, i * C_QK:(i + 1) * C_QK] ** 2, axis=-1, keepdims=True) + EPS)
              for i in range(n)]
        qs = qs * _by_block(rn, lane_q, C_QK) * gq_ref[...]
        k_rows, v_rows = [], []
        for i in range(n):
            kn, kr, v = _mla_head_k(kv, kr_g, kr_ss, grp * n + i, gk)
            k_rows.append(_place(jnp.concatenate([kn, kr], axis=-1), i, n))
            v_rows.append(_place(v, i, n))
        s = _bdot_nt(qs, jnp.concatenate(k_rows, axis=0)) * scale
        ps, rden = [], []
        for i in range(n):
            si = s[:, i * t:(i + 1) * t]
            pi = jnp.exp(si - si.max(axis=-1, keepdims=True))
            ps.append(pi)
            rden.append(1.0 / pi.sum(axis=-1, keepdims=True))
        o = _bdot(jnp.concatenate(ps, axis=-1), jnp.concatenate(v_rows, axis=0))
        outs.append(o * _by_block(rden, lane_o, C_V))
    o_ref[...] = jnp.concatenate(outs, axis=-1).astype(o_ref.dtype)


def _attn_c_prompt(q, kv, p1, n_seq, t, gq, gk):
    return pl.pallas_call(
        _attn_c_prompt_kernel,
        grid=(n_seq,),
        in_specs=[pl.BlockSpec((t, q.shape[1]), lambda b: (b, 0)),
                  pl.BlockSpec((t, kv.shape[1]), lambda b: (b, 0)),
                  pl.BlockSpec((t, p1.shape[1]), lambda b: (b, 0)),
                  pl.BlockSpec((1, C_STACK * C_QK), lambda b: (0, 0)),
                  pl.BlockSpec((1, C_QK), lambda b: (0, 0))],
        out_specs=pl.BlockSpec((t, C_HEADS * C_V), lambda b: (b, 0)),
        out_shape=jax.ShapeDtypeStruct((n_seq * t, C_HEADS * C_V), BF16),
        compiler_params=_params(("arbitrary",), VMEM_LIMIT),
        name="attn_c_context",
    )(q, kv, p1, jnp.tile(gq, C_STACK).reshape(1, -1), gk.reshape(1, -1))


def _attn_c_sample_kernel(q_ref, kv_ref, p_ref, kvc_ref, krc_ref, gq_ref, gk_ref, cos_ref, sin_ref,
                          o_ref, *, tq):
    scale = C_QK ** -0.5
    i = pl.program_id(1)
    q0 = pl.multiple_of(i * tq, tq)
    q = q_ref[...]
    kv = kv_ref[...]
    kvc = kvc_ref[...]
    gq, gk = gq_ref[...], gk_ref[...]
    cos, sin = cos_ref[...], sin_ref[...]
    cq, sq = cos_ref[pl.ds(q0, tq), :], sin_ref[pl.ds(q0, tq), :]
    kr_raw = p_ref[...][:, C_Q_LORA + C_KV_LORA:]
    kr_ss = jnp.sum(kr_raw * kr_raw, axis=-1, keepdims=True)
    kr_g = _rope(kr_raw * gk[:, C_NOPE:], cos, sin)
    krc_raw = krc_ref[...]
    krc_ss = jnp.sum(krc_raw * krc_raw, axis=-1, keepdims=True)
    krc_g = krc_raw * gk[:, C_NOPE:]
    outs = []
    for h in range(C_HEADS):
        qn, qr = _mla_head_q(q, h, gq)
        qr = _rope(qr, cq, sq)
        kn, kr, v = _mla_head_k(kv, kr_g, kr_ss, h, gk)
        knc, krc, vc = _mla_head_k(kvc, krc_g, krc_ss, h, gk)
        s1 = (_bdot_nt(qn, kn) + _bdot_nt(qr, kr)) * scale
        s2 = (_bdot_nt(qn, knc) + _bdot_nt(qr, krc)) * scale
        outs.append(_softmax_attend([(s1, v), (s2, vc)], None))
    o_ref[...] = jnp.concatenate(outs, axis=-1).astype(o_ref.dtype)


def _attn_c_sample(q, kv, p1, row0, n_seq, t, kr_ctx, gq, gk, tq=256):
    n = p1.shape[0]
    past = kr_ctx.shape[1]
    nq = t // tq
    cos, sin = _rope_tables(t, C_ROPE)
    return pl.pallas_call(
        functools.partial(_attn_c_sample_kernel, tq=tq),
        grid=(n_seq, nq),
        in_specs=[pl.BlockSpec((tq, q.shape[1]), lambda b, i: (row0 // tq + b * nq + i, 0)),
                  pl.BlockSpec((t, kv.shape[1]), lambda b, i: (row0 // t + b, 0)),
                  pl.BlockSpec((t, p1.shape[1]), lambda b, i: (row0 // t + b, 0)),
                  pl.BlockSpec((past, kv.shape[1]), lambda b, i: (n // past + b, 0)),
                  pl.BlockSpec((None, past, C_ROPE), lambda b, i: (b, 0, 0)),
                  pl.BlockSpec((1, C_QK), lambda b, i: (0, 0)),
                  pl.BlockSpec((1, C_QK), lambda b, i: (0, 0)),
                  pl.BlockSpec((t, C_ROPE), lambda b, i: (0, 0)),
                  pl.BlockSpec((t, C_ROPE), lambda b, i: (0, 0))],
        out_specs=pl.BlockSpec((tq, C_HEADS * C_V), lambda b, i: (b * nq + i, 0)),
        out_shape=jax.ShapeDtypeStruct((n_seq * t, C_HEADS * C_V), BF16),
        compiler_params=_params(("arbitrary", "arbitrary"), VMEM_LIMIT),
        name="attn_c_latent",
    )(q, kv, p1, kv, kr_ctx, gq.reshape(1, -1), gk.reshape(1, -1), cos, sin)


ROUTE_TILE = 512
META_LANES = 128


def _router_kernel(x_ref, g_ref, sc_ref, sh_ref, wr_ref, br_ref, h_ref, meta_ref, cnt_ref, run_ref):
    @pl.when(pl.program_id(0) == 0)
    def _():
        run_ref[...] = jnp.zeros_like(run_ref)

    h = _rms(x_ref[...], g_ref[...]) * (1 + sc_ref[...]) + sh_ref[...]
    h_ref[...] = h
    tm = h.shape[0]
    logits = _hdot(h, wr_ref[...]) + br_ref[...]
    lane = lax.broadcasted_iota(jnp.int32, logits.shape, 1)
    work = logits
    picks, tops, ids = [], [], []
    for _ in range(TOP_K):
        m = work.max(axis=-1, keepdims=True)
        first = jnp.min(jnp.where(work == m, lane, N_EXPERTS), axis=-1, keepdims=True)
        pick = lane == first
        picks.append(pick)
        tops.append(m)
        ids.append(first)
        work = jnp.where(pick, -jnp.inf, work)
    sel = sum(p.astype(F32) for p in picks)
    earlier = (lax.broadcasted_iota(jnp.int32, (tm, tm), 0)
               > lax.broadcasted_iota(jnp.int32, (tm, tm), 1)).astype(BF16)
    before = run_ref[...] + jnp.dot(earlier, sel.astype(BF16), preferred_element_type=F32)
    ws = [jnp.exp(t - tops[0]) for t in tops]
    den = sum(ws)
    mlane = lax.broadcasted_iota(jnp.int32, (tm, META_LANES), 1)
    meta = jnp.zeros((tm, META_LANES), F32)
    for k in range(TOP_K):
        rank = jnp.sum(jnp.where(picks[k], before, 0.0), axis=-1, keepdims=True)
        meta = jnp.where(mlane == k, ids[k].astype(F32), meta)
        meta = jnp.where(mlane == TOP_K + k, rank, meta)
        meta = jnp.where(mlane == 2 * TOP_K + k, ws[k] / den, meta)
    meta_ref[...] = meta
    run_ref[...] += jnp.sum(sel, axis=0, keepdims=True)
    cnt_ref[...] = run_ref[...]


def _router(x, gain, scale, shift, w_router, b_router, layer, n_p, t_s, tm=512):
    n, d = x.shape
    e = w_router.shape[-1]
    group = _group_of_tile(tm, n_p, t_s)
    return pl.pallas_call(
        _router_kernel,
        grid=(n // tm,),
        in_specs=[pl.BlockSpec((tm, d), lambda i: (i, 0)),
                  pl.BlockSpec((1, d), lambda i: (0, 0)),
                  pl.BlockSpec((None, 1, d), lambda i: (group(i), 0, 0)),
                  pl.BlockSpec((None, 1, d), lambda i: (group(i), 0, 0)),
                  pl.BlockSpec((None, d, e), lambda i: (layer, 0, 0)),
                  pl.BlockSpec((None, 1, e), lambda i: (layer, 0, 0))],
        out_specs=[pl.BlockSpec((tm, d), lambda i: (i, 0)),
                   pl.BlockSpec((tm, META_LANES), lambda i: (i, 0)),
                   pl.BlockSpec((1, e), lambda i: (0, 0))],
        out_shape=[jax.ShapeDtypeStruct((n, d), F32),
                   jax.ShapeDtypeStruct((n, META_LANES), F32),
                   jax.ShapeDtypeStruct((1, e), F32)],
        scratch_shapes=[pltpu.VMEM((1, e), F32)],
        compiler_params=_params(("arbitrary",)),
        name="router",
    )(x, gain.reshape(1, d), scale.reshape(N_GROUPS, 1, d), shift.reshape(N_GROUPS, 1, d),
      w_router, b_router.reshape(-1, 1, e))


def _route_plan(meta, cnt, n_tiles):
    ids = meta[:, :TOP_K].astype(jnp.int32)
    ranks = meta[:, TOP_K:2 * TOP_K].astype(jnp.int32)
    gates = meta[:, 2 * TOP_K:3 * TOP_K]
    counts = cnt[0].astype(jnp.int32)
    padded = (counts + ROUTE_TILE - 1) // ROUTE_TILE * ROUTE_TILE
    ends = jnp.cumsum(padded)
    offs = ends - padded
    pos = offs[ids] + ranks
    tile_start = jnp.arange(n_tiles, dtype=jnp.int32) * ROUTE_TILE
    n_valid = ends[-1] // ROUTE_TILE
    te = jnp.minimum(jnp.sum(ends[None, :] <= tile_start[:, None], axis=1), N_EXPERTS - 1)
    te = jnp.where(tile_start < ends[-1], te, te[jnp.maximum(n_valid - 1, 0)]).astype(jnp.int32)
    rows = jnp.clip((offs + counts)[te] - tile_start, 0, ROUTE_TILE).astype(jnp.int32)
    ragged = jnp.where(counts % ROUTE_TILE != 0, ends // ROUTE_TILE - 1, -1)
    tail = tile_start[n_tiles - N_EXPERTS:] // ROUTE_TILE
    fill = jnp.concatenate([ragged, jnp.where(tail >= n_valid, tail, -1)]).astype(jnp.int32)
    return pos, gates, te, rows, n_valid.reshape(1).astype(jnp.int32), fill


def _dispatch_kernel(fill_ref, pos_ref, h_ref, xs_ref, zero_ref, sem, zsem):
    tm = h_ref.shape[0]

    @pl.when(pl.program_id(0) == 0)
    def _():
        zero_ref[...] = jnp.zeros_like(zero_ref)

        def fill_copy(j):
            row0 = pl.multiple_of(fill_ref[j] * ROUTE_TILE, ROUTE_TILE)
            return pltpu.make_async_copy(zero_ref, xs_ref.at[pl.ds(row0, ROUTE_TILE), :], zsem)

        def start(j, carry):
            @pl.when(fill_ref[j] >= 0)
            def _():
                fill_copy(j).start()
            return carry

        def wait(j, carry):
            @pl.when(fill_ref[j] >= 0)
            def _():
                fill_copy(j).wait()
            return carry

        lax.fori_loop(0, fill_ref.shape[0], start, 0)
        lax.fori_loop(0, fill_ref.shape[0], wait, 0)

    def body(r, carry):
        for k in range(TOP_K):
            p = pos_ref[0, r * TOP_K + k]
            pltpu.make_async_copy(h_ref.at[pl.ds(r, 1), :], xs_ref.at[pl.ds(p, 1), :], sem).start()
        return carry

    lax.fori_loop(0, tm, body, 0, unroll=4)
    for _ in range(TOP_K):
        pltpu.make_async_copy(h_ref, h_ref, sem).wait()


def _dispatch(h, pos, fill, n_slots, tm=512):
    n, d = h.shape
    grid_spec = pltpu.PrefetchScalarGridSpec(
        num_scalar_prefetch=1,
        grid=(n // tm,),
        in_specs=[pl.BlockSpec((None, 1, tm * TOP_K), lambda i, fl: (i, 0, 0), memory_space=pltpu.SMEM),
                  pl.BlockSpec((tm, d), lambda i, fl: (i, 0))],
        out_specs=pl.BlockSpec(memory_space=pl.ANY),
        scratch_shapes=[pltpu.VMEM((ROUTE_TILE, d), h.dtype), pltpu.SemaphoreType.DMA(()),
                        pltpu.SemaphoreType.DMA(())])
    return pl.pallas_call(
        _dispatch_kernel,
        grid_spec=grid_spec,
        out_shape=jax.ShapeDtypeStruct((n_slots, d), h.dtype),
        compiler_params=_params(("arbitrary",)),
        name="moe_dispatch",
    )(fill, pos.reshape(n // tm, 1, tm * TOP_K), h)


def _experts_kernel(te_ref, rows_ref, nv_ref, x_ref, wgu_ref, bgu_ref, wd_ref, bd_ref, y_ref,
                    wgub_ref, wdb_ref, *, d_ff):
    i = pl.program_id(0)
    valid = i < nv_ref[0]
    fresh = jnp.logical_or(i == 0, te_ref[i] != te_ref[jnp.maximum(i - 1, 0)])

    @pl.when(jnp.logical_and(valid, fresh))
    def _():
        wgub_ref[...] = wgu_ref[...].astype(BF16)
        wdb_ref[...] = wd_ref[...].astype(BF16)

    @pl.when(valid)
    def _():
        row = lax.broadcasted_iota(jnp.int32, x_ref.shape, 0)
        x = jnp.where(row < rows_ref[i], x_ref[...], 0.0).astype(BF16)
        gu = jnp.dot(x, wgub_ref[...], preferred_element_type=F32) + bgu_ref[...]
        gate = jnp.minimum(gu[:, :d_ff], SWIGLU_LIMIT)
        up = jnp.clip(gu[:, d_ff:], -SWIGLU_LIMIT, SWIGLU_LIMIT)
        act = (up + 1) * gate * jax.nn.sigmoid(SWIGLU_ALPHA * gate)
        y_ref[...] = jnp.dot(act.astype(BF16), wdb_ref[...], preferred_element_type=F32) + bd_ref[...]

    @pl.when(jnp.logical_not(valid))
    def _():
        y_ref[...] = jnp.zeros_like(y_ref)


def _experts(xs, te, rows, n_valid, w_gu, b_gu, w_down, b_down, layer):
    n_slots, d = xs.shape
    _, e, _, two_ff = w_gu.shape
    last = lambda i, nv: jnp.minimum(i, nv[0] - 1)
    grid_spec = pltpu.PrefetchScalarGridSpec(
        num_scalar_prefetch=3,
        grid=(n_slots // ROUTE_TILE,),
        in_specs=[pl.BlockSpec((ROUTE_TILE, d), lambda i, te, rw, nv: (last(i, nv), 0)),
                  pl.BlockSpec((None, None, d, two_ff), lambda i, te, rw, nv: (layer, te[i], 0, 0)),
                  pl.BlockSpec((None, None, 1, two_ff), lambda i, te, rw, nv: (layer, te[i], 0, 0)),
                  pl.BlockSpec((None, None, two_ff // 2, d), lambda i, te, rw, nv: (layer, te[i], 0, 0)),
                  pl.BlockSpec((None, None, 1, d), lambda i, te, rw, nv: (layer, te[i], 0, 0))],
        out_specs=pl.BlockSpec((ROUTE_TILE, d), lambda i, te, rw, nv: (i, 0)),
        scratch_shapes=[pltpu.VMEM((d, two_ff), BF16), pltpu.VMEM((two_ff // 2, d), BF16)])
    return pl.pallas_call(
        functools.partial(_experts_kernel, d_ff=two_ff // 2),
        grid_spec=grid_spec,
        out_shape=jax.ShapeDtypeStruct((n_slots, d), F32),
        compiler_params=_params(("arbitrary",), VMEM_LIMIT),
        name="moe_experts",
    )(te, rows, n_valid, xs, w_gu, b_gu.reshape(b_gu.shape[0], e, 1, two_ff), w_down,
      b_down.reshape(b_down.shape[0], e, 1, d))


def _combine_kernel(pos_ref, x_ref, gate_ref, gts_ref, y_ref, o_ref, buf_ref, sem):
    tm = x_ref.shape[0]

    def body(r, carry):
        for k in range(TOP_K):
            p = pos_ref[0, r * TOP_K + k]
            pltpu.make_async_copy(y_ref.at[pl.ds(p, 1), :], buf_ref.at[k, pl.ds(r, 1), :], sem).start()
        return carry

    lax.fori_loop(0, tm, body, 0, unroll=4)
    for k in range(TOP_K):
        pltpu.make_async_copy(buf_ref.at[k], buf_ref.at[k], sem).wait()
    g = gts_ref[...]
    acc = g[:, 0:1] * buf_ref[0]
    for k in range(1, TOP_K):
        acc = acc + g[:, k:k + 1] * buf_ref[k]
    o_ref[...] = x_ref[...] + gate_ref[...] * acc


def _combine(y, pos, gates, x, gate, n_p, t_s, tm=256):
    n, d = x.shape
    group = _group_of_tile(tm, n_p, t_s)
    return pl.pallas_call(
        _combine_kernel,
        grid=(n // tm,),
        in_specs=[pl.BlockSpec((None, 1, tm * TOP_K), lambda i: (i, 0, 0), memory_space=pltpu.SMEM),
                  pl.BlockSpec((tm, d), lambda i: (i, 0)),
                  pl.BlockSpec((None, 1, d), lambda i: (group(i), 0, 0)),
                  pl.BlockSpec((tm, TOP_K), lambda i: (i, 0)),
                  pl.BlockSpec(memory_space=pl.ANY)],
        out_specs=pl.BlockSpec((tm, d), lambda i: (i, 0)),
        out_shape=jax.ShapeDtypeStruct((n, d), F32),
        scratch_shapes=[pltpu.VMEM((TOP_K, tm, d), F32), pltpu.SemaphoreType.DMA(())],
        compiler_params=_params(("arbitrary",)),
        name="moe_combine",
    )(pos.reshape(n // tm, 1, tm * TOP_K), x, gate.reshape(N_GROUPS, 1, d), gates, y)


def _moe(x, gain, scale, shift, gate, w_router, b_router, w_gu, b_gu, w_down, b_down, layer, n_p, t_s):
    n = x.shape[0]
    n_slots = n * TOP_K + N_EXPERTS * ROUTE_TILE
    h, meta, cnt = _router(x, gain, scale, shift, w_router, b_router, layer, n_p, t_s)
    pos, gates, te, rows, n_valid, fill = _route_plan(meta, cnt, n_slots // ROUTE_TILE)
    xs = _dispatch(h, pos, fill, n_slots)
    y = _experts(xs, te, rows, n_valid, w_gu, b_gu, w_down, b_down, layer)
    return _combine(y, pos, gates, x, gate, n_p, t_s)


def kernel(x_prompt, x_sample, c, cache_a_k, cache_a_v, state_b_fwd, state_b_bwd, cache_c_ckv,
           cache_c_krope, c_ctx, w_mod, b_mod, norm_mix, norm_ffn, e_w_in, e_w_out, e_a_qnorm,
           e_a_knorm, e_a_sink, e_b_conv, e_b_alog, e_b_dtbias, e_b_onorm, o_w_in, o_q_lora_norm,
           o_kv_lora_norm, o_w_uq, o_w_ukv, o_qnorm, o_knorm, o_w_out, moe_w_router, moe_b_router,
           moe_w_gu, moe_b_gu, moe_w_down, moe_b_down):
    bp, tp, d = x_prompt.shape
    bs, ts, _ = x_sample.shape
    depth = w_mod.shape[0]
    n_p, n_s = bp * tp, bs * ts
    n = n_p + n_s
    assert bs + 1 <= N_GROUPS and ts % 512 == 0 and n_p % ts == 0

    x = jnp.concatenate([x_prompt.reshape(n_p, d), x_sample.reshape(n_s, d)], axis=0)
    cond = jnp.concatenate([c_ctx[None], c, jnp.zeros((N_GROUPS - 1 - bs, d), F32)], axis=0)
    mod = _adaln(cond, w_mod, b_mod)

    new_a_k, new_a_v, new_b_fwd, new_b_bwd, new_c_ckv, new_c_krope = [], [], [], [], [], []
    for layer in range(depth):
        sh1, sc1, g1, sh2, sc2, g2 = (mod[layer, j] for j in range(6))
        h = _modulate(x, norm_mix[layer], sc1, sh1, n_p, ts)
        i = layer // 2
        if layer % 2 == 0:
            main_w = (A_HEADS + 2 * A_KV_HEADS) * A_HEAD_DIM + 4 * B_HEADS * B_DK
            proj = _matmul(h, e_w_in, i, 0, main_w // 2, main_w, name="even_in_proj")
            tail = _matmul(h, e_w_in[i][None, :, main_w:], 0, 0, 4 * B_HEADS, 4 * B_HEADS,
                           name="even_gate_proj")
            oa_p, kn_p = _attn_a_prompt(proj, bp, tp, e_a_sink[i], e_a_qnorm[i], e_a_knorm[i])
            oa_s = _attn_a_sample(proj, n_p, bs, ts, cache_a_k[:, i], cache_a_v[:, i],
                                  e_a_sink[i], e_a_qnorm[i], e_a_knorm[i])
            zeros = jnp.zeros((bp, B_HEADS, B_DK, B_DV), F32)
            ob_p, s_f, s_b = _deltanet(proj, tail, 0, bp, tp, e_b_conv[i], e_b_alog[i],
                                       e_b_dtbias[i], e_b_onorm[i], zeros, zeros)
            ob_s, _, _ = _deltanet(proj, tail, n_p, bs, ts, e_b_conv[i], e_b_alog[i],
                                   e_b_dtbias[i], e_b_onorm[i], state_b_fwd[:, i], state_b_bwd[:, i])
            mix = jnp.concatenate([jnp.concatenate([oa_p, ob_p], axis=1),
                                   jnp.concatenate([oa_s, ob_s], axis=1)], axis=0)
            x = _matmul_residual(mix, e_w_out, i, x, g1, n_p, ts)
            kw = A_KV_HEADS * A_HEAD_DIM
            new_a_k.append(kn_p.reshape(bp, tp, A_KV_HEADS, A_HEAD_DIM).transpose(0, 2, 1, 3))
            v_p = proj[:n_p, A_HEADS * A_HEAD_DIM + kw:A_HEADS * A_HEAD_DIM + 2 * kw]
            new_a_v.append(v_p.reshape(bp, tp, A_KV_HEADS, A_HEAD_DIM).transpose(0, 2, 1, 3))
            new_b_fwd.append(s_f)
            new_b_bwd.append(s_b)
        else:
            p1 = _matmul(h, o_w_in, i, 0, o_w_in.shape[-1], o_w_in.shape[-1], name="odd_in_proj")
            cq, ckv, ckv_b = _lora_norm(p1, o_q_lora_norm[i], o_kv_lora_norm[i])
            q = _matmul(cq, o_w_uq, i, 0, o_w_uq.shape[-1] // 2, o_w_uq.shape[-1], name="odd_uq")
            ckv_all = jnp.concatenate([ckv_b, cache_c_ckv[:, i].reshape(-1, C_KV_LORA).astype(BF16)], 0)
            kv = _matmul(ckv_all, o_w_ukv, i, 0, o_w_ukv.shape[-1] // 2, o_w_ukv.shape[-1],
                         name="odd_ukv")
            o_p = _attn_c_prompt(q, kv, p1, bp, tp, o_qnorm[i], o_knorm[i])
            o_s = _attn_c_sample(q, kv, p1, n_p, bs, ts, cache_c_krope[:, i], o_qnorm[i], o_knorm[i])
            x = _matmul_residual(jnp.concatenate([o_p, o_s], axis=0), o_w_out, i, x, g1, n_p, ts)
            new_c_ckv.append(ckv[:n_p].reshape(bp, tp, C_KV_LORA))
            new_c_krope.append(p1[:n_p, C_Q_LORA + C_KV_LORA:].reshape(bp, tp, C_ROPE))
        x = _moe(x, norm_ffn[layer], sc2, sh2, g2, moe_w_router, moe_b_router, moe_w_gu, moe_b_gu,
                 moe_w_down, moe_b_down, layer, n_p, ts)

    return (x[:n_p].reshape(bp, tp, d), x[n_p:].reshape(bs, ts, d),
            jnp.stack(new_a_k, axis=1), jnp.stack(new_a_v, axis=1),
            jnp.stack(new_b_fwd, axis=1), jnp.stack(new_b_bwd, axis=1),
            jnp.stack(new_c_ckv, axis=1), jnp.stack(new_c_krope, axis=1))
```

```python
import functools
import math

import numpy as np
import jax
import jax.numpy as jnp
from jax import lax
from jax.experimental import pallas as pl
from jax.experimental.pallas import tpu as pltpu

F32 = jnp.float32
BF16 = jnp.bfloat16
HIGHEST = lax.Precision.HIGHEST

EPS = 1e-6
NEG_INF = -1e30
ROPE_BASE = 10000.0
GRID_W = 64
N_GROUPS = 8

A_HEADS, A_KV_HEADS, A_GROUP, A_HEAD_DIM, A_WINDOW, A_BLOCK = 8, 2, 4, 64, 128, 128
B_HEADS, B_DK, B_DV, B_CHUNK = 8, 64, 64, 64
C_HEADS, C_NOPE, C_ROPE, C_V, C_Q_LORA, C_KV_LORA = 16, 64, 32, 64, 384, 256
C_QK = C_NOPE + C_ROPE
N_EXPERTS, TOP_K = 32, 4
SWIGLU_LIMIT, SWIGLU_ALPHA = 7.0, 1.702

VMEM_LIMIT = 56 * 1024 * 1024


def _params(sem, vmem=None):
    return pltpu.CompilerParams(dimension_semantics=sem, vmem_limit_bytes=vmem)


def _bdot(a, b):
    return jnp.dot(a.astype(BF16), b.astype(BF16), preferred_element_type=F32)


def _bdot_nt(a, b):
    return lax.dot_general(a.astype(BF16), b.astype(BF16), (((1,), (1,)), ((), ())),
                           preferred_element_type=F32)


def _bdot_tn(a, b):
    return lax.dot_general(a.astype(BF16), b.astype(BF16), (((0,), (0,)), ((), ())),
                           preferred_element_type=F32)


def _hdot(a, b):
    return jnp.dot(a, b, preferred_element_type=F32, precision=HIGHEST)


def _hdot_nt(a, b):
    return lax.dot_general(a, b, (((1,), (1,)), ((), ())), preferred_element_type=F32,
                           precision=HIGHEST)


def _rms(x, gain):
    return x * lax.rsqrt(jnp.mean(x * x, axis=-1, keepdims=True) + EPS) * gain


def _silu(x):
    return x * jax.nn.sigmoid(x)


def _group_of_tile(tm, n_p, t_s):
    def group(i):
        r = i * tm
        return jnp.where(r < n_p, 0, 1 + (r - n_p) // t_s)
    return group


def _adaln_kernel(cond_ref, w_ref, b_ref, o_ref):
    o_ref[...] = _bdot(_silu(cond_ref[...]), w_ref[...]) + b_ref[...]


def _adaln(cond, w_mod, b_mod):
    depth, d, _ = w_mod.shape
    return pl.pallas_call(
        _adaln_kernel,
        grid=(depth, 6),
        in_specs=[pl.BlockSpec((N_GROUPS, d), lambda l, j: (0, 0)),
                  pl.BlockSpec((None, d, d), lambda l, j: (l, 0, j)),
                  pl.BlockSpec((None, 1, d), lambda l, j: (l, 0, j))],
        out_specs=pl.BlockSpec((None, None, N_GROUPS, d), lambda l, j: (l, j, 0, 0)),
        out_shape=jax.ShapeDtypeStruct((depth, 6, N_GROUPS, d), F32),
        compiler_params=_params(("arbitrary", "arbitrary")),
        name="adaln",
    )(cond, w_mod, b_mod.reshape(depth, 1, 6 * d))


def _modulate_kernel(x_ref, g_ref, sc_ref, sh_ref, o_ref):
    y = _rms(x_ref[...], g_ref[...])
    o_ref[...] = (y * (1 + sc_ref[...]) + sh_ref[...]).astype(o_ref.dtype)


def _modulate(x, gain, scale, shift, n_p, t_s, tm=512):
    n, d = x.shape
    group = _group_of_tile(tm, n_p, t_s)
    return pl.pallas_call(
        _modulate_kernel,
        grid=(n // tm,),
        in_specs=[pl.BlockSpec((tm, d), lambda i: (i, 0)),
                  pl.BlockSpec((1, d), lambda i: (0, 0)),
                  pl.BlockSpec((None, 1, d), lambda i: (group(i), 0, 0)),
                  pl.BlockSpec((None, 1, d), lambda i: (group(i), 0, 0))],
        out_specs=pl.BlockSpec((tm, d), lambda i: (i, 0)),
        out_shape=jax.ShapeDtypeStruct((n, d), BF16),
        compiler_params=_params(("arbitrary",)),
        name="modulate",
    )(x, gain.reshape(1, d), scale.reshape(N_GROUPS, 1, d), shift.reshape(N_GROUPS, 1, d))


def _mm_kernel(x_ref, w_ref, o_ref, wb_ref):
    @pl.when(pl.program_id(1) == 0)
    def _():
        wb_ref[...] = w_ref[...].astype(BF16)
    o_ref[...] = jnp.dot(x_ref[...], wb_ref[...], preferred_element_type=F32).astype(o_ref.dtype)


def _matmul(x, w3, layer, col0_blocks, tn, n_out, out_dtype=F32, tm=512, name="matmul"):
    n, k = x.shape
    return pl.pallas_call(
        _mm_kernel,
        grid=(n_out // tn, n // tm),
        in_specs=[pl.BlockSpec((tm, k), lambda j, i: (i, 0)),
                  pl.BlockSpec((None, k, tn), lambda j, i: (layer, 0, col0_blocks + j))],
        out_specs=pl.BlockSpec((tm, tn), lambda j, i: (i, j)),
        out_shape=jax.ShapeDtypeStruct((n, n_out), out_dtype),
        scratch_shapes=[pltpu.VMEM((k, tn), BF16)],
        compiler_params=_params(("arbitrary", "arbitrary"), VMEM_LIMIT),
        name=name,
    )(x, w3)


def _mm_res_kernel(x_ref, w_ref, res_ref, gate_ref, o_ref, wb_ref):
    @pl.when(pl.program_id(0) == 0)
    def _():
        wb_ref[...] = w_ref[...].astype(BF16)
    y = jnp.dot(x_ref[...], wb_ref[...], preferred_element_type=F32)
    o_ref[...] = res_ref[...] + gate_ref[...] * y


def _matmul_residual(mix, w3, layer, res, gate, n_p, t_s, tm=512):
    n, k = mix.shape
    d = res.shape[1]
    group = _group_of_tile(tm, n_p, t_s)
    return pl.pallas_call(
        _mm_res_kernel,
        grid=(n // tm,),
        in_specs=[pl.BlockSpec((tm, k), lambda i: (i, 0)),
                  pl.BlockSpec((None, k, d), lambda i: (layer, 0, 0)),
                  pl.BlockSpec((tm, d), lambda i: (i, 0)),
                  pl.BlockSpec((None, 1, d), lambda i: (group(i), 0, 0))],
        out_specs=pl.BlockSpec((tm, d), lambda i: (i, 0)),
        out_shape=jax.ShapeDtypeStruct((n, d), F32),
        scratch_shapes=[pltpu.VMEM((k, d), BF16)],
        compiler_params=_params(("arbitrary",), VMEM_LIMIT),
        name="out_proj_residual",
    )(mix, w3, res, gate.reshape(N_GROUPS, 1, d))


def _rope_tables(t_len, d):
    half, quarter = d // 2, d // 4
    pos = np.arange(t_len)
    row, col = pos // GRID_W, pos % GRID_W
    inv = ROPE_BASE ** (-np.arange(quarter, dtype=np.float64) / quarter)
    ang_r = row[:, None] * inv[None, :]
    ang_c = col[:, None] * inv[None, :]
    cos = np.concatenate([np.cos(ang_r), np.cos(ang_r), np.cos(ang_c), np.cos(ang_c)], axis=1)
    sin = np.concatenate([-np.sin(ang_r), np.sin(ang_r), -np.sin(ang_c), np.sin(ang_c)], axis=1)
    return jnp.asarray(cos, F32), jnp.asarray(sin, F32)


def _swap_pairs(x):
    q = x.shape[-1] // 4
    return jnp.concatenate([x[:, q:2 * q], x[:, :q], x[:, 3 * q:], x[:, 2 * q:3 * q]], axis=-1)


def _rope(x, cos, sin):
    return x * cos + _swap_pairs(x) * sin


def _softmax_attend(parts, sink):
    m = parts[0][0].max(axis=-1, keepdims=True)
    for s, _ in parts[1:]:
        m = jnp.maximum(m, s.max(axis=-1, keepdims=True))
    if sink is not None:
        m = jnp.maximum(m, sink)
    den = jnp.exp(sink - m) if sink is not None else 0.0
    acc = None
    for s, v in parts:
        p = jnp.exp(s - m)
        den = den + p.sum(axis=-1, keepdims=True)
        o = _bdot(p, v)
        acc = o if acc is None else acc + o
    return acc / den


def _group_sinks(sink_ref, hk, rows):
    head = lax.broadcasted_iota(jnp.int32, (A_GROUP * rows, 1), 0) // rows
    col = jnp.full((A_GROUP * rows, 1), sink_ref[hk * A_GROUP], F32)
    for g in range(1, A_GROUP):
        col = jnp.where(head == g, sink_ref[hk * A_GROUP + g], col)
    return col


def _attn_a_prompt_kernel(sink_ref, q_ref, kv_ref, gq_ref, gk_ref, o_ref, kn_ref):
    scale = A_HEAD_DIM ** -0.5
    q = q_ref[...]
    kv = kv_ref[...]
    t = q.shape[0]
    outs = []
    kns = []
    for hk in range(A_KV_HEADS):
        k = _rms(kv[:, hk * A_HEAD_DIM:(hk + 1) * A_HEAD_DIM], gk_ref[...])
        v = kv[:, (A_KV_HEADS + hk) * A_HEAD_DIM:(A_KV_HEADS + hk + 1) * A_HEAD_DIM]
        kns.append(k)
        qg = jnp.concatenate([_rms(q[:, h * A_HEAD_DIM:(h + 1) * A_HEAD_DIM], gq_ref[...])
                              for h in range(hk * A_GROUP, (hk + 1) * A_GROUP)], axis=0)
        s = _bdot_nt(qg, k) * scale
        o = _softmax_attend([(s, v)], _group_sinks(sink_ref, hk, t))
        outs += [o[g * t:(g + 1) * t] for g in range(A_GROUP)]
    o_ref[...] = jnp.concatenate(outs, axis=-1).astype(o_ref.dtype)
    kn_ref[...] = jnp.concatenate(kns, axis=-1)


def _attn_a_prompt(proj, n_seq, t, sink, gq, gk):
    qw = A_HEADS * A_HEAD_DIM
    kvw = 2 * A_KV_HEADS * A_HEAD_DIM
    return pl.pallas_call(
        _attn_a_prompt_kernel,
        grid=(n_seq,),
        in_specs=[pl.BlockSpec(memory_space=pltpu.SMEM),
                  pl.BlockSpec((t, qw), lambda b: (b, 0)),
                  pl.BlockSpec((t, kvw), lambda b: (b, qw // kvw)),
                  pl.BlockSpec((1, A_HEAD_DIM), lambda b: (0, 0)),
                  pl.BlockSpec((1, A_HEAD_DIM), lambda b: (0, 0))],
        out_specs=[pl.BlockSpec((t, qw), lambda b: (b, 0)),
                   pl.BlockSpec((t, A_KV_HEADS * A_HEAD_DIM), lambda b: (b, 0))],
        out_shape=[jax.ShapeDtypeStruct((n_seq * t, qw), BF16),
                   jax.ShapeDtypeStruct((n_seq * t, A_KV_HEADS * A_HEAD_DIM), F32)],
        compiler_params=_params(("arbitrary",)),
        name="attn_a_context",
    )(sink, proj, proj, gq.reshape(1, -1), gk.reshape(1, -1))


def _attn_a_sample_kernel(sink_ref, q_ref, kv_ref, kc_ref, vc_ref, gq_ref, gk_ref, cos_ref, sin_ref,
                          o_ref, *, t):
    scale = A_HEAD_DIM ** -0.5
    i = pl.program_id(1)
    win = 3 * A_BLOCK
    q0 = pl.multiple_of(i * A_BLOCK, A_BLOCK)
    k0 = pl.multiple_of(jnp.clip((i - 1) * A_BLOCK, 0, t - win), A_BLOCK)
    q = q_ref[...]
    kv = kv_ref[pl.ds(k0, win), :]
    cq, sq = cos_ref[pl.ds(q0, A_BLOCK), :], sin_ref[pl.ds(q0, A_BLOCK), :]
    ck, sk = cos_ref[pl.ds(k0, win), :], sin_ref[pl.ds(k0, win), :]
    rows = A_GROUP * A_BLOCK
    qpos = q0 + lax.broadcasted_iota(jnp.int32, (rows, win), 0) % A_BLOCK
    kpos = k0 + lax.broadcasted_iota(jnp.int32, (rows, win), 1)
    mask = jnp.abs(qpos - kpos) <= A_WINDOW
    outs = []
    for hk in range(A_KV_HEADS):
        k = _rope(_rms(kv[:, hk * A_HEAD_DIM:(hk + 1) * A_HEAD_DIM], gk_ref[...]), ck, sk)
        v = kv[:, (A_KV_HEADS + hk) * A_HEAD_DIM:(A_KV_HEADS + hk + 1) * A_HEAD_DIM]
        qg = jnp.concatenate([_rope(_rms(q[:, h * A_HEAD_DIM:(h + 1) * A_HEAD_DIM], gq_ref[...]), cq, sq)
                              for h in range(hk * A_GROUP, (hk + 1) * A_GROUP)], axis=0)
        s1 = jnp.where(mask, _bdot_nt(qg, k) * scale, NEG_INF)
        s2 = _bdot_nt(qg, kc_ref[hk]) * scale
        o = _softmax_attend([(s1, v), (s2, vc_ref[hk])], _group_sinks(sink_ref, hk, A_BLOCK))
        outs += [o[g * A_BLOCK:(g + 1) * A_BLOCK] for g in range(A_GROUP)]
    o_ref[...] = jnp.concatenate(outs, axis=-1).astype(o_ref.dtype)


def _attn_a_sample(proj, row0, n_seq, t, k_ctx, v_ctx, sink, gq, gk):
    qw = A_HEADS * A_HEAD_DIM
    kvw = 2 * A_KV_HEADS * A_HEAD_DIM
    nqb = t // A_BLOCK
    cos, sin = _rope_tables(t, A_HEAD_DIM)
    past = k_ctx.shape[2]
    return pl.pallas_call(
        functools.partial(_attn_a_sample_kernel, t=t),
        grid=(n_seq, nqb),
        in_specs=[pl.BlockSpec(memory_space=pltpu.SMEM),
                  pl.BlockSpec((A_BLOCK, qw), lambda b, i: (row0 // A_BLOCK + b * nqb + i, 0)),
                  pl.BlockSpec((t, kvw), lambda b, i: (row0 // t + b, qw // kvw)),
                  pl.BlockSpec((None, A_KV_HEADS, past, A_HEAD_DIM), lambda b, i: (b, 0, 0, 0)),
                  pl.BlockSpec((None, A_KV_HEADS, past, A_HEAD_DIM), lambda b, i: (b, 0, 0, 0)),
                  pl.BlockSpec((1, A_HEAD_DIM), lambda b, i: (0, 0)),
                  pl.BlockSpec((1, A_HEAD_DIM), lambda b, i: (0, 0)),
                  pl.BlockSpec((t, A_HEAD_DIM), lambda b, i: (0, 0)),
                  pl.BlockSpec((t, A_HEAD_DIM), lambda b, i: (0, 0))],
        out_specs=pl.BlockSpec((A_BLOCK, qw), lambda b, i: (b * nqb + i, 0)),
        out_shape=jax.ShapeDtypeStruct((n_seq * t, qw), BF16),
        compiler_params=_params(("arbitrary", "arbitrary")),
        name="attn_a_latent",
    )(sink, proj, proj, k_ctx, v_ctx, gq.reshape(1, -1), gk.reshape(1, -1), cos, sin)


def _per_head_lanes(x, fn):
    lane = lax.broadcasted_iota(jnp.int32, x.shape, 1)
    lo = fn(x[:, :B_DK])
    hi = fn(x[:, B_DK:])
    return jnp.where(lane < B_DK, lo, hi)


def _conv_silu(x, w):
    t = x.shape[0]
    row = lax.broadcasted_iota(jnp.int32, x.shape, 0)
    prev = jnp.where(row == 0, 0.0, pltpu.roll(x, 1, 0))
    nxt = jnp.where(row == t - 1, 0.0, pltpu.roll(x, t - 1, 0))
    return _silu(prev * w[0:1, :] + x * w[1:2, :] + nxt * w[2:3, :])


M_SAME, M_TRI, M_TRI_T, M_STRICT, M_BDIAG = range(5)


def _delta_masks(r):
    c = B_CHUNK
    ii = lax.broadcasted_iota(jnp.int32, (r, r), 0)
    jj = lax.broadcasted_iota(jnp.int32, (r, r), 1)
    same = (ii // c) == (jj // c)
    ahead = jnp.where(ii < r // 2, ii - jj, jj - ii)
    tri = jnp.where(same, ahead, -1) >= 0
    tri_t = jnp.where(same, ahead, 1) <= 0
    strict = jnp.where(same, ahead, -1) > 0
    bdiag = (ii // 16) == (jj // 16)
    return [x.astype(F32) for x in (same, tri, tri_t, strict, bdiag)]


def _delta_prepare(q, k, v, g_col, g_row, beta, mask_ref):
    c = B_CHUNK
    r = q.shape[0]
    dot = functools.partial(jnp.dot, preferred_element_type=F32)
    gc_col = jnp.sum(mask_ref[M_TRI] * g_row, axis=1, keepdims=True)
    gc_row = jnp.sum(mask_ref[M_TRI_T] * g_col, axis=0, keepdims=True)
    g_tot = jnp.sum(mask_ref[M_SAME] * g_row, axis=1, keepdims=True)
    ex = jnp.exp((gc_col - gc_row) * mask_ref[M_TRI])
    kb = k * beta
    qk = _bdot_nt(jnp.concatenate([kb, q], axis=0), k)
    m = qk[:r] * (ex * mask_ref[M_STRICT])
    aqk = qk[r:] * (ex * mask_ref[M_TRI])
    dg = m * mask_ref[M_BDIAG]
    off = m - dg
    n1 = -dg
    n1b = n1.astype(BF16)
    n2 = dot(n1b, n1b)
    n2b = n2.astype(BF16)
    t = dot(jnp.concatenate([n1b, n2b], axis=0), n2b)
    xs = n1 + n2 + t[:r]
    n4 = t[r:]
    n4b = n4.astype(BF16)
    t = dot(jnp.concatenate([xs.astype(BF16), n4b], axis=0), n4b)
    xs = xs + n4 + t[:r]
    n8 = t[r:]
    xs = xs + n8 + dot(xs.astype(BF16), n8.astype(BF16))
    xsb = xs.astype(BF16)
    f = -(off + dot(xsb, off.astype(BF16)))
    fb = f.astype(BF16)
    t = dot(fb, jnp.concatenate([xsb, fb], axis=1))
    ys = xs + f + t[:, :r]
    f2 = t[:, r:]
    ts = ys + f2 + dot(f2.astype(BF16), ys.astype(BF16))
    egc = jnp.exp(gc_col)
    rhs = jnp.concatenate([v * beta, kb * egc], axis=-1)
    uw = rhs + _bdot(ts, rhs)

    def block_diag(x):
        return jnp.concatenate([x] * (r // c), axis=-1) * mask_ref[M_SAME]

    kd = k * jnp.exp(g_tot - gc_col)
    kd_t = jnp.concatenate([kd, jnp.zeros_like(kd)], axis=-1).T[:c]
    kd_t = jnp.concatenate([kd_t] * (r // c), axis=0) * mask_ref[M_SAME]
    e_tot = jnp.broadcast_to(jnp.exp(g_tot), v.shape)
    return uw[:, :B_DV], e_tot, block_diag(uw[:, B_DV:]), aqk, block_diag(q * egc), kd_t


def _deltanet_kernel(alog_ref, dtb_ref, q_ref, k_ref, v_ref, z_ref, cwq_ref, cwk_ref, cwv_ref,
                     tail_ref, tailt_ref, onorm_ref, s0f_ref, s0b_ref,
                     o_ref, sf_ref, sb_ref,
                     qc_ref, kc_ref, vc_ref, oacc_ref, u_ref, et_ref, w_ref, aqk_ref, qg_ref, kdt_ref,
                     mask_ref, *, n_chunks):
    hp = pl.program_id(1)
    c = B_CHUNK

    @pl.when(jnp.logical_and(pl.program_id(0) == 0, hp == 0))
    def _():
        for i, x in enumerate(_delta_masks(4 * c)):
            mask_ref[i] = x

    def l2n(x):
        ss = _per_head_lanes(x * x, lambda a: jnp.sum(a, axis=-1, keepdims=True))
        return x * lax.rsqrt(ss + EPS)

    qc_ref[...] = l2n(_conv_silu(q_ref[...], cwq_ref[...])) * (B_DK ** -0.5)
    kc_ref[...] = l2n(_conv_silu(k_ref[...], cwk_ref[...]))
    vc_ref[...] = _conv_silu(v_ref[...], cwv_ref[...])
    oacc_ref[...] = jnp.zeros_like(oacc_ref)

    lane32 = lax.broadcasted_iota(jnp.int32, (c, 4 * B_HEADS), 1)

    def gates(chunk, d, head):
        tail = tail_ref[pl.ds(pl.multiple_of(chunk * c, c), c), :]
        ia = 2 * d * B_HEADS + head
        ib = ia + B_HEADS
        a_col = jnp.sum(jnp.where(lane32 == ia, tail, 0.0), axis=1, keepdims=True)
        b_col = jnp.sum(jnp.where(lane32 == ib, tail, 0.0), axis=1, keepdims=True)
        a_row = tailt_ref[chunk, pl.ds(ia, 1), :]
        na = -jnp.exp(alog_ref[d, head])
        bias = dtb_ref[d, head]
        g_col = na * jax.nn.softplus(a_col + bias)
        g_row = na * jax.nn.softplus(a_row + bias)
        return g_col, g_row, jax.nn.sigmoid(b_col)

    def stacked(ref, chunks):
        parts = []
        for chunk in chunks:
            x = ref[pl.ds(pl.multiple_of(chunk * c, c), c), :]
            parts += [x[:, :B_DK], x[:, B_DK:]]
        return jnp.concatenate(parts, axis=0)

    def prepare(j, carry):
        chunks = (j, n_chunks - 1 - j)
        gs = [gates(chunks[d], d, 2 * hp + hh) for d in range(2) for hh in range(2)]
        g_col = jnp.concatenate([g[0] for g in gs], axis=0)
        g_row = jnp.concatenate([g[1] for g in gs], axis=1)
        beta = jnp.concatenate([g[2] for g in gs], axis=0)
        outs = _delta_prepare(stacked(qc_ref, chunks), stacked(kc_ref, chunks), stacked(vc_ref, chunks),
                              g_col, g_row, beta, mask_ref)
        for ref, x in zip((u_ref, et_ref, w_ref, aqk_ref, qg_ref, kdt_ref), outs):
            ref[j] = x.astype(ref.dtype)
        return carry

    lax.fori_loop(0, n_chunks, prepare, 0, unroll=4)

    def scan(j, s):
        sb = s.astype(BF16)
        delta = u_ref[j] - jnp.dot(w_ref[j], sb, preferred_element_type=F32)
        db = delta.astype(BF16)
        o = (jnp.dot(qg_ref[j], sb, preferred_element_type=F32)
             + jnp.dot(aqk_ref[j], db, preferred_element_type=F32))
        for d, chunk in enumerate((j, n_chunks - 1 - j)):
            rows = pl.ds(pl.multiple_of(chunk * c, c), c)
            oacc_ref[rows, :] += jnp.concatenate([o[2 * d * c:(2 * d + 1) * c],
                                                  o[(2 * d + 1) * c:(2 * d + 2) * c]], axis=-1)
        return s * et_ref[j] + jnp.dot(kdt_ref[j], db, preferred_element_type=F32)

    init = jnp.concatenate([s0f_ref[0], s0f_ref[1], s0b_ref[0], s0b_ref[1]], axis=0)
    fin = lax.fori_loop(0, n_chunks, scan, init)
    sf_ref[0], sf_ref[1], sb_ref[0], sb_ref[1] = (fin[i * B_DK:(i + 1) * B_DK] for i in range(4))

    o = oacc_ref[...]
    ms = _per_head_lanes(o * o, lambda a: jnp.mean(a, axis=-1, keepdims=True))
    o_ref[...] = (o * lax.rsqrt(ms + EPS) * onorm_ref[...] * _silu(z_ref[...])).astype(o_ref.dtype)


def _deltanet(proj, tail, row0, n_seq, t, conv_w, a_log, dt_bias, o_norm, s0_f, s0_b):
    c = B_CHUNK
    n_chunks = t // c
    lw = 2 * B_DK
    col_q = (A_HEADS + 2 * A_KV_HEADS) * A_HEAD_DIM // lw
    nhp = B_HEADS // 2
    rows = tail[row0:row0 + n_seq * t]
    tail_t = rows.reshape(n_seq, n_chunks, c, 4 * B_HEADS).transpose(0, 1, 3, 2)
    onorm2 = jnp.concatenate([o_norm, o_norm]).reshape(1, lw)
    b0 = row0 // t
    seq_blk = lambda off: pl.BlockSpec((t, lw), lambda b, h: (b0 + b, col_q + off + h))
    cw_blk = lambda off: pl.BlockSpec((3, lw), lambda b, h: (0, off + h))
    st_blk = pl.BlockSpec((None, 2, B_DK, B_DV), lambda b, h: (b, h, 0, 0))
    return pl.pallas_call(
        functools.partial(_deltanet_kernel, n_chunks=n_chunks),
        grid=(n_seq, nhp),
        in_specs=[pl.BlockSpec(memory_space=pltpu.SMEM), pl.BlockSpec(memory_space=pltpu.SMEM),
                  seq_blk(0), seq_blk(nhp), seq_blk(2 * nhp), seq_blk(3 * nhp),
                  cw_blk(0), cw_blk(nhp), cw_blk(2 * nhp),
                  pl.BlockSpec((t, 4 * B_HEADS), lambda b, h: (b, 0)),
                  pl.BlockSpec((None, n_chunks, 4 * B_HEADS, c), lambda b, h: (b, 0, 0, 0)),
                  pl.BlockSpec((1, lw), lambda b, h: (0, 0)),
                  st_blk, st_blk],
        out_specs=[pl.BlockSpec((t, lw), lambda b, h: (b, h)), st_blk, st_blk],
        out_shape=[jax.ShapeDtypeStruct((n_seq * t, B_HEADS * B_DV), BF16),
                   jax.ShapeDtypeStruct((n_seq, B_HEADS, B_DK, B_DV), F32),
                   jax.ShapeDtypeStruct((n_seq, B_HEADS, B_DK, B_DV), F32)],
        scratch_shapes=[pltpu.VMEM((t, lw), F32)] * 4 + [pltpu.VMEM((n_chunks, 4 * c, B_DV), F32)] * 2
        + [pltpu.VMEM((n_chunks, 4 * c, 4 * c), BF16)] * 4 + [pltpu.VMEM((5, 4 * c, 4 * c), F32)],
        compiler_params=_params(("arbitrary", "arbitrary")),
        name="deltanet",
    )(a_log, dt_bias, proj, proj, proj, proj, conv_w, conv_w, conv_w, rows, tail_t, onorm2, s0_f, s0_b)


def _lora_norm_kernel(p_ref, gq_ref, gkv_ref, cq_ref, ckv_ref, ckvb_ref):
    p = p_ref[...]
    cq_ref[...] = _rms(p[:, :C_Q_LORA], gq_ref[...]).astype(cq_ref.dtype)
    ckv = _rms(p[:, C_Q_LORA:C_Q_LORA + C_KV_LORA], gkv_ref[...])
    ckv_ref[...] = ckv
    ckvb_ref[...] = ckv.astype(ckvb_ref.dtype)


def _lora_norm(p1, gq, gkv, tm=512):
    n, w = p1.shape
    return pl.pallas_call(
        _lora_norm_kernel,
        grid=(n // tm,),
        in_specs=[pl.BlockSpec((tm, w), lambda i: (i, 0)),
                  pl.BlockSpec((1, C_Q_LORA), lambda i: (0, 0)),
                  pl.BlockSpec((1, C_KV_LORA), lambda i: (0, 0))],
        out_specs=[pl.BlockSpec((tm, C_Q_LORA), lambda i: (i, 0)),
                   pl.BlockSpec((tm, C_KV_LORA), lambda i: (i, 0)),
                   pl.BlockSpec((tm, C_KV_LORA), lambda i: (i, 0))],
        out_shape=[jax.ShapeDtypeStruct((n, C_Q_LORA), BF16),
                   jax.ShapeDtypeStruct((n, C_KV_LORA), F32),
                   jax.ShapeDtypeStruct((n, C_KV_LORA), BF16)],
        compiler_params=_params(("arbitrary",)),
        name="lora_norm",
    )(p1, gq.reshape(1, -1), gkv.reshape(1, -1))


def _mla_head_q(q, h, gq):
    qn, qr = q[:, h * C_QK:h * C_QK + C_NOPE], q[:, h * C_QK + C_NOPE:(h + 1) * C_QK]
    rn = lax.rsqrt((jnp.sum(qn * qn, axis=-1, keepdims=True)
                    + jnp.sum(qr * qr, axis=-1, keepdims=True)) / C_QK + EPS)
    return qn * rn * gq[:, :C_NOPE], qr * rn * gq[:, C_NOPE:]


def _mla_head_k(kv, kr, kr_ss, h, gk):
    kn = kv[:, h * (C_NOPE + C_V):h * (C_NOPE + C_V) + C_NOPE]
    v = kv[:, h * (C_NOPE + C_V) + C_NOPE:(h + 1) * (C_NOPE + C_V)]
    rn = lax.rsqrt((jnp.sum(kn * kn, axis=-1, keepdims=True) + kr_ss) / C_QK + EPS)
    return kn * rn * gk[:, :C_NOPE], kr * rn, v


C_STACK = 4


def _place(x, i, n):
    t, w = x.shape
    parts = ([jnp.zeros((t, i * w), x.dtype)] if i else []) + [x]
    if i < n - 1:
        parts.append(jnp.zeros((t, (n - 1 - i) * w), x.dtype))
    return jnp.concatenate(parts, axis=-1)


def _by_block(cols, lane, w):
    out = cols[-1]
    for i in reversed(range(len(cols) - 1)):
        out = jnp.where(lane < (i + 1) * w, cols[i], out)
    return out


def _attn_c_prompt_kernel(q_ref, kv_ref, p_ref, gq_ref, gk_ref, o_ref):
    scale = C_QK ** -0.5
    n = C_STACK
    q = q_ref[...]
    kv = kv_ref[...]
    gk = gk_ref[...]
    t = q.shape[0]
    kr_raw = p_ref[...][:, C_Q_LORA + C_KV_LORA:]
    kr_ss = jnp.sum(kr_raw * kr_raw, axis=-1, keepdims=True)
    kr_g = kr_raw * gk[:, C_NOPE:]
    lane_q = lax.broadcasted_iota(jnp.int32, (t, n * C_QK), 1)
    lane_o = lax.broadcasted_iota(jnp.int32, (t, n * C_V), 1)
    outs = []
    for grp in range(C_HEADS // n):
        qs = q[:, grp * n * C_QK:(grp + 1) * n * C_QK]
        rn = [lax.rsqrt(jnp.mean(qs[:, i * C_QK:(i + 1) * C_QK] ** 2, axis=-1, keepdims=True) + EPS)
              for i in range(n)]
        qs = qs * _by_block(rn, lane_q, C_QK) * gq_ref[...]
        k_rows, v_rows = [], []
        for i in range(n):
            kn, kr, v = _mla_head_k(kv, kr_g, kr_ss, grp * n + i, gk)
            k_rows.append(_place(jnp.concatenate([kn, kr], axis=-1), i, n))
            v_rows.append(_place(v, i, n))
        s = _bdot_nt(qs, jnp.concatenate(k_rows, axis=0)) * scale
        ps, rden = [], []
        for i in range(n):
            si = s[:, i * t:(i + 1) * t]
            pi = jnp.exp(si - si.max(axis=-1, keepdims=True))
            ps.append(pi)
            rden.append(1.0 / pi.sum(axis=-1, keepdims=True))
        o = _bdot(jnp.concatenate(ps, axis=-1), jnp.concatenate(v_rows, axis=0))
        outs.append(o * _by_block(rden, lane_o, C_V))
    o_ref[...] = jnp.concatenate(outs, axis=-1).astype(o_ref.dtype)


def _attn_c_prompt(q, kv, p1, n_seq, t, gq, gk):
    return pl.pallas_call(
        _attn_c_prompt_kernel,
        grid=(n_seq,),
        in_specs=[pl.BlockSpec((t, q.shape[1]), lambda b: (b, 0)),
                  pl.BlockSpec((t, kv.shape[1]), lambda b: (b, 0)),
                  pl.BlockSpec((t, p1.shape[1]), lambda b: (b, 0)),
                  pl.BlockSpec((1, C_STACK * C_QK), lambda b: (0, 0)),
                  pl.BlockSpec((1, C_QK), lambda b: (0, 0))],
        out_specs=pl.BlockSpec((t, C_HEADS * C_V), lambda b: (b, 0)),
        out_shape=jax.ShapeDtypeStruct((n_seq * t, C_HEADS * C_V), BF16),
        compiler_params=_params(("arbitrary",), VMEM_LIMIT),
        name="attn_c_context",
    )(q, kv, p1, jnp.tile(gq, C_STACK).reshape(1, -1), gk.reshape(1, -1))


def _attn_c_sample_kernel(q_ref, kv_ref, p_ref, kvc_ref, krc_ref, gq_ref, gk_ref, cos_ref, sin_ref,
                          o_ref, *, tq):
    scale = C_QK ** -0.5
    i = pl.program_id(1)
    q0 = pl.multiple_of(i * tq, tq)
    q = q_ref[...]
    kv = kv_ref[...]
    kvc = kvc_ref[...]
    gq, gk = gq_ref[...], gk_ref[...]
    cos, sin = cos_ref[...], sin_ref[...]
    cq, sq = cos_ref[pl.ds(q0, tq), :], sin_ref[pl.ds(q0, tq), :]
    kr_raw = p_ref[...][:, C_Q_LORA + C_KV_LORA:]
    kr_ss = jnp.sum(kr_raw * kr_raw, axis=-1, keepdims=True)
    kr_g = _rope(kr_raw * gk[:, C_NOPE:], cos, sin)
    krc_raw = krc_ref[...]
    krc_ss = jnp.sum(krc_raw * krc_raw, axis=-1, keepdims=True)
    krc_g = krc_raw * gk[:, C_NOPE:]
    outs = []
    for h in range(C_HEADS):
        qn, qr = _mla_head_q(q, h, gq)
        qr = _rope(qr, cq, sq)
        kn, kr, v = _mla_head_k(kv, kr_g, kr_ss, h, gk)
        knc, krc, vc = _mla_head_k(kvc, krc_g, krc_ss, h, gk)
        s1 = (_bdot_nt(qn, kn) + _bdot_nt(qr, kr)) * scale
        s2 = (_bdot_nt(qn, knc) + _bdot_nt(qr, krc)) * scale
        outs.append(_softmax_attend([(s1, v), (s2, vc)], None))
    o_ref[...] = jnp.concatenate(outs, axis=-1).astype(o_ref.dtype)


def _attn_c_sample(q, kv, p1, row0, n_seq, t, kr_ctx, gq, gk, tq=256):
    n = p1.shape[0]
    past = kr_ctx.shape[1]
    nq = t // tq
    cos, sin = _rope_tables(t, C_ROPE)
    return pl.pallas_call(
        functools.partial(_attn_c_sample_kernel, tq=tq),
        grid=(n_seq, nq),
        in_specs=[pl.BlockSpec((tq, q.shape[1]), lambda b, i: (row0 // tq + b * nq + i, 0)),
                  pl.BlockSpec((t, kv.shape[1]), lambda b, i: (row0 // t + b, 0)),
                  pl.BlockSpec((t, p1.shape[1]), lambda b, i: (row0 // t + b, 0)),
                  pl.BlockSpec((past, kv.shape[1]), lambda b, i: (n // past + b, 0)),
                  pl.BlockSpec((None, past, C_ROPE), lambda b, i: (b, 0, 0)),
                  pl.BlockSpec((1, C_QK), lambda b, i: (0, 0)),
                  pl.BlockSpec((1, C_QK), lambda b, i: (0, 0)),
                  pl.BlockSpec((t, C_ROPE), lambda b, i: (0, 0)),
                  pl.BlockSpec((t, C_ROPE), lambda b, i: (0, 0))],
        out_specs=pl.BlockSpec((tq, C_HEADS * C_V), lambda b, i: (b * nq + i, 0)),
        out_shape=jax.ShapeDtypeStruct((n_seq * t, C_HEADS * C_V), BF16),
        compiler_params=_params(("arbitrary", "arbitrary"), VMEM_LIMIT),
        name="attn_c_latent",
    )(q, kv, p1, kv, kr_ctx, gq.reshape(1, -1), gk.reshape(1, -1), cos, sin)


ROUTE_TILE = 512
ROUTE_BLOCK = 512
SLOT_ALIGN = 8
LOCAL_SLOTS = 2304
META_LANES = 128


def _router_kernel(x_ref, g_ref, sc_ref, sh_ref, wr_ref, br_ref, h_ref, meta_ref, cntb_ref, cnt_ref,
                   run_ref):
    @pl.when(pl.program_id(0) == 0)
    def _():
        run_ref[...] = jnp.zeros_like(run_ref)

    h = _rms(x_ref[...], g_ref[...]) * (1 + sc_ref[...]) + sh_ref[...]
    h_ref[...] = h.astype(h_ref.dtype)
    tm = h.shape[0]
    logits = _hdot(h, wr_ref[...]) + br_ref[...]
    lane = lax.broadcasted_iota(jnp.int32, logits.shape, 1)
    work = logits
    picks, tops = [], []
    for _ in range(TOP_K):
        m = work.max(axis=-1, keepdims=True)
        first = jnp.min(jnp.where(work == m, lane, N_EXPERTS), axis=-1, keepdims=True)
        pick = lane == first
        picks.append(pick)
        tops.append(m)
        work = jnp.where(pick, -jnp.inf, work)
    sel = sum(p.astype(F32) for p in picks)
    earlier = (lax.broadcasted_iota(jnp.int32, (tm, tm), 0)
               > lax.broadcasted_iota(jnp.int32, (tm, tm), 1)).astype(BF16)
    inside = jnp.dot(earlier, sel.astype(BF16), preferred_element_type=F32)
    cnt = jnp.sum(sel, axis=0, keepdims=True)
    run = jnp.ceil(cnt / SLOT_ALIGN) * SLOT_ALIGN
    lower_e = (lax.broadcasted_iota(jnp.int32, (N_EXPERTS, N_EXPERTS), 0)
               < lax.broadcasted_iota(jnp.int32, (N_EXPERTS, N_EXPERTS), 1)).astype(BF16)
    start = jnp.dot(jnp.broadcast_to(run, (8, N_EXPERTS)).astype(BF16), lower_e,
                    preferred_element_type=F32)[0:1]
    slot = start + inside
    ws = [jnp.exp(t - tops[0]) for t in tops]
    den = sum(ws)
    mlane = lax.broadcasted_iota(jnp.int32, (tm, META_LANES), 1)
    meta = jnp.zeros((tm, META_LANES), F32)
    for k in range(TOP_K):
        meta = jnp.where(mlane == k, jnp.sum(jnp.where(picks[k], slot, 0.0), axis=-1, keepdims=True), meta)
        meta = jnp.where(mlane == TOP_K + k, ws[k] / den, meta)
    meta_ref[...] = meta
    cntb_ref[...] = cnt
    run_ref[...] += run
    cnt_ref[...] = run_ref[...]


def _router(x, gain, scale, shift, w_router, b_router, layer, n_p, t_s):
    n, d = x.shape
    tm = ROUTE_BLOCK
    e = w_router.shape[-1]
    group = _group_of_tile(tm, n_p, t_s)
    return pl.pallas_call(
        _router_kernel,
        grid=(n // tm,),
        in_specs=[pl.BlockSpec((tm, d), lambda i: (i, 0)),
                  pl.BlockSpec((1, d), lambda i: (0, 0)),
                  pl.BlockSpec((None, 1, d), lambda i: (group(i), 0, 0)),
                  pl.BlockSpec((None, 1, d), lambda i: (group(i), 0, 0)),
                  pl.BlockSpec((None, d, e), lambda i: (layer, 0, 0)),
                  pl.BlockSpec((None, 1, e), lambda i: (layer, 0, 0))],
        out_specs=[pl.BlockSpec((tm, d), lambda i: (i, 0)),
                   pl.BlockSpec((tm, META_LANES), lambda i: (i, 0)),
                   pl.BlockSpec((None, 1, e), lambda i: (i, 0, 0)),
                   pl.BlockSpec((1, e), lambda i: (0, 0))],
        out_shape=[jax.ShapeDtypeStruct((n, d), BF16),
                   jax.ShapeDtypeStruct((n, META_LANES), F32),
                   jax.ShapeDtypeStruct((n // tm, 1, e), F32),
                   jax.ShapeDtypeStruct((1, e), F32)],
        scratch_shapes=[pltpu.VMEM((1, e), F32)],
        compiler_params=_params(("arbitrary",)),
        name="router",
    )(x, gain.reshape(1, d), scale.reshape(N_GROUPS, 1, d), shift.reshape(N_GROUPS, 1, d),
      w_router, b_router.reshape(-1, 1, e))


def _route_plan(cnt_blk, cnt_tot, n_tiles, min_tiles):
    cb = cnt_blk[:, 0, :].astype(jnp.int32)
    run = (cb + SLOT_ALIGN - 1) // SLOT_ALIGN * SLOT_ALIGN
    counts = cnt_tot[0].astype(jnp.int32)
    padded = (counts + ROUTE_TILE - 1) // ROUTE_TILE * ROUTE_TILE
    ends = jnp.cumsum(padded)
    offs = ends - padded
    gstart = offs[None, :] + jnp.cumsum(run, axis=0) - run
    lstart = jnp.cumsum(run, axis=1) - run
    tile_start = jnp.arange(n_tiles, dtype=jnp.int32) * ROUTE_TILE
    n_valid = ends[-1] // ROUTE_TILE
    te = jnp.minimum(jnp.sum(ends[None, :] <= tile_start[:, None], axis=1), N_EXPERTS - 1)
    te = jnp.where(tile_start < ends[-1], te, te[jnp.maximum(n_valid - 1, 0)]).astype(jnp.int32)
    rows = jnp.clip((offs + counts)[te] - tile_start, 0, ROUTE_TILE).astype(jnp.int32)
    ragged = jnp.where(counts % ROUTE_TILE != 0, ends // ROUTE_TILE - 1, -1)
    tail = jnp.arange(min_tiles, n_tiles, dtype=jnp.int32)
    fill = jnp.concatenate([ragged, jnp.where(tail >= n_valid, tail, -1)]).astype(jnp.int32)
    runs = (gstart.reshape(-1).astype(jnp.int32), lstart.reshape(-1).astype(jnp.int32),
            (run // SLOT_ALIGN).reshape(-1).astype(jnp.int32))
    return runs, te, rows, n_valid.reshape(1).astype(jnp.int32), fill


def _for_each_run_chunk(blk, gstart_ref, lstart_ref, nch_ref, fn):
    def per_expert(e, carry):
        idx = blk * N_EXPERTS + e
        g0, l0 = gstart_ref[idx], lstart_ref[idx]

        def per_chunk(i, c2):
            fn(pl.multiple_of(l0 + i * SLOT_ALIGN, SLOT_ALIGN), pl.multiple_of(g0 + i * SLOT_ALIGN, SLOT_ALIGN))
            return c2

        lax.fori_loop(0, nch_ref[idx], per_chunk, 0)
        return carry

    lax.fori_loop(0, N_EXPERTS, per_expert, 0)


def _dispatch_kernel(fill_ref, gstart_ref, lstart_ref, nch_ref, meta_ref, h_ref, xs_ref,
                     loc_ref, zero_ref, sem, zsem):
    blk = pl.program_id(0)

    @pl.when(blk == 0)
    def _():
        zero_ref[...] = jnp.zeros_like(zero_ref)

        def fill_copy(j):
            row0 = pl.multiple_of(fill_ref[j] * ROUTE_TILE, ROUTE_TILE)
            return pltpu.make_async_copy(zero_ref, xs_ref.at[pl.ds(row0, ROUTE_TILE), :], zsem)

        def start(j, carry):
            @pl.when(fill_ref[j] >= 0)
            def _():
                fill_copy(j).start()
            return carry

        def wait(j, carry):
            @pl.when(fill_ref[j] >= 0)
            def _():
                fill_copy(j).wait()
            return carry

        lax.fori_loop(0, fill_ref.shape[0], start, 0)
        lax.fori_loop(0, fill_ref.shape[0], wait, 0)

    hb = h_ref[...]
    tm = hb.shape[0]
    slots_t = meta_ref[...].T[0:TOP_K]
    rows = 256
    for c0 in range(0, LOCAL_SLOTS, rows):
        slot = (c0 + lax.broadcasted_iota(jnp.int32, (rows, tm), 0)).astype(F32)
        onehot = sum(jnp.where(slot == slots_t[k:k + 1], 1.0, 0.0) for k in range(TOP_K))
        loc_ref[c0:c0 + rows, :] = jnp.dot(onehot.astype(BF16), hb,
                                           preferred_element_type=F32).astype(loc_ref.dtype)

    def copy(l0, g0):
        return pltpu.make_async_copy(loc_ref.at[pl.ds(l0, SLOT_ALIGN), :], xs_ref.at[pl.ds(g0, SLOT_ALIGN), :],
                                     sem)

    _for_each_run_chunk(blk, gstart_ref, lstart_ref, nch_ref, lambda l0, g0: copy(l0, g0).start())
    _for_each_run_chunk(blk, gstart_ref, lstart_ref, nch_ref, lambda l0, g0: copy(l0, g0).wait())


def _dispatch(h, meta, runs, fill, n_slots):
    n, d = h.shape
    tm = ROUTE_BLOCK
    grid_spec = pltpu.PrefetchScalarGridSpec(
        num_scalar_prefetch=4,
        grid=(n // tm,),
        in_specs=[pl.BlockSpec((tm, META_LANES), lambda i, *_: (i, 0)),
                  pl.BlockSpec((tm, d), lambda i, *_: (i, 0))],
        out_specs=pl.BlockSpec(memory_space=pl.ANY),
        scratch_shapes=[pltpu.VMEM((LOCAL_SLOTS, d), F32), pltpu.VMEM((ROUTE_TILE, d), F32),
                        pltpu.SemaphoreType.DMA(()), pltpu.SemaphoreType.DMA(())])
    return pl.pallas_call(
        _dispatch_kernel,
        grid_spec=grid_spec,
        out_shape=jax.ShapeDtypeStruct((n_slots, d), F32),
        compiler_params=_params(("arbitrary",), VMEM_LIMIT),
        name="moe_dispatch",
    )(fill, *runs, meta, h)


def _experts_kernel(te_ref, rows_ref, nv_ref, x_ref, wgu_ref, bgu_ref, wd_ref, bd_ref, y_ref,
                    wgub_ref, wdb_ref, *, d_ff):
    i = pl.program_id(0)
    valid = i < nv_ref[0]
    fresh = jnp.logical_or(i == 0, te_ref[i] != te_ref[jnp.maximum(i - 1, 0)])

    @pl.when(jnp.logical_and(valid, fresh))
    def _():
        wgub_ref[...] = wgu_ref[...].astype(BF16)
        wdb_ref[...] = wd_ref[...].astype(BF16)

    @pl.when(valid)
    def _():
        row = lax.broadcasted_iota(jnp.int32, x_ref.shape, 0)
        x = jnp.where(row < rows_ref[i], x_ref[...], 0.0).astype(BF16)
        gu = jnp.dot(x, wgub_ref[...], preferred_element_type=F32) + bgu_ref[...]
        gate = jnp.minimum(gu[:, :d_ff], SWIGLU_LIMIT)
        up = jnp.clip(gu[:, d_ff:], -SWIGLU_LIMIT, SWIGLU_LIMIT)
        act = (up + 1) * gate * jax.nn.sigmoid(SWIGLU_ALPHA * gate)
        y_ref[...] = jnp.dot(act.astype(BF16), wdb_ref[...], preferred_element_type=F32) + bd_ref[...]

    @pl.when(jnp.logical_not(valid))
    def _():
        y_ref[...] = jnp.zeros_like(y_ref)


def _experts(xs, te, rows, n_valid, w_gu, b_gu, w_down, b_down, layer):
    n_slots, d = xs.shape
    _, e, _, two_ff = w_gu.shape
    last = lambda i, nv: jnp.minimum(i, nv[0] - 1)
    grid_spec = pltpu.PrefetchScalarGridSpec(
        num_scalar_prefetch=3,
        grid=(n_slots // ROUTE_TILE,),
        in_specs=[pl.BlockSpec((ROUTE_TILE, d), lambda i, te, rw, nv: (last(i, nv), 0)),
                  pl.BlockSpec((None, None, d, two_ff), lambda i, te, rw, nv: (layer, te[i], 0, 0)),
                  pl.BlockSpec((None, None, 1, two_ff), lambda i, te, rw, nv: (layer, te[i], 0, 0)),
                  pl.BlockSpec((None, None, two_ff // 2, d), lambda i, te, rw, nv: (layer, te[i], 0, 0)),
                  pl.BlockSpec((None, None, 1, d), lambda i, te, rw, nv: (layer, te[i], 0, 0))],
        out_specs=pl.BlockSpec((ROUTE_TILE, d), lambda i, te, rw, nv: (i, 0)),
        scratch_shapes=[pltpu.VMEM((d, two_ff), BF16), pltpu.VMEM((two_ff // 2, d), BF16)])
    return pl.pallas_call(
        functools.partial(_experts_kernel, d_ff=two_ff // 2),
        grid_spec=grid_spec,
        out_shape=jax.ShapeDtypeStruct((n_slots, d), F32),
        compiler_params=_params(("arbitrary",), VMEM_LIMIT),
        name="moe_experts",
    )(te, rows, n_valid, xs, w_gu, b_gu.reshape(b_gu.shape[0], e, 1, two_ff), w_down,
      b_down.reshape(b_down.shape[0], e, 1, d))


def _combine_kernel(gstart_ref, lstart_ref, nch_ref, meta_ref, x_ref, gate_ref, y_ref, o_ref, loc_ref, sem):
    blk = pl.program_id(0)

    @pl.when(blk == 0)
    def _():
        loc_ref[...] = jnp.zeros_like(loc_ref)

    def copy(l0, g0):
        return pltpu.make_async_copy(y_ref.at[pl.ds(g0, SLOT_ALIGN), :], loc_ref.at[pl.ds(l0, SLOT_ALIGN), :],
                                     sem)

    _for_each_run_chunk(blk, gstart_ref, lstart_ref, nch_ref, lambda l0, g0: copy(l0, g0).start())
    _for_each_run_chunk(blk, gstart_ref, lstart_ref, nch_ref, lambda l0, g0: copy(l0, g0).wait())

    meta = meta_ref[...]
    tm = meta.shape[0]
    cols = 256
    acc = jnp.zeros(o_ref.shape, F32)
    for c0 in range(0, LOCAL_SLOTS, cols):
        slot = (c0 + lax.broadcasted_iota(jnp.int32, (tm, cols), 1)).astype(F32)
        wts = sum(jnp.where(slot == meta[:, k:k + 1], meta[:, TOP_K + k:TOP_K + k + 1], 0.0)
                  for k in range(TOP_K))
        acc = acc + jnp.dot(wts.astype(BF16), loc_ref[c0:c0 + cols, :].astype(BF16),
                            preferred_element_type=F32)
    o_ref[...] = x_ref[...] + gate_ref[...] * acc


def _combine(y, meta, runs, x, gate, n_p, t_s):
    n, d = x.shape
    tm = ROUTE_BLOCK
    group = _group_of_tile(tm, n_p, t_s)
    grid_spec = pltpu.PrefetchScalarGridSpec(
        num_scalar_prefetch=3,
        grid=(n // tm,),
        in_specs=[pl.BlockSpec((tm, META_LANES), lambda i, *_: (i, 0)),
                  pl.BlockSpec((tm, d), lambda i, *_: (i, 0)),
                  pl.BlockSpec((None, 1, d), lambda i, *_: (group(i), 0, 0)),
                  pl.BlockSpec(memory_space=pl.ANY)],
        out_specs=pl.BlockSpec((tm, d), lambda i, *_: (i, 0)),
        scratch_shapes=[pltpu.VMEM((LOCAL_SLOTS, d), F32), pltpu.SemaphoreType.DMA(())])
    return pl.pallas_call(
        _combine_kernel,
        grid_spec=grid_spec,
        out_shape=jax.ShapeDtypeStruct((n, d), F32),
        compiler_params=_params(("arbitrary",), VMEM_LIMIT),
        name="moe_combine",
    )(*runs, meta, x, gate.reshape(N_GROUPS, 1, d), y)


def _moe(x, gain, scale, shift, gate, w_router, b_router, w_gu, b_gu, w_down, b_down, layer, n_p, t_s):
    n = x.shape[0]
    assert LOCAL_SLOTS >= ROUTE_BLOCK * TOP_K + N_EXPERTS * (SLOT_ALIGN - 1) and n % ROUTE_BLOCK == 0
    min_tiles = n * TOP_K // ROUTE_TILE
    max_slots = n * TOP_K + (n // ROUTE_BLOCK) * N_EXPERTS * (SLOT_ALIGN - 1) + N_EXPERTS * (ROUTE_TILE - 1)
    n_tiles = -(-max_slots // ROUTE_TILE)
    h, meta, cnt_blk, cnt_tot = _router(x, gain, scale, shift, w_router, b_router, layer, n_p, t_s)
    runs, te, rows, n_valid, fill = _route_plan(cnt_blk, cnt_tot, n_tiles, min_tiles)
    xs = _dispatch(h, meta, runs, fill, n_tiles * ROUTE_TILE)
    y = _experts(xs, te, rows, n_valid, w_gu, b_gu, w_down, b_down, layer)
    return _combine(y, meta, runs, x, gate, n_p, t_s)


def kernel(x_prompt, x_sample, c, cache_a_k, cache_a_v, state_b_fwd, state_b_bwd, cache_c_ckv,
           cache_c_krope, c_ctx, w_mod, b_mod, norm_mix, norm_ffn, e_w_in, e_w_out, e_a_qnorm,
           e_a_knorm, e_a_sink, e_b_conv, e_b_alog, e_b_dtbias, e_b_onorm, o_w_in, o_q_lora_norm,
           o_kv_lora_norm, o_w_uq, o_w_ukv, o_qnorm, o_knorm, o_w_out, moe_w_router, moe_b_router,
           moe_w_gu, moe_b_gu, moe_w_down, moe_b_down):
    bp, tp, d = x_prompt.shape
    bs, ts, _ = x_sample.shape
    depth = w_mod.shape[0]
    n_p, n_s = bp * tp, bs * ts
    n = n_p + n_s
    assert bs + 1 <= N_GROUPS and ts % 512 == 0 and n_p % ts == 0

    x = jnp.concatenate([x_prompt.reshape(n_p, d), x_sample.reshape(n_s, d)], axis=0)
    cond = jnp.concatenate([c_ctx[None], c, jnp.zeros((N_GROUPS - 1 - bs, d), F32)], axis=0)
    mod = _adaln(cond, w_mod, b_mod)

    new_a_k, new_a_v, new_b_fwd, new_b_bwd, new_c_ckv, new_c_krope = [], [], [], [], [], []
    for layer in range(depth):
        sh1, sc1, g1, sh2, sc2, g2 = (mod[layer, j] for j in range(6))
        h = _modulate(x, norm_mix[layer], sc1, sh1, n_p, ts)
        i = layer // 2
        if layer % 2 == 0:
            main_w = (A_HEADS + 2 * A_KV_HEADS) * A_HEAD_DIM + 4 * B_HEADS * B_DK
            proj = _matmul(h, e_w_in, i, 0, main_w // 2, main_w, name="even_in_proj")
            tail = _matmul(h, e_w_in[i][None, :, main_w:], 0, 0, 4 * B_HEADS, 4 * B_HEADS,
                           name="even_gate_proj")
            oa_p, kn_p = _attn_a_prompt(proj, bp, tp, e_a_sink[i], e_a_qnorm[i], e_a_knorm[i])
            oa_s = _attn_a_sample(proj, n_p, bs, ts, cache_a_k[:, i], cache_a_v[:, i],
                                  e_a_sink[i], e_a_qnorm[i], e_a_knorm[i])
            zeros = jnp.zeros((bp, B_HEADS, B_DK, B_DV), F32)
            ob_p, s_f, s_b = _deltanet(proj, tail, 0, bp, tp, e_b_conv[i], e_b_alog[i],
                                       e_b_dtbias[i], e_b_onorm[i], zeros, zeros)
            ob_s, _, _ = _deltanet(proj, tail, n_p, bs, ts, e_b_conv[i], e_b_alog[i],
                                   e_b_dtbias[i], e_b_onorm[i], state_b_fwd[:, i], state_b_bwd[:, i])
            mix = jnp.concatenate([jnp.concatenate([oa_p, ob_p], axis=1),
                                   jnp.concatenate([oa_s, ob_s], axis=1)], axis=0)
            x = _matmul_residual(mix, e_w_out, i, x, g1, n_p, ts)
            kw = A_KV_HEADS * A_HEAD_DIM
            new_a_k.append(kn_p.reshape(bp, tp, A_KV_HEADS, A_HEAD_DIM).transpose(0, 2, 1, 3))
            v_p = proj[:n_p, A_HEADS * A_HEAD_DIM + kw:A_HEADS * A_HEAD_DIM + 2 * kw]
            new_a_v.append(v_p.reshape(bp, tp, A_KV_HEADS, A_HEAD_DIM).transpose(0, 2, 1, 3))
            new_b_fwd.append(s_f)
            new_b_bwd.append(s_b)
        else:
            p1 = _matmul(h, o_w_in, i, 0, o_w_in.shape[-1], o_w_in.shape[-1], name="odd_in_proj")
            cq, ckv, ckv_b = _lora_norm(p1, o_q_lora_norm[i], o_kv_lora_norm[i])
            q = _matmul(cq, o_w_uq, i, 0, o_w_uq.shape[-1] // 2, o_w_uq.shape[-1], name="odd_uq")
            ckv_all = jnp.concatenate([ckv_b, cache_c_ckv[:, i].reshape(-1, C_KV_LORA).astype(BF16)], 0)
            kv = _matmul(ckv_all, o_w_ukv, i, 0, o_w_ukv.shape[-1] // 2, o_w_ukv.shape[-1],
                         name="odd_ukv")
            o_p = _attn_c_prompt(q, kv, p1, bp, tp, o_qnorm[i], o_knorm[i])
            o_s = _attn_c_sample(q, kv, p1, n_p, bs, ts, cache_c_krope[:, i], o_qnorm[i], o_knorm[i])
            x = _matmul_residual(jnp.concatenate([o_p, o_s], axis=0), o_w_out, i, x, g1, n_p, ts)
            new_c_ckv.append(ckv[:n_p].reshape(bp, tp, C_KV_LORA))
            new_c_krope.append(p1[:n_p, C_Q_LORA + C_KV_LORA:].reshape(bp, tp, C_ROPE))
        x = _moe(x, norm_ffn[layer], sc2, sh2, g2, moe_w_router, moe_b_router, moe_w_gu, moe_b_gu,
                 moe_w_down, moe_b_down, layer, n_p, ts)

    return (x[:n_p].reshape(bp, tp, d), x[n_p:].reshape(bs, ts, d),
            jnp.stack(new_a_k, axis=1), jnp.stack(new_a_v, axis=1),
            jnp.stack(new_b_fwd, axis=1), jnp.stack(new_b_bwd, axis=1),
            jnp.stack(new_c_ckv, axis=1), jnp.stack(new_c_krope, axis=1))
```

```python
import functools
import math

import numpy as np
import jax
import jax.numpy as jnp
from jax import lax
from jax.experimental import pallas as pl
from jax.experimental.pallas import tpu as pltpu

F32 = jnp.float32
BF16 = jnp.bfloat16
HIGHEST = lax.Precision.HIGHEST

EPS = 1e-6
NEG_INF = -1e30
ROPE_BASE = 10000.0
GRID_W = 64
N_GROUPS = 8

A_HEADS, A_KV_HEADS, A_GROUP, A_HEAD_DIM, A_WINDOW, A_BLOCK = 8, 2, 4, 64, 128, 128
B_HEADS, B_DK, B_DV, B_CHUNK = 8, 64, 64, 64
C_HEADS, C_NOPE, C_ROPE, C_V, C_Q_LORA, C_KV_LORA = 16, 64, 32, 64, 384, 256
C_QK = C_NOPE + C_ROPE
N_EXPERTS, TOP_K = 32, 4
SWIGLU_LIMIT, SWIGLU_ALPHA = 7.0, 1.702

VMEM_LIMIT = 56 * 1024 * 1024


def _params(sem, vmem=None):
    return pltpu.CompilerParams(dimension_semantics=sem, vmem_limit_bytes=vmem)


def _bdot(a, b):
    return jnp.dot(a.astype(BF16), b.astype(BF16), preferred_element_type=F32)


def _bdot_nt(a, b):
    return lax.dot_general(a.astype(BF16), b.astype(BF16), (((1,), (1,)), ((), ())),
                           preferred_element_type=F32)


def _bdot_tn(a, b):
    return lax.dot_general(a.astype(BF16), b.astype(BF16), (((0,), (0,)), ((), ())),
                           preferred_element_type=F32)


def _hdot(a, b):
    return jnp.dot(a, b, preferred_element_type=F32, precision=HIGHEST)


def _hdot_nt(a, b):
    return lax.dot_general(a, b, (((1,), (1,)), ((), ())), preferred_element_type=F32,
                           precision=HIGHEST)


def _rms(x, gain):
    return x * lax.rsqrt(jnp.mean(x * x, axis=-1, keepdims=True) + EPS) * gain


def _silu(x):
    return x * jax.nn.sigmoid(x)


def _group_of_tile(tm, n_p, t_s):
    def group(i):
        r = i * tm
        return jnp.where(r < n_p, 0, 1 + (r - n_p) // t_s)
    return group


def _adaln_kernel(cond_ref, w_ref, b_ref, o_ref):
    o_ref[...] = _bdot(_silu(cond_ref[...]), w_ref[...]) + b_ref[...]


def _adaln(cond, w_mod, b_mod):
    depth, d, _ = w_mod.shape
    return pl.pallas_call(
        _adaln_kernel,
        grid=(depth, 6),
        in_specs=[pl.BlockSpec((N_GROUPS, d), lambda l, j: (0, 0)),
                  pl.BlockSpec((None, d, d), lambda l, j: (l, 0, j)),
                  pl.BlockSpec((None, 1, d), lambda l, j: (l, 0, j))],
        out_specs=pl.BlockSpec((None, None, N_GROUPS, d), lambda l, j: (l, j, 0, 0)),
        out_shape=jax.ShapeDtypeStruct((depth, 6, N_GROUPS, d), F32),
        compiler_params=_params(("arbitrary", "arbitrary")),
        name="adaln",
    )(cond, w_mod, b_mod.reshape(depth, 1, 6 * d))


def _modulate_kernel(x_ref, g_ref, sc_ref, sh_ref, o_ref):
    y = _rms(x_ref[...], g_ref[...])
    o_ref[...] = (y * (1 + sc_ref[...]) + sh_ref[...]).astype(o_ref.dtype)


def _modulate(x, gain, scale, shift, n_p, t_s, tm=512):
    n, d = x.shape
    group = _group_of_tile(tm, n_p, t_s)
    return pl.pallas_call(
        _modulate_kernel,
        grid=(n // tm,),
        in_specs=[pl.BlockSpec((tm, d), lambda i: (i, 0)),
                  pl.BlockSpec((1, d), lambda i: (0, 0)),
                  pl.BlockSpec((None, 1, d), lambda i: (group(i), 0, 0)),
                  pl.BlockSpec((None, 1, d), lambda i: (group(i), 0, 0))],
        out_specs=pl.BlockSpec((tm, d), lambda i: (i, 0)),
        out_shape=jax.ShapeDtypeStruct((n, d), BF16),
        compiler_params=_params(("arbitrary",)),
        name="modulate",
    )(x, gain.reshape(1, d), scale.reshape(N_GROUPS, 1, d), shift.reshape(N_GROUPS, 1, d))


def _mm_kernel(x_ref, w_ref, o_ref, wb_ref):
    @pl.when(pl.program_id(1) == 0)
    def _():
        wb_ref[...] = w_ref[...].astype(BF16)
    o_ref[...] = jnp.dot(x_ref[...], wb_ref[...], preferred_element_type=F32).astype(o_ref.dtype)


def _matmul(x, w3, layer, col0_blocks, tn, n_out, out_dtype=F32, tm=512, name="matmul"):
    n, k = x.shape
    return pl.pallas_call(
        _mm_kernel,
        grid=(n_out // tn, n // tm),
        in_specs=[pl.BlockSpec((tm, k), lambda j, i: (i, 0)),
                  pl.BlockSpec((None, k, tn), lambda j, i: (layer, 0, col0_blocks + j))],
        out_specs=pl.BlockSpec((tm, tn), lambda j, i: (i, j)),
        out_shape=jax.ShapeDtypeStruct((n, n_out), out_dtype),
        scratch_shapes=[pltpu.VMEM((k, tn), BF16)],
        compiler_params=_params(("arbitrary", "arbitrary"), VMEM_LIMIT),
        name=name,
    )(x, w3)


def _mm_res_kernel(x_ref, w_ref, res_ref, gate_ref, o_ref, wb_ref):
    @pl.when(pl.program_id(0) == 0)
    def _():
        wb_ref[...] = w_ref[...].astype(BF16)
    y = jnp.dot(x_ref[...], wb_ref[...], preferred_element_type=F32)
    o_ref[...] = res_ref[...] + gate_ref[...] * y


def _matmul_residual(mix, w3, layer, res, gate, n_p, t_s, tm=512):
    n, k = mix.shape
    d = res.shape[1]
    group = _group_of_tile(tm, n_p, t_s)
    return pl.pallas_call(
        _mm_res_kernel,
        grid=(n // tm,),
        in_specs=[pl.BlockSpec((tm, k), lambda i: (i, 0)),
                  pl.BlockSpec((None, k, d), lambda i: (layer, 0, 0)),
                  pl.BlockSpec((tm, d), lambda i: (i, 0)),
                  pl.BlockSpec((None, 1, d), lambda i: (group(i), 0, 0))],
        out_specs=pl.BlockSpec((tm, d), lambda i: (i, 0)),
        out_shape=jax.ShapeDtypeStruct((n, d), F32),
        scratch_shapes=[pltpu.VMEM((k, d), BF16)],
        compiler_params=_params(("arbitrary",), VMEM_LIMIT),
        name="out_proj_residual",
    )(mix, w3, res, gate.reshape(N_GROUPS, 1, d))


def _rope_tables(t_len, d):
    half, quarter = d // 2, d // 4
    pos = np.arange(t_len)
    row, col = pos // GRID_W, pos % GRID_W
    inv = ROPE_BASE ** (-np.arange(quarter, dtype=np.float64) / quarter)
    ang_r = row[:, None] * inv[None, :]
    ang_c = col[:, None] * inv[None, :]
    cos = np.concatenate([np.cos(ang_r), np.cos(ang_r), np.cos(ang_c), np.cos(ang_c)], axis=1)
    sin = np.concatenate([-np.sin(ang_r), np.sin(ang_r), -np.sin(ang_c), np.sin(ang_c)], axis=1)
    return jnp.asarray(cos, F32), jnp.asarray(sin, F32)


def _swap_pairs(x):
    q = x.shape[-1] // 4
    return jnp.concatenate([x[:, q:2 * q], x[:, :q], x[:, 3 * q:], x[:, 2 * q:3 * q]], axis=-1)


def _rope(x, cos, sin):
    return x * cos + _swap_pairs(x) * sin


def _softmax_attend(parts, sink):
    m = parts[0][0].max(axis=-1, keepdims=True)
    for s, _ in parts[1:]:
        m = jnp.maximum(m, s.max(axis=-1, keepdims=True))
    if sink is not None:
        m = jnp.maximum(m, sink)
    den = jnp.exp(sink - m) if sink is not None else 0.0
    acc = None
    for s, v in parts:
        p = jnp.exp(s - m)
        den = den + p.sum(axis=-1, keepdims=True)
        o = _bdot(p, v)
        acc = o if acc is None else acc + o
    return acc / den


def _group_sinks(sink_ref, hk, rows):
    head = lax.broadcasted_iota(jnp.int32, (A_GROUP * rows, 1), 0) // rows
    col = jnp.full((A_GROUP * rows, 1), sink_ref[hk * A_GROUP], F32)
    for g in range(1, A_GROUP):
        col = jnp.where(head == g, sink_ref[hk * A_GROUP + g], col)
    return col


def _attn_a_prompt_kernel(sink_ref, q_ref, kv_ref, gq_ref, gk_ref, o_ref, kn_ref):
    scale = A_HEAD_DIM ** -0.5
    q = q_ref[...]
    kv = kv_ref[...]
    t = q.shape[0]
    outs = []
    kns = []
    for hk in range(A_KV_HEADS):
        k = _rms(kv[:, hk * A_HEAD_DIM:(hk + 1) * A_HEAD_DIM], gk_ref[...])
        v = kv[:, (A_KV_HEADS + hk) * A_HEAD_DIM:(A_KV_HEADS + hk + 1) * A_HEAD_DIM]
        kns.append(k)
        qg = jnp.concatenate([_rms(q[:, h * A_HEAD_DIM:(h + 1) * A_HEAD_DIM], gq_ref[...])
                              for h in range(hk * A_GROUP, (hk + 1) * A_GROUP)], axis=0)
        s = _bdot_nt(qg, k) * scale
        o = _softmax_attend([(s, v)], _group_sinks(sink_ref, hk, t))
        outs += [o[g * t:(g + 1) * t] for g in range(A_GROUP)]
    o_ref[...] = jnp.concatenate(outs, axis=-1).astype(o_ref.dtype)
    kn_ref[...] = jnp.concatenate(kns, axis=-1)


def _attn_a_prompt(proj, n_seq, t, sink, gq, gk):
    qw = A_HEADS * A_HEAD_DIM
    kvw = 2 * A_KV_HEADS * A_HEAD_DIM
    return pl.pallas_call(
        _attn_a_prompt_kernel,
        grid=(n_seq,),
        in_specs=[pl.BlockSpec(memory_space=pltpu.SMEM),
                  pl.BlockSpec((t, qw), lambda b: (b, 0)),
                  pl.BlockSpec((t, kvw), lambda b: (b, qw // kvw)),
                  pl.BlockSpec((1, A_HEAD_DIM), lambda b: (0, 0)),
                  pl.BlockSpec((1, A_HEAD_DIM), lambda b: (0, 0))],
        out_specs=[pl.BlockSpec((t, qw), lambda b: (b, 0)),
                   pl.BlockSpec((t, A_KV_HEADS * A_HEAD_DIM), lambda b: (b, 0))],
        out_shape=[jax.ShapeDtypeStruct((n_seq * t, qw), BF16),
                   jax.ShapeDtypeStruct((n_seq * t, A_KV_HEADS * A_HEAD_DIM), F32)],
        compiler_params=_params(("arbitrary",)),
        name="attn_a_context",
    )(sink, proj, proj, gq.reshape(1, -1), gk.reshape(1, -1))


def _attn_a_sample_kernel(sink_ref, q_ref, kv_ref, kc_ref, vc_ref, gq_ref, gk_ref, cos_ref, sin_ref,
                          o_ref, *, t):
    scale = A_HEAD_DIM ** -0.5
    i = pl.program_id(1)
    win = 3 * A_BLOCK
    q0 = pl.multiple_of(i * A_BLOCK, A_BLOCK)
    k0 = pl.multiple_of(jnp.clip((i - 1) * A_BLOCK, 0, t - win), A_BLOCK)
    q = q_ref[...]
    kv = kv_ref[pl.ds(k0, win), :]
    cq, sq = cos_ref[pl.ds(q0, A_BLOCK), :], sin_ref[pl.ds(q0, A_BLOCK), :]
    ck, sk = cos_ref[pl.ds(k0, win), :], sin_ref[pl.ds(k0, win), :]
    rows = A_GROUP * A_BLOCK
    qpos = q0 + lax.broadcasted_iota(jnp.int32, (rows, win), 0) % A_BLOCK
    kpos = k0 + lax.broadcasted_iota(jnp.int32, (rows, win), 1)
    mask = jnp.abs(qpos - kpos) <= A_WINDOW
    outs = []
    for hk in range(A_KV_HEADS):
        k = _rope(_rms(kv[:, hk * A_HEAD_DIM:(hk + 1) * A_HEAD_DIM], gk_ref[...]), ck, sk)
        v = kv[:, (A_KV_HEADS + hk) * A_HEAD_DIM:(A_KV_HEADS + hk + 1) * A_HEAD_DIM]
        qg = jnp.concatenate([_rope(_rms(q[:, h * A_HEAD_DIM:(h + 1) * A_HEAD_DIM], gq_ref[...]), cq, sq)
                              for h in range(hk * A_GROUP, (hk + 1) * A_GROUP)], axis=0)
        s1 = jnp.where(mask, _bdot_nt(qg, k) * scale, NEG_INF)
        s2 = _bdot_nt(qg, kc_ref[hk]) * scale
        o = _softmax_attend([(s1, v), (s2, vc_ref[hk])], _group_sinks(sink_ref, hk, A_BLOCK))
        outs += [o[g * A_BLOCK:(g + 1) * A_BLOCK] for g in range(A_GROUP)]
    o_ref[...] = jnp.concatenate(outs, axis=-1).astype(o_ref.dtype)


def _attn_a_sample(proj, row0, n_seq, t, k_ctx, v_ctx, sink, gq, gk):
    qw = A_HEADS * A_HEAD_DIM
    kvw = 2 * A_KV_HEADS * A_HEAD_DIM
    nqb = t // A_BLOCK
    cos, sin = _rope_tables(t, A_HEAD_DIM)
    past = k_ctx.shape[2]
    return pl.pallas_call(
        functools.partial(_attn_a_sample_kernel, t=t),
        grid=(n_seq, nqb),
        in_specs=[pl.BlockSpec(memory_space=pltpu.SMEM),
                  pl.BlockSpec((A_BLOCK, qw), lambda b, i: (row0 // A_BLOCK + b * nqb + i, 0)),
                  pl.BlockSpec((t, kvw), lambda b, i: (row0 // t + b, qw // kvw)),
                  pl.BlockSpec((None, A_KV_HEADS, past, A_HEAD_DIM), lambda b, i: (b, 0, 0, 0)),
                  pl.BlockSpec((None, A_KV_HEADS, past, A_HEAD_DIM), lambda b, i: (b, 0, 0, 0)),
                  pl.BlockSpec((1, A_HEAD_DIM), lambda b, i: (0, 0)),
                  pl.BlockSpec((1, A_HEAD_DIM), lambda b, i: (0, 0)),
                  pl.BlockSpec((t, A_HEAD_DIM), lambda b, i: (0, 0)),
                  pl.BlockSpec((t, A_HEAD_DIM), lambda b, i: (0, 0))],
        out_specs=pl.BlockSpec((A_BLOCK, qw), lambda b, i: (b * nqb + i, 0)),
        out_shape=jax.ShapeDtypeStruct((n_seq * t, qw), BF16),
        compiler_params=_params(("arbitrary", "arbitrary")),
        name="attn_a_latent",
    )(sink, proj, proj, k_ctx, v_ctx, gq.reshape(1, -1), gk.reshape(1, -1), cos, sin)


def _per_head_lanes(x, fn):
    lane = lax.broadcasted_iota(jnp.int32, x.shape, 1)
    lo = fn(x[:, :B_DK])
    hi = fn(x[:, B_DK:])
    return jnp.where(lane < B_DK, lo, hi)


def _conv_silu(x, w):
    t = x.shape[0]
    row = lax.broadcasted_iota(jnp.int32, x.shape, 0)
    prev = jnp.where(row == 0, 0.0, pltpu.roll(x, 1, 0))
    nxt = jnp.where(row == t - 1, 0.0, pltpu.roll(x, t - 1, 0))
    return _silu(prev * w[0:1, :] + x * w[1:2, :] + nxt * w[2:3, :])


M_SAME, M_TRI, M_TRI_T, M_STRICT, M_BDIAG = range(5)


def _delta_masks(r):
    c = B_CHUNK
    ii = lax.broadcasted_iota(jnp.int32, (r, r), 0)
    jj = lax.broadcasted_iota(jnp.int32, (r, r), 1)
    same = (ii // c) == (jj // c)
    ahead = jnp.where(ii < r // 2, ii - jj, jj - ii)
    tri = jnp.where(same, ahead, -1) >= 0
    tri_t = jnp.where(same, ahead, 1) <= 0
    strict = jnp.where(same, ahead, -1) > 0
    bdiag = (ii // 16) == (jj // 16)
    return [x.astype(F32) for x in (same, tri, tri_t, strict, bdiag)]


def _delta_prepare(q, k, v, g_col, g_row, beta, mask_ref):
    c = B_CHUNK
    r = q.shape[0]
    dot = functools.partial(jnp.dot, preferred_element_type=F32)
    gc_col = jnp.sum(mask_ref[M_TRI] * g_row, axis=1, keepdims=True)
    gc_row = jnp.sum(mask_ref[M_TRI_T] * g_col, axis=0, keepdims=True)
    g_tot = jnp.sum(mask_ref[M_SAME] * g_row, axis=1, keepdims=True)
    ex = jnp.exp((gc_col - gc_row) * mask_ref[M_TRI])
    kb = k * beta
    qk = _bdot_nt(jnp.concatenate([kb, q], axis=0), k)
    m = qk[:r] * (ex * mask_ref[M_STRICT])
    aqk = qk[r:] * (ex * mask_ref[M_TRI])
    dg = m * mask_ref[M_BDIAG]
    off = m - dg
    n1 = -dg
    n1b = n1.astype(BF16)
    n2 = dot(n1b, n1b)
    n2b = n2.astype(BF16)
    t = dot(jnp.concatenate([n1b, n2b], axis=0), n2b)
    xs = n1 + n2 + t[:r]
    n4 = t[r:]
    n4b = n4.astype(BF16)
    t = dot(jnp.concatenate([xs.astype(BF16), n4b], axis=0), n4b)
    xs = xs + n4 + t[:r]
    n8 = t[r:]
    xs = xs + n8 + dot(xs.astype(BF16), n8.astype(BF16))
    xsb = xs.astype(BF16)
    f = -(off + dot(xsb, off.astype(BF16)))
    fb = f.astype(BF16)
    t = dot(fb, jnp.concatenate([xsb, fb], axis=1))
    ys = xs + f + t[:, :r]
    f2 = t[:, r:]
    ts = ys + f2 + dot(f2.astype(BF16), ys.astype(BF16))
    egc = jnp.exp(gc_col)
    rhs = jnp.concatenate([v * beta, kb * egc], axis=-1)
    uw = rhs + _bdot(ts, rhs)

    def block_diag(x):
        return jnp.concatenate([x] * (r // c), axis=-1) * mask_ref[M_SAME]

    kd = k * jnp.exp(g_tot - gc_col)
    kd_t = jnp.concatenate([kd, jnp.zeros_like(kd)], axis=-1).T[:c]
    kd_t = jnp.concatenate([kd_t] * (r // c), axis=0) * mask_ref[M_SAME]
    e_tot = jnp.broadcast_to(jnp.exp(g_tot), v.shape)
    return uw[:, :B_DV], e_tot, block_diag(uw[:, B_DV:]), aqk, block_diag(q * egc), kd_t


def _deltanet_kernel(alog_ref, dtb_ref, q_ref, k_ref, v_ref, z_ref, cwq_ref, cwk_ref, cwv_ref,
                     tail_ref, tailt_ref, onorm_ref, s0f_ref, s0b_ref,
                     o_ref, sf_ref, sb_ref,
                     qc_ref, kc_ref, vc_ref, oacc_ref, u_ref, et_ref, w_ref, aqk_ref, qg_ref, kdt_ref,
                     mask_ref, *, n_chunks):
    hp = pl.program_id(1)
    c = B_CHUNK

    @pl.when(jnp.logical_and(pl.program_id(0) == 0, hp == 0))
    def _():
        for i, x in enumerate(_delta_masks(4 * c)):
            mask_ref[i] = x

    def l2n(x):
        ss = _per_head_lanes(x * x, lambda a: jnp.sum(a, axis=-1, keepdims=True))
        return x * lax.rsqrt(ss + EPS)

    qc_ref[...] = l2n(_conv_silu(q_ref[...], cwq_ref[...])) * (B_DK ** -0.5)
    kc_ref[...] = l2n(_conv_silu(k_ref[...], cwk_ref[...]))
    vc_ref[...] = _conv_silu(v_ref[...], cwv_ref[...])
    oacc_ref[...] = jnp.zeros_like(oacc_ref)

    lane32 = lax.broadcasted_iota(jnp.int32, (c, 4 * B_HEADS), 1)

    def gates(chunk, d, head):
        tail = tail_ref[pl.ds(pl.multiple_of(chunk * c, c), c), :]
        ia = 2 * d * B_HEADS + head
        ib = ia + B_HEADS
        a_col = jnp.sum(jnp.where(lane32 == ia, tail, 0.0), axis=1, keepdims=True)
        b_col = jnp.sum(jnp.where(lane32 == ib, tail, 0.0), axis=1, keepdims=True)
        a_row = tailt_ref[chunk, pl.ds(ia, 1), :]
        na = -jnp.exp(alog_ref[d, head])
        bias = dtb_ref[d, head]
        g_col = na * jax.nn.softplus(a_col + bias)
        g_row = na * jax.nn.softplus(a_row + bias)
        return g_col, g_row, jax.nn.sigmoid(b_col)

    def stacked(ref, chunks):
        parts = []
        for chunk in chunks:
            x = ref[pl.ds(pl.multiple_of(chunk * c, c), c), :]
            parts += [x[:, :B_DK], x[:, B_DK:]]
        return jnp.concatenate(parts, axis=0)

    def prepare(j, carry):
        chunks = (j, n_chunks - 1 - j)
        gs = [gates(chunks[d], d, 2 * hp + hh) for d in range(2) for hh in range(2)]
        g_col = jnp.concatenate([g[0] for g in gs], axis=0)
        g_row = jnp.concatenate([g[1] for g in gs], axis=1)
        beta = jnp.concatenate([g[2] for g in gs], axis=0)
        outs = _delta_prepare(stacked(qc_ref, chunks), stacked(kc_ref, chunks), stacked(vc_ref, chunks),
                              g_col, g_row, beta, mask_ref)
        for ref, x in zip((u_ref, et_ref, w_ref, aqk_ref, qg_ref, kdt_ref), outs):
            ref[j] = x.astype(ref.dtype)
        return carry

    lax.fori_loop(0, n_chunks, prepare, 0, unroll=4)

    def scan(j, s):
        sb = s.astype(BF16)
        delta = u_ref[j] - jnp.dot(w_ref[j], sb, preferred_element_type=F32)
        db = delta.astype(BF16)
        o = (jnp.dot(qg_ref[j], sb, preferred_element_type=F32)
             + jnp.dot(aqk_ref[j], db, preferred_element_type=F32))
        for d, chunk in enumerate((j, n_chunks - 1 - j)):
            rows = pl.ds(pl.multiple_of(chunk * c, c), c)
            oacc_ref[rows, :] += jnp.concatenate([o[2 * d * c:(2 * d + 1) * c],
                                                  o[(2 * d + 1) * c:(2 * d + 2) * c]], axis=-1)
        return s * et_ref[j] + jnp.dot(kdt_ref[j], db, preferred_element_type=F32)

    init = jnp.concatenate([s0f_ref[0], s0f_ref[1], s0b_ref[0], s0b_ref[1]], axis=0)
    fin = lax.fori_loop(0, n_chunks, scan, init)
    sf_ref[0], sf_ref[1], sb_ref[0], sb_ref[1] = (fin[i * B_DK:(i + 1) * B_DK] for i in range(4))

    o = oacc_ref[...]
    ms = _per_head_lanes(o * o, lambda a: jnp.mean(a, axis=-1, keepdims=True))
    o_ref[...] = (o * lax.rsqrt(ms + EPS) * onorm_ref[...] * _silu(z_ref[...])).astype(o_ref.dtype)


def _deltanet(proj, tail, row0, n_seq, t, conv_w, a_log, dt_bias, o_norm, s0_f, s0_b):
    c = B_CHUNK
    n_chunks = t // c
    lw = 2 * B_DK
    col_q = (A_HEADS + 2 * A_KV_HEADS) * A_HEAD_DIM // lw
    nhp = B_HEADS // 2
    rows = tail[row0:row0 + n_seq * t]
    tail_t = rows.reshape(n_seq, n_chunks, c, 4 * B_HEADS).transpose(0, 1, 3, 2)
    onorm2 = jnp.concatenate([o_norm, o_norm]).reshape(1, lw)
    b0 = row0 // t
    seq_blk = lambda off: pl.BlockSpec((t, lw), lambda b, h: (b0 + b, col_q + off + h))
    cw_blk = lambda off: pl.BlockSpec((3, lw), lambda b, h: (0, off + h))
    st_blk = pl.BlockSpec((None, 2, B_DK, B_DV), lambda b, h: (b, h, 0, 0))
    return pl.pallas_call(
        functools.partial(_deltanet_kernel, n_chunks=n_chunks),
        grid=(n_seq, nhp),
        in_specs=[pl.BlockSpec(memory_space=pltpu.SMEM), pl.BlockSpec(memory_space=pltpu.SMEM),
                  seq_blk(0), seq_blk(nhp), seq_blk(2 * nhp), seq_blk(3 * nhp),
                  cw_blk(0), cw_blk(nhp), cw_blk(2 * nhp),
                  pl.BlockSpec((t, 4 * B_HEADS), lambda b, h: (b, 0)),
                  pl.BlockSpec((None, n_chunks, 4 * B_HEADS, c), lambda b, h: (b, 0, 0, 0)),
                  pl.BlockSpec((1, lw), lambda b, h: (0, 0)),
                  st_blk, st_blk],
        out_specs=[pl.BlockSpec((t, lw), lambda b, h: (b, h)), st_blk, st_blk],
        out_shape=[jax.ShapeDtypeStruct((n_seq * t, B_HEADS * B_DV), BF16),
                   jax.ShapeDtypeStruct((n_seq, B_HEADS, B_DK, B_DV), F32),
                   jax.ShapeDtypeStruct((n_seq, B_HEADS, B_DK, B_DV), F32)],
        scratch_shapes=[pltpu.VMEM((t, lw), F32)] * 4 + [pltpu.VMEM((n_chunks, 4 * c, B_DV), F32)] * 2
        + [pltpu.VMEM((n_chunks, 4 * c, 4 * c), BF16)] * 4 + [pltpu.VMEM((5, 4 * c, 4 * c), F32)],
        compiler_params=_params(("arbitrary", "arbitrary")),
        name="deltanet",
    )(a_log, dt_bias, proj, proj, proj, proj, conv_w, conv_w, conv_w, rows, tail_t, onorm2, s0_f, s0_b)


def _lora_norm_kernel(p_ref, gq_ref, gkv_ref, cq_ref, ckv_ref, ckvb_ref):
    p = p_ref[...]
    cq_ref[...] = _rms(p[:, :C_Q_LORA], gq_ref[...]).astype(cq_ref.dtype)
    ckv = _rms(p[:, C_Q_LORA:C_Q_LORA + C_KV_LORA], gkv_ref[...])
    ckv_ref[...] = ckv
    ckvb_ref[...] = ckv.astype(ckvb_ref.dtype)


def _lora_norm(p1, gq, gkv, tm=512):
    n, w = p1.shape
    return pl.pallas_call(
        _lora_norm_kernel,
        grid=(n // tm,),
        in_specs=[pl.BlockSpec((tm, w), lambda i: (i, 0)),
                  pl.BlockSpec((1, C_Q_LORA), lambda i: (0, 0)),
                  pl.BlockSpec((1, C_KV_LORA), lambda i: (0, 0))],
        out_specs=[pl.BlockSpec((tm, C_Q_LORA), lambda i: (i, 0)),
                   pl.BlockSpec((tm, C_KV_LORA), lambda i: (i, 0)),
                   pl.BlockSpec((tm, C_KV_LORA), lambda i: (i, 0))],
        out_shape=[jax.ShapeDtypeStruct((n, C_Q_LORA), BF16),
                   jax.ShapeDtypeStruct((n, C_KV_LORA), F32),
                   jax.ShapeDtypeStruct((n, C_KV_LORA), BF16)],
        compiler_params=_params(("arbitrary",)),
        name="lora_norm",
    )(p1, gq.reshape(1, -1), gkv.reshape(1, -1))


def _mla_head_q(q, h, gq):
    qn, qr = q[:, h * C_QK:h * C_QK + C_NOPE], q[:, h * C_QK + C_NOPE:(h + 1) * C_QK]
    rn = lax.rsqrt((jnp.sum(qn * qn, axis=-1, keepdims=True)
                    + jnp.sum(qr * qr, axis=-1, keepdims=True)) / C_QK + EPS)
    return qn * rn * gq[:, :C_NOPE], qr * rn * gq[:, C_NOPE:]


def _mla_head_k(kv, kr, kr_ss, h, gk):
    kn = kv[:, h * (C_NOPE + C_V):h * (C_NOPE + C_V) + C_NOPE]
    v = kv[:, h * (C_NOPE + C_V) + C_NOPE:(h + 1) * (C_NOPE + C_V)]
    rn = lax.rsqrt((jnp.sum(kn * kn, axis=-1, keepdims=True) + kr_ss) / C_QK + EPS)
    return kn * rn * gk[:, :C_NOPE], kr * rn, v


C_STACK = 4


def _place(x, i, n):
    t, w = x.shape
    parts = ([jnp.zeros((t, i * w), x.dtype)] if i else []) + [x]
    if i < n - 1:
        parts.append(jnp.zeros((t, (n - 1 - i) * w), x.dtype))
    return jnp.concatenate(parts, axis=-1)


def _by_block(cols, lane, w):
    out = cols[-1]
    for i in reversed(range(len(cols) - 1)):
        out = jnp.where(lane < (i + 1) * w, cols[i], out)
    return out


def _attn_c_prompt_kernel(q_ref, kv_ref, p_ref, gq_ref, gk_ref, o_ref):
    scale = C_QK ** -0.5
    n = C_STACK
    q = q_ref[...]
    kv = kv_ref[...]
    gk = gk_ref[...]
    t = q.shape[0]
    kr_raw = p_ref[...][:, C_Q_LORA + C_KV_LORA:]
    kr_ss = jnp.sum(kr_raw * kr_raw, axis=-1, keepdims=True)
    kr_g = kr_raw * gk[:, C_NOPE:]
    lane_q = lax.broadcasted_iota(jnp.int32, (t, n * C_QK), 1)
    lane_o = lax.broadcasted_iota(jnp.int32, (t, n * C_V), 1)
    outs = []
    for grp in range(C_HEADS // n):
        qs = q[:, grp * n * C_QK:(grp + 1) * n * C_QK]
        rn = [lax.rsqrt(jnp.mean(qs[:, i * C_QK:(i + 1) * C_QK] ** 2, axis=-1, keepdims=True) + EPS)
              for i in range(n)]
        qs = qs * _by_block(rn, lane_q, C_QK) * gq_ref[...]
        k_rows, v_rows = [], []
        for i in range(n):
            kn, kr, v = _mla_head_k(kv, kr_g, kr_ss, grp * n + i, gk)
            k_rows.append(_place(jnp.concatenate([kn, kr], axis=-1), i, n))
            v_rows.append(_place(v, i, n))
        s = _bdot_nt(qs, jnp.concatenate(k_rows, axis=0)) * scale
        ps, rden = [], []
        for i in range(n):
            si = s[:, i * t:(i + 1) * t]
            pi = jnp.exp(si - si.max(axis=-1, keepdims=True))
            ps.append(pi)
            rden.append(1.0 / pi.sum(axis=-1, keepdims=True))
        o = _bdot(jnp.concatenate(ps, axis=-1), jnp.concatenate(v_rows, axis=0))
        outs.append(o * _by_block(rden, lane_o, C_V))
    o_ref[...] = jnp.concatenate(outs, axis=-1).astype(o_ref.dtype)


def _attn_c_prompt(q, kv, p1, n_seq, t, gq, gk):
    return pl.pallas_call(
        _attn_c_prompt_kernel,
        grid=(n_seq,),
        in_specs=[pl.BlockSpec((t, q.shape[1]), lambda b: (b, 0)),
                  pl.BlockSpec((t, kv.shape[1]), lambda b: (b, 0)),
                  pl.BlockSpec((t, p1.shape[1]), lambda b: (b, 0)),
                  pl.BlockSpec((1, C_STACK * C_QK), lambda b: (0, 0)),
                  pl.BlockSpec((1, C_QK), lambda b: (0, 0))],
        out_specs=pl.BlockSpec((t, C_HEADS * C_V), lambda b: (b, 0)),
        out_shape=jax.ShapeDtypeStruct((n_seq * t, C_HEADS * C_V), BF16),
        compiler_params=_params(("arbitrary",), VMEM_LIMIT),
        name="attn_c_context",
    )(q, kv, p1, jnp.tile(gq, C_STACK).reshape(1, -1), gk.reshape(1, -1))


def _attn_c_sample_kernel(q_ref, kv_ref, p_ref, kvc_ref, krc_ref, gq_ref, gk_ref, cos_ref, sin_ref,
                          o_ref, *, tq):
    scale = C_QK ** -0.5
    i = pl.program_id(1)
    q0 = pl.multiple_of(i * tq, tq)
    q = q_ref[...]
    kv = kv_ref[...]
    kvc = kvc_ref[...]
    gq, gk = gq_ref[...], gk_ref[...]
    cos, sin = cos_ref[...], sin_ref[...]
    cq, sq = cos_ref[pl.ds(q0, tq), :], sin_ref[pl.ds(q0, tq), :]
    kr_raw = p_ref[...][:, C_Q_LORA + C_KV_LORA:]
    kr_ss = jnp.sum(kr_raw * kr_raw, axis=-1, keepdims=True)
    kr_g = _rope(kr_raw * gk[:, C_NOPE:], cos, sin)
    krc_raw = krc_ref[...]
    krc_ss = jnp.sum(krc_raw * krc_raw, axis=-1, keepdims=True)
    krc_g = krc_raw * gk[:, C_NOPE:]
    outs = []
    for h in range(C_HEADS):
        qn, qr = _mla_head_q(q, h, gq)
        qr = _rope(qr, cq, sq)
        kn, kr, v = _mla_head_k(kv, kr_g, kr_ss, h, gk)
        knc, krc, vc = _mla_head_k(kvc, krc_g, krc_ss, h, gk)
        s1 = (_bdot_nt(qn, kn) + _bdot_nt(qr, kr)) * scale
        s2 = (_bdot_nt(qn, knc) + _bdot_nt(qr, krc)) * scale
        outs.append(_softmax_attend([(s1, v), (s2, vc)], None))
    o_ref[...] = jnp.concatenate(outs, axis=-1).astype(o_ref.dtype)


def _attn_c_sample(q, kv, p1, row0, n_seq, t, kr_ctx, gq, gk, tq=256):
    n = p1.shape[0]
    past = kr_ctx.shape[1]
    nq = t // tq
    cos, sin = _rope_tables(t, C_ROPE)
    return pl.pallas_call(
        functools.partial(_attn_c_sample_kernel, tq=tq),
        grid=(n_seq, nq),
        in_specs=[pl.BlockSpec((tq, q.shape[1]), lambda b, i: (row0 // tq + b * nq + i, 0)),
                  pl.BlockSpec((t, kv.shape[1]), lambda b, i: (row0 // t + b, 0)),
                  pl.BlockSpec((t, p1.shape[1]), lambda b, i: (row0 // t + b, 0)),
                  pl.BlockSpec((past, kv.shape[1]), lambda b, i: (n // past + b, 0)),
                  pl.BlockSpec((None, past, C_ROPE), lambda b, i: (b, 0, 0)),
                  pl.BlockSpec((1, C_QK), lambda b, i: (0, 0)),
                  pl.BlockSpec((1, C_QK), lambda b, i: (0, 0)),
                  pl.BlockSpec((t, C_ROPE), lambda b, i: (0, 0)),
                  pl.BlockSpec((t, C_ROPE), lambda b, i: (0, 0))],
        out_specs=pl.BlockSpec((tq, C_HEADS * C_V), lambda b, i: (b * nq + i, 0)),
        out_shape=jax.ShapeDtypeStruct((n_seq * t, C_HEADS * C_V), BF16),
        compiler_params=_params(("arbitrary", "arbitrary"), VMEM_LIMIT),
        name="attn_c_latent",
    )(q, kv, p1, kv, kr_ctx, gq.reshape(1, -1), gk.reshape(1, -1), cos, sin)


ROUTE_TILE = 512
ROUTE_BLOCK = 512
SLOT_ALIGN = 8
LOCAL_SLOTS = 2304
META_LANES = 128
FF_CHUNK = 256


def _router_kernel(x_ref, g_ref, sc_ref, sh_ref, wr_ref, br_ref, h_ref, meta_ref, cntb_ref, cnt_ref,
                   run_ref):
    @pl.when(pl.program_id(0) == 0)
    def _():
        run_ref[...] = jnp.zeros_like(run_ref)

    h = _rms(x_ref[...], g_ref[...]) * (1 + sc_ref[...]) + sh_ref[...]
    h_ref[...] = h.astype(h_ref.dtype)
    tm = h.shape[0]
    logits = _hdot(h, wr_ref[...]) + br_ref[...]
    lane = lax.broadcasted_iota(jnp.int32, logits.shape, 1)
    work = logits
    picks, tops = [], []
    for _ in range(TOP_K):
        m = work.max(axis=-1, keepdims=True)
        first = jnp.min(jnp.where(work == m, lane, N_EXPERTS), axis=-1, keepdims=True)
        pick = lane == first
        picks.append(pick)
        tops.append(m)
        work = jnp.where(pick, -jnp.inf, work)
    sel = sum(p.astype(F32) for p in picks)
    earlier = (lax.broadcasted_iota(jnp.int32, (tm, tm), 0)
               > lax.broadcasted_iota(jnp.int32, (tm, tm), 1)).astype(BF16)
    inside = jnp.dot(earlier, sel.astype(BF16), preferred_element_type=F32)
    cnt = jnp.sum(sel, axis=0, keepdims=True)
    run = jnp.ceil(cnt / SLOT_ALIGN) * SLOT_ALIGN
    lower_e = (lax.broadcasted_iota(jnp.int32, (N_EXPERTS, N_EXPERTS), 0)
               < lax.broadcasted_iota(jnp.int32, (N_EXPERTS, N_EXPERTS), 1)).astype(BF16)
    start = jnp.dot(jnp.broadcast_to(run, (8, N_EXPERTS)).astype(BF16), lower_e,
                    preferred_element_type=F32)[0:1]
    slot = start + inside
    ws = [jnp.exp(t - tops[0]) for t in tops]
    den = sum(ws)
    mlane = lax.broadcasted_iota(jnp.int32, (tm, META_LANES), 1)
    meta = jnp.zeros((tm, META_LANES), F32)
    for k in range(TOP_K):
        meta = jnp.where(mlane == k, jnp.sum(jnp.where(picks[k], slot, 0.0), axis=-1, keepdims=True), meta)
        meta = jnp.where(mlane == TOP_K + k, ws[k] / den, meta)
    meta_ref[...] = meta
    cntb_ref[...] = cnt
    run_ref[...] += run
    cnt_ref[...] = run_ref[...]


def _router(x, gain, scale, shift, w_router, b_router, layer, n_p, t_s):
    n, d = x.shape
    tm = ROUTE_BLOCK
    e = w_router.shape[-1]
    group = _group_of_tile(tm, n_p, t_s)
    return pl.pallas_call(
        _router_kernel,
        grid=(n // tm,),
        in_specs=[pl.BlockSpec((tm, d), lambda i: (i, 0)),
                  pl.BlockSpec((1, d), lambda i: (0, 0)),
                  pl.BlockSpec((None, 1, d), lambda i: (group(i), 0, 0)),
                  pl.BlockSpec((None, 1, d), lambda i: (group(i), 0, 0)),
                  pl.BlockSpec((None, d, e), lambda i: (layer, 0, 0)),
                  pl.BlockSpec((None, 1, e), lambda i: (layer, 0, 0))],
        out_specs=[pl.BlockSpec((tm, d), lambda i: (i, 0)),
                   pl.BlockSpec((tm, META_LANES), lambda i: (i, 0)),
                   pl.BlockSpec((None, 1, e), lambda i: (i, 0, 0)),
                   pl.BlockSpec((1, e), lambda i: (0, 0))],
        out_shape=[jax.ShapeDtypeStruct((n, d), BF16),
                   jax.ShapeDtypeStruct((n, META_LANES), F32),
                   jax.ShapeDtypeStruct((n // tm, 1, e), F32),
                   jax.ShapeDtypeStruct((1, e), F32)],
        scratch_shapes=[pltpu.VMEM((1, e), F32)],
        compiler_params=_params(("arbitrary",)),
        name="router",
    )(x, gain.reshape(1, d), scale.reshape(N_GROUPS, 1, d), shift.reshape(N_GROUPS, 1, d),
      w_router, b_router.reshape(-1, 1, e))


def _route_plan(cnt_blk, cnt_tot, n_tiles, min_tiles):
    cb = cnt_blk[:, 0, :].astype(jnp.int32)
    run = (cb + SLOT_ALIGN - 1) // SLOT_ALIGN * SLOT_ALIGN
    counts = cnt_tot[0].astype(jnp.int32)
    padded = (counts + ROUTE_TILE - 1) // ROUTE_TILE * ROUTE_TILE
    ends = jnp.cumsum(padded)
    offs = ends - padded
    gstart = offs[None, :] + jnp.cumsum(run, axis=0) - run
    lstart = jnp.cumsum(run, axis=1) - run
    tile_start = jnp.arange(n_tiles, dtype=jnp.int32) * ROUTE_TILE
    n_valid = ends[-1] // ROUTE_TILE
    te = jnp.minimum(jnp.sum(ends[None, :] <= tile_start[:, None], axis=1), N_EXPERTS - 1)
    te = jnp.where(tile_start < ends[-1], te, te[jnp.maximum(n_valid - 1, 0)]).astype(jnp.int32)
    rows = jnp.clip((offs + counts)[te] - tile_start, 0, ROUTE_TILE).astype(jnp.int32)
    ragged = jnp.where(counts % ROUTE_TILE != 0, ends // ROUTE_TILE - 1, -1)
    tail = jnp.arange(min_tiles, n_tiles, dtype=jnp.int32)
    fill = jnp.concatenate([ragged, jnp.where(tail >= n_valid, tail, -1)]).astype(jnp.int32)
    runs = (gstart.reshape(-1).astype(jnp.int32), lstart.reshape(-1).astype(jnp.int32),
            (run // SLOT_ALIGN).reshape(-1).astype(jnp.int32),
            (jnp.sum(run, axis=1) // SLOT_ALIGN).astype(jnp.int32))
    return runs, te, rows, n_valid.reshape(1).astype(jnp.int32), fill


RUN_PIECE = 4
WAIT_PIECE = 8


def _start_runs(blk, gstart_ref, lstart_ref, nch_ref, copy):
    big = RUN_PIECE * SLOT_ALIGN

    def per_expert(e, carry):
        idx = blk * N_EXPERTS + e
        g0, l0, n = gstart_ref[idx], lstart_ref[idx], nch_ref[idx]
        n_big = n // RUN_PIECE

        def large(i, c2):
            off = pl.multiple_of(i * big, big)
            copy(pl.multiple_of(l0 + off, SLOT_ALIGN), pl.multiple_of(g0 + off, SLOT_ALIGN), big).start()
            return c2

        def small(i, c2):
            off = pl.multiple_of(n_big * big + i * SLOT_ALIGN, SLOT_ALIGN)
            copy(pl.multiple_of(l0 + off, SLOT_ALIGN), pl.multiple_of(g0 + off, SLOT_ALIGN), SLOT_ALIGN).start()
            return c2

        lax.fori_loop(0, n_big, large, 0)
        lax.fori_loop(0, n - n_big * RUN_PIECE, small, 0)
        return carry

    lax.fori_loop(0, N_EXPERTS, per_expert, 0)


def _wait_runs(n_chunks, copy):
    n_big = n_chunks // WAIT_PIECE

    def large(i, carry):
        copy(0, 0, WAIT_PIECE * SLOT_ALIGN).wait()
        return carry

    def small(i, carry):
        copy(0, 0, SLOT_ALIGN).wait()
        return carry

    lax.fori_loop(0, n_big, large, 0)
    lax.fori_loop(0, n_chunks - n_big * WAIT_PIECE, small, 0)


def _dispatch_kernel(fill_ref, gstart_ref, lstart_ref, nch_ref, ntot_ref, meta_ref, h_ref, xs_ref,
                     loc_ref, zero_ref, sem, zsem):
    blk = pl.program_id(0)

    @pl.when(blk == 0)
    def _():
        zero_ref[...] = jnp.zeros_like(zero_ref)

        def fill_copy(j):
            row0 = pl.multiple_of(fill_ref[j] * ROUTE_TILE, ROUTE_TILE)
            return pltpu.make_async_copy(zero_ref, xs_ref.at[pl.ds(row0, ROUTE_TILE), :], zsem)

        def start(j, carry):
            @pl.when(fill_ref[j] >= 0)
            def _():
                fill_copy(j).start()
            return carry

        def wait(j, carry):
            @pl.when(fill_ref[j] >= 0)
            def _():
                fill_copy(j).wait()
            return carry

        lax.fori_loop(0, fill_ref.shape[0], start, 0)
        lax.fori_loop(0, fill_ref.shape[0], wait, 0)

    hb = h_ref[...]
    tm = hb.shape[0]
    slots_t = meta_ref[...].T[0:TOP_K]
    rows = 256
    for c0 in range(0, LOCAL_SLOTS, rows):
        slot = (c0 + lax.broadcasted_iota(jnp.int32, (rows, tm), 0)).astype(F32)
        onehot = sum(jnp.where(slot == slots_t[k:k + 1], 1.0, 0.0) for k in range(TOP_K))
        loc_ref[c0:c0 + rows, :] = jnp.dot(onehot.astype(BF16), hb,
                                           preferred_element_type=F32).astype(loc_ref.dtype)

    def copy(l0, g0, rows):
        return pltpu.make_async_copy(loc_ref.at[pl.ds(l0, rows), :], xs_ref.at[pl.ds(g0, rows), :], sem)

    _start_runs(blk, gstart_ref, lstart_ref, nch_ref, copy)
    _wait_runs(ntot_ref[blk], copy)


def _dispatch(h, meta, runs, fill, n_slots):
    n, d = h.shape
    tm = ROUTE_BLOCK
    grid_spec = pltpu.PrefetchScalarGridSpec(
        num_scalar_prefetch=5,
        grid=(n // tm,),
        in_specs=[pl.BlockSpec((tm, META_LANES), lambda i, *_: (i, 0)),
                  pl.BlockSpec((tm, d), lambda i, *_: (i, 0))],
        out_specs=pl.BlockSpec(memory_space=pl.ANY),
        scratch_shapes=[pltpu.VMEM((LOCAL_SLOTS, d), F32), pltpu.VMEM((ROUTE_TILE, d), F32),
                        pltpu.SemaphoreType.DMA(()), pltpu.SemaphoreType.DMA(())])
    return pl.pallas_call(
        _dispatch_kernel,
        grid_spec=grid_spec,
        out_shape=jax.ShapeDtypeStruct((n_slots, d), F32),
        compiler_params=_params(("arbitrary",), VMEM_LIMIT),
        name="moe_dispatch",
    )(fill, *runs, meta, h)


def _experts_kernel(te_ref, rows_ref, nv_ref, x_ref, wgu_ref, bgu_ref, wd_ref, bd_ref, y_ref,
                    wgub_ref, wdb_ref, *, d_ff):
    i = pl.program_id(0)
    valid = i < nv_ref[0]
    fresh = jnp.logical_or(i == 0, te_ref[i] != te_ref[jnp.maximum(i - 1, 0)])

    @pl.when(jnp.logical_and(valid, fresh))
    def _():
        wgub_ref[...] = wgu_ref[...].astype(BF16)
        wdb_ref[...] = wd_ref[...].astype(BF16)

    @pl.when(valid)
    def _():
        row = lax.broadcasted_iota(jnp.int32, x_ref.shape, 0)
        x = jnp.where(row < rows_ref[i], x_ref[...], 0.0).astype(BF16)
        acc = None
        for c0 in range(0, d_ff, FF_CHUNK):
            g_cols, u_cols = slice(c0, c0 + FF_CHUNK), slice(d_ff + c0, d_ff + c0 + FF_CHUNK)
            gate = jnp.dot(x, wgub_ref[:, g_cols], preferred_element_type=F32) + bgu_ref[:, g_cols]
            up = jnp.dot(x, wgub_ref[:, u_cols], preferred_element_type=F32) + bgu_ref[:, u_cols]
            gate = jnp.minimum(gate, SWIGLU_LIMIT)
            up = jnp.clip(up, -SWIGLU_LIMIT, SWIGLU_LIMIT)
            act = (up + 1) * gate * jax.nn.sigmoid(SWIGLU_ALPHA * gate)
            part = jnp.dot(act.astype(BF16), wdb_ref[c0:c0 + FF_CHUNK, :], preferred_element_type=F32)
            acc = part if acc is None else acc + part
        y_ref[...] = acc + bd_ref[...]

    @pl.when(jnp.logical_not(valid))
    def _():
        y_ref[...] = jnp.zeros_like(y_ref)


def _experts(xs, te, rows, n_valid, w_gu, b_gu, w_down, b_down, layer):
    n_slots, d = xs.shape
    _, e, _, two_ff = w_gu.shape
    last = lambda i, nv: jnp.minimum(i, nv[0] - 1)
    grid_spec = pltpu.PrefetchScalarGridSpec(
        num_scalar_prefetch=3,
        grid=(n_slots // ROUTE_TILE,),
        in_specs=[pl.BlockSpec((ROUTE_TILE, d), lambda i, te, rw, nv: (last(i, nv), 0)),
                  pl.BlockSpec((None, None, d, two_ff), lambda i, te, rw, nv: (layer, te[i], 0, 0)),
                  pl.BlockSpec((None, None, 1, two_ff), lambda i, te, rw, nv: (layer, te[i], 0, 0)),
                  pl.BlockSpec((None, None, two_ff // 2, d), lambda i, te, rw, nv: (layer, te[i], 0, 0)),
                  pl.BlockSpec((None, None, 1, d), lambda i, te, rw, nv: (layer, te[i], 0, 0))],
        out_specs=pl.BlockSpec((ROUTE_TILE, d), lambda i, te, rw, nv: (i, 0)),
        scratch_shapes=[pltpu.VMEM((d, two_ff), BF16), pltpu.VMEM((two_ff // 2, d), BF16)])
    return pl.pallas_call(
        functools.partial(_experts_kernel, d_ff=two_ff // 2),
        grid_spec=grid_spec,
        out_shape=jax.ShapeDtypeStruct((n_slots, d), F32),
        compiler_params=_params(("arbitrary",), VMEM_LIMIT),
        name="moe_experts",
    )(te, rows, n_valid, xs, w_gu, b_gu.reshape(b_gu.shape[0], e, 1, two_ff), w_down,
      b_down.reshape(b_down.shape[0], e, 1, d))


def _combine_kernel(gstart_ref, lstart_ref, nch_ref, ntot_ref, meta_ref, x_ref, gate_ref, y_ref, o_ref,
                    loc_ref, sem):
    blk = pl.program_id(0)

    @pl.when(blk == 0)
    def _():
        loc_ref[...] = jnp.zeros_like(loc_ref)

    def copy(l0, g0, rows):
        return pltpu.make_async_copy(y_ref.at[pl.ds(g0, rows), :], loc_ref.at[pl.ds(l0, rows), :], sem)

    _start_runs(blk, gstart_ref, lstart_ref, nch_ref, copy)
    _wait_runs(ntot_ref[blk], copy)

    meta = meta_ref[...]
    tm = meta.shape[0]
    cols = 256
    acc = jnp.zeros(o_ref.shape, F32)
    for c0 in range(0, LOCAL_SLOTS, cols):
        slot = (c0 + lax.broadcasted_iota(jnp.int32, (tm, cols), 1)).astype(F32)
        wts = sum(jnp.where(slot == meta[:, k:k + 1], meta[:, TOP_K + k:TOP_K + k + 1], 0.0)
                  for k in range(TOP_K))
        acc = acc + jnp.dot(wts.astype(BF16), loc_ref[c0:c0 + cols, :].astype(BF16),
                            preferred_element_type=F32)
    o_ref[...] = x_ref[...] + gate_ref[...] * acc


def _combine(y, meta, runs, x, gate, n_p, t_s):
    n, d = x.shape
    tm = ROUTE_BLOCK
    group = _group_of_tile(tm, n_p, t_s)
    grid_spec = pltpu.PrefetchScalarGridSpec(
        num_scalar_prefetch=4,
        grid=(n // tm,),
        in_specs=[pl.BlockSpec((tm, META_LANES), lambda i, *_: (i, 0)),
                  pl.BlockSpec((tm, d), lambda i, *_: (i, 0)),
                  pl.BlockSpec((None, 1, d), lambda i, *_: (group(i), 0, 0)),
                  pl.BlockSpec(memory_space=pl.ANY)],
        out_specs=pl.BlockSpec((tm, d), lambda i, *_: (i, 0)),
        scratch_shapes=[pltpu.VMEM((LOCAL_SLOTS, d), F32), pltpu.SemaphoreType.DMA(())])
    return pl.pallas_call(
        _combine_kernel,
        grid_spec=grid_spec,
        out_shape=jax.ShapeDtypeStruct((n, d), F32),
        compiler_params=_params(("arbitrary",), VMEM_LIMIT),
        name="moe_combine",
    )(*runs, meta, x, gate.reshape(N_GROUPS, 1, d), y)


def _moe(x, gain, scale, shift, gate, w_router, b_router, w_gu, b_gu, w_down, b_down, layer, n_p, t_s):
    n = x.shape[0]
    assert LOCAL_SLOTS >= ROUTE_BLOCK * TOP_K + N_EXPERTS * (SLOT_ALIGN - 1) and n % ROUTE_BLOCK == 0
    min_tiles = n * TOP_K // ROUTE_TILE
    max_slots = n * TOP_K + (n // ROUTE_BLOCK) * N_EXPERTS * (SLOT_ALIGN - 1) + N_EXPERTS * (ROUTE_TILE - 1)
    n_tiles = -(-max_slots // ROUTE_TILE)
    h, meta, cnt_blk, cnt_tot = _router(x, gain, scale, shift, w_router, b_router, layer, n_p, t_s)
    runs, te, rows, n_valid, fill = _route_plan(cnt_blk, cnt_tot, n_tiles, min_tiles)
    xs = _dispatch(h, meta, runs, fill, n_tiles * ROUTE_TILE)
    y = _experts(xs, te, rows, n_valid, w_gu, b_gu, w_down, b_down, layer)
    return _combine(y, meta, runs, x, gate, n_p, t_s)


def kernel(x_prompt, x_sample, c, cache_a_k, cache_a_v, state_b_fwd, state_b_bwd, cache_c_ckv,
           cache_c_krope, c_ctx, w_mod, b_mod, norm_mix, norm_ffn, e_w_in, e_w_out, e_a_qnorm,
           e_a_knorm, e_a_sink, e_b_conv, e_b_alog, e_b_dtbias, e_b_onorm, o_w_in, o_q_lora_norm,
           o_kv_lora_norm, o_w_uq, o_w_ukv, o_qnorm, o_knorm, o_w_out, moe_w_router, moe_b_router,
           moe_w_gu, moe_b_gu, moe_w_down, moe_b_down):
    bp, tp, d = x_prompt.shape
    bs, ts, _ = x_sample.shape
    depth = w_mod.shape[0]
    n_p, n_s = bp * tp, bs * ts
    n = n_p + n_s
    assert bs + 1 <= N_GROUPS and ts % 512 == 0 and n_p % ts == 0

    x = jnp.concatenate([x_prompt.reshape(n_p, d), x_sample.reshape(n_s, d)], axis=0)
    cond = jnp.concatenate([c_ctx[None], c, jnp.zeros((N_GROUPS - 1 - bs, d), F32)], axis=0)
    mod = _adaln(cond, w_mod, b_mod)

    new_a_k, new_a_v, new_b_fwd, new_b_bwd, new_c_ckv, new_c_krope = [], [], [], [], [], []
    for layer in range(depth):
        sh1, sc1, g1, sh2, sc2, g2 = (mod[layer, j] for j in range(6))
        h = _modulate(x, norm_mix[layer], sc1, sh1, n_p, ts)
        i = layer // 2
        if layer % 2 == 0:
            main_w = (A_HEADS + 2 * A_KV_HEADS) * A_HEAD_DIM + 4 * B_HEADS * B_DK
            proj = _matmul(h, e_w_in, i, 0, main_w // 2, main_w, name="even_in_proj")
            tail = _matmul(h, e_w_in[i][None, :, main_w:], 0, 0, 4 * B_HEADS, 4 * B_HEADS,
                           name="even_gate_proj")
            oa_p, kn_p = _attn_a_prompt(proj, bp, tp, e_a_sink[i], e_a_qnorm[i], e_a_knorm[i])
            oa_s = _attn_a_sample(proj, n_p, bs, ts, cache_a_k[:, i], cache_a_v[:, i],
                                  e_a_sink[i], e_a_qnorm[i], e_a_knorm[i])
            zeros = jnp.zeros((bp, B_HEADS, B_DK, B_DV), F32)
            ob_p, s_f, s_b = _deltanet(proj, tail, 0, bp, tp, e_b_conv[i], e_b_alog[i],
                                       e_b_dtbias[i], e_b_onorm[i], zeros, zeros)
            ob_s, _, _ = _deltanet(proj, tail, n_p, bs, ts, e_b_conv[i], e_b_alog[i],
                                   e_b_dtbias[i], e_b_onorm[i], state_b_fwd[:, i], state_b_bwd[:, i])
            mix = jnp.concatenate([jnp.concatenate([oa_p, ob_p], axis=1),
                                   jnp.concatenate([oa_s, ob_s], axis=1)], axis=0)
            x = _matmul_residual(mix, e_w_out, i, x, g1, n_p, ts)
            kw = A_KV_HEADS * A_HEAD_DIM
            new_a_k.append(kn_p.reshape(bp, tp, A_KV_HEADS, A_HEAD_DIM).transpose(0, 2, 1, 3))
            v_p = proj[:n_p, A_HEADS * A_HEAD_DIM + kw:A_HEADS * A_HEAD_DIM + 2 * kw]
            new_a_v.append(v_p.reshape(bp, tp, A_KV_HEADS, A_HEAD_DIM).transpose(0, 2, 1, 3))
            new_b_fwd.append(s_f)
            new_b_bwd.append(s_b)
        else:
            p1 = _matmul(h, o_w_in, i, 0, o_w_in.shape[-1], o_w_in.shape[-1], name="odd_in_proj")
            cq, ckv, ckv_b = _lora_norm(p1, o_q_lora_norm[i], o_kv_lora_norm[i])
            q = _matmul(cq, o_w_uq, i, 0, o_w_uq.shape[-1] // 2, o_w_uq.shape[-1], name="odd_uq")
            ckv_all = jnp.concatenate([ckv_b, cache_c_ckv[:, i].reshape(-1, C_KV_LORA).astype(BF16)], 0)
            kv = _matmul(ckv_all, o_w_ukv, i, 0, o_w_ukv.shape[-1] // 2, o_w_ukv.shape[-1],
                         name="odd_ukv")
            o_p = _attn_c_prompt(q, kv, p1, bp, tp, o_qnorm[i], o_knorm[i])
            o_s = _attn_c_sample(q, kv, p1, n_p, bs, ts, cache_c_krope[:, i], o_qnorm[i], o_knorm[i])
            x = _matmul_residual(jnp.concatenate([o_p, o_s], axis=0), o_w_out, i, x, g1, n_p, ts)
            new_c_ckv.append(ckv[:n_p].reshape(bp, tp, C_KV_LORA))
            new_c_krope.append(p1[:n_p, C_Q_LORA + C_KV_LORA:].reshape(bp, tp, C_ROPE))
        x = _moe(x, norm_ffn[layer], sc2, sh2, g2, moe_w_router, moe_b_router, moe_w_gu, moe_b_gu,
                 moe_w_down, moe_b_down, layer, n_p, ts)

    return (x[:n_p].reshape(bp, tp, d), x[n_p:].reshape(bs, ts, d),
            jnp.stack(new_a_k, axis=1), jnp.stack(new_a_v, axis=1),
            jnp.stack(new_b_fwd, axis=1), jnp.stack(new_b_bwd, axis=1),
            jnp.stack(new_c_ckv, axis=1), jnp.stack(new_c_krope, axis=1))
```

```python
import functools
import math

import numpy as np
import jax
import jax.numpy as jnp
from jax import lax
from jax.experimental import pallas as pl
from jax.experimental.pallas import tpu as pltpu

F32 = jnp.float32
BF16 = jnp.bfloat16
HIGHEST = lax.Precision.HIGHEST

EPS = 1e-6
NEG_INF = -1e30
ROPE_BASE = 10000.0
GRID_W = 64
N_GROUPS = 8

A_HEADS, A_KV_HEADS, A_GROUP, A_HEAD_DIM, A_WINDOW, A_BLOCK = 8, 2, 4, 64, 128, 128
B_HEADS, B_DK, B_DV, B_CHUNK = 8, 64, 64, 64
C_HEADS, C_NOPE, C_ROPE, C_V, C_Q_LORA, C_KV_LORA = 16, 64, 32, 64, 384, 256
C_QK = C_NOPE + C_ROPE
N_EXPERTS, TOP_K = 32, 4
SWIGLU_LIMIT, SWIGLU_ALPHA = 7.0, 1.702

VMEM_LIMIT = 56 * 1024 * 1024


def _params(sem, vmem=None):
    return pltpu.CompilerParams(dimension_semantics=sem, vmem_limit_bytes=vmem)


def _bdot(a, b):
    return jnp.dot(a.astype(BF16), b.astype(BF16), preferred_element_type=F32)


def _bdot_nt(a, b):
    return lax.dot_general(a.astype(BF16), b.astype(BF16), (((1,), (1,)), ((), ())),
                           preferred_element_type=F32)


def _bdot_tn(a, b):
    return lax.dot_general(a.astype(BF16), b.astype(BF16), (((0,), (0,)), ((), ())),
                           preferred_element_type=F32)


def _hdot(a, b):
    return jnp.dot(a, b, preferred_element_type=F32, precision=HIGHEST)


def _hdot_nt(a, b):
    return lax.dot_general(a, b, (((1,), (1,)), ((), ())), preferred_element_type=F32,
                           precision=HIGHEST)


def _rms(x, gain):
    return x * lax.rsqrt(jnp.mean(x * x, axis=-1, keepdims=True) + EPS) * gain


def _silu(x):
    return x * jax.nn.sigmoid(x)


def _group_of_tile(tm, n_p, t_s):
    def group(i):
        r = i * tm
        return jnp.where(r < n_p, 0, 1 + (r - n_p) // t_s)
    return group


def _adaln_kernel(cond_ref, w_ref, b_ref, o_ref):
    o_ref[...] = _bdot(_silu(cond_ref[...]), w_ref[...]) + b_ref[...]


def _adaln(cond, w_mod, b_mod):
    depth, d, _ = w_mod.shape
    return pl.pallas_call(
        _adaln_kernel,
        grid=(depth, 6),
        in_specs=[pl.BlockSpec((N_GROUPS, d), lambda l, j: (0, 0)),
                  pl.BlockSpec((None, d, d), lambda l, j: (l, 0, j)),
                  pl.BlockSpec((None, 1, d), lambda l, j: (l, 0, j))],
        out_specs=pl.BlockSpec((None, None, N_GROUPS, d), lambda l, j: (l, j, 0, 0)),
        out_shape=jax.ShapeDtypeStruct((depth, 6, N_GROUPS, d), F32),
        compiler_params=_params(("arbitrary", "arbitrary")),
        name="adaln",
    )(cond, w_mod, b_mod.reshape(depth, 1, 6 * d))


def _row_specs(arrays, tm, pos):
    specs, ends, start = [], [], 0
    for arr in arrays:
        nt = arr.shape[0] // tm
        specs.append(pl.BlockSpec((tm, arr.shape[1]),
                                  lambda *ids, s=start, nt=nt: (jnp.clip(ids[pos] - s, 0, nt - 1), 0)))
        start += nt
        ends.append(start)
    return specs, tuple(ends)


def _pick_rows(refs, ends, i):
    x = refs[-1][...]
    for ref, end in zip(reversed(refs[:-1]), reversed(ends[:-1])):
        x = jnp.where(i < end, ref[...], x)
    return x


def _modulate_kernel(*refs, ends):
    n = len(ends)
    g_ref, sc_ref, sh_ref, o_ref = refs[n:]
    y = _rms(_pick_rows(refs[:n], ends, pl.program_id(0)), g_ref[...])
    o_ref[...] = (y * (1 + sc_ref[...]) + sh_ref[...]).astype(o_ref.dtype)


def _modulate(xs, gain, scale, shift, n_p, t_s, tm=512):
    n, d = sum(x.shape[0] for x in xs), xs[0].shape[1]
    group = _group_of_tile(tm, n_p, t_s)
    specs, ends = _row_specs(xs, tm, 0)
    return pl.pallas_call(
        functools.partial(_modulate_kernel, ends=ends),
        grid=(n // tm,),
        in_specs=specs + [pl.BlockSpec((1, d), lambda i: (0, 0)),
                          pl.BlockSpec((None, 1, d), lambda i: (group(i), 0, 0)),
                          pl.BlockSpec((None, 1, d), lambda i: (group(i), 0, 0))],
        out_specs=pl.BlockSpec((tm, d), lambda i: (i, 0)),
        out_shape=jax.ShapeDtypeStruct((n, d), BF16),
        compiler_params=_params(("arbitrary",)),
        name="modulate",
    )(*xs, gain.reshape(1, d), scale.reshape(N_GROUPS, 1, d), shift.reshape(N_GROUPS, 1, d))


def _mm_kernel(*refs, ends):
    n = len(ends)
    w_ref, o_ref, wb_ref = refs[n:]

    @pl.when(pl.program_id(1) == 0)
    def _():
        wb_ref[...] = w_ref[...].astype(BF16)
    x = _pick_rows(refs[:n], ends, pl.program_id(1))
    o_ref[...] = jnp.dot(x, wb_ref[...], preferred_element_type=F32).astype(o_ref.dtype)


def _matmul(xs, w3, layer, col0_blocks, tn, n_out, out_dtype=F32, tm=512, name="matmul"):
    n, k = sum(x.shape[0] for x in xs), xs[0].shape[1]
    specs, ends = _row_specs(xs, tm, 1)
    return pl.pallas_call(
        functools.partial(_mm_kernel, ends=ends),
        grid=(n_out // tn, n // tm),
        in_specs=specs + [pl.BlockSpec((None, k, tn), lambda j, i: (layer, 0, col0_blocks + j))],
        out_specs=pl.BlockSpec((tm, tn), lambda j, i: (i, j)),
        out_shape=jax.ShapeDtypeStruct((n, n_out), out_dtype),
        scratch_shapes=[pltpu.VMEM((k, tn), BF16)],
        compiler_params=_params(("arbitrary", "arbitrary"), VMEM_LIMIT),
        name=name,
    )(*xs, w3)


def _mm_res_kernel(*refs, col_ends, res_ends):
    i = pl.program_id(0)
    pos = 0
    cols = []
    for ends in col_ends:
        cols.append(refs[pos:pos + len(ends)])
        pos += len(ends)
    res_refs = refs[pos:pos + len(res_ends)]
    w_ref, gate_ref, o_ref, wb_ref = refs[pos + len(res_ends):]

    @pl.when(i == 0)
    def _():
        wb_ref[...] = w_ref[...].astype(BF16)
    y, k0 = None, 0
    for col_refs, ends in zip(cols, col_ends):
        x = _pick_rows(col_refs, ends, i)
        part = jnp.dot(x, wb_ref[k0:k0 + x.shape[1], :], preferred_element_type=F32)
        y = part if y is None else y + part
        k0 += x.shape[1]
    o_ref[...] = _pick_rows(res_refs, res_ends, i) + gate_ref[...] * y


def _matmul_residual(mix_cols, w3, layer, res, gate, n_p, t_s, tm=512):
    n, d = sum(r.shape[0] for r in res), res[0].shape[1]
    k = sum(col[0].shape[1] for col in mix_cols)
    group = _group_of_tile(tm, n_p, t_s)
    specs, col_ends = [], []
    for col in mix_cols:
        s, e = _row_specs(col, tm, 0)
        specs += s
        col_ends.append(e)
    res_specs, res_ends = _row_specs(res, tm, 0)
    return pl.pallas_call(
        functools.partial(_mm_res_kernel, col_ends=tuple(col_ends), res_ends=res_ends),
        grid=(n // tm,),
        in_specs=specs + res_specs + [pl.BlockSpec((None, k, d), lambda i: (layer, 0, 0)),
                                      pl.BlockSpec((None, 1, d), lambda i: (group(i), 0, 0))],
        out_specs=pl.BlockSpec((tm, d), lambda i: (i, 0)),
        out_shape=jax.ShapeDtypeStruct((n, d), F32),
        scratch_shapes=[pltpu.VMEM((k, d), BF16)],
        compiler_params=_params(("arbitrary",), VMEM_LIMIT),
        name="out_proj_residual",
    )(*[piece for col in mix_cols for piece in col], *res, w3, gate.reshape(N_GROUPS, 1, d))


def _rope_tables(t_len, d):
    half, quarter = d // 2, d // 4
    pos = np.arange(t_len)
    row, col = pos // GRID_W, pos % GRID_W
    inv = ROPE_BASE ** (-np.arange(quarter, dtype=np.float64) / quarter)
    ang_r = row[:, None] * inv[None, :]
    ang_c = col[:, None] * inv[None, :]
    cos = np.concatenate([np.cos(ang_r), np.cos(ang_r), np.cos(ang_c), np.cos(ang_c)], axis=1)
    sin = np.concatenate([-np.sin(ang_r), np.sin(ang_r), -np.sin(ang_c), np.sin(ang_c)], axis=1)
    return jnp.asarray(cos, F32), jnp.asarray(sin, F32)


def _swap_pairs(x):
    q = x.shape[-1] // 4
    return jnp.concatenate([x[:, q:2 * q], x[:, :q], x[:, 3 * q:], x[:, 2 * q:3 * q]], axis=-1)


def _rope(x, cos, sin):
    return x * cos + _swap_pairs(x) * sin


def _softmax_attend(parts, sink):
    m = parts[0][0].max(axis=-1, keepdims=True)
    for s, _ in parts[1:]:
        m = jnp.maximum(m, s.max(axis=-1, keepdims=True))
    if sink is not None:
        m = jnp.maximum(m, sink)
    den = jnp.exp(sink - m) if sink is not None else 0.0
    acc = None
    for s, v in parts:
        p = jnp.exp(s - m)
        den = den + p.sum(axis=-1, keepdims=True)
        o = _bdot(p, v)
        acc = o if acc is None else acc + o
    return acc / den


def _group_sinks(sink_ref, hk, rows):
    head = lax.broadcasted_iota(jnp.int32, (A_GROUP * rows, 1), 0) // rows
    col = jnp.full((A_GROUP * rows, 1), sink_ref[hk * A_GROUP], F32)
    for g in range(1, A_GROUP):
        col = jnp.where(head == g, sink_ref[hk * A_GROUP + g], col)
    return col


def _attn_a_prompt_kernel(sink_ref, q_ref, kv_ref, gq_ref, gk_ref, o_ref, kn_ref):
    scale = A_HEAD_DIM ** -0.5
    q = q_ref[...]
    kv = kv_ref[...]
    t = q.shape[0]
    outs = []
    kns = []
    for hk in range(A_KV_HEADS):
        k = _rms(kv[:, hk * A_HEAD_DIM:(hk + 1) * A_HEAD_DIM], gk_ref[...])
        v = kv[:, (A_KV_HEADS + hk) * A_HEAD_DIM:(A_KV_HEADS + hk + 1) * A_HEAD_DIM]
        kns.append(k)
        qg = jnp.concatenate([_rms(q[:, h * A_HEAD_DIM:(h + 1) * A_HEAD_DIM], gq_ref[...])
                              for h in range(hk * A_GROUP, (hk + 1) * A_GROUP)], axis=0)
        s = _bdot_nt(qg, k) * scale
        o = _softmax_attend([(s, v)], _group_sinks(sink_ref, hk, t))
        outs += [o[g * t:(g + 1) * t] for g in range(A_GROUP)]
    o_ref[...] = jnp.concatenate(outs, axis=-1).astype(o_ref.dtype)
    kn_ref[...] = jnp.concatenate(kns, axis=-1)


def _attn_a_prompt(proj, n_seq, t, sink, gq, gk):
    qw = A_HEADS * A_HEAD_DIM
    kvw = 2 * A_KV_HEADS * A_HEAD_DIM
    return pl.pallas_call(
        _attn_a_prompt_kernel,
        grid=(n_seq,),
        in_specs=[pl.BlockSpec(memory_space=pltpu.SMEM),
                  pl.BlockSpec((t, qw), lambda b: (b, 0)),
                  pl.BlockSpec((t, kvw), lambda b: (b, qw // kvw)),
                  pl.BlockSpec((1, A_HEAD_DIM), lambda b: (0, 0)),
                  pl.BlockSpec((1, A_HEAD_DIM), lambda b: (0, 0))],
        out_specs=[pl.BlockSpec((t, qw), lambda b: (b, 0)),
                   pl.BlockSpec((t, A_KV_HEADS * A_HEAD_DIM), lambda b: (b, 0))],
        out_shape=[jax.ShapeDtypeStruct((n_seq * t, qw), BF16),
                   jax.ShapeDtypeStruct((n_seq * t, A_KV_HEADS * A_HEAD_DIM), F32)],
        compiler_params=_params(("arbitrary",)),
        name="attn_a_context",
    )(sink, proj, proj, gq.reshape(1, -1), gk.reshape(1, -1))


def _attn_a_sample_kernel(sink_ref, q_ref, kv_ref, kc_ref, vc_ref, gq_ref, gk_ref, cos_ref, sin_ref,
                          o_ref, *, t):
    scale = A_HEAD_DIM ** -0.5
    i = pl.program_id(1)
    win = 3 * A_BLOCK
    q0 = pl.multiple_of(i * A_BLOCK, A_BLOCK)
    k0 = pl.multiple_of(jnp.clip((i - 1) * A_BLOCK, 0, t - win), A_BLOCK)
    q = q_ref[...]
    kv = kv_ref[pl.ds(k0, win), :]
    cq, sq = cos_ref[pl.ds(q0, A_BLOCK), :], sin_ref[pl.ds(q0, A_BLOCK), :]
    ck, sk = cos_ref[pl.ds(k0, win), :], sin_ref[pl.ds(k0, win), :]
    rows = A_GROUP * A_BLOCK
    qpos = q0 + lax.broadcasted_iota(jnp.int32, (rows, win), 0) % A_BLOCK
    kpos = k0 + lax.broadcasted_iota(jnp.int32, (rows, win), 1)
    mask = jnp.abs(qpos - kpos) <= A_WINDOW
    outs = []
    for hk in range(A_KV_HEADS):
        k = _rope(_rms(kv[:, hk * A_HEAD_DIM:(hk + 1) * A_HEAD_DIM], gk_ref[...]), ck, sk)
        v = kv[:, (A_KV_HEADS + hk) * A_HEAD_DIM:(A_KV_HEADS + hk + 1) * A_HEAD_DIM]
        qg = jnp.concatenate([_rope(_rms(q[:, h * A_HEAD_DIM:(h + 1) * A_HEAD_DIM], gq_ref[...]), cq, sq)
                              for h in range(hk * A_GROUP, (hk + 1) * A_GROUP)], axis=0)
        s1 = jnp.where(mask, _bdot_nt(qg, k) * scale, NEG_INF)
        s2 = _bdot_nt(qg, kc_ref[hk]) * scale
        o = _softmax_attend([(s1, v), (s2, vc_ref[hk])], _group_sinks(sink_ref, hk, A_BLOCK))
        outs += [o[g * A_BLOCK:(g + 1) * A_BLOCK] for g in range(A_GROUP)]
    o_ref[...] = jnp.concatenate(outs, axis=-1).astype(o_ref.dtype)


def _attn_a_sample(proj, row0, n_seq, t, k_ctx, v_ctx, sink, gq, gk):
    qw = A_HEADS * A_HEAD_DIM
    kvw = 2 * A_KV_HEADS * A_HEAD_DIM
    nqb = t // A_BLOCK
    cos, sin = _rope_tables(t, A_HEAD_DIM)
    past = k_ctx.shape[2]
    return pl.pallas_call(
        functools.partial(_attn_a_sample_kernel, t=t),
        grid=(n_seq, nqb),
        in_specs=[pl.BlockSpec(memory_space=pltpu.SMEM),
                  pl.BlockSpec((A_BLOCK, qw), lambda b, i: (row0 // A_BLOCK + b * nqb + i, 0)),
                  pl.BlockSpec((t, kvw), lambda b, i: (row0 // t + b, qw // kvw)),
                  pl.BlockSpec((None, A_KV_HEADS, past, A_HEAD_DIM), lambda b, i: (b, 0, 0, 0)),
                  pl.BlockSpec((None, A_KV_HEADS, past, A_HEAD_DIM), lambda b, i: (b, 0, 0, 0)),
                  pl.BlockSpec((1, A_HEAD_DIM), lambda b, i: (0, 0)),
                  pl.BlockSpec((1, A_HEAD_DIM), lambda b, i: (0, 0)),
                  pl.BlockSpec((t, A_HEAD_DIM), lambda b, i: (0, 0)),
                  pl.BlockSpec((t, A_HEAD_DIM), lambda b, i: (0, 0))],
        out_specs=pl.BlockSpec((A_BLOCK, qw), lambda b, i: (b * nqb + i, 0)),
        out_shape=jax.ShapeDtypeStruct((n_seq * t, qw), BF16),
        compiler_params=_params(("arbitrary", "arbitrary")),
        name="attn_a_latent",
    )(sink, proj, proj, k_ctx, v_ctx, gq.reshape(1, -1), gk.reshape(1, -1), cos, sin)


def _per_head_lanes(x, fn):
    lane = lax.broadcasted_iota(jnp.int32, x.shape, 1)
    lo = fn(x[:, :B_DK])
    hi = fn(x[:, B_DK:])
    return jnp.where(lane < B_DK, lo, hi)


def _conv_silu(x, w):
    t = x.shape[0]
    row = lax.broadcasted_iota(jnp.int32, x.shape, 0)
    prev = jnp.where(row == 0, 0.0, pltpu.roll(x, 1, 0))
    nxt = jnp.where(row == t - 1, 0.0, pltpu.roll(x, t - 1, 0))
    return _silu(prev * w[0:1, :] + x * w[1:2, :] + nxt * w[2:3, :])


M_SAME, M_TRI, M_TRI_T, M_STRICT, M_BDIAG = range(5)


def _delta_masks(r):
    c = B_CHUNK
    ii = lax.broadcasted_iota(jnp.int32, (r, r), 0)
    jj = lax.broadcasted_iota(jnp.int32, (r, r), 1)
    same = (ii // c) == (jj // c)
    ahead = jnp.where(ii < r // 2, ii - jj, jj - ii)
    tri = jnp.where(same, ahead, -1) >= 0
    tri_t = jnp.where(same, ahead, 1) <= 0
    strict = jnp.where(same, ahead, -1) > 0
    bdiag = (ii // 16) == (jj // 16)
    return [x.astype(F32) for x in (same, tri, tri_t, strict, bdiag)]


def _delta_prepare(q, k, v, g_col, g_row, beta, mask_ref):
    c = B_CHUNK
    r = q.shape[0]
    dot = functools.partial(jnp.dot, preferred_element_type=F32)
    gc_col = jnp.sum(mask_ref[M_TRI] * g_row, axis=1, keepdims=True)
    gc_row = jnp.sum(mask_ref[M_TRI_T] * g_col, axis=0, keepdims=True)
    g_tot = jnp.sum(mask_ref[M_SAME] * g_row, axis=1, keepdims=True)
    ex = jnp.exp((gc_col - gc_row) * mask_ref[M_TRI])
    kb = k * beta
    qk = _bdot_nt(jnp.concatenate([kb, q], axis=0), k)
    m = qk[:r] * (ex * mask_ref[M_STRICT])
    aqk = qk[r:] * (ex * mask_ref[M_TRI])
    dg = m * mask_ref[M_BDIAG]
    off = m - dg
    n1 = -dg
    n1b = n1.astype(BF16)
    n2 = dot(n1b, n1b)
    n2b = n2.astype(BF16)
    t = dot(jnp.concatenate([n1b, n2b], axis=0), n2b)
    xs = n1 + n2 + t[:r]
    n4 = t[r:]
    n4b = n4.astype(BF16)
    t = dot(jnp.concatenate([xs.astype(BF16), n4b], axis=0), n4b)
    xs = xs + n4 + t[:r]
    n8 = t[r:]
    xs = xs + n8 + dot(xs.astype(BF16), n8.astype(BF16))
    xsb = xs.astype(BF16)
    f = -(off + dot(xsb, off.astype(BF16)))
    fb = f.astype(BF16)
    t = dot(fb, jnp.concatenate([xsb, fb], axis=1))
    ys = xs + f + t[:, :r]
    f2 = t[:, r:]
    ts = ys + f2 + dot(f2.astype(BF16), ys.astype(BF16))
    egc = jnp.exp(gc_col)
    rhs = jnp.concatenate([v * beta, kb * egc], axis=-1)
    uw = rhs + _bdot(ts, rhs)

    def block_diag(x):
        return jnp.concatenate([x] * (r // c), axis=-1) * mask_ref[M_SAME]

    kd = k * jnp.exp(g_tot - gc_col)
    kd_t = jnp.concatenate([kd, jnp.zeros_like(kd)], axis=-1).T[:c]
    kd_t = jnp.concatenate([kd_t] * (r // c), axis=0) * mask_ref[M_SAME]
    e_tot = jnp.broadcast_to(jnp.exp(g_tot), v.shape)
    return uw[:, :B_DV], e_tot, block_diag(uw[:, B_DV:]), aqk, block_diag(q * egc), kd_t


def _deltanet_kernel(alog_ref, dtb_ref, q_ref, k_ref, v_ref, z_ref, cwq_ref, cwk_ref, cwv_ref,
                     tail_ref, tailt_ref, onorm_ref, s0f_ref, s0b_ref,
                     o_ref, sf_ref, sb_ref,
                     qc_ref, kc_ref, vc_ref, oacc_ref, u_ref, et_ref, w_ref, aqk_ref, qg_ref, kdt_ref,
                     mask_ref, *, n_chunks):
    hp = pl.program_id(1)
    c = B_CHUNK

    @pl.when(jnp.logical_and(pl.program_id(0) == 0, hp == 0))
    def _():
        for i, x in enumerate(_delta_masks(4 * c)):
            mask_ref[i] = x

    def l2n(x):
        ss = _per_head_lanes(x * x, lambda a: jnp.sum(a, axis=-1, keepdims=True))
        return x * lax.rsqrt(ss + EPS)

    qc_ref[...] = l2n(_conv_silu(q_ref[...], cwq_ref[...])) * (B_DK ** -0.5)
    kc_ref[...] = l2n(_conv_silu(k_ref[...], cwk_ref[...]))
    vc_ref[...] = _conv_silu(v_ref[...], cwv_ref[...])
    oacc_ref[...] = jnp.zeros_like(oacc_ref)

    lane32 = lax.broadcasted_iota(jnp.int32, (c, 4 * B_HEADS), 1)

    def gates(chunk, d, head):
        tail = tail_ref[pl.ds(pl.multiple_of(chunk * c, c), c), :]
        ia = 2 * d * B_HEADS + head
        ib = ia + B_HEADS
        a_col = jnp.sum(jnp.where(lane32 == ia, tail, 0.0), axis=1, keepdims=True)
        b_col = jnp.sum(jnp.where(lane32 == ib, tail, 0.0), axis=1, keepdims=True)
        a_row = tailt_ref[chunk, pl.ds(ia, 1), :]
        na = -jnp.exp(alog_ref[d, head])
        bias = dtb_ref[d, head]
        g_col = na * jax.nn.softplus(a_col + bias)
        g_row = na * jax.nn.softplus(a_row + bias)
        return g_col, g_row, jax.nn.sigmoid(b_col)

    def stacked(ref, chunks):
        parts = []
        for chunk in chunks:
            x = ref[pl.ds(pl.multiple_of(chunk * c, c), c), :]
            parts += [x[:, :B_DK], x[:, B_DK:]]
        return jnp.concatenate(parts, axis=0)

    def prepare(j, carry):
        chunks = (j, n_chunks - 1 - j)
        gs = [gates(chunks[d], d, 2 * hp + hh) for d in range(2) for hh in range(2)]
        g_col = jnp.concatenate([g[0] for g in gs], axis=0)
        g_row = jnp.concatenate([g[1] for g in gs], axis=1)
        beta = jnp.concatenate([g[2] for g in gs], axis=0)
        outs = _delta_prepare(stacked(qc_ref, chunks), stacked(kc_ref, chunks), stacked(vc_ref, chunks),
                              g_col, g_row, beta, mask_ref)
        for ref, x in zip((u_ref, et_ref, w_ref, aqk_ref, qg_ref, kdt_ref), outs):
            ref[j] = x.astype(ref.dtype)
        return carry

    lax.fori_loop(0, n_chunks, prepare, 0, unroll=4)

    def scan(j, s):
        sb = s.astype(BF16)
        delta = u_ref[j] - jnp.dot(w_ref[j], sb, preferred_element_type=F32)
        db = delta.astype(BF16)
        o = (jnp.dot(qg_ref[j], sb, preferred_element_type=F32)
             + jnp.dot(aqk_ref[j], db, preferred_element_type=F32))
        for d, chunk in enumerate((j, n_chunks - 1 - j)):
            rows = pl.ds(pl.multiple_of(chunk * c, c), c)
            oacc_ref[rows, :] += jnp.concatenate([o[2 * d * c:(2 * d + 1) * c],
                                                  o[(2 * d + 1) * c:(2 * d + 2) * c]], axis=-1)
        return s * et_ref[j] + jnp.dot(kdt_ref[j], db, preferred_element_type=F32)

    init = jnp.concatenate([s0f_ref[0], s0f_ref[1], s0b_ref[0], s0b_ref[1]], axis=0)
    fin = lax.fori_loop(0, n_chunks, scan, init)
    sf_ref[0], sf_ref[1], sb_ref[0], sb_ref[1] = (fin[i * B_DK:(i + 1) * B_DK] for i in range(4))

    o = oacc_ref[...]
    ms = _per_head_lanes(o * o, lambda a: jnp.mean(a, axis=-1, keepdims=True))
    o_ref[...] = (o * lax.rsqrt(ms + EPS) * onorm_ref[...] * _silu(z_ref[...])).astype(o_ref.dtype)


def _deltanet(proj, tail, row0, n_seq, t, conv_w, a_log, dt_bias, o_norm, s0_f, s0_b):
    c = B_CHUNK
    n_chunks = t // c
    lw = 2 * B_DK
    col_q = (A_HEADS + 2 * A_KV_HEADS) * A_HEAD_DIM // lw
    nhp = B_HEADS // 2
    rows = tail[row0:row0 + n_seq * t]
    tail_t = rows.reshape(n_seq, n_chunks, c, 4 * B_HEADS).transpose(0, 1, 3, 2)
    onorm2 = jnp.concatenate([o_norm, o_norm]).reshape(1, lw)
    b0 = row0 // t
    seq_blk = lambda off: pl.BlockSpec((t, lw), lambda b, h: (b0 + b, col_q + off + h))
    cw_blk = lambda off: pl.BlockSpec((3, lw), lambda b, h: (0, off + h))
    st_blk = pl.BlockSpec((None, 2, B_DK, B_DV), lambda b, h: (b, h, 0, 0))
    return pl.pallas_call(
        functools.partial(_deltanet_kernel, n_chunks=n_chunks),
        grid=(n_seq, nhp),
        in_specs=[pl.BlockSpec(memory_space=pltpu.SMEM), pl.BlockSpec(memory_space=pltpu.SMEM),
                  seq_blk(0), seq_blk(nhp), seq_blk(2 * nhp), seq_blk(3 * nhp),
                  cw_blk(0), cw_blk(nhp), cw_blk(2 * nhp),
                  pl.BlockSpec((t, 4 * B_HEADS), lambda b, h: (b, 0)),
                  pl.BlockSpec((None, n_chunks, 4 * B_HEADS, c), lambda b, h: (b, 0, 0, 0)),
                  pl.BlockSpec((1, lw), lambda b, h: (0, 0)),
                  st_blk, st_blk],
        out_specs=[pl.BlockSpec((t, lw), lambda b, h: (b, h)), st_blk, st_blk],
        out_shape=[jax.ShapeDtypeStruct((n_seq * t, B_HEADS * B_DV), BF16),
                   jax.ShapeDtypeStruct((n_seq, B_HEADS, B_DK, B_DV), F32),
                   jax.ShapeDtypeStruct((n_seq, B_HEADS, B_DK, B_DV), F32)],
        scratch_shapes=[pltpu.VMEM((t, lw), F32)] * 4 + [pltpu.VMEM((n_chunks, 4 * c, B_DV), F32)] * 2
        + [pltpu.VMEM((n_chunks, 4 * c, 4 * c), BF16)] * 4 + [pltpu.VMEM((5, 4 * c, 4 * c), F32)],
        compiler_params=_params(("arbitrary", "arbitrary")),
        name="deltanet",
    )(a_log, dt_bias, proj, proj, proj, proj, conv_w, conv_w, conv_w, rows, tail_t, onorm2, s0_f, s0_b)


def _lora_norm_kernel(p_ref, gq_ref, gkv_ref, cq_ref, ckv_ref, ckvb_ref):
    p = p_ref[...]
    cq_ref[...] = _rms(p[:, :C_Q_LORA], gq_ref[...]).astype(cq_ref.dtype)
    ckv = _rms(p[:, C_Q_LORA:C_Q_LORA + C_KV_LORA], gkv_ref[...])
    ckv_ref[...] = ckv
    ckvb_ref[...] = ckv.astype(ckvb_ref.dtype)


def _lora_norm(p1, gq, gkv, tm=512):
    n, w = p1.shape
    return pl.pallas_call(
        _lora_norm_kernel,
        grid=(n // tm,),
        in_specs=[pl.BlockSpec((tm, w), lambda i: (i, 0)),
                  pl.BlockSpec((1, C_Q_LORA), lambda i: (0, 0)),
                  pl.BlockSpec((1, C_KV_LORA), lambda i: (0, 0))],
        out_specs=[pl.BlockSpec((tm, C_Q_LORA), lambda i: (i, 0)),
                   pl.BlockSpec((tm, C_KV_LORA), lambda i: (i, 0)),
                   pl.BlockSpec((tm, C_KV_LORA), lambda i: (i, 0))],
        out_shape=[jax.ShapeDtypeStruct((n, C_Q_LORA), BF16),
                   jax.ShapeDtypeStruct((n, C_KV_LORA), F32),
                   jax.ShapeDtypeStruct((n, C_KV_LORA), BF16)],
        compiler_params=_params(("arbitrary",)),
        name="lora_norm",
    )(p1, gq.reshape(1, -1), gkv.reshape(1, -1))


def _mla_head_q(q, h, gq):
    qn, qr = q[:, h * C_QK:h * C_QK + C_NOPE], q[:, h * C_QK + C_NOPE:(h + 1) * C_QK]
    rn = lax.rsqrt((jnp.sum(qn * qn, axis=-1, keepdims=True)
                    + jnp.sum(qr * qr, axis=-1, keepdims=True)) / C_QK + EPS)
    return qn * rn * gq[:, :C_NOPE], qr * rn * gq[:, C_NOPE:]


def _mla_head_k(kv, kr, kr_ss, h, gk):
    kn = kv[:, h * (C_NOPE + C_V):h * (C_NOPE + C_V) + C_NOPE]
    v = kv[:, h * (C_NOPE + C_V) + C_NOPE:(h + 1) * (C_NOPE + C_V)]
    rn = lax.rsqrt((jnp.sum(kn * kn, axis=-1, keepdims=True) + kr_ss) / C_QK + EPS)
    return kn * rn * gk[:, :C_NOPE], kr * rn, v


C_STACK = 4


def _place(x, i, n):
    t, w = x.shape
    parts = ([jnp.zeros((t, i * w), x.dtype)] if i else []) + [x]
    if i < n - 1:
        parts.append(jnp.zeros((t, (n - 1 - i) * w), x.dtype))
    return jnp.concatenate(parts, axis=-1)


def _by_block(cols, lane, w):
    out = cols[-1]
    for i in reversed(range(len(cols) - 1)):
        out = jnp.where(lane < (i + 1) * w, cols[i], out)
    return out


def _attn_c_prompt_kernel(q_ref, kv_ref, p_ref, gq_ref, gk_ref, o_ref):
    scale = C_QK ** -0.5
    n = C_STACK
    q = q_ref[...]
    kv = kv_ref[...]
    gk = gk_ref[...]
    t = q.shape[0]
    kr_raw = p_ref[...][:, C_Q_LORA + C_KV_LORA:]
    kr_ss = jnp.sum(kr_raw * kr_raw, axis=-1, keepdims=True)
    kr_g = kr_raw * gk[:, C_NOPE:]
    lane_q = lax.broadcasted_iota(jnp.int32, (t, n * C_QK), 1)
    lane_o = lax.broadcasted_iota(jnp.int32, (t, n * C_V), 1)
    outs = []
    for grp in range(C_HEADS // n):
        qs = q[:, grp * n * C_QK:(grp + 1) * n * C_QK]
        rn = [lax.rsqrt(jnp.mean(qs[:, i * C_QK:(i + 1) * C_QK] ** 2, axis=-1, keepdims=True) + EPS)
              for i in range(n)]
        qs = qs * _by_block(rn, lane_q, C_QK) * gq_ref[...]
        k_rows, v_rows = [], []
        for i in range(n):
            kn, kr, v = _mla_head_k(kv, kr_g, kr_ss, grp * n + i, gk)
            k_rows.append(_place(jnp.concatenate([kn, kr], axis=-1), i, n))
            v_rows.append(_place(v, i, n))
        s = _bdot_nt(qs, jnp.concatenate(k_rows, axis=0)) * scale
        ps, rden = [], []
        for i in range(n):
            si = s[:, i * t:(i + 1) * t]
            pi = jnp.exp(si - si.max(axis=-1, keepdims=True))
            ps.append(pi)
            rden.append(1.0 / pi.sum(axis=-1, keepdims=True))
        o = _bdot(jnp.concatenate(ps, axis=-1), jnp.concatenate(v_rows, axis=0))
        outs.append(o * _by_block(rden, lane_o, C_V))
    o_ref[...] = jnp.concatenate(outs, axis=-1).astype(o_ref.dtype)


def _attn_c_prompt(q, kv, p1, n_seq, t, gq, gk):
    return pl.pallas_call(
        _attn_c_prompt_kernel,
        grid=(n_seq,),
        in_specs=[pl.BlockSpec((t, q.shape[1]), lambda b: (b, 0)),
                  pl.BlockSpec((t, kv.shape[1]), lambda b: (b, 0)),
                  pl.BlockSpec((t, p1.shape[1]), lambda b: (b, 0)),
                  pl.BlockSpec((1, C_STACK * C_QK), lambda b: (0, 0)),
                  pl.BlockSpec((1, C_QK), lambda b: (0, 0))],
        out_specs=pl.BlockSpec((t, C_HEADS * C_V), lambda b: (b, 0)),
        out_shape=jax.ShapeDtypeStruct((n_seq * t, C_HEADS * C_V), BF16),
        compiler_params=_params(("arbitrary",), VMEM_LIMIT),
        name="attn_c_context",
    )(q, kv, p1, jnp.tile(gq, C_STACK).reshape(1, -1), gk.reshape(1, -1))


def _attn_c_sample_kernel(q_ref, kv_ref, p_ref, kvc_ref, krc_ref, gq_ref, gk_ref, cos_ref, sin_ref,
                          o_ref, *, tq):
    scale = C_QK ** -0.5
    i = pl.program_id(1)
    q0 = pl.multiple_of(i * tq, tq)
    q = q_ref[...]
    kv = kv_ref[...]
    kvc = kvc_ref[...]
    gq, gk = gq_ref[...], gk_ref[...]
    cos, sin = cos_ref[...], sin_ref[...]
    cq, sq = cos_ref[pl.ds(q0, tq), :], sin_ref[pl.ds(q0, tq), :]
    kr_raw = p_ref[...][:, C_Q_LORA + C_KV_LORA:]
    kr_ss = jnp.sum(kr_raw * kr_raw, axis=-1, keepdims=True)
    kr_g = _rope(kr_raw * gk[:, C_NOPE:], cos, sin)
    krc_raw = krc_ref[...]
    krc_ss = jnp.sum(krc_raw * krc_raw, axis=-1, keepdims=True)
    krc_g = krc_raw * gk[:, C_NOPE:]
    outs = []
    for h in range(C_HEADS):
        qn, qr = _mla_head_q(q, h, gq)
        qr = _rope(qr, cq, sq)
        kn, kr, v = _mla_head_k(kv, kr_g, kr_ss, h, gk)
        knc, krc, vc = _mla_head_k(kvc, krc_g, krc_ss, h, gk)
        s1 = (_bdot_nt(qn, kn) + _bdot_nt(qr, kr)) * scale
        s2 = (_bdot_nt(qn, knc) + _bdot_nt(qr, krc)) * scale
        outs.append(_softmax_attend([(s1, v), (s2, vc)], None))
    o_ref[...] = jnp.concatenate(outs, axis=-1).astype(o_ref.dtype)


def _attn_c_sample(q, kv, p1, row0, n_seq, t, kr_ctx, gq, gk, tq=256):
    n = p1.shape[0]
    past = kr_ctx.shape[1]
    nq = t // tq
    cos, sin = _rope_tables(t, C_ROPE)
    return pl.pallas_call(
        functools.partial(_attn_c_sample_kernel, tq=tq),
        grid=(n_seq, nq),
        in_specs=[pl.BlockSpec((tq, q.shape[1]), lambda b, i: (row0 // tq + b * nq + i, 0)),
                  pl.BlockSpec((t, kv.shape[1]), lambda b, i: (row0 // t + b, 0)),
                  pl.BlockSpec((t, p1.shape[1]), lambda b, i: (row0 // t + b, 0)),
                  pl.BlockSpec((past, kv.shape[1]), lambda b, i: (n // past + b, 0)),
                  pl.BlockSpec((None, past, C_ROPE), lambda b, i: (b, 0, 0)),
                  pl.BlockSpec((1, C_QK), lambda b, i: (0, 0)),
                  pl.BlockSpec((1, C_QK), lambda b, i: (0, 0)),
                  pl.BlockSpec((t, C_ROPE), lambda b, i: (0, 0)),
                  pl.BlockSpec((t, C_ROPE), lambda b, i: (0, 0))],
        out_specs=pl.BlockSpec((tq, C_HEADS * C_V), lambda b, i: (b * nq + i, 0)),
        out_shape=jax.ShapeDtypeStruct((n_seq * t, C_HEADS * C_V), BF16),
        compiler_params=_params(("arbitrary", "arbitrary"), VMEM_LIMIT),
        name="attn_c_latent",
    )(q, kv, p1, kv, kr_ctx, gq.reshape(1, -1), gk.reshape(1, -1), cos, sin)


ROUTE_TILE = 512
ROUTE_BLOCK = 512
SLOT_ALIGN = 8
LOCAL_SLOTS = 2304
META_LANES = 128
FF_CHUNK = 256


def _router_kernel(x_ref, g_ref, sc_ref, sh_ref, wr_ref, br_ref, h_ref, meta_ref, cntb_ref, cnt_ref,
                   run_ref):
    @pl.when(pl.program_id(0) == 0)
    def _():
        run_ref[...] = jnp.zeros_like(run_ref)

    h = _rms(x_ref[...], g_ref[...]) * (1 + sc_ref[...]) + sh_ref[...]
    h_ref[...] = h.astype(h_ref.dtype)
    tm = h.shape[0]
    logits = _hdot(h, wr_ref[...]) + br_ref[...]
    lane = lax.broadcasted_iota(jnp.int32, logits.shape, 1)
    work = logits
    picks, tops = [], []
    for _ in range(TOP_K):
        m = work.max(axis=-1, keepdims=True)
        first = jnp.min(jnp.where(work == m, lane, N_EXPERTS), axis=-1, keepdims=True)
        pick = lane == first
        picks.append(pick)
        tops.append(m)
        work = jnp.where(pick, -jnp.inf, work)
    sel = sum(p.astype(F32) for p in picks)
    earlier = (lax.broadcasted_iota(jnp.int32, (tm, tm), 0)
               > lax.broadcasted_iota(jnp.int32, (tm, tm), 1)).astype(BF16)
    inside = jnp.dot(earlier, sel.astype(BF16), preferred_element_type=F32)
    cnt = jnp.sum(sel, axis=0, keepdims=True)
    run = jnp.ceil(cnt / SLOT_ALIGN) * SLOT_ALIGN
    lower_e = (lax.broadcasted_iota(jnp.int32, (N_EXPERTS, N_EXPERTS), 0)
               < lax.broadcasted_iota(jnp.int32, (N_EXPERTS, N_EXPERTS), 1)).astype(BF16)
    start = jnp.dot(jnp.broadcast_to(run, (8, N_EXPERTS)).astype(BF16), lower_e,
                    preferred_element_type=F32)[0:1]
    slot = start + inside
    ws = [jnp.exp(t - tops[0]) for t in tops]
    den = sum(ws)
    mlane = lax.broadcasted_iota(jnp.int32, (tm, META_LANES), 1)
    meta = jnp.zeros((tm, META_LANES), F32)
    for k in range(TOP_K):
        meta = jnp.where(mlane == k, jnp.sum(jnp.where(picks[k], slot, 0.0), axis=-1, keepdims=True), meta)
        meta = jnp.where(mlane == TOP_K + k, ws[k] / den, meta)
    meta_ref[...] = meta
    cntb_ref[...] = cnt
    run_ref[...] += run
    cnt_ref[...] = run_ref[...]


def _router(x, gain, scale, shift, w_router, b_router, layer, n_p, t_s):
    n, d = x.shape
    tm = ROUTE_BLOCK
    e = w_router.shape[-1]
    group = _group_of_tile(tm, n_p, t_s)
    return pl.pallas_call(
        _router_kernel,
        grid=(n // tm,),
        in_specs=[pl.BlockSpec((tm, d), lambda i: (i, 0)),
                  pl.BlockSpec((1, d), lambda i: (0, 0)),
                  pl.BlockSpec((None, 1, d), lambda i: (group(i), 0, 0)),
                  pl.BlockSpec((None, 1, d), lambda i: (group(i), 0, 0)),
                  pl.BlockSpec((None, d, e), lambda i: (layer, 0, 0)),
                  pl.BlockSpec((None, 1, e), lambda i: (layer, 0, 0))],
        out_specs=[pl.BlockSpec((tm, d), lambda i: (i, 0)),
                   pl.BlockSpec((tm, META_LANES), lambda i: (i, 0)),
                   pl.BlockSpec((None, 1, e), lambda i: (i, 0, 0)),
                   pl.BlockSpec((1, e), lambda i: (0, 0))],
        out_shape=[jax.ShapeDtypeStruct((n, d), BF16),
                   jax.ShapeDtypeStruct((n, META_LANES), F32),
                   jax.ShapeDtypeStruct((n // tm, 1, e), F32),
                   jax.ShapeDtypeStruct((1, e), F32)],
        scratch_shapes=[pltpu.VMEM((1, e), F32)],
        compiler_params=_params(("arbitrary",)),
        name="router",
    )(x, gain.reshape(1, d), scale.reshape(N_GROUPS, 1, d), shift.reshape(N_GROUPS, 1, d),
      w_router, b_router.reshape(-1, 1, e))


def _route_plan(cnt_blk, cnt_tot, n_tiles, min_tiles):
    cb = cnt_blk[:, 0, :].astype(jnp.int32)
    run = (cb + SLOT_ALIGN - 1) // SLOT_ALIGN * SLOT_ALIGN
    counts = cnt_tot[0].astype(jnp.int32)
    padded = (counts + ROUTE_TILE - 1) // ROUTE_TILE * ROUTE_TILE
    ends = jnp.cumsum(padded)
    offs = ends - padded
    gstart = offs[None, :] + jnp.cumsum(run, axis=0) - run
    lstart = jnp.cumsum(run, axis=1) - run
    tile_start = jnp.arange(n_tiles, dtype=jnp.int32) * ROUTE_TILE
    n_valid = ends[-1] // ROUTE_TILE
    te = jnp.minimum(jnp.sum(ends[None, :] <= tile_start[:, None], axis=1), N_EXPERTS - 1)
    te = jnp.where(tile_start < ends[-1], te, te[jnp.maximum(n_valid - 1, 0)]).astype(jnp.int32)
    rows = jnp.clip((offs + counts)[te] - tile_start, 0, ROUTE_TILE).astype(jnp.int32)
    ragged = jnp.where(counts % ROUTE_TILE != 0, ends // ROUTE_TILE - 1, -1)
    tail = jnp.arange(min_tiles, n_tiles, dtype=jnp.int32)
    fill = jnp.concatenate([ragged, jnp.where(tail >= n_valid, tail, -1)]).astype(jnp.int32)
    runs = (gstart.reshape(-1).astype(jnp.int32), lstart.reshape(-1).astype(jnp.int32),
            (run // SLOT_ALIGN).reshape(-1).astype(jnp.int32),
            (jnp.sum(run, axis=1) // SLOT_ALIGN).astype(jnp.int32))
    return runs, te, rows, n_valid.reshape(1).astype(jnp.int32), fill


RUN_PIECE = 4
WAIT_PIECE = 8


def _start_runs(blk, gstart_ref, lstart_ref, nch_ref, copy):
    big = RUN_PIECE * SLOT_ALIGN

    def per_expert(e, carry):
        idx = blk * N_EXPERTS + e
        g0, l0, n = gstart_ref[idx], lstart_ref[idx], nch_ref[idx]
        n_big = n // RUN_PIECE

        def large(i, c2):
            off = pl.multiple_of(i * big, big)
            copy(pl.multiple_of(l0 + off, SLOT_ALIGN), pl.multiple_of(g0 + off, SLOT_ALIGN), big).start()
            return c2

        def small(i, c2):
            off = pl.multiple_of(n_big * big + i * SLOT_ALIGN, SLOT_ALIGN)
            copy(pl.multiple_of(l0 + off, SLOT_ALIGN), pl.multiple_of(g0 + off, SLOT_ALIGN), SLOT_ALIGN).start()
            return c2

        lax.fori_loop(0, n_big, large, 0)
        lax.fori_loop(0, n - n_big * RUN_PIECE, small, 0)
        return carry

    lax.fori_loop(0, N_EXPERTS, per_expert, 0)


def _wait_runs(n_chunks, copy):
    n_big = n_chunks // WAIT_PIECE

    def large(i, carry):
        copy(0, 0, WAIT_PIECE * SLOT_ALIGN).wait()
        return carry

    def small(i, carry):
        copy(0, 0, SLOT_ALIGN).wait()
        return carry

    lax.fori_loop(0, n_big, large, 0)
    lax.fori_loop(0, n_chunks - n_big * WAIT_PIECE, small, 0)


def _dispatch_kernel(fill_ref, gstart_ref, lstart_ref, nch_ref, ntot_ref, meta_ref, h_ref, xs_ref,
                     loc_ref, zero_ref, sem, zsem):
    blk = pl.program_id(0)

    @pl.when(blk == 0)
    def _():
        zero_ref[...] = jnp.zeros_like(zero_ref)

        def fill_copy(j):
            row0 = pl.multiple_of(fill_ref[j] * ROUTE_TILE, ROUTE_TILE)
            return pltpu.make_async_copy(zero_ref, xs_ref.at[pl.ds(row0, ROUTE_TILE), :], zsem)

        def start(j, carry):
            @pl.when(fill_ref[j] >= 0)
            def _():
                fill_copy(j).start()
            return carry

        def wait(j, carry):
            @pl.when(fill_ref[j] >= 0)
            def _():
                fill_copy(j).wait()
            return carry

        lax.fori_loop(0, fill_ref.shape[0], start, 0)
        lax.fori_loop(0, fill_ref.shape[0], wait, 0)

    hb = h_ref[...]
    tm = hb.shape[0]
    slots_t = meta_ref[...].T[0:TOP_K]
    rows = 256
    for c0 in range(0, LOCAL_SLOTS, rows):
        slot = (c0 + lax.broadcasted_iota(jnp.int32, (rows, tm), 0)).astype(F32)
        onehot = sum(jnp.where(slot == slots_t[k:k + 1], 1.0, 0.0) for k in range(TOP_K))
        loc_ref[c0:c0 + rows, :] = jnp.dot(onehot.astype(BF16), hb,
                                           preferred_element_type=F32).astype(loc_ref.dtype)

    def copy(l0, g0, rows):
        return pltpu.make_async_copy(loc_ref.at[pl.ds(l0, rows), :], xs_ref.at[pl.ds(g0, rows), :], sem)

    _start_runs(blk, gstart_ref, lstart_ref, nch_ref, copy)
    _wait_runs(ntot_ref[blk], copy)


def _dispatch(h, meta, runs, fill, n_slots):
    n, d = h.shape
    tm = ROUTE_BLOCK
    grid_spec = pltpu.PrefetchScalarGridSpec(
        num_scalar_prefetch=5,
        grid=(n // tm,),
        in_specs=[pl.BlockSpec((tm, META_LANES), lambda i, *_: (i, 0)),
                  pl.BlockSpec((tm, d), lambda i, *_: (i, 0))],
        out_specs=pl.BlockSpec(memory_space=pl.ANY),
        scratch_shapes=[pltpu.VMEM((LOCAL_SLOTS, d), F32), pltpu.VMEM((ROUTE_TILE, d), F32),
                        pltpu.SemaphoreType.DMA(()), pltpu.SemaphoreType.DMA(())])
    return pl.pallas_call(
        _dispatch_kernel,
        grid_spec=grid_spec,
        out_shape=jax.ShapeDtypeStruct((n_slots, d), F32),
        compiler_params=_params(("arbitrary",), VMEM_LIMIT),
        name="moe_dispatch",
    )(fill, *runs, meta, h)


def _experts_kernel(te_ref, rows_ref, nv_ref, x_ref, wgu_ref, bgu_ref, wd_ref, bd_ref, y_ref,
                    wgub_ref, wdb_ref, *, d_ff):
    i = pl.program_id(0)
    valid = i < nv_ref[0]
    fresh = jnp.logical_or(i == 0, te_ref[i] != te_ref[jnp.maximum(i - 1, 0)])

    @pl.when(jnp.logical_and(valid, fresh))
    def _():
        wgub_ref[...] = wgu_ref[...].astype(BF16)
        wdb_ref[...] = wd_ref[...].astype(BF16)

    @pl.when(valid)
    def _():
        row = lax.broadcasted_iota(jnp.int32, x_ref.shape, 0)
        x = jnp.where(row < rows_ref[i], x_ref[...], 0.0).astype(BF16)
        acc = None
        for c0 in range(0, d_ff, FF_CHUNK):
            g_cols, u_cols = slice(c0, c0 + FF_CHUNK), slice(d_ff + c0, d_ff + c0 + FF_CHUNK)
            gate = jnp.dot(x, wgub_ref[:, g_cols], preferred_element_type=F32) + bgu_ref[:, g_cols]
            up = jnp.dot(x, wgub_ref[:, u_cols], preferred_element_type=F32) + bgu_ref[:, u_cols]
            gate = jnp.minimum(gate, SWIGLU_LIMIT)
            up = jnp.clip(up, -SWIGLU_LIMIT, SWIGLU_LIMIT)
            act = (up + 1) * gate * jax.nn.sigmoid(SWIGLU_ALPHA * gate)
            part = jnp.dot(act.astype(BF16), wdb_ref[c0:c0 + FF_CHUNK, :], preferred_element_type=F32)
            acc = part if acc is None else acc + part
        y_ref[...] = acc + bd_ref[...]

    @pl.when(jnp.logical_not(valid))
    def _():
        y_ref[...] = jnp.zeros_like(y_ref)


def _experts(xs, te, rows, n_valid, w_gu, b_gu, w_down, b_down, layer):
    n_slots, d = xs.shape
    _, e, _, two_ff = w_gu.shape
    last = lambda i, nv: jnp.minimum(i, nv[0] - 1)
    grid_spec = pltpu.PrefetchScalarGridSpec(
        num_scalar_prefetch=3,
        grid=(n_slots // ROUTE_TILE,),
        in_specs=[pl.BlockSpec((ROUTE_TILE, d), lambda i, te, rw, nv: (last(i, nv), 0)),
                  pl.BlockSpec((None, None, d, two_ff), lambda i, te, rw, nv: (layer, te[i], 0, 0)),
                  pl.BlockSpec((None, None, 1, two_ff), lambda i, te, rw, nv: (layer, te[i], 0, 0)),
                  pl.BlockSpec((None, None, two_ff // 2, d), lambda i, te, rw, nv: (layer, te[i], 0, 0)),
                  pl.BlockSpec((None, None, 1, d), lambda i, te, rw, nv: (layer, te[i], 0, 0))],
        out_specs=pl.BlockSpec((ROUTE_TILE, d), lambda i, te, rw, nv: (i, 0)),
        scratch_shapes=[pltpu.VMEM((d, two_ff), BF16), pltpu.VMEM((two_ff // 2, d), BF16)])
    return pl.pallas_call(
        functools.partial(_experts_kernel, d_ff=two_ff // 2),
        grid_spec=grid_spec,
        out_shape=jax.ShapeDtypeStruct((n_slots, d), F32),
        compiler_params=_params(("arbitrary",), VMEM_LIMIT),
        name="moe_experts",
    )(te, rows, n_valid, xs, w_gu, b_gu.reshape(b_gu.shape[0], e, 1, two_ff), w_down,
      b_down.reshape(b_down.shape[0], e, 1, d))


def _combine_kernel(gstart_ref, lstart_ref, nch_ref, ntot_ref, meta_ref, x_ref, gate_ref, y_ref, *refs,
                    split_tiles):
    *o_refs, loc_ref, sem = refs
    blk = pl.program_id(0)

    @pl.when(blk == 0)
    def _():
        loc_ref[...] = jnp.zeros_like(loc_ref)

    def copy(l0, g0, rows):
        return pltpu.make_async_copy(y_ref.at[pl.ds(g0, rows), :], loc_ref.at[pl.ds(l0, rows), :], sem)

    _start_runs(blk, gstart_ref, lstart_ref, nch_ref, copy)
    _wait_runs(ntot_ref[blk], copy)

    meta = meta_ref[...]
    tm = meta.shape[0]
    cols = 256
    acc = jnp.zeros(x_ref.shape, F32)
    for c0 in range(0, LOCAL_SLOTS, cols):
        slot = (c0 + lax.broadcasted_iota(jnp.int32, (tm, cols), 1)).astype(F32)
        wts = sum(jnp.where(slot == meta[:, k:k + 1], meta[:, TOP_K + k:TOP_K + k + 1], 0.0)
                  for k in range(TOP_K))
        acc = acc + jnp.dot(wts.astype(BF16), loc_ref[c0:c0 + cols, :].astype(BF16),
                            preferred_element_type=F32)
    out = x_ref[...] + gate_ref[...] * acc
    if split_tiles is None:
        o_refs[0][...] = out
    else:
        @pl.when(blk < split_tiles)
        def _():
            o_refs[0][...] = out

        @pl.when(blk >= split_tiles)
        def _():
            o_refs[1][...] = out


def _combine(y, meta, runs, x, gate, n_p, t_s, split=False):
    n, d = x.shape
    tm = ROUTE_BLOCK
    group = _group_of_tile(tm, n_p, t_s)
    st = n_p // tm
    if split:
        out_specs = [pl.BlockSpec((tm, d), lambda i, *_: (jnp.minimum(i, st - 1), 0)),
                     pl.BlockSpec((tm, d), lambda i, *_: (jnp.maximum(i - st, 0), 0))]
        out_shape = [jax.ShapeDtypeStruct((n_p, d), F32), jax.ShapeDtypeStruct((n - n_p, d), F32)]
    else:
        out_specs = pl.BlockSpec((tm, d), lambda i, *_: (i, 0))
        out_shape = jax.ShapeDtypeStruct((n, d), F32)
    grid_spec = pltpu.PrefetchScalarGridSpec(
        num_scalar_prefetch=4,
        grid=(n // tm,),
        in_specs=[pl.BlockSpec((tm, META_LANES), lambda i, *_: (i, 0)),
                  pl.BlockSpec((tm, d), lambda i, *_: (i, 0)),
                  pl.BlockSpec((None, 1, d), lambda i, *_: (group(i), 0, 0)),
                  pl.BlockSpec(memory_space=pl.ANY)],
        out_specs=out_specs,
        scratch_shapes=[pltpu.VMEM((LOCAL_SLOTS, d), F32), pltpu.SemaphoreType.DMA(())])
    return pl.pallas_call(
        functools.partial(_combine_kernel, split_tiles=st if split else None),
        grid_spec=grid_spec,
        out_shape=out_shape,
        compiler_params=_params(("arbitrary",), VMEM_LIMIT),
        name="moe_combine",
    )(*runs, meta, x, gate.reshape(N_GROUPS, 1, d), y)


def _moe(x, gain, scale, shift, gate, w_router, b_router, w_gu, b_gu, w_down, b_down, layer, n_p, t_s,
         split=False):
    n = x.shape[0]
    assert LOCAL_SLOTS >= ROUTE_BLOCK * TOP_K + N_EXPERTS * (SLOT_ALIGN - 1) and n % ROUTE_BLOCK == 0
    min_tiles = n * TOP_K // ROUTE_TILE
    max_slots = n * TOP_K + (n // ROUTE_BLOCK) * N_EXPERTS * (SLOT_ALIGN - 1) + N_EXPERTS * (ROUTE_TILE - 1)
    n_tiles = -(-max_slots // ROUTE_TILE)
    h, meta, cnt_blk, cnt_tot = _router(x, gain, scale, shift, w_router, b_router, layer, n_p, t_s)
    runs, te, rows, n_valid, fill = _route_plan(cnt_blk, cnt_tot, n_tiles, min_tiles)
    xs = _dispatch(h, meta, runs, fill, n_tiles * ROUTE_TILE)
    y = _experts(xs, te, rows, n_valid, w_gu, b_gu, w_down, b_down, layer)
    return _combine(y, meta, runs, x, gate, n_p, t_s, split)


def kernel(x_prompt, x_sample, c, cache_a_k, cache_a_v, state_b_fwd, state_b_bwd, cache_c_ckv,
           cache_c_krope, c_ctx, w_mod, b_mod, norm_mix, norm_ffn, e_w_in, e_w_out, e_a_qnorm,
           e_a_knorm, e_a_sink, e_b_conv, e_b_alog, e_b_dtbias, e_b_onorm, o_w_in, o_q_lora_norm,
           o_kv_lora_norm, o_w_uq, o_w_ukv, o_qnorm, o_knorm, o_w_out, moe_w_router, moe_b_router,
           moe_w_gu, moe_b_gu, moe_w_down, moe_b_down):
    bp, tp, d = x_prompt.shape
    bs, ts, _ = x_sample.shape
    depth = w_mod.shape[0]
    n_p, n_s = bp * tp, bs * ts
    n = n_p + n_s
    assert bs + 1 <= N_GROUPS and ts % 512 == 0 and n_p % ts == 0

    x = (x_prompt.reshape(n_p, d), x_sample.reshape(n_s, d))
    cond = jnp.concatenate([c_ctx[None], c, jnp.zeros((N_GROUPS - 1 - bs, d), F32)], axis=0)
    mod = _adaln(cond, w_mod, b_mod)

    new_a_k, new_a_v, new_b_fwd, new_b_bwd, new_c_ckv, new_c_krope = [], [], [], [], [], []
    for layer in range(depth):
        sh1, sc1, g1, sh2, sc2, g2 = (mod[layer, j] for j in range(6))
        h = _modulate(x, norm_mix[layer], sc1, sh1, n_p, ts)
        i = layer // 2
        if layer % 2 == 0:
            main_w = (A_HEADS + 2 * A_KV_HEADS) * A_HEAD_DIM + 4 * B_HEADS * B_DK
            proj = _matmul((h,), e_w_in, i, 0, main_w // 2, main_w, name="even_in_proj")
            tail = _matmul((h,), e_w_in[i][None, :, main_w:], 0, 0, 4 * B_HEADS, 4 * B_HEADS,
                           name="even_gate_proj")
            oa_p, kn_p = _attn_a_prompt(proj, bp, tp, e_a_sink[i], e_a_qnorm[i], e_a_knorm[i])
            oa_s = _attn_a_sample(proj, n_p, bs, ts, cache_a_k[:, i], cache_a_v[:, i],
                                  e_a_sink[i], e_a_qnorm[i], e_a_knorm[i])
            zeros = jnp.zeros((bp, B_HEADS, B_DK, B_DV), F32)
            ob_p, s_f, s_b = _deltanet(proj, tail, 0, bp, tp, e_b_conv[i], e_b_alog[i],
                                       e_b_dtbias[i], e_b_onorm[i], zeros, zeros)
            ob_s, _, _ = _deltanet(proj, tail, n_p, bs, ts, e_b_conv[i], e_b_alog[i],
                                   e_b_dtbias[i], e_b_onorm[i], state_b_fwd[:, i], state_b_bwd[:, i])
            x = (_matmul_residual([(oa_p, oa_s), (ob_p, ob_s)], e_w_out, i, x, g1, n_p, ts),)
            kw = A_KV_HEADS * A_HEAD_DIM
            new_a_k.append(kn_p.reshape(bp, tp, A_KV_HEADS, A_HEAD_DIM).transpose(0, 2, 1, 3))
            v_p = proj[:n_p, A_HEADS * A_HEAD_DIM + kw:A_HEADS * A_HEAD_DIM + 2 * kw]
            new_a_v.append(v_p.reshape(bp, tp, A_KV_HEADS, A_HEAD_DIM).transpose(0, 2, 1, 3))
            new_b_fwd.append(s_f)
            new_b_bwd.append(s_b)
        else:
            p1 = _matmul((h,), o_w_in, i, 0, o_w_in.shape[-1], o_w_in.shape[-1], name="odd_in_proj")
            cq, ckv, ckv_b = _lora_norm(p1, o_q_lora_norm[i], o_kv_lora_norm[i])
            q = _matmul((cq,), o_w_uq, i, 0, o_w_uq.shape[-1] // 2, o_w_uq.shape[-1], name="odd_uq")
            ckv_ctx = cache_c_ckv[:, i].reshape(-1, C_KV_LORA).astype(BF16)
            kv = _matmul((ckv_b, ckv_ctx), o_w_ukv, i, 0, o_w_ukv.shape[-1] // 2, o_w_ukv.shape[-1],
                         name="odd_ukv")
            o_p = _attn_c_prompt(q, kv, p1, bp, tp, o_qnorm[i], o_knorm[i])
            o_s = _attn_c_sample(q, kv, p1, n_p, bs, ts, cache_c_krope[:, i], o_qnorm[i], o_knorm[i])
            x = (_matmul_residual([(o_p, o_s)], o_w_out, i, x, g1, n_p, ts),)
            new_c_ckv.append(ckv[:n_p].reshape(bp, tp, C_KV_LORA))
            new_c_krope.append(p1[:n_p, C_Q_LORA + C_KV_LORA:].reshape(bp, tp, C_ROPE))
        x = _moe(x[0], norm_ffn[layer], sc2, sh2, g2, moe_w_router, moe_b_router, moe_w_gu, moe_b_gu,
                 moe_w_down, moe_b_down, layer, n_p, ts, split=layer == depth - 1)
        x = tuple(x) if layer == depth - 1 else (x,)

    return (x[0].reshape(bp, tp, d), x[1].reshape(bs, ts, d),
            jnp.stack(new_a_k, axis=1), jnp.stack(new_a_v, axis=1),
            jnp.stack(new_b_fwd, axis=1), jnp.stack(new_b_bwd, axis=1),
            jnp.stack(new_c_ckv, axis=1), jnp.stack(new_c_krope, axis=1))
```

```python
import functools
import math

import numpy as np
import jax
import jax.numpy as jnp
from jax import lax
from jax.experimental import pallas as pl
from jax.experimental.pallas import tpu as pltpu

F32 = jnp.float32
BF16 = jnp.bfloat16
HIGHEST = lax.Precision.HIGHEST

EPS = 1e-6
NEG_INF = -1e30
ROPE_BASE = 10000.0
GRID_W = 64
N_GROUPS = 8

A_HEADS, A_KV_HEADS, A_GROUP, A_HEAD_DIM, A_WINDOW, A_BLOCK = 8, 2, 4, 64, 128, 128
B_HEADS, B_DK, B_DV, B_CHUNK = 8, 64, 64, 64
C_HEADS, C_NOPE, C_ROPE, C_V, C_Q_LORA, C_KV_LORA = 16, 64, 32, 64, 384, 256
C_QK = C_NOPE + C_ROPE
N_EXPERTS, TOP_K = 32, 4
SWIGLU_LIMIT, SWIGLU_ALPHA = 7.0, 1.702

VMEM_LIMIT = 56 * 1024 * 1024


def _params(sem, vmem=None):
    return pltpu.CompilerParams(dimension_semantics=sem, vmem_limit_bytes=vmem)


def _bdot(a, b):
    return jnp.dot(a.astype(BF16), b.astype(BF16), preferred_element_type=F32)


def _bdot_nt(a, b):
    return lax.dot_general(a.astype(BF16), b.astype(BF16), (((1,), (1,)), ((), ())),
                           preferred_element_type=F32)


def _bdot_tn(a, b):
    return lax.dot_general(a.astype(BF16), b.astype(BF16), (((0,), (0,)), ((), ())),
                           preferred_element_type=F32)


def _hdot(a, b):
    return jnp.dot(a, b, preferred_element_type=F32, precision=HIGHEST)


def _hdot_nt(a, b):
    return lax.dot_general(a, b, (((1,), (1,)), ((), ())), preferred_element_type=F32,
                           precision=HIGHEST)


def _rms(x, gain):
    return x * lax.rsqrt(jnp.mean(x * x, axis=-1, keepdims=True) + EPS) * gain


def _silu(x):
    return x * jax.nn.sigmoid(x)


def _group_of_tile(tm, n_p, t_s):
    def group(i):
        r = i * tm
        return jnp.where(r < n_p, 0, 1 + (r - n_p) // t_s)
    return group


def _adaln_kernel(cond_ref, w_ref, b_ref, o_ref):
    o_ref[...] = _bdot(_silu(cond_ref[...]), w_ref[...]) + b_ref[...]


def _adaln(cond, w_mod, b_mod):
    depth, d, _ = w_mod.shape
    return pl.pallas_call(
        _adaln_kernel,
        grid=(depth, 6),
        in_specs=[pl.BlockSpec((N_GROUPS, d), lambda l, j: (0, 0)),
                  pl.BlockSpec((None, d, d), lambda l, j: (l, 0, j)),
                  pl.BlockSpec((None, 1, d), lambda l, j: (l, 0, j))],
        out_specs=pl.BlockSpec((None, None, N_GROUPS, d), lambda l, j: (l, j, 0, 0)),
        out_shape=jax.ShapeDtypeStruct((depth, 6, N_GROUPS, d), F32),
        compiler_params=_params(("arbitrary", "arbitrary")),
        name="adaln",
    )(cond, w_mod, b_mod.reshape(depth, 1, 6 * d))


def _row_specs(arrays, tm, pos):
    specs, ends, start = [], [], 0
    for arr in arrays:
        nt = arr.shape[0] // tm
        specs.append(pl.BlockSpec((tm, arr.shape[1]),
                                  lambda *ids, s=start, nt=nt: (jnp.clip(ids[pos] - s, 0, nt - 1), 0)))
        start += nt
        ends.append(start)
    return specs, tuple(ends)


def _pick_rows(refs, ends, i):
    x = refs[-1][...]
    for ref, end in zip(reversed(refs[:-1]), reversed(ends[:-1])):
        x = jnp.where(i < end, ref[...], x)
    return x


def _modulate_kernel(*refs, ends):
    n = len(ends)
    g_ref, sc_ref, sh_ref, o_ref = refs[n:]
    y = _rms(_pick_rows(refs[:n], ends, pl.program_id(0)), g_ref[...])
    o_ref[...] = (y * (1 + sc_ref[...]) + sh_ref[...]).astype(o_ref.dtype)


def _modulate(xs, gain, scale, shift, n_p, t_s, tm=512):
    n, d = sum(x.shape[0] for x in xs), xs[0].shape[1]
    group = _group_of_tile(tm, n_p, t_s)
    specs, ends = _row_specs(xs, tm, 0)
    return pl.pallas_call(
        functools.partial(_modulate_kernel, ends=ends),
        grid=(n // tm,),
        in_specs=specs + [pl.BlockSpec((1, d), lambda i: (0, 0)),
                          pl.BlockSpec((None, 1, d), lambda i: (group(i), 0, 0)),
                          pl.BlockSpec((None, 1, d), lambda i: (group(i), 0, 0))],
        out_specs=pl.BlockSpec((tm, d), lambda i: (i, 0)),
        out_shape=jax.ShapeDtypeStruct((n, d), BF16),
        compiler_params=_params(("arbitrary",)),
        name="modulate",
    )(*xs, gain.reshape(1, d), scale.reshape(N_GROUPS, 1, d), shift.reshape(N_GROUPS, 1, d))


def _mm_kernel(*refs, ends):
    n = len(ends)
    w_ref, o_ref, wb_ref = refs[n:]

    @pl.when(pl.program_id(1) == 0)
    def _():
        wb_ref[...] = w_ref[...].astype(BF16)
    x = _pick_rows(refs[:n], ends, pl.program_id(1))
    o_ref[...] = jnp.dot(x, wb_ref[...], preferred_element_type=F32).astype(o_ref.dtype)


def _matmul(xs, w3, layer, col0_blocks, tn, n_out, out_dtype=F32, tm=512, name="matmul"):
    n, k = sum(x.shape[0] for x in xs), xs[0].shape[1]
    specs, ends = _row_specs(xs, tm, 1)
    return pl.pallas_call(
        functools.partial(_mm_kernel, ends=ends),
        grid=(n_out // tn, n // tm),
        in_specs=specs + [pl.BlockSpec((None, k, tn), lambda j, i: (layer, 0, col0_blocks + j))],
        out_specs=pl.BlockSpec((tm, tn), lambda j, i: (i, j)),
        out_shape=jax.ShapeDtypeStruct((n, n_out), out_dtype),
        scratch_shapes=[pltpu.VMEM((k, tn), BF16)],
        compiler_params=_params(("arbitrary", "arbitrary"), VMEM_LIMIT),
        name=name,
    )(*xs, w3)


def _mm_res_kernel(*refs, col_ends, res_ends):
    i = pl.program_id(0)
    pos = 0
    cols = []
    for ends in col_ends:
        cols.append(refs[pos:pos + len(ends)])
        pos += len(ends)
    res_refs = refs[pos:pos + len(res_ends)]
    w_ref, gate_ref, o_ref, wb_ref = refs[pos + len(res_ends):]

    @pl.when(i == 0)
    def _():
        wb_ref[...] = w_ref[...].astype(BF16)
    y, k0 = None, 0
    for col_refs, ends in zip(cols, col_ends):
        x = _pick_rows(col_refs, ends, i)
        part = jnp.dot(x, wb_ref[k0:k0 + x.shape[1], :], preferred_element_type=F32)
        y = part if y is None else y + part
        k0 += x.shape[1]
    o_ref[...] = _pick_rows(res_refs, res_ends, i) + gate_ref[...] * y


def _matmul_residual(mix_cols, w3, layer, res, gate, n_p, t_s, tm=512):
    n, d = sum(r.shape[0] for r in res), res[0].shape[1]
    k = sum(col[0].shape[1] for col in mix_cols)
    group = _group_of_tile(tm, n_p, t_s)
    specs, col_ends = [], []
    for col in mix_cols:
        s, e = _row_specs(col, tm, 0)
        specs += s
        col_ends.append(e)
    res_specs, res_ends = _row_specs(res, tm, 0)
    return pl.pallas_call(
        functools.partial(_mm_res_kernel, col_ends=tuple(col_ends), res_ends=res_ends),
        grid=(n // tm,),
        in_specs=specs + res_specs + [pl.BlockSpec((None, k, d), lambda i: (layer, 0, 0)),
                                      pl.BlockSpec((None, 1, d), lambda i: (group(i), 0, 0))],
        out_specs=pl.BlockSpec((tm, d), lambda i: (i, 0)),
        out_shape=jax.ShapeDtypeStruct((n, d), F32),
        scratch_shapes=[pltpu.VMEM((k, d), BF16)],
        compiler_params=_params(("arbitrary",), VMEM_LIMIT),
        name="out_proj_residual",
    )(*[piece for col in mix_cols for piece in col], *res, w3, gate.reshape(N_GROUPS, 1, d))


def _rope_tables(t_len, d):
    half, quarter = d // 2, d // 4
    pos = np.arange(t_len)
    row, col = pos // GRID_W, pos % GRID_W
    inv = ROPE_BASE ** (-np.arange(quarter, dtype=np.float64) / quarter)
    ang_r = row[:, None] * inv[None, :]
    ang_c = col[:, None] * inv[None, :]
    cos = np.concatenate([np.cos(ang_r), np.cos(ang_r), np.cos(ang_c), np.cos(ang_c)], axis=1)
    sin = np.concatenate([-np.sin(ang_r), np.sin(ang_r), -np.sin(ang_c), np.sin(ang_c)], axis=1)
    return jnp.asarray(cos, F32), jnp.asarray(sin, F32)


def _swap_pairs(x):
    q = x.shape[-1] // 4
    return jnp.concatenate([x[:, q:2 * q], x[:, :q], x[:, 3 * q:], x[:, 2 * q:3 * q]], axis=-1)


def _rope(x, cos, sin):
    return x * cos + _swap_pairs(x) * sin


def _softmax_attend(parts, sink):
    m = parts[0][0].max(axis=-1, keepdims=True)
    for s, _ in parts[1:]:
        m = jnp.maximum(m, s.max(axis=-1, keepdims=True))
    if sink is not None:
        m = jnp.maximum(m, sink)
    den = jnp.exp(sink - m) if sink is not None else 0.0
    acc = None
    for s, v in parts:
        p = jnp.exp(s - m)
        den = den + p.sum(axis=-1, keepdims=True)
        o = _bdot(p, v)
        acc = o if acc is None else acc + o
    return acc / den


def _group_sinks(sink_ref, hk, rows):
    head = lax.broadcasted_iota(jnp.int32, (A_GROUP * rows, 1), 0) // rows
    col = jnp.full((A_GROUP * rows, 1), sink_ref[hk * A_GROUP], F32)
    for g in range(1, A_GROUP):
        col = jnp.where(head == g, sink_ref[hk * A_GROUP + g], col)
    return col


def _attn_a_prompt_kernel(sink_ref, q_ref, kv_ref, gq_ref, gk_ref, o_ref, kn_ref):
    scale = A_HEAD_DIM ** -0.5
    q = q_ref[...]
    kv = kv_ref[...]
    t = q.shape[0]
    outs = []
    kns = []
    for hk in range(A_KV_HEADS):
        k = _rms(kv[:, hk * A_HEAD_DIM:(hk + 1) * A_HEAD_DIM], gk_ref[...])
        v = kv[:, (A_KV_HEADS + hk) * A_HEAD_DIM:(A_KV_HEADS + hk + 1) * A_HEAD_DIM]
        kns.append(k)
        qg = jnp.concatenate([_rms(q[:, h * A_HEAD_DIM:(h + 1) * A_HEAD_DIM], gq_ref[...])
                              for h in range(hk * A_GROUP, (hk + 1) * A_GROUP)], axis=0)
        s = _bdot_nt(qg, k) * scale
        o = _softmax_attend([(s, v)], _group_sinks(sink_ref, hk, t))
        outs += [o[g * t:(g + 1) * t] for g in range(A_GROUP)]
    o_ref[...] = jnp.concatenate(outs, axis=-1).astype(o_ref.dtype)
    kn_ref[...] = jnp.concatenate(kns, axis=-1)


def _attn_a_prompt(proj, n_seq, t, sink, gq, gk):
    qw = A_HEADS * A_HEAD_DIM
    kvw = 2 * A_KV_HEADS * A_HEAD_DIM
    return pl.pallas_call(
        _attn_a_prompt_kernel,
        grid=(n_seq,),
        in_specs=[pl.BlockSpec(memory_space=pltpu.SMEM),
                  pl.BlockSpec((t, qw), lambda b: (b, 0)),
                  pl.BlockSpec((t, kvw), lambda b: (b, qw // kvw)),
                  pl.BlockSpec((1, A_HEAD_DIM), lambda b: (0, 0)),
                  pl.BlockSpec((1, A_HEAD_DIM), lambda b: (0, 0))],
        out_specs=[pl.BlockSpec((t, qw), lambda b: (b, 0)),
                   pl.BlockSpec((t, A_KV_HEADS * A_HEAD_DIM), lambda b: (b, 0))],
        out_shape=[jax.ShapeDtypeStruct((n_seq * t, qw), BF16),
                   jax.ShapeDtypeStruct((n_seq * t, A_KV_HEADS * A_HEAD_DIM), F32)],
        compiler_params=_params(("arbitrary",)),
        name="attn_a_context",
    )(sink, proj, proj, gq.reshape(1, -1), gk.reshape(1, -1))


def _attn_a_sample_kernel(sink_ref, q_ref, kv_ref, kc_ref, vc_ref, gq_ref, gk_ref, cos_ref, sin_ref,
                          o_ref, *, t):
    scale = A_HEAD_DIM ** -0.5
    i = pl.program_id(1)
    win = 3 * A_BLOCK
    q0 = pl.multiple_of(i * A_BLOCK, A_BLOCK)
    k0 = pl.multiple_of(jnp.clip((i - 1) * A_BLOCK, 0, t - win), A_BLOCK)
    q = q_ref[...]
    kv = kv_ref[pl.ds(k0, win), :]
    cq, sq = cos_ref[pl.ds(q0, A_BLOCK), :], sin_ref[pl.ds(q0, A_BLOCK), :]
    ck, sk = cos_ref[pl.ds(k0, win), :], sin_ref[pl.ds(k0, win), :]
    rows = A_GROUP * A_BLOCK
    past = kc_ref.shape[1]
    col = lax.broadcasted_iota(jnp.int32, (rows, win + past), 1)
    qpos = q0 + lax.broadcasted_iota(jnp.int32, (rows, win + past), 0) % A_BLOCK
    mask = jnp.logical_or(col >= win, jnp.abs(qpos - (k0 + col)) <= A_WINDOW)
    outs = []
    for hk in range(A_KV_HEADS):
        k = _rope(_rms(kv[:, hk * A_HEAD_DIM:(hk + 1) * A_HEAD_DIM], gk_ref[...]), ck, sk)
        v = kv[:, (A_KV_HEADS + hk) * A_HEAD_DIM:(A_KV_HEADS + hk + 1) * A_HEAD_DIM]
        qg = jnp.concatenate([_rope(_rms(q[:, h * A_HEAD_DIM:(h + 1) * A_HEAD_DIM], gq_ref[...]), cq, sq)
                              for h in range(hk * A_GROUP, (hk + 1) * A_GROUP)], axis=0)
        s = _bdot_nt(qg, jnp.concatenate([k, kc_ref[hk]], axis=0)) * scale
        s = jnp.where(mask, s, NEG_INF)
        o = _softmax_attend([(s, jnp.concatenate([v, vc_ref[hk]], axis=0))],
                            _group_sinks(sink_ref, hk, A_BLOCK))
        outs += [o[g * A_BLOCK:(g + 1) * A_BLOCK] for g in range(A_GROUP)]
    o_ref[...] = jnp.concatenate(outs, axis=-1).astype(o_ref.dtype)


def _attn_a_sample(proj, row0, n_seq, t, k_ctx, v_ctx, sink, gq, gk):
    qw = A_HEADS * A_HEAD_DIM
    kvw = 2 * A_KV_HEADS * A_HEAD_DIM
    nqb = t // A_BLOCK
    cos, sin = _rope_tables(t, A_HEAD_DIM)
    past = k_ctx.shape[2]
    return pl.pallas_call(
        functools.partial(_attn_a_sample_kernel, t=t),
        grid=(n_seq, nqb),
        in_specs=[pl.BlockSpec(memory_space=pltpu.SMEM),
                  pl.BlockSpec((A_BLOCK, qw), lambda b, i: (row0 // A_BLOCK + b * nqb + i, 0)),
                  pl.BlockSpec((t, kvw), lambda b, i: (row0 // t + b, qw // kvw)),
                  pl.BlockSpec((None, A_KV_HEADS, past, A_HEAD_DIM), lambda b, i: (b, 0, 0, 0)),
                  pl.BlockSpec((None, A_KV_HEADS, past, A_HEAD_DIM), lambda b, i: (b, 0, 0, 0)),
                  pl.BlockSpec((1, A_HEAD_DIM), lambda b, i: (0, 0)),
                  pl.BlockSpec((1, A_HEAD_DIM), lambda b, i: (0, 0)),
                  pl.BlockSpec((t, A_HEAD_DIM), lambda b, i: (0, 0)),
                  pl.BlockSpec((t, A_HEAD_DIM), lambda b, i: (0, 0))],
        out_specs=pl.BlockSpec((A_BLOCK, qw), lambda b, i: (b * nqb + i, 0)),
        out_shape=jax.ShapeDtypeStruct((n_seq * t, qw), BF16),
        compiler_params=_params(("arbitrary", "arbitrary")),
        name="attn_a_latent",
    )(sink, proj, proj, k_ctx, v_ctx, gq.reshape(1, -1), gk.reshape(1, -1), cos, sin)


def _per_head_lanes(x, fn):
    lane = lax.broadcasted_iota(jnp.int32, x.shape, 1)
    lo = fn(x[:, :B_DK])
    hi = fn(x[:, B_DK:])
    return jnp.where(lane < B_DK, lo, hi)


def _conv_silu(x, w):
    t = x.shape[0]
    row = lax.broadcasted_iota(jnp.int32, x.shape, 0)
    prev = jnp.where(row == 0, 0.0, pltpu.roll(x, 1, 0))
    nxt = jnp.where(row == t - 1, 0.0, pltpu.roll(x, t - 1, 0))
    return _silu(prev * w[0:1, :] + x * w[1:2, :] + nxt * w[2:3, :])


M_SAME, M_TRI, M_TRI_T, M_STRICT, M_BDIAG = range(5)


def _delta_masks(r):
    c = B_CHUNK
    ii = lax.broadcasted_iota(jnp.int32, (r, r), 0)
    jj = lax.broadcasted_iota(jnp.int32, (r, r), 1)
    same = (ii // c) == (jj // c)
    ahead = jnp.where(ii < r // 2, ii - jj, jj - ii)
    tri = jnp.where(same, ahead, -1) >= 0
    tri_t = jnp.where(same, ahead, 1) <= 0
    strict = jnp.where(same, ahead, -1) > 0
    bdiag = (ii // 16) == (jj // 16)
    return [x.astype(F32) for x in (same, tri, tri_t, strict, bdiag)]


def _delta_prepare(q, k, v, g_col, g_row, beta, mask_ref):
    c = B_CHUNK
    r = q.shape[0]
    dot = functools.partial(jnp.dot, preferred_element_type=F32)
    gc_col = jnp.sum(mask_ref[M_TRI] * g_row, axis=1, keepdims=True)
    gc_row = jnp.sum(mask_ref[M_TRI_T] * g_col, axis=0, keepdims=True)
    g_tot = jnp.sum(mask_ref[M_SAME] * g_row, axis=1, keepdims=True)
    ex = jnp.exp((gc_col - gc_row) * mask_ref[M_TRI])
    kb = k * beta
    qk = _bdot_nt(jnp.concatenate([kb, q], axis=0), k)
    m = qk[:r] * (ex * mask_ref[M_STRICT])
    aqk = qk[r:] * (ex * mask_ref[M_TRI])
    dg = m * mask_ref[M_BDIAG]
    off = m - dg
    n1 = -dg
    n1b = n1.astype(BF16)
    n2 = dot(n1b, n1b)
    n2b = n2.astype(BF16)
    t = dot(jnp.concatenate([n1b, n2b], axis=0), n2b)
    xs = n1 + n2 + t[:r]
    n4 = t[r:]
    n4b = n4.astype(BF16)
    t = dot(jnp.concatenate([xs.astype(BF16), n4b], axis=0), n4b)
    xs = xs + n4 + t[:r]
    n8 = t[r:]
    xs = xs + n8 + dot(xs.astype(BF16), n8.astype(BF16))
    xsb = xs.astype(BF16)
    f = -(off + dot(xsb, off.astype(BF16)))
    fb = f.astype(BF16)
    t = dot(fb, jnp.concatenate([xsb, fb], axis=1))
    ys = xs + f + t[:, :r]
    f2 = t[:, r:]
    ts = ys + f2 + dot(f2.astype(BF16), ys.astype(BF16))
    egc = jnp.exp(gc_col)
    rhs = jnp.concatenate([v * beta, kb * egc], axis=-1)
    uw = rhs + _bdot(ts, rhs)

    def block_diag(x):
        return jnp.concatenate([x] * (r // c), axis=-1) * mask_ref[M_SAME]

    kd = k * jnp.exp(g_tot - gc_col)
    kd_t = jnp.concatenate([kd, jnp.zeros_like(kd)], axis=-1).T[:c]
    kd_t = jnp.concatenate([kd_t] * (r // c), axis=0) * mask_ref[M_SAME]
    e_tot = jnp.broadcast_to(jnp.exp(g_tot), v.shape)
    return uw[:, :B_DV], e_tot, block_diag(uw[:, B_DV:]), aqk, block_diag(q * egc), kd_t


def _deltanet_kernel(alog_ref, dtb_ref, q_ref, k_ref, v_ref, z_ref, cwq_ref, cwk_ref, cwv_ref,
                     tail_ref, tailt_ref, onorm_ref, s0f_ref, s0b_ref,
                     o_ref, sf_ref, sb_ref,
                     qc_ref, kc_ref, vc_ref, oacc_ref, u_ref, et_ref, w_ref, aqk_ref, qg_ref, kdt_ref,
                     mask_ref, *, n_chunks):
    hp = pl.program_id(1)
    c = B_CHUNK

    @pl.when(jnp.logical_and(pl.program_id(0) == 0, hp == 0))
    def _():
        for i, x in enumerate(_delta_masks(4 * c)):
            mask_ref[i] = x

    def l2n(x):
        ss = _per_head_lanes(x * x, lambda a: jnp.sum(a, axis=-1, keepdims=True))
        return x * lax.rsqrt(ss + EPS)

    qc_ref[...] = l2n(_conv_silu(q_ref[...], cwq_ref[...])) * (B_DK ** -0.5)
    kc_ref[...] = l2n(_conv_silu(k_ref[...], cwk_ref[...]))
    vc_ref[...] = _conv_silu(v_ref[...], cwv_ref[...])
    oacc_ref[...] = jnp.zeros_like(oacc_ref)

    lane32 = lax.broadcasted_iota(jnp.int32, (c, 4 * B_HEADS), 1)

    def gates(chunk, d, head):
        tail = tail_ref[pl.ds(pl.multiple_of(chunk * c, c), c), :]
        ia = 2 * d * B_HEADS + head
        ib = ia + B_HEADS
        a_col = jnp.sum(jnp.where(lane32 == ia, tail, 0.0), axis=1, keepdims=True)
        b_col = jnp.sum(jnp.where(lane32 == ib, tail, 0.0), axis=1, keepdims=True)
        a_row = tailt_ref[chunk, pl.ds(ia, 1), :]
        na = -jnp.exp(alog_ref[d, head])
        bias = dtb_ref[d, head]
        g_col = na * jax.nn.softplus(a_col + bias)
        g_row = na * jax.nn.softplus(a_row + bias)
        return g_col, g_row, jax.nn.sigmoid(b_col)

    def stacked(ref, chunks):
        parts = []
        for chunk in chunks:
            x = ref[pl.ds(pl.multiple_of(chunk * c, c), c), :]
            parts += [x[:, :B_DK], x[:, B_DK:]]
        return jnp.concatenate(parts, axis=0)

    def prepare(j, carry):
        chunks = (j, n_chunks - 1 - j)
        gs = [gates(chunks[d], d, 2 * hp + hh) for d in range(2) for hh in range(2)]
        g_col = jnp.concatenate([g[0] for g in gs], axis=0)
        g_row = jnp.concatenate([g[1] for g in gs], axis=1)
        beta = jnp.concatenate([g[2] for g in gs], axis=0)
        outs = _delta_prepare(stacked(qc_ref, chunks), stacked(kc_ref, chunks), stacked(vc_ref, chunks),
                              g_col, g_row, beta, mask_ref)
        for ref, x in zip((u_ref, et_ref, w_ref, aqk_ref, qg_ref, kdt_ref), outs):
            ref[j] = x.astype(ref.dtype)
        return carry

    lax.fori_loop(0, n_chunks, prepare, 0, unroll=4)

    def scan(j, s):
        sb = s.astype(BF16)
        delta = u_ref[j] - jnp.dot(w_ref[j], sb, preferred_element_type=F32)
        db = delta.astype(BF16)
        o = (jnp.dot(qg_ref[j], sb, preferred_element_type=F32)
             + jnp.dot(aqk_ref[j], db, preferred_element_type=F32))
        for d, chunk in enumerate((j, n_chunks - 1 - j)):
            rows = pl.ds(pl.multiple_of(chunk * c, c), c)
            oacc_ref[rows, :] += jnp.concatenate([o[2 * d * c:(2 * d + 1) * c],
                                                  o[(2 * d + 1) * c:(2 * d + 2) * c]], axis=-1)
        return s * et_ref[j] + jnp.dot(kdt_ref[j], db, preferred_element_type=F32)

    init = jnp.concatenate([s0f_ref[0], s0f_ref[1], s0b_ref[0], s0b_ref[1]], axis=0)
    fin = lax.fori_loop(0, n_chunks, scan, init)
    sf_ref[0], sf_ref[1], sb_ref[0], sb_ref[1] = (fin[i * B_DK:(i + 1) * B_DK] for i in range(4))

    o = oacc_ref[...]
    ms = _per_head_lanes(o * o, lambda a: jnp.mean(a, axis=-1, keepdims=True))
    o_ref[...] = (o * lax.rsqrt(ms + EPS) * onorm_ref[...] * _silu(z_ref[...])).astype(o_ref.dtype)


def _deltanet(proj, tail, row0, n_seq, t, conv_w, a_log, dt_bias, o_norm, s0_f, s0_b):
    c = B_CHUNK
    n_chunks = t // c
    lw = 2 * B_DK
    col_q = (A_HEADS + 2 * A_KV_HEADS) * A_HEAD_DIM // lw
    nhp = B_HEADS // 2
    rows = tail[row0:row0 + n_seq * t]
    tail_t = rows.reshape(n_seq, n_chunks, c, 4 * B_HEADS).transpose(0, 1, 3, 2)
    onorm2 = jnp.concatenate([o_norm, o_norm]).reshape(1, lw)
    b0 = row0 // t
    seq_blk = lambda off: pl.BlockSpec((t, lw), lambda b, h: (b0 + b, col_q + off + h))
    cw_blk = lambda off: pl.BlockSpec((3, lw), lambda b, h: (0, off + h))
    st_blk = pl.BlockSpec((None, 2, B_DK, B_DV), lambda b, h: (b, h, 0, 0))
    return pl.pallas_call(
        functools.partial(_deltanet_kernel, n_chunks=n_chunks),
        grid=(n_seq, nhp),
        in_specs=[pl.BlockSpec(memory_space=pltpu.SMEM), pl.BlockSpec(memory_space=pltpu.SMEM),
                  seq_blk(0), seq_blk(nhp), seq_blk(2 * nhp), seq_blk(3 * nhp),
                  cw_blk(0), cw_blk(nhp), cw_blk(2 * nhp),
                  pl.BlockSpec((t, 4 * B_HEADS), lambda b, h: (b, 0)),
                  pl.BlockSpec((None, n_chunks, 4 * B_HEADS, c), lambda b, h: (b, 0, 0, 0)),
                  pl.BlockSpec((1, lw), lambda b, h: (0, 0)),
                  st_blk, st_blk],
        out_specs=[pl.BlockSpec((t, lw), lambda b, h: (b, h)), st_blk, st_blk],
        out_shape=[jax.ShapeDtypeStruct((n_seq * t, B_HEADS * B_DV), BF16),
                   jax.ShapeDtypeStruct((n_seq, B_HEADS, B_DK, B_DV), F32),
                   jax.ShapeDtypeStruct((n_seq, B_HEADS, B_DK, B_DV), F32)],
        scratch_shapes=[pltpu.VMEM((t, lw), F32)] * 4 + [pltpu.VMEM((n_chunks, 4 * c, B_DV), F32)] * 2
        + [pltpu.VMEM((n_chunks, 4 * c, 4 * c), BF16)] * 4 + [pltpu.VMEM((5, 4 * c, 4 * c), F32)],
        compiler_params=_params(("arbitrary", "arbitrary")),
        name="deltanet",
    )(a_log, dt_bias, proj, proj, proj, proj, conv_w, conv_w, conv_w, rows, tail_t, onorm2, s0_f, s0_b)


def _lora_norm_kernel(p_ref, gq_ref, gkv_ref, cq_ref, ckv_ref, ckvb_ref):
    p = p_ref[...]
    cq_ref[...] = _rms(p[:, :C_Q_LORA], gq_ref[...]).astype(cq_ref.dtype)
    ckv = _rms(p[:, C_Q_LORA:C_Q_LORA + C_KV_LORA], gkv_ref[...])
    ckv_ref[...] = ckv
    ckvb_ref[...] = ckv.astype(ckvb_ref.dtype)


def _lora_norm(p1, gq, gkv, tm=512):
    n, w = p1.shape
    return pl.pallas_call(
        _lora_norm_kernel,
        grid=(n // tm,),
        in_specs=[pl.BlockSpec((tm, w), lambda i: (i, 0)),
                  pl.BlockSpec((1, C_Q_LORA), lambda i: (0, 0)),
                  pl.BlockSpec((1, C_KV_LORA), lambda i: (0, 0))],
        out_specs=[pl.BlockSpec((tm, C_Q_LORA), lambda i: (i, 0)),
                   pl.BlockSpec((tm, C_KV_LORA), lambda i: (i, 0)),
                   pl.BlockSpec((tm, C_KV_LORA), lambda i: (i, 0))],
        out_shape=[jax.ShapeDtypeStruct((n, C_Q_LORA), BF16),
                   jax.ShapeDtypeStruct((n, C_KV_LORA), F32),
                   jax.ShapeDtypeStruct((n, C_KV_LORA), BF16)],
        compiler_params=_params(("arbitrary",)),
        name="lora_norm",
    )(p1, gq.reshape(1, -1), gkv.reshape(1, -1))


def _mla_head_k(kv, kr, kr_ss, h, gk):
    kn = kv[:, h * (C_NOPE + C_V):h * (C_NOPE + C_V) + C_NOPE]
    v = kv[:, h * (C_NOPE + C_V) + C_NOPE:(h + 1) * (C_NOPE + C_V)]
    rn = lax.rsqrt((jnp.sum(kn * kn, axis=-1, keepdims=True) + kr_ss) / C_QK + EPS)
    return kn * rn * gk[:, :C_NOPE], kr * rn, v


C_STACK = 4


def _place(x, i, n):
    t, w = x.shape
    parts = ([jnp.zeros((t, i * w), x.dtype)] if i else []) + [x]
    if i < n - 1:
        parts.append(jnp.zeros((t, (n - 1 - i) * w), x.dtype))
    return jnp.concatenate(parts, axis=-1)


def _by_block(cols, lane, w):
    out = cols[-1]
    for i in reversed(range(len(cols) - 1)):
        out = jnp.where(lane < (i + 1) * w, cols[i], out)
    return out


def _attn_c_prompt_kernel(q_ref, kv_ref, p_ref, gq_ref, gk_ref, o_ref):
    scale = C_QK ** -0.5
    n = C_STACK
    q = q_ref[...]
    kv = kv_ref[...]
    gk = gk_ref[...]
    t = q.shape[0]
    kr_raw = p_ref[...][:, C_Q_LORA + C_KV_LORA:]
    kr_ss = jnp.sum(kr_raw * kr_raw, axis=-1, keepdims=True)
    kr_g = kr_raw * gk[:, C_NOPE:]
    lane_q = lax.broadcasted_iota(jnp.int32, (t, n * C_QK), 1)
    lane_o = lax.broadcasted_iota(jnp.int32, (t, n * C_V), 1)
    outs = []
    for grp in range(C_HEADS // n):
        qs = q[:, grp * n * C_QK:(grp + 1) * n * C_QK]
        rn = [lax.rsqrt(jnp.mean(qs[:, i * C_QK:(i + 1) * C_QK] ** 2, axis=-1, keepdims=True) + EPS)
              for i in range(n)]
        qs = qs * _by_block(rn, lane_q, C_QK) * gq_ref[...]
        k_rows, v_rows = [], []
        for i in range(n):
            kn, kr, v = _mla_head_k(kv, kr_g, kr_ss, grp * n + i, gk)
            k_rows.append(_place(jnp.concatenate([kn, kr], axis=-1), i, n))
            v_rows.append(_place(v, i, n))
        s = _bdot_nt(qs, jnp.concatenate(k_rows, axis=0)) * scale
        ps, rden = [], []
        for i in range(n):
            si = s[:, i * t:(i + 1) * t]
            pi = jnp.exp(si - si.max(axis=-1, keepdims=True))
            ps.append(pi)
            rden.append(1.0 / pi.sum(axis=-1, keepdims=True))
        o = _bdot(jnp.concatenate(ps, axis=-1), jnp.concatenate(v_rows, axis=0))
        outs.append(o * _by_block(rden, lane_o, C_V))
    o_ref[...] = jnp.concatenate(outs, axis=-1).astype(o_ref.dtype)


def _attn_c_prompt(q, kv, p1, n_seq, t, gq, gk):
    return pl.pallas_call(
        _attn_c_prompt_kernel,
        grid=(n_seq,),
        in_specs=[pl.BlockSpec((t, q.shape[1]), lambda b: (b, 0)),
                  pl.BlockSpec((t, kv.shape[1]), lambda b: (b, 0)),
                  pl.BlockSpec((t, p1.shape[1]), lambda b: (b, 0)),
                  pl.BlockSpec((1, C_STACK * C_QK), lambda b: (0, 0)),
                  pl.BlockSpec((1, C_QK), lambda b: (0, 0))],
        out_specs=pl.BlockSpec((t, C_HEADS * C_V), lambda b: (b, 0)),
        out_shape=jax.ShapeDtypeStruct((n_seq * t, C_HEADS * C_V), BF16),
        compiler_params=_params(("arbitrary",), VMEM_LIMIT),
        name="attn_c_context",
    )(q, kv, p1, jnp.tile(gq, C_STACK).reshape(1, -1), gk.reshape(1, -1))


def _attn_c_sample_kernel(q_ref, kv_ref, p_ref, kvc_ref, krc_ref, gq_ref, gk_ref, cos_ref, sin_ref,
                          o_ref, *, tq):
    scale = C_QK ** -0.5
    i = pl.program_id(1)
    q0 = pl.multiple_of(i * tq, tq)
    q = q_ref[...]
    kv = kv_ref[...]
    kvc = kvc_ref[...]
    gq, gk = gq_ref[...], gk_ref[...]
    cq, sq = cos_ref[pl.ds(q0, tq), :], sin_ref[pl.ds(q0, tq), :]
    kr_raw = p_ref[...][:, C_Q_LORA + C_KV_LORA:]
    krc_raw = krc_ref[...]
    kr_all = jnp.concatenate([_rope(kr_raw * gk[:, C_NOPE:], cos_ref[...], sin_ref[...]),
                              krc_raw * gk[:, C_NOPE:]], axis=0)
    kr_ss = jnp.concatenate([jnp.sum(kr_raw * kr_raw, axis=-1, keepdims=True),
                             jnp.sum(krc_raw * krc_raw, axis=-1, keepdims=True)], axis=0)
    kr_pad = jnp.concatenate([jnp.zeros((kr_all.shape[0], C_NOPE), F32), kr_all], axis=-1)
    lane = lax.broadcasted_iota(jnp.int32, kr_pad.shape, 1)
    outs = []
    for h in range(C_HEADS):
        qh = q[:, h * C_QK:(h + 1) * C_QK]
        qh = qh * lax.rsqrt(jnp.mean(qh * qh, axis=-1, keepdims=True) + EPS) * gq
        qh = jnp.concatenate([qh[:, :C_NOPE], _rope(qh[:, C_NOPE:], cq, sq)], axis=-1)
        c0 = h * (C_NOPE + C_V)
        kn = jnp.concatenate([kv[:, c0:c0 + C_QK], kvc[:, c0:c0 + C_QK]], axis=0)
        v = jnp.concatenate([kv[:, c0 + C_NOPE:c0 + C_NOPE + C_V], kvc[:, c0 + C_NOPE:c0 + C_NOPE + C_V]],
                            axis=0)
        ss = jnp.sum(kn[:, :C_NOPE] * kn[:, :C_NOPE], axis=-1, keepdims=True) + kr_ss
        k = jnp.where(lane < C_NOPE, kn * gk, kr_pad) * lax.rsqrt(ss / C_QK + EPS)
        s = _bdot_nt(qh, k) * scale
        outs.append(_softmax_attend([(s, v)], None))
    o_ref[...] = jnp.concatenate(outs, axis=-1).astype(o_ref.dtype)


def _attn_c_sample(q, kv, p1, row0, n_seq, t, kr_ctx, gq, gk, tq=256):
    n = p1.shape[0]
    past = kr_ctx.shape[1]
    nq = t // tq
    cos, sin = _rope_tables(t, C_ROPE)
    return pl.pallas_call(
        functools.partial(_attn_c_sample_kernel, tq=tq),
        grid=(n_seq, nq),
        in_specs=[pl.BlockSpec((tq, q.shape[1]), lambda b, i: (row0 // tq + b * nq + i, 0)),
                  pl.BlockSpec((t, kv.shape[1]), lambda b, i: (row0 // t + b, 0)),
                  pl.BlockSpec((t, p1.shape[1]), lambda b, i: (row0 // t + b, 0)),
                  pl.BlockSpec((past, kv.shape[1]), lambda b, i: (n // past + b, 0)),
                  pl.BlockSpec((None, past, C_ROPE), lambda b, i: (b, 0, 0)),
                  pl.BlockSpec((1, C_QK), lambda b, i: (0, 0)),
                  pl.BlockSpec((1, C_QK), lambda b, i: (0, 0)),
                  pl.BlockSpec((t, C_ROPE), lambda b, i: (0, 0)),
                  pl.BlockSpec((t, C_ROPE), lambda b, i: (0, 0))],
        out_specs=pl.BlockSpec((tq, C_HEADS * C_V), lambda b, i: (b * nq + i, 0)),
        out_shape=jax.ShapeDtypeStruct((n_seq * t, C_HEADS * C_V), BF16),
        compiler_params=_params(("arbitrary", "arbitrary"), VMEM_LIMIT),
        name="attn_c_latent",
    )(q, kv, p1, kv, kr_ctx, gq.reshape(1, -1), gk.reshape(1, -1), cos, sin)


ROUTE_TILE = 512
ROUTE_BLOCK = 512
SLOT_ALIGN = 8
LOCAL_SLOTS = 2304
META_LANES = 128
FF_CHUNK = 256


def _router_kernel(x_ref, g_ref, sc_ref, sh_ref, wr_ref, br_ref, h_ref, meta_ref, cntb_ref, cnt_ref,
                   run_ref):
    @pl.when(pl.program_id(0) == 0)
    def _():
        run_ref[...] = jnp.zeros_like(run_ref)

    h = _rms(x_ref[...], g_ref[...]) * (1 + sc_ref[...]) + sh_ref[...]
    h_ref[...] = h.astype(h_ref.dtype)
    tm = h.shape[0]
    logits = _hdot(h, wr_ref[...]) + br_ref[...]
    lane = lax.broadcasted_iota(jnp.int32, logits.shape, 1)
    work = logits
    picks, tops = [], []
    for _ in range(TOP_K):
        m = work.max(axis=-1, keepdims=True)
        first = jnp.min(jnp.where(work == m, lane, N_EXPERTS), axis=-1, keepdims=True)
        pick = lane == first
        picks.append(pick)
        tops.append(m)
        work = jnp.where(pick, -jnp.inf, work)
    sel = sum(p.astype(F32) for p in picks)
    earlier = (lax.broadcasted_iota(jnp.int32, (tm, tm), 0)
               > lax.broadcasted_iota(jnp.int32, (tm, tm), 1)).astype(BF16)
    inside = jnp.dot(earlier, sel.astype(BF16), preferred_element_type=F32)
    cnt = jnp.sum(sel, axis=0, keepdims=True)
    run = jnp.ceil(cnt / SLOT_ALIGN) * SLOT_ALIGN
    lower_e = (lax.broadcasted_iota(jnp.int32, (N_EXPERTS, N_EXPERTS), 0)
               < lax.broadcasted_iota(jnp.int32, (N_EXPERTS, N_EXPERTS), 1)).astype(BF16)
    start = jnp.dot(jnp.broadcast_to(run, (8, N_EXPERTS)).astype(BF16), lower_e,
                    preferred_element_type=F32)[0:1]
    slot = start + inside
    ws = [jnp.exp(t - tops[0]) for t in tops]
    den = sum(ws)
    mlane = lax.broadcasted_iota(jnp.int32, (tm, META_LANES), 1)
    meta = jnp.zeros((tm, META_LANES), F32)
    for k in range(TOP_K):
        meta = jnp.where(mlane == k, jnp.sum(jnp.where(picks[k], slot, 0.0), axis=-1, keepdims=True), meta)
        meta = jnp.where(mlane == TOP_K + k, ws[k] / den, meta)
    meta_ref[...] = meta
    cntb_ref[...] = cnt
    run_ref[...] += run
    cnt_ref[...] = run_ref[...]


def _router(x, gain, scale, shift, w_router, b_router, layer, n_p, t_s):
    n, d = x.shape
    tm = ROUTE_BLOCK
    e = w_router.shape[-1]
    group = _group_of_tile(tm, n_p, t_s)
    return pl.pallas_call(
        _router_kernel,
        grid=(n // tm,),
        in_specs=[pl.BlockSpec((tm, d), lambda i: (i, 0)),
                  pl.BlockSpec((1, d), lambda i: (0, 0)),
                  pl.BlockSpec((None, 1, d), lambda i: (group(i), 0, 0)),
                  pl.BlockSpec((None, 1, d), lambda i: (group(i), 0, 0)),
                  pl.BlockSpec((None, d, e), lambda i: (layer, 0, 0)),
                  pl.BlockSpec((None, 1, e), lambda i: (layer, 0, 0))],
        out_specs=[pl.BlockSpec((tm, d), lambda i: (i, 0)),
                   pl.BlockSpec((tm, META_LANES), lambda i: (i, 0)),
                   pl.BlockSpec((None, 1, e), lambda i: (i, 0, 0)),
                   pl.BlockSpec((1, e), lambda i: (0, 0))],
        out_shape=[jax.ShapeDtypeStruct((n, d), BF16),
                   jax.ShapeDtypeStruct((n, META_LANES), F32),
                   jax.ShapeDtypeStruct((n // tm, 1, e), F32),
                   jax.ShapeDtypeStruct((1, e), F32)],
        scratch_shapes=[pltpu.VMEM((1, e), F32)],
        compiler_params=_params(("arbitrary",)),
        name="router",
    )(x, gain.reshape(1, d), scale.reshape(N_GROUPS, 1, d), shift.reshape(N_GROUPS, 1, d),
      w_router, b_router.reshape(-1, 1, e))


def _route_plan(cnt_blk, cnt_tot, n_tiles, min_tiles):
    cb = cnt_blk[:, 0, :].astype(jnp.int32)
    run = (cb + SLOT_ALIGN - 1) // SLOT_ALIGN * SLOT_ALIGN
    counts = cnt_tot[0].astype(jnp.int32)
    padded = (counts + ROUTE_TILE - 1) // ROUTE_TILE * ROUTE_TILE
    ends = jnp.cumsum(padded)
    offs = ends - padded
    gstart = offs[None, :] + jnp.cumsum(run, axis=0) - run
    lstart = jnp.cumsum(run, axis=1) - run
    tile_start = jnp.arange(n_tiles, dtype=jnp.int32) * ROUTE_TILE
    n_valid = ends[-1] // ROUTE_TILE
    te = jnp.minimum(jnp.sum(ends[None, :] <= tile_start[:, None], axis=1), N_EXPERTS - 1)
    te = jnp.where(tile_start < ends[-1], te, te[jnp.maximum(n_valid - 1, 0)]).astype(jnp.int32)
    rows = jnp.clip((offs + counts)[te] - tile_start, 0, ROUTE_TILE).astype(jnp.int32)
    ragged = jnp.where(counts % ROUTE_TILE != 0, ends // ROUTE_TILE - 1, -1)
    tail = jnp.arange(min_tiles, n_tiles, dtype=jnp.int32)
    fill = jnp.concatenate([ragged, jnp.where(tail >= n_valid, tail, -1)]).astype(jnp.int32)
    runs = (gstart.reshape(-1).astype(jnp.int32), lstart.reshape(-1).astype(jnp.int32),
            (run // SLOT_ALIGN).reshape(-1).astype(jnp.int32),
            (jnp.sum(run, axis=1) // SLOT_ALIGN).astype(jnp.int32))
    return runs, te, rows, n_valid.reshape(1).astype(jnp.int32), fill


RUN_PIECE = 4
WAIT_PIECE = 8


def _start_runs(blk, gstart_ref, lstart_ref, nch_ref, copy):
    big = RUN_PIECE * SLOT_ALIGN

    def per_expert(e, carry):
        idx = blk * N_EXPERTS + e
        g0, l0, n = gstart_ref[idx], lstart_ref[idx], nch_ref[idx]
        n_big = n // RUN_PIECE

        def large(i, c2):
            off = pl.multiple_of(i * big, big)
            copy(pl.multiple_of(l0 + off, SLOT_ALIGN), pl.multiple_of(g0 + off, SLOT_ALIGN), big).start()
            return c2

        def small(i, c2):
            off = pl.multiple_of(n_big * big + i * SLOT_ALIGN, SLOT_ALIGN)
            copy(pl.multiple_of(l0 + off, SLOT_ALIGN), pl.multiple_of(g0 + off, SLOT_ALIGN), SLOT_ALIGN).start()
            return c2

        lax.fori_loop(0, n_big, large, 0)
        lax.fori_loop(0, n - n_big * RUN_PIECE, small, 0)
        return carry

    lax.fori_loop(0, N_EXPERTS, per_expert, 0)


def _wait_runs(n_chunks, copy):
    n_big = n_chunks // WAIT_PIECE

    def large(i, carry):
        copy(0, 0, WAIT_PIECE * SLOT_ALIGN).wait()
        return carry

    def small(i, carry):
        copy(0, 0, SLOT_ALIGN).wait()
        return carry

    lax.fori_loop(0, n_big, large, 0)
    lax.fori_loop(0, n_chunks - n_big * WAIT_PIECE, small, 0)


def _dispatch_kernel(fill_ref, gstart_ref, lstart_ref, nch_ref, ntot_ref, meta_ref, h_ref, xs_ref,
                     loc_ref, zero_ref, sem, zsem):
    blk = pl.program_id(0)

    @pl.when(blk == 0)
    def _():
        zero_ref[...] = jnp.zeros_like(zero_ref)

        def fill_copy(j):
            row0 = pl.multiple_of(fill_ref[j] * ROUTE_TILE, ROUTE_TILE)
            return pltpu.make_async_copy(zero_ref, xs_ref.at[pl.ds(row0, ROUTE_TILE), :], zsem)

        def start(j, carry):
            @pl.when(fill_ref[j] >= 0)
            def _():
                fill_copy(j).start()
            return carry

        def wait(j, carry):
            @pl.when(fill_ref[j] >= 0)
            def _():
                fill_copy(j).wait()
            return carry

        lax.fori_loop(0, fill_ref.shape[0], start, 0)
        lax.fori_loop(0, fill_ref.shape[0], wait, 0)

    buf = blk % 2
    last = pl.num_programs(0) - 1

    def copier(b):
        def copy(l0, g0, rows):
            return pltpu.make_async_copy(loc_ref.at[b, pl.ds(l0, rows), :], xs_ref.at[pl.ds(g0, rows), :],
                                         sem.at[b])
        return copy

    @pl.when(blk >= 2)
    def _():
        _wait_runs(ntot_ref[blk - 2], copier(buf))

    hb = h_ref[...]
    tm = hb.shape[0]
    slots_t = meta_ref[...].T[0:TOP_K]
    rows = 256
    for c0 in range(0, LOCAL_SLOTS, rows):
        slot = (c0 + lax.broadcasted_iota(jnp.int32, (rows, tm), 0)).astype(F32)
        onehot = sum(jnp.where(slot == slots_t[k:k + 1], 1.0, 0.0) for k in range(TOP_K))
        loc_ref[buf, c0:c0 + rows, :] = jnp.dot(onehot.astype(BF16), hb,
                                                preferred_element_type=F32).astype(loc_ref.dtype)

    _start_runs(blk, gstart_ref, lstart_ref, nch_ref, copier(buf))

    @pl.when(blk == last)
    def _():
        @pl.when(blk >= 1)
        def _():
            _wait_runs(ntot_ref[blk - 1], copier(1 - buf))
        _wait_runs(ntot_ref[blk], copier(buf))


def _dispatch(h, meta, runs, fill, n_slots):
    n, d = h.shape
    tm = ROUTE_BLOCK
    grid_spec = pltpu.PrefetchScalarGridSpec(
        num_scalar_prefetch=5,
        grid=(n // tm,),
        in_specs=[pl.BlockSpec((tm, META_LANES), lambda i, *_: (i, 0)),
                  pl.BlockSpec((tm, d), lambda i, *_: (i, 0))],
        out_specs=pl.BlockSpec(memory_space=pl.ANY),
        scratch_shapes=[pltpu.VMEM((2, LOCAL_SLOTS, d), F32), pltpu.VMEM((ROUTE_TILE, d), F32),
                        pltpu.SemaphoreType.DMA((2,)), pltpu.SemaphoreType.DMA(())])
    return pl.pallas_call(
        _dispatch_kernel,
        grid_spec=grid_spec,
        out_shape=jax.ShapeDtypeStruct((n_slots, d), F32),
        compiler_params=_params(("arbitrary",), VMEM_LIMIT),
        name="moe_dispatch",
    )(fill, *runs, meta, h)


def _experts_kernel(te_ref, rows_ref, nv_ref, x_ref, wgu_ref, bgu_ref, wd_ref, bd_ref, y_ref,
                    wgub_ref, wdb_ref, *, d_ff):
    i = pl.program_id(0)
    valid = i < nv_ref[0]
    fresh = jnp.logical_or(i == 0, te_ref[i] != te_ref[jnp.maximum(i - 1, 0)])

    @pl.when(jnp.logical_and(valid, fresh))
    def _():
        wgub_ref[...] = wgu_ref[...].astype(BF16)
        wdb_ref[...] = wd_ref[...].astype(BF16)

    @pl.when(valid)
    def _():
        row = lax.broadcasted_iota(jnp.int32, x_ref.shape, 0)
        x = jnp.where(row < rows_ref[i], x_ref[...], 0.0).astype(BF16)
        acc = None
        for c0 in range(0, d_ff, FF_CHUNK):
            g_cols, u_cols = slice(c0, c0 + FF_CHUNK), slice(d_ff + c0, d_ff + c0 + FF_CHUNK)
            gate = jnp.dot(x, wgub_ref[:, g_cols], preferred_element_type=F32) + bgu_ref[:, g_cols]
            up = jnp.dot(x, wgub_ref[:, u_cols], preferred_element_type=F32) + bgu_ref[:, u_cols]
            gate = jnp.minimum(gate, SWIGLU_LIMIT)
            up = jnp.clip(up, -SWIGLU_LIMIT, SWIGLU_LIMIT)
            act = (up + 1) * gate * jax.nn.sigmoid(SWIGLU_ALPHA * gate)
            part = jnp.dot(act.astype(BF16), wdb_ref[c0:c0 + FF_CHUNK, :], preferred_element_type=F32)
            acc = part if acc is None else acc + part
        y_ref[...] = acc + bd_ref[...]

    @pl.when(jnp.logical_not(valid))
    def _():
        y_ref[...] = jnp.zeros_like(y_ref)


def _experts(xs, te, rows, n_valid, w_gu, b_gu, w_down, b_down, layer):
    n_slots, d = xs.shape
    _, e, _, two_ff = w_gu.shape
    last = lambda i, nv: jnp.minimum(i, nv[0] - 1)
    grid_spec = pltpu.PrefetchScalarGridSpec(
        num_scalar_prefetch=3,
        grid=(n_slots // ROUTE_TILE,),
        in_specs=[pl.BlockSpec((ROUTE_TILE, d), lambda i, te, rw, nv: (last(i, nv), 0)),
                  pl.BlockSpec((None, None, d, two_ff), lambda i, te, rw, nv: (layer, te[i], 0, 0)),
                  pl.BlockSpec((None, None, 1, two_ff), lambda i, te, rw, nv: (layer, te[i], 0, 0)),
                  pl.BlockSpec((None, None, two_ff // 2, d), lambda i, te, rw, nv: (layer, te[i], 0, 0)),
                  pl.BlockSpec((None, None, 1, d), lambda i, te, rw, nv: (layer, te[i], 0, 0))],
        out_specs=pl.BlockSpec((ROUTE_TILE, d), lambda i, te, rw, nv: (i, 0)),
        scratch_shapes=[pltpu.VMEM((d, two_ff), BF16), pltpu.VMEM((two_ff // 2, d), BF16)])
    return pl.pallas_call(
        functools.partial(_experts_kernel, d_ff=two_ff // 2),
        grid_spec=grid_spec,
        out_shape=jax.ShapeDtypeStruct((n_slots, d), F32),
        compiler_params=_params(("arbitrary",), VMEM_LIMIT),
        name="moe_experts",
    )(te, rows, n_valid, xs, w_gu, b_gu.reshape(b_gu.shape[0], e, 1, two_ff), w_down,
      b_down.reshape(b_down.shape[0], e, 1, d))


def _combine_kernel(gstart_ref, lstart_ref, nch_ref, ntot_ref, meta_ref, x_ref, gate_ref, y_ref, *refs,
                    split_tiles):
    *o_refs, loc_ref, sem = refs
    blk = pl.program_id(0)

    buf = blk % 2

    def copier(b):
        def copy(l0, g0, rows):
            return pltpu.make_async_copy(y_ref.at[pl.ds(g0, rows), :], loc_ref.at[b, pl.ds(l0, rows), :],
                                         sem.at[b])
        return copy

    @pl.when(blk == 0)
    def _():
        loc_ref[...] = jnp.zeros_like(loc_ref)
        _start_runs(blk, gstart_ref, lstart_ref, nch_ref, copier(buf))

    @pl.when(blk + 1 < pl.num_programs(0))
    def _():
        _start_runs(blk + 1, gstart_ref, lstart_ref, nch_ref, copier(1 - buf))

    _wait_runs(ntot_ref[blk], copier(buf))

    meta = meta_ref[...]
    tm = meta.shape[0]
    cols = 256
    acc = jnp.zeros(x_ref.shape, F32)
    for c0 in range(0, LOCAL_SLOTS, cols):
        slot = (c0 + lax.broadcasted_iota(jnp.int32, (tm, cols), 1)).astype(F32)
        wts = sum(jnp.where(slot == meta[:, k:k + 1], meta[:, TOP_K + k:TOP_K + k + 1], 0.0)
                  for k in range(TOP_K))
        acc = acc + jnp.dot(wts.astype(BF16), loc_ref[buf, c0:c0 + cols, :].astype(BF16),
                            preferred_element_type=F32)
    out = x_ref[...] + gate_ref[...] * acc
    if split_tiles is None:
        o_refs[0][...] = out
    else:
        @pl.when(blk < split_tiles)
        def _():
            o_refs[0][...] = out

        @pl.when(blk >= split_tiles)
        def _():
            o_refs[1][...] = out


def _combine(y, meta, runs, x, gate, n_p, t_s, split=False):
    n, d = x.shape
    tm = ROUTE_BLOCK
    group = _group_of_tile(tm, n_p, t_s)
    st = n_p // tm
    if split:
        out_specs = [pl.BlockSpec((tm, d), lambda i, *_: (jnp.minimum(i, st - 1), 0)),
                     pl.BlockSpec((tm, d), lambda i, *_: (jnp.maximum(i - st, 0), 0))]
        out_shape = [jax.ShapeDtypeStruct((n_p, d), F32), jax.ShapeDtypeStruct((n - n_p, d), F32)]
    else:
        out_specs = pl.BlockSpec((tm, d), lambda i, *_: (i, 0))
        out_shape = jax.ShapeDtypeStruct((n, d), F32)
    grid_spec = pltpu.PrefetchScalarGridSpec(
        num_scalar_prefetch=4,
        grid=(n // tm,),
        in_specs=[pl.BlockSpec((tm, META_LANES), lambda i, *_: (i, 0)),
                  pl.BlockSpec((tm, d), lambda i, *_: (i, 0)),
                  pl.BlockSpec((None, 1, d), lambda i, *_: (group(i), 0, 0)),
                  pl.BlockSpec(memory_space=pl.ANY)],
        out_specs=out_specs,
        scratch_shapes=[pltpu.VMEM((2, LOCAL_SLOTS, d), F32), pltpu.SemaphoreType.DMA((2,))])
    return pl.pallas_call(
        functools.partial(_combine_kernel, split_tiles=st if split else None),
        grid_spec=grid_spec,
        out_shape=out_shape,
        compiler_params=_params(("arbitrary",), VMEM_LIMIT),
        name="moe_combine",
    )(*runs, meta, x, gate.reshape(N_GROUPS, 1, d), y)


def _moe(x, gain, scale, shift, gate, w_router, b_router, w_gu, b_gu, w_down, b_down, layer, n_p, t_s,
         split=False):
    n = x.shape[0]
    assert LOCAL_SLOTS >= ROUTE_BLOCK * TOP_K + N_EXPERTS * (SLOT_ALIGN - 1) and n % ROUTE_BLOCK == 0
    min_tiles = n * TOP_K // ROUTE_TILE
    max_slots = n * TOP_K + (n // ROUTE_BLOCK) * N_EXPERTS * (SLOT_ALIGN - 1) + N_EXPERTS * (ROUTE_TILE - 1)
    n_tiles = -(-max_slots // ROUTE_TILE)
    h, meta, cnt_blk, cnt_tot = _router(x, gain, scale, shift, w_router, b_router, layer, n_p, t_s)
    runs, te, rows, n_valid, fill = _route_plan(cnt_blk, cnt_tot, n_tiles, min_tiles)
    xs = _dispatch(h, meta, runs, fill, n_tiles * ROUTE_TILE)
    y = _experts(xs, te, rows, n_valid, w_gu, b_gu, w_down, b_down, layer)
    return _combine(y, meta, runs, x, gate, n_p, t_s, split)


def kernel(x_prompt, x_sample, c, cache_a_k, cache_a_v, state_b_fwd, state_b_bwd, cache_c_ckv,
           cache_c_krope, c_ctx, w_mod, b_mod, norm_mix, norm_ffn, e_w_in, e_w_out, e_a_qnorm,
           e_a_knorm, e_a_sink, e_b_conv, e_b_alog, e_b_dtbias, e_b_onorm, o_w_in, o_q_lora_norm,
           o_kv_lora_norm, o_w_uq, o_w_ukv, o_qnorm, o_knorm, o_w_out, moe_w_router, moe_b_router,
           moe_w_gu, moe_b_gu, moe_w_down, moe_b_down):
    bp, tp, d = x_prompt.shape
    bs, ts, _ = x_sample.shape
    depth = w_mod.shape[0]
    n_p, n_s = bp * tp, bs * ts
    n = n_p + n_s
    assert bs + 1 <= N_GROUPS and ts % 512 == 0 and n_p % ts == 0

    x = (x_prompt.reshape(n_p, d), x_sample.reshape(n_s, d))
    cond = jnp.concatenate([c_ctx[None], c, jnp.zeros((N_GROUPS - 1 - bs, d), F32)], axis=0)
    mod = _adaln(cond, w_mod, b_mod)

    new_a_k, new_a_v, new_b_fwd, new_b_bwd, new_c_ckv, new_c_krope = [], [], [], [], [], []
    for layer in range(depth):
        sh1, sc1, g1, sh2, sc2, g2 = (mod[layer, j] for j in range(6))
        h = _modulate(x, norm_mix[layer], sc1, sh1, n_p, ts)
        i = layer // 2
        if layer % 2 == 0:
            main_w = (A_HEADS + 2 * A_KV_HEADS) * A_HEAD_DIM + 4 * B_HEADS * B_DK
            proj = _matmul((h,), e_w_in, i, 0, main_w // 2, main_w, name="even_in_proj")
            tail = _matmul((h,), e_w_in[i][None, :, main_w:], 0, 0, 4 * B_HEADS, 4 * B_HEADS,
                           name="even_gate_proj")
            oa_p, kn_p = _attn_a_prompt(proj, bp, tp, e_a_sink[i], e_a_qnorm[i], e_a_knorm[i])
            oa_s = _attn_a_sample(proj, n_p, bs, ts, cache_a_k[:, i], cache_a_v[:, i],
                                  e_a_sink[i], e_a_qnorm[i], e_a_knorm[i])
            zeros = jnp.zeros((bp, B_HEADS, B_DK, B_DV), F32)
            ob_p, s_f, s_b = _deltanet(proj, tail, 0, bp, tp, e_b_conv[i], e_b_alog[i],
                                       e_b_dtbias[i], e_b_onorm[i], zeros, zeros)
            ob_s, _, _ = _deltanet(proj, tail, n_p, bs, ts, e_b_conv[i], e_b_alog[i],
                                   e_b_dtbias[i], e_b_onorm[i], state_b_fwd[:, i], state_b_bwd[:, i])
            x = (_matmul_residual([(oa_p, oa_s), (ob_p, ob_s)], e_w_out, i, x, g1, n_p, ts),)
            kw = A_KV_HEADS * A_HEAD_DIM
            new_a_k.append(kn_p.reshape(bp, tp, A_KV_HEADS, A_HEAD_DIM).transpose(0, 2, 1, 3))
            v_p = proj[:n_p, A_HEADS * A_HEAD_DIM + kw:A_HEADS * A_HEAD_DIM + 2 * kw]
            new_a_v.append(v_p.reshape(bp, tp, A_KV_HEADS, A_HEAD_DIM).transpose(0, 2, 1, 3))
            new_b_fwd.append(s_f)
            new_b_bwd.append(s_b)
        else:
            p1 = _matmul((h,), o_w_in, i, 0, o_w_in.shape[-1], o_w_in.shape[-1], name="odd_in_proj")
            cq, ckv, ckv_b = _lora_norm(p1, o_q_lora_norm[i], o_kv_lora_norm[i])
            q = _matmul((cq,), o_w_uq, i, 0, o_w_uq.shape[-1] // 2, o_w_uq.shape[-1], name="odd_uq")
            ckv_ctx = cache_c_ckv[:, i].reshape(-1, C_KV_LORA).astype(BF16)
            kv = _matmul((ckv_b, ckv_ctx), o_w_ukv, i, 0, o_w_ukv.shape[-1] // 2, o_w_ukv.shape[-1],
                         name="odd_ukv")
            o_p = _attn_c_prompt(q, kv, p1, bp, tp, o_qnorm[i], o_knorm[i])
            o_s = _attn_c_sample(q, kv, p1, n_p, bs, ts, cache_c_krope[:, i], o_qnorm[i], o_knorm[i])
            x = (_matmul_residual([(o_p, o_s)], o_w_out, i, x, g1, n_p, ts),)
            new_c_ckv.append(ckv[:n_p].reshape(bp, tp, C_KV_LORA))
            new_c_krope.append(p1[:n_p, C_Q_LORA + C_KV_LORA:].reshape(bp, tp, C_ROPE))
        x = _moe(x[0], norm_ffn[layer], sc2, sh2, g2, moe_w_router, moe_b_router, moe_w_gu, moe_b_gu,
                 moe_w_down, moe_b_down, layer, n_p, ts, split=layer == depth - 1)
        x = tuple(x) if layer == depth - 1 else (x,)

    return (x[0].reshape(bp, tp, d), x[1].reshape(bs, ts, d),
            jnp.stack(new_a_k, axis=1), jnp.stack(new_a_v, axis=1),
            jnp.stack(new_b_fwd, axis=1), jnp.stack(new_b_bwd, axis=1),
            jnp.stack(new_c_ckv, axis=1), jnp.stack(new_c_krope, axis=1))
```

```python
import functools
import math

import numpy as np
import jax
import jax.numpy as jnp
from jax import lax
from jax.experimental import pallas as pl
from jax.experimental.pallas import tpu as pltpu

F32 = jnp.float32
BF16 = jnp.bfloat16
HIGHEST = lax.Precision.HIGHEST

EPS = 1e-6
NEG_INF = -1e30
ROPE_BASE = 10000.0
GRID_W = 64
N_GROUPS = 8

A_HEADS, A_KV_HEADS, A_GROUP, A_HEAD_DIM, A_WINDOW, A_BLOCK = 8, 2, 4, 64, 128, 128
B_HEADS, B_DK, B_DV, B_CHUNK = 8, 64, 64, 64
C_HEADS, C_NOPE, C_ROPE, C_V, C_Q_LORA, C_KV_LORA = 16, 64, 32, 64, 384, 256
C_QK = C_NOPE + C_ROPE
N_EXPERTS, TOP_K = 32, 4
SWIGLU_LIMIT, SWIGLU_ALPHA = 7.0, 1.702

VMEM_LIMIT = 56 * 1024 * 1024


def _params(sem, vmem=None):
    return pltpu.CompilerParams(dimension_semantics=sem, vmem_limit_bytes=vmem)


def _bdot(a, b):
    return jnp.dot(a.astype(BF16), b.astype(BF16), preferred_element_type=F32)


def _bdot_nt(a, b):
    return lax.dot_general(a.astype(BF16), b.astype(BF16), (((1,), (1,)), ((), ())),
                           preferred_element_type=F32)


def _bdot_tn(a, b):
    return lax.dot_general(a.astype(BF16), b.astype(BF16), (((0,), (0,)), ((), ())),
                           preferred_element_type=F32)


def _hdot(a, b):
    return jnp.dot(a, b, preferred_element_type=F32, precision=HIGHEST)


def _hdot_nt(a, b):
    return lax.dot_general(a, b, (((1,), (1,)), ((), ())), preferred_element_type=F32,
                           precision=HIGHEST)


def _rms(x, gain):
    return x * lax.rsqrt(jnp.mean(x * x, axis=-1, keepdims=True) + EPS) * gain


def _silu(x):
    return x * jax.nn.sigmoid(x)


def _group_of_tile(tm, n_p, t_s):
    def group(i):
        r = i * tm
        return jnp.where(r < n_p, 0, 1 + (r - n_p) // t_s)
    return group


def _adaln_kernel(cond_ref, w_ref, b_ref, o_ref):
    o_ref[...] = _bdot(_silu(cond_ref[...]), w_ref[...]) + b_ref[...]


def _adaln(cond, w_mod, b_mod):
    depth, d, _ = w_mod.shape
    return pl.pallas_call(
        _adaln_kernel,
        grid=(depth, 6),
        in_specs=[pl.BlockSpec((N_GROUPS, d), lambda l, j: (0, 0)),
                  pl.BlockSpec((None, d, d), lambda l, j: (l, 0, j)),
                  pl.BlockSpec((None, 1, d), lambda l, j: (l, 0, j))],
        out_specs=pl.BlockSpec((None, None, N_GROUPS, d), lambda l, j: (l, j, 0, 0)),
        out_shape=jax.ShapeDtypeStruct((depth, 6, N_GROUPS, d), F32),
        compiler_params=_params(("arbitrary", "arbitrary")),
        name="adaln",
    )(cond, w_mod, b_mod.reshape(depth, 1, 6 * d))


def _row_specs(arrays, tm, pos):
    specs, ends, start = [], [], 0
    for arr in arrays:
        nt = arr.shape[0] // tm
        specs.append(pl.BlockSpec((tm, arr.shape[1]),
                                  lambda *ids, s=start, nt=nt: (jnp.clip(ids[pos] - s, 0, nt - 1), 0)))
        start += nt
        ends.append(start)
    return specs, tuple(ends)


def _pick_rows(refs, ends, i):
    x = refs[-1][...]
    for ref, end in zip(reversed(refs[:-1]), reversed(ends[:-1])):
        x = jnp.where(i < end, ref[...], x)
    return x


def _modulate_kernel(*refs, ends):
    n = len(ends)
    g_ref, sc_ref, sh_ref, o_ref = refs[n:]
    y = _rms(_pick_rows(refs[:n], ends, pl.program_id(0)), g_ref[...])
    o_ref[...] = (y * (1 + sc_ref[...]) + sh_ref[...]).astype(o_ref.dtype)


def _modulate(xs, gain, scale, shift, n_p, t_s, tm=512):
    n, d = sum(x.shape[0] for x in xs), xs[0].shape[1]
    group = _group_of_tile(tm, n_p, t_s)
    specs, ends = _row_specs(xs, tm, 0)
    return pl.pallas_call(
        functools.partial(_modulate_kernel, ends=ends),
        grid=(n // tm,),
        in_specs=specs + [pl.BlockSpec((1, d), lambda i: (0, 0)),
                          pl.BlockSpec((None, 1, d), lambda i: (group(i), 0, 0)),
                          pl.BlockSpec((None, 1, d), lambda i: (group(i), 0, 0))],
        out_specs=pl.BlockSpec((tm, d), lambda i: (i, 0)),
        out_shape=jax.ShapeDtypeStruct((n, d), BF16),
        compiler_params=_params(("arbitrary",)),
        name="modulate",
    )(*xs, gain.reshape(1, d), scale.reshape(N_GROUPS, 1, d), shift.reshape(N_GROUPS, 1, d))


def _mm_kernel(*refs, ends):
    n = len(ends)
    w_ref, o_ref, wb_ref = refs[n:]

    @pl.when(pl.program_id(1) == 0)
    def _():
        wb_ref[...] = w_ref[...].astype(BF16)
    x = _pick_rows(refs[:n], ends, pl.program_id(1))
    o_ref[...] = jnp.dot(x, wb_ref[...], preferred_element_type=F32).astype(o_ref.dtype)


def _matmul(xs, w3, layer, col0_blocks, tn, n_out, out_dtype=F32, tm=512, name="matmul"):
    n, k = sum(x.shape[0] for x in xs), xs[0].shape[1]
    specs, ends = _row_specs(xs, tm, 1)
    return pl.pallas_call(
        functools.partial(_mm_kernel, ends=ends),
        grid=(n_out // tn, n // tm),
        in_specs=specs + [pl.BlockSpec((None, k, tn), lambda j, i: (layer, 0, col0_blocks + j))],
        out_specs=pl.BlockSpec((tm, tn), lambda j, i: (i, j)),
        out_shape=jax.ShapeDtypeStruct((n, n_out), out_dtype),
        scratch_shapes=[pltpu.VMEM((k, tn), BF16)],
        compiler_params=_params(("arbitrary", "arbitrary"), VMEM_LIMIT),
        name=name,
    )(*xs, w3)


def _mm_res_kernel(*refs, col_ends, res_ends):
    i = pl.program_id(0)
    pos = 0
    cols = []
    for ends in col_ends:
        cols.append(refs[pos:pos + len(ends)])
        pos += len(ends)
    res_refs = refs[pos:pos + len(res_ends)]
    w_ref, gate_ref, o_ref, wb_ref = refs[pos + len(res_ends):]

    @pl.when(i == 0)
    def _():
        wb_ref[...] = w_ref[...].astype(BF16)
    y, k0 = None, 0
    for col_refs, ends in zip(cols, col_ends):
        x = _pick_rows(col_refs, ends, i)
        part = jnp.dot(x, wb_ref[k0:k0 + x.shape[1], :], preferred_element_type=F32)
        y = part if y is None else y + part
        k0 += x.shape[1]
    o_ref[...] = _pick_rows(res_refs, res_ends, i) + gate_ref[...] * y


def _matmul_residual(mix_cols, w3, layer, res, gate, n_p, t_s, tm=512):
    n, d = sum(r.shape[0] for r in res), res[0].shape[1]
    k = sum(col[0].shape[1] for col in mix_cols)
    group = _group_of_tile(tm, n_p, t_s)
    specs, col_ends = [], []
    for col in mix_cols:
        s, e = _row_specs(col, tm, 0)
        specs += s
        col_ends.append(e)
    res_specs, res_ends = _row_specs(res, tm, 0)
    return pl.pallas_call(
        functools.partial(_mm_res_kernel, col_ends=tuple(col_ends), res_ends=res_ends),
        grid=(n // tm,),
        in_specs=specs + res_specs + [pl.BlockSpec((None, k, d), lambda i: (layer, 0, 0)),
                                      pl.BlockSpec((None, 1, d), lambda i: (group(i), 0, 0))],
        out_specs=pl.BlockSpec((tm, d), lambda i: (i, 0)),
        out_shape=jax.ShapeDtypeStruct((n, d), F32),
        scratch_shapes=[pltpu.VMEM((k, d), BF16)],
        compiler_params=_params(("arbitrary",), VMEM_LIMIT),
        name="out_proj_residual",
    )(*[piece for col in mix_cols for piece in col], *res, w3, gate.reshape(N_GROUPS, 1, d))


def _rope_tables(t_len, d):
    half, quarter = d // 2, d // 4
    pos = np.arange(t_len)
    row, col = pos // GRID_W, pos % GRID_W
    inv = ROPE_BASE ** (-np.arange(quarter, dtype=np.float64) / quarter)
    ang_r = row[:, None] * inv[None, :]
    ang_c = col[:, None] * inv[None, :]
    cos = np.concatenate([np.cos(ang_r), np.cos(ang_r), np.cos(ang_c), np.cos(ang_c)], axis=1)
    sin = np.concatenate([-np.sin(ang_r), np.sin(ang_r), -np.sin(ang_c), np.sin(ang_c)], axis=1)
    return jnp.asarray(cos, F32), jnp.asarray(sin, F32)


def _swap_pairs(x):
    q = x.shape[-1] // 4
    return jnp.concatenate([x[:, q:2 * q], x[:, :q], x[:, 3 * q:], x[:, 2 * q:3 * q]], axis=-1)


def _rope(x, cos, sin):
    return x * cos + _swap_pairs(x) * sin


def _softmax_attend(parts, sink):
    m = parts[0][0].max(axis=-1, keepdims=True)
    for s, _ in parts[1:]:
        m = jnp.maximum(m, s.max(axis=-1, keepdims=True))
    if sink is not None:
        m = jnp.maximum(m, sink)
    den = jnp.exp(sink - m) if sink is not None else 0.0
    acc = None
    for s, v in parts:
        p = jnp.exp(s - m)
        den = den + p.sum(axis=-1, keepdims=True)
        o = _bdot(p, v)
        acc = o if acc is None else acc + o
    return acc / den


def _group_sinks(sink_ref, hk, rows):
    head = lax.broadcasted_iota(jnp.int32, (A_GROUP * rows, 1), 0) // rows
    col = jnp.full((A_GROUP * rows, 1), sink_ref[hk * A_GROUP], F32)
    for g in range(1, A_GROUP):
        col = jnp.where(head == g, sink_ref[hk * A_GROUP + g], col)
    return col


def _attn_a_prompt_kernel(sink_ref, q_ref, kv_ref, gq_ref, gk_ref, o_ref, kn_ref):
    scale = A_HEAD_DIM ** -0.5
    q = q_ref[...]
    kv = kv_ref[...]
    t = q.shape[0]
    outs = []
    kns = []
    for hk in range(A_KV_HEADS):
        k = _rms(kv[:, hk * A_HEAD_DIM:(hk + 1) * A_HEAD_DIM], gk_ref[...])
        v = kv[:, (A_KV_HEADS + hk) * A_HEAD_DIM:(A_KV_HEADS + hk + 1) * A_HEAD_DIM]
        kns.append(k)
        qg = jnp.concatenate([_rms(q[:, h * A_HEAD_DIM:(h + 1) * A_HEAD_DIM], gq_ref[...])
                              for h in range(hk * A_GROUP, (hk + 1) * A_GROUP)], axis=0)
        s = _bdot_nt(qg, k) * scale
        o = _softmax_attend([(s, v)], _group_sinks(sink_ref, hk, t))
        outs += [o[g * t:(g + 1) * t] for g in range(A_GROUP)]
    o_ref[...] = jnp.concatenate(outs, axis=-1).astype(o_ref.dtype)
    kn_ref[...] = jnp.concatenate(kns, axis=-1)


def _attn_a_prompt(proj, n_seq, t, sink, gq, gk):
    qw = A_HEADS * A_HEAD_DIM
    kvw = 2 * A_KV_HEADS * A_HEAD_DIM
    return pl.pallas_call(
        _attn_a_prompt_kernel,
        grid=(n_seq,),
        in_specs=[pl.BlockSpec(memory_space=pltpu.SMEM),
                  pl.BlockSpec((t, qw), lambda b: (b, 0)),
                  pl.BlockSpec((t, kvw), lambda b: (b, qw // kvw)),
                  pl.BlockSpec((1, A_HEAD_DIM), lambda b: (0, 0)),
                  pl.BlockSpec((1, A_HEAD_DIM), lambda b: (0, 0))],
        out_specs=[pl.BlockSpec((t, qw), lambda b: (b, 0)),
                   pl.BlockSpec((t, A_KV_HEADS * A_HEAD_DIM), lambda b: (b, 0))],
        out_shape=[jax.ShapeDtypeStruct((n_seq * t, qw), BF16),
                   jax.ShapeDtypeStruct((n_seq * t, A_KV_HEADS * A_HEAD_DIM), F32)],
        compiler_params=_params(("arbitrary",)),
        name="attn_a_context",
    )(sink, proj, proj, gq.reshape(1, -1), gk.reshape(1, -1))


def _attn_a_sample_kernel(sink_ref, q_ref, kv_ref, kc_ref, vc_ref, gq_ref, gk_ref, cos_ref, sin_ref,
                          o_ref, *, t):
    scale = A_HEAD_DIM ** -0.5
    i = pl.program_id(1)
    win = 3 * A_BLOCK
    q0 = pl.multiple_of(i * A_BLOCK, A_BLOCK)
    k0 = pl.multiple_of(jnp.clip((i - 1) * A_BLOCK, 0, t - win), A_BLOCK)
    q = q_ref[...]
    kv = kv_ref[pl.ds(k0, win), :]
    cq, sq = cos_ref[pl.ds(q0, A_BLOCK), :], sin_ref[pl.ds(q0, A_BLOCK), :]
    ck, sk = cos_ref[pl.ds(k0, win), :], sin_ref[pl.ds(k0, win), :]
    rows = A_GROUP * A_BLOCK
    qpos = q0 + lax.broadcasted_iota(jnp.int32, (rows, win), 0) % A_BLOCK
    kpos = k0 + lax.broadcasted_iota(jnp.int32, (rows, win), 1)
    mask = jnp.abs(qpos - kpos) <= A_WINDOW
    outs = []
    for hk in range(A_KV_HEADS):
        k = _rope(_rms(kv[:, hk * A_HEAD_DIM:(hk + 1) * A_HEAD_DIM], gk_ref[...]), ck, sk)
        v = kv[:, (A_KV_HEADS + hk) * A_HEAD_DIM:(A_KV_HEADS + hk + 1) * A_HEAD_DIM]
        qg = jnp.concatenate([_rope(_rms(q[:, h * A_HEAD_DIM:(h + 1) * A_HEAD_DIM], gq_ref[...]), cq, sq)
                              for h in range(hk * A_GROUP, (hk + 1) * A_GROUP)], axis=0)
        s1 = jnp.where(mask, _bdot_nt(qg, k) * scale, NEG_INF)
        s2 = _bdot_nt(qg, kc_ref[hk]) * scale
        o = _softmax_attend([(s1, v), (s2, vc_ref[hk])], _group_sinks(sink_ref, hk, A_BLOCK))
        outs += [o[g * A_BLOCK:(g + 1) * A_BLOCK] for g in range(A_GROUP)]
    o_ref[...] = jnp.concatenate(outs, axis=-1).astype(o_ref.dtype)


def _attn_a_sample(proj, row0, n_seq, t, k_ctx, v_ctx, sink, gq, gk):
    qw = A_HEADS * A_HEAD_DIM
    kvw = 2 * A_KV_HEADS * A_HEAD_DIM
    nqb = t // A_BLOCK
    cos, sin = _rope_tables(t, A_HEAD_DIM)
    past = k_ctx.shape[2]
    return pl.pallas_call(
        functools.partial(_attn_a_sample_kernel, t=t),
        grid=(n_seq, nqb),
        in_specs=[pl.BlockSpec(memory_space=pltpu.SMEM),
                  pl.BlockSpec((A_BLOCK, qw), lambda b, i: (row0 // A_BLOCK + b * nqb + i, 0)),
                  pl.BlockSpec((t, kvw), lambda b, i: (row0 // t + b, qw // kvw)),
                  pl.BlockSpec((None, A_KV_HEADS, past, A_HEAD_DIM), lambda b, i: (b, 0, 0, 0)),
                  pl.BlockSpec((None, A_KV_HEADS, past, A_HEAD_DIM), lambda b, i: (b, 0, 0, 0)),
                  pl.BlockSpec((1, A_HEAD_DIM), lambda b, i: (0, 0)),
                  pl.BlockSpec((1, A_HEAD_DIM), lambda b, i: (0, 0)),
                  pl.BlockSpec((t, A_HEAD_DIM), lambda b, i: (0, 0)),
                  pl.BlockSpec((t, A_HEAD_DIM), lambda b, i: (0, 0))],
        out_specs=pl.BlockSpec((A_BLOCK, qw), lambda b, i: (b * nqb + i, 0)),
        out_shape=jax.ShapeDtypeStruct((n_seq * t, qw), BF16),
        compiler_params=_params(("arbitrary", "arbitrary")),
        name="attn_a_latent",
    )(sink, proj, proj, k_ctx, v_ctx, gq.reshape(1, -1), gk.reshape(1, -1), cos, sin)


def _per_head_lanes(x, fn):
    lane = lax.broadcasted_iota(jnp.int32, x.shape, 1)
    lo = fn(x[:, :B_DK])
    hi = fn(x[:, B_DK:])
    return jnp.where(lane < B_DK, lo, hi)


def _conv_silu(x, w):
    t = x.shape[0]
    row = lax.broadcasted_iota(jnp.int32, x.shape, 0)
    prev = jnp.where(row == 0, 0.0, pltpu.roll(x, 1, 0))
    nxt = jnp.where(row == t - 1, 0.0, pltpu.roll(x, t - 1, 0))
    return _silu(prev * w[0:1, :] + x * w[1:2, :] + nxt * w[2:3, :])


M_SAME, M_TRI, M_TRI_T, M_STRICT, M_BDIAG = range(5)


def _delta_masks(r):
    c = B_CHUNK
    ii = lax.broadcasted_iota(jnp.int32, (r, r), 0)
    jj = lax.broadcasted_iota(jnp.int32, (r, r), 1)
    same = (ii // c) == (jj // c)
    ahead = jnp.where(ii < r // 2, ii - jj, jj - ii)
    tri = jnp.where(same, ahead, -1) >= 0
    tri_t = jnp.where(same, ahead, 1) <= 0
    strict = jnp.where(same, ahead, -1) > 0
    bdiag = (ii // 16) == (jj // 16)
    return [x.astype(F32) for x in (same, tri, tri_t, strict, bdiag)]


def _delta_prepare(q, k, v, g_col, g_row, beta, mask_ref):
    c = B_CHUNK
    r = q.shape[0]
    dot = functools.partial(jnp.dot, preferred_element_type=F32)
    gc_col = jnp.sum(mask_ref[M_TRI] * g_row, axis=1, keepdims=True)
    gc_row = jnp.sum(mask_ref[M_TRI_T] * g_col, axis=0, keepdims=True)
    g_tot = jnp.sum(mask_ref[M_SAME] * g_row, axis=1, keepdims=True)
    ex = jnp.exp((gc_col - gc_row) * mask_ref[M_TRI])
    kb = k * beta
    qk = _bdot_nt(jnp.concatenate([kb, q], axis=0), k)
    m = qk[:r] * (ex * mask_ref[M_STRICT])
    aqk = qk[r:] * (ex * mask_ref[M_TRI])
    dg = m * mask_ref[M_BDIAG]
    off = m - dg
    n1 = -dg
    n1b = n1.astype(BF16)
    n2 = dot(n1b, n1b)
    n2b = n2.astype(BF16)
    t = dot(jnp.concatenate([n1b, n2b], axis=0), n2b)
    xs = n1 + n2 + t[:r]
    n4 = t[r:]
    n4b = n4.astype(BF16)
    t = dot(jnp.concatenate([xs.astype(BF16), n4b], axis=0), n4b)
    xs = xs + n4 + t[:r]
    n8 = t[r:]
    xs = xs + n8 + dot(xs.astype(BF16), n8.astype(BF16))
    xsb = xs.astype(BF16)
    f = -(off + dot(xsb, off.astype(BF16)))
    fb = f.astype(BF16)
    t = dot(fb, jnp.concatenate([xsb, fb], axis=1))
    ys = xs + f + t[:, :r]
    f2 = t[:, r:]
    ts = ys + f2 + dot(f2.astype(BF16), ys.astype(BF16))
    egc = jnp.exp(gc_col)
    rhs = jnp.concatenate([v * beta, kb * egc], axis=-1)
    uw = rhs + _bdot(ts, rhs)

    def block_diag(x):
        return jnp.concatenate([x] * (r // c), axis=-1) * mask_ref[M_SAME]

    kd = k * jnp.exp(g_tot - gc_col)
    kd_t = jnp.concatenate([kd, jnp.zeros_like(kd)], axis=-1).T[:c]
    kd_t = jnp.concatenate([kd_t] * (r // c), axis=0) * mask_ref[M_SAME]
    e_tot = jnp.broadcast_to(jnp.exp(g_tot), v.shape)
    return uw[:, :B_DV], e_tot, block_diag(uw[:, B_DV:]), aqk, block_diag(q * egc), kd_t


def _deltanet_kernel(alog_ref, dtb_ref, q_ref, k_ref, v_ref, z_ref, cwq_ref, cwk_ref, cwv_ref,
                     tail_ref, tailt_ref, onorm_ref, s0f_ref, s0b_ref,
                     o_ref, sf_ref, sb_ref,
                     qc_ref, kc_ref, vc_ref, oacc_ref, u_ref, et_ref, w_ref, aqk_ref, qg_ref, kdt_ref,
                     mask_ref, *, n_chunks):
    hp = pl.program_id(1)
    c = B_CHUNK

    @pl.when(jnp.logical_and(pl.program_id(0) == 0, hp == 0))
    def _():
        for i, x in enumerate(_delta_masks(4 * c)):
            mask_ref[i] = x

    def l2n(x):
        ss = _per_head_lanes(x * x, lambda a: jnp.sum(a, axis=-1, keepdims=True))
        return x * lax.rsqrt(ss + EPS)

    qc_ref[...] = l2n(_conv_silu(q_ref[...], cwq_ref[...])) * (B_DK ** -0.5)
    kc_ref[...] = l2n(_conv_silu(k_ref[...], cwk_ref[...]))
    vc_ref[...] = _conv_silu(v_ref[...], cwv_ref[...])
    oacc_ref[...] = jnp.zeros_like(oacc_ref)

    lane32 = lax.broadcasted_iota(jnp.int32, (c, 4 * B_HEADS), 1)

    def gates(chunk, d, head):
        tail = tail_ref[pl.ds(pl.multiple_of(chunk * c, c), c), :]
        ia = 2 * d * B_HEADS + head
        ib = ia + B_HEADS
        a_col = jnp.sum(jnp.where(lane32 == ia, tail, 0.0), axis=1, keepdims=True)
        b_col = jnp.sum(jnp.where(lane32 == ib, tail, 0.0), axis=1, keepdims=True)
        a_row = tailt_ref[chunk, pl.ds(ia, 1), :]
        na = -jnp.exp(alog_ref[d, head])
        bias = dtb_ref[d, head]
        g_col = na * jax.nn.softplus(a_col + bias)
        g_row = na * jax.nn.softplus(a_row + bias)
        return g_col, g_row, jax.nn.sigmoid(b_col)

    def stacked(ref, chunks):
        parts = []
        for chunk in chunks:
            x = ref[pl.ds(pl.multiple_of(chunk * c, c), c), :]
            parts += [x[:, :B_DK], x[:, B_DK:]]
        return jnp.concatenate(parts, axis=0)

    def prepare(j, carry):
        chunks = (j, n_chunks - 1 - j)
        gs = [gates(chunks[d], d, 2 * hp + hh) for d in range(2) for hh in range(2)]
        g_col = jnp.concatenate([g[0] for g in gs], axis=0)
        g_row = jnp.concatenate([g[1] for g in gs], axis=1)
        beta = jnp.concatenate([g[2] for g in gs], axis=0)
        outs = _delta_prepare(stacked(qc_ref, chunks), stacked(kc_ref, chunks), stacked(vc_ref, chunks),
                              g_col, g_row, beta, mask_ref)
        for ref, x in zip((u_ref, et_ref, w_ref, aqk_ref, qg_ref, kdt_ref), outs):
            ref[j] = x.astype(ref.dtype)
        return carry

    lax.fori_loop(0, n_chunks, prepare, 0, unroll=4)

    def scan(j, s):
        sb = s.astype(BF16)
        delta = u_ref[j] - jnp.dot(w_ref[j], sb, preferred_element_type=F32)
        db = delta.astype(BF16)
        o = (jnp.dot(qg_ref[j], sb, preferred_element_type=F32)
             + jnp.dot(aqk_ref[j], db, preferred_element_type=F32))
        for d, chunk in enumerate((j, n_chunks - 1 - j)):
            rows = pl.ds(pl.multiple_of(chunk * c, c), c)
            oacc_ref[rows, :] += jnp.concatenate([o[2 * d * c:(2 * d + 1) * c],
                                                  o[(2 * d + 1) * c:(2 * d + 2) * c]], axis=-1)
        return s * et_ref[j] + jnp.dot(kdt_ref[j], db, preferred_element_type=F32)

    init = jnp.concatenate([s0f_ref[0], s0f_ref[1], s0b_ref[0], s0b_ref[1]], axis=0)
    fin = lax.fori_loop(0, n_chunks, scan, init)
    sf_ref[0], sf_ref[1], sb_ref[0], sb_ref[1] = (fin[i * B_DK:(i + 1) * B_DK] for i in range(4))

    o = oacc_ref[...]
    ms = _per_head_lanes(o * o, lambda a: jnp.mean(a, axis=-1, keepdims=True))
    o_ref[...] = (o * lax.rsqrt(ms + EPS) * onorm_ref[...] * _silu(z_ref[...])).astype(o_ref.dtype)


def _deltanet(proj, tail, row0, n_seq, t, conv_w, a_log, dt_bias, o_norm, s0_f, s0_b):
    c = B_CHUNK
    n_chunks = t // c
    lw = 2 * B_DK
    col_q = (A_HEADS + 2 * A_KV_HEADS) * A_HEAD_DIM // lw
    nhp = B_HEADS // 2
    rows = tail[row0:row0 + n_seq * t]
    tail_t = rows.reshape(n_seq, n_chunks, c, 4 * B_HEADS).transpose(0, 1, 3, 2)
    onorm2 = jnp.concatenate([o_norm, o_norm]).reshape(1, lw)
    b0 = row0 // t
    seq_blk = lambda off: pl.BlockSpec((t, lw), lambda b, h: (b0 + b, col_q + off + h))
    cw_blk = lambda off: pl.BlockSpec((3, lw), lambda b, h: (0, off + h))
    st_blk = pl.BlockSpec((None, 2, B_DK, B_DV), lambda b, h: (b, h, 0, 0))
    return pl.pallas_call(
        functools.partial(_deltanet_kernel, n_chunks=n_chunks),
        grid=(n_seq, nhp),
        in_specs=[pl.BlockSpec(memory_space=pltpu.SMEM), pl.BlockSpec(memory_space=pltpu.SMEM),
                  seq_blk(0), seq_blk(nhp), seq_blk(2 * nhp), seq_blk(3 * nhp),
                  cw_blk(0), cw_blk(nhp), cw_blk(2 * nhp),
                  pl.BlockSpec((t, 4 * B_HEADS), lambda b, h: (b, 0)),
                  pl.BlockSpec((None, n_chunks, 4 * B_HEADS, c), lambda b, h: (b, 0, 0, 0)),
                  pl.BlockSpec((1, lw), lambda b, h: (0, 0)),
                  st_blk, st_blk],
        out_specs=[pl.BlockSpec((t, lw), lambda b, h: (b, h)), st_blk, st_blk],
        out_shape=[jax.ShapeDtypeStruct((n_seq * t, B_HEADS * B_DV), BF16),
                   jax.ShapeDtypeStruct((n_seq, B_HEADS, B_DK, B_DV), F32),
                   jax.ShapeDtypeStruct((n_seq, B_HEADS, B_DK, B_DV), F32)],
        scratch_shapes=[pltpu.VMEM((t, lw), F32)] * 4 + [pltpu.VMEM((n_chunks, 4 * c, B_DV), F32)] * 2
        + [pltpu.VMEM((n_chunks, 4 * c, 4 * c), BF16)] * 4 + [pltpu.VMEM((5, 4 * c, 4 * c), F32)],
        compiler_params=_params(("arbitrary", "arbitrary")),
        name="deltanet",
    )(a_log, dt_bias, proj, proj, proj, proj, conv_w, conv_w, conv_w, rows, tail_t, onorm2, s0_f, s0_b)


def _lora_norm_kernel(p_ref, gq_ref, gkv_ref, cq_ref, ckv_ref, ckvb_ref):
    p = p_ref[...]
    cq_ref[...] = _rms(p[:, :C_Q_LORA], gq_ref[...]).astype(cq_ref.dtype)
    ckv = _rms(p[:, C_Q_LORA:C_Q_LORA + C_KV_LORA], gkv_ref[...])
    ckv_ref[...] = ckv
    ckvb_ref[...] = ckv.astype(ckvb_ref.dtype)


def _lora_norm(p1, gq, gkv, tm=512):
    n, w = p1.shape
    return pl.pallas_call(
        _lora_norm_kernel,
        grid=(n // tm,),
        in_specs=[pl.BlockSpec((tm, w), lambda i: (i, 0)),
                  pl.BlockSpec((1, C_Q_LORA), lambda i: (0, 0)),
                  pl.BlockSpec((1, C_KV_LORA), lambda i: (0, 0))],
        out_specs=[pl.BlockSpec((tm, C_Q_LORA), lambda i: (i, 0)),
                   pl.BlockSpec((tm, C_KV_LORA), lambda i: (i, 0)),
                   pl.BlockSpec((tm, C_KV_LORA), lambda i: (i, 0))],
        out_shape=[jax.ShapeDtypeStruct((n, C_Q_LORA), BF16),
                   jax.ShapeDtypeStruct((n, C_KV_LORA), F32),
                   jax.ShapeDtypeStruct((n, C_KV_LORA), BF16)],
        compiler_params=_params(("arbitrary",)),
        name="lora_norm",
    )(p1, gq.reshape(1, -1), gkv.reshape(1, -1))


def _mla_head_k(kv, kr, kr_ss, h, gk):
    kn = kv[:, h * (C_NOPE + C_V):h * (C_NOPE + C_V) + C_NOPE]
    v = kv[:, h * (C_NOPE + C_V) + C_NOPE:(h + 1) * (C_NOPE + C_V)]
    rn = lax.rsqrt((jnp.sum(kn * kn, axis=-1, keepdims=True) + kr_ss) / C_QK + EPS)
    return kn * rn * gk[:, :C_NOPE], kr * rn, v


C_STACK = 4


def _place(x, i, n):
    t, w = x.shape
    parts = ([jnp.zeros((t, i * w), x.dtype)] if i else []) + [x]
    if i < n - 1:
        parts.append(jnp.zeros((t, (n - 1 - i) * w), x.dtype))
    return jnp.concatenate(parts, axis=-1)


def _by_block(cols, lane, w):
    out = cols[-1]
    for i in reversed(range(len(cols) - 1)):
        out = jnp.where(lane < (i + 1) * w, cols[i], out)
    return out


def _attn_c_prompt_kernel(q_ref, kv_ref, p_ref, gq_ref, gk_ref, o_ref):
    scale = C_QK ** -0.5
    n = C_STACK
    q = q_ref[...]
    kv = kv_ref[...]
    gk = gk_ref[...]
    t = q.shape[0]
    kr_raw = p_ref[...][:, C_Q_LORA + C_KV_LORA:]
    kr_ss = jnp.sum(kr_raw * kr_raw, axis=-1, keepdims=True)
    kr_g = kr_raw * gk[:, C_NOPE:]
    lane_q = lax.broadcasted_iota(jnp.int32, (t, n * C_QK), 1)
    lane_o = lax.broadcasted_iota(jnp.int32, (t, n * C_V), 1)
    outs = []
    for grp in range(C_HEADS // n):
        qs = q[:, grp * n * C_QK:(grp + 1) * n * C_QK]
        rn = [lax.rsqrt(jnp.mean(qs[:, i * C_QK:(i + 1) * C_QK] ** 2, axis=-1, keepdims=True) + EPS)
              for i in range(n)]
        qs = qs * _by_block(rn, lane_q, C_QK) * gq_ref[...]
        k_rows, v_rows = [], []
        for i in range(n):
            kn, kr, v = _mla_head_k(kv, kr_g, kr_ss, grp * n + i, gk)
            k_rows.append(_place(jnp.concatenate([kn, kr], axis=-1), i, n))
            v_rows.append(_place(v, i, n))
        s = _bdot_nt(qs, jnp.concatenate(k_rows, axis=0)) * scale
        ps, rden = [], []
        for i in range(n):
            si = s[:, i * t:(i + 1) * t]
            pi = jnp.exp(si - si.max(axis=-1, keepdims=True))
            ps.append(pi)
            rden.append(1.0 / pi.sum(axis=-1, keepdims=True))
        o = _bdot(jnp.concatenate(ps, axis=-1), jnp.concatenate(v_rows, axis=0))
        outs.append(o * _by_block(rden, lane_o, C_V))
    o_ref[...] = jnp.concatenate(outs, axis=-1).astype(o_ref.dtype)


def _attn_c_prompt(q, kv, p1, n_seq, t, gq, gk):
    return pl.pallas_call(
        _attn_c_prompt_kernel,
        grid=(n_seq,),
        in_specs=[pl.BlockSpec((t, q.shape[1]), lambda b: (b, 0)),
                  pl.BlockSpec((t, kv.shape[1]), lambda b: (b, 0)),
                  pl.BlockSpec((t, p1.shape[1]), lambda b: (b, 0)),
                  pl.BlockSpec((1, C_STACK * C_QK), lambda b: (0, 0)),
                  pl.BlockSpec((1, C_QK), lambda b: (0, 0))],
        out_specs=pl.BlockSpec((t, C_HEADS * C_V), lambda b: (b, 0)),
        out_shape=jax.ShapeDtypeStruct((n_seq * t, C_HEADS * C_V), BF16),
        compiler_params=_params(("arbitrary",), VMEM_LIMIT),
        name="attn_c_context",
    )(q, kv, p1, jnp.tile(gq, C_STACK).reshape(1, -1), gk.reshape(1, -1))


def _attn_c_sample_kernel(q_ref, kv_ref, p_ref, kvc_ref, krc_ref, gq_ref, gk_ref, cos_ref, sin_ref,
                          o_ref, *, tq):
    scale = C_QK ** -0.5
    i = pl.program_id(1)
    q0 = pl.multiple_of(i * tq, tq)
    q = q_ref[...]
    kv = kv_ref[...]
    kvc = kvc_ref[...]
    gq, gk = gq_ref[...], gk_ref[...]
    cq, sq = cos_ref[pl.ds(q0, tq), :], sin_ref[pl.ds(q0, tq), :]
    kr_raw = p_ref[...][:, C_Q_LORA + C_KV_LORA:]
    krc_raw = krc_ref[...]
    kr_all = jnp.concatenate([_rope(kr_raw * gk[:, C_NOPE:], cos_ref[...], sin_ref[...]),
                              krc_raw * gk[:, C_NOPE:]], axis=0)
    kr_ss = jnp.concatenate([jnp.sum(kr_raw * kr_raw, axis=-1, keepdims=True),
                             jnp.sum(krc_raw * krc_raw, axis=-1, keepdims=True)], axis=0)
    kr_pad = jnp.concatenate([jnp.zeros((kr_all.shape[0], C_NOPE), F32), kr_all], axis=-1)
    lane = lax.broadcasted_iota(jnp.int32, kr_pad.shape, 1)
    outs = []
    for h in range(C_HEADS):
        qh = q[:, h * C_QK:(h + 1) * C_QK]
        qh = qh * lax.rsqrt(jnp.mean(qh * qh, axis=-1, keepdims=True) + EPS) * gq
        qh = jnp.concatenate([qh[:, :C_NOPE], _rope(qh[:, C_NOPE:], cq, sq)], axis=-1)
        c0 = h * (C_NOPE + C_V)
        kn = jnp.concatenate([kv[:, c0:c0 + C_QK], kvc[:, c0:c0 + C_QK]], axis=0)
        v = jnp.concatenate([kv[:, c0 + C_NOPE:c0 + C_NOPE + C_V], kvc[:, c0 + C_NOPE:c0 + C_NOPE + C_V]],
                            axis=0)
        ss = jnp.sum(kn[:, :C_NOPE] * kn[:, :C_NOPE], axis=-1, keepdims=True) + kr_ss
        k = jnp.where(lane < C_NOPE, kn * gk, kr_pad) * lax.rsqrt(ss / C_QK + EPS)
        s = _bdot_nt(qh, k) * scale
        outs.append(_softmax_attend([(s, v)], None))
    o_ref[...] = jnp.concatenate(outs, axis=-1).astype(o_ref.dtype)


def _attn_c_sample(q, kv, p1, row0, n_seq, t, kr_ctx, gq, gk, tq=256):
    n = p1.shape[0]
    past = kr_ctx.shape[1]
    nq = t // tq
    cos, sin = _rope_tables(t, C_ROPE)
    return pl.pallas_call(
        functools.partial(_attn_c_sample_kernel, tq=tq),
        grid=(n_seq, nq),
        in_specs=[pl.BlockSpec((tq, q.shape[1]), lambda b, i: (row0 // tq + b * nq + i, 0)),
                  pl.BlockSpec((t, kv.shape[1]), lambda b, i: (row0 // t + b, 0)),
                  pl.BlockSpec((t, p1.shape[1]), lambda b, i: (row0 // t + b, 0)),
                  pl.BlockSpec((past, kv.shape[1]), lambda b, i: (n // past + b, 0)),
                  pl.BlockSpec((None, past, C_ROPE), lambda b, i: (b, 0, 0)),
                  pl.BlockSpec((1, C_QK), lambda b, i: (0, 0)),
                  pl.BlockSpec((1, C_QK), lambda b, i: (0, 0)),
                  pl.BlockSpec((t, C_ROPE), lambda b, i: (0, 0)),
                  pl.BlockSpec((t, C_ROPE), lambda b, i: (0, 0))],
        out_specs=pl.BlockSpec((tq, C_HEADS * C_V), lambda b, i: (b * nq + i, 0)),
        out_shape=jax.ShapeDtypeStruct((n_seq * t, C_HEADS * C_V), BF16),
        compiler_params=_params(("arbitrary", "arbitrary"), VMEM_LIMIT),
        name="attn_c_latent",
    )(q, kv, p1, kv, kr_ctx, gq.reshape(1, -1), gk.reshape(1, -1), cos, sin)


ROUTE_TILE = 512
ROUTE_BLOCK = 512
SLOT_ALIGN = 8
LOCAL_SLOTS = 2304
META_LANES = 128
FF_CHUNK = 256


def _router_kernel(x_ref, g_ref, sc_ref, sh_ref, wr_ref, br_ref, h_ref, meta_ref, cntb_ref, cnt_ref,
                   run_ref):
    @pl.when(pl.program_id(0) == 0)
    def _():
        run_ref[...] = jnp.zeros_like(run_ref)

    h = _rms(x_ref[...], g_ref[...]) * (1 + sc_ref[...]) + sh_ref[...]
    h_ref[...] = h.astype(h_ref.dtype)
    tm = h.shape[0]
    logits = _hdot(h, wr_ref[...]) + br_ref[...]
    lane = lax.broadcasted_iota(jnp.int32, logits.shape, 1)
    work = logits
    picks, tops = [], []
    for _ in range(TOP_K):
        m = work.max(axis=-1, keepdims=True)
        first = jnp.min(jnp.where(work == m, lane, N_EXPERTS), axis=-1, keepdims=True)
        pick = lane == first
        picks.append(pick)
        tops.append(m)
        work = jnp.where(pick, -jnp.inf, work)
    sel = sum(p.astype(F32) for p in picks)
    earlier = (lax.broadcasted_iota(jnp.int32, (tm, tm), 0)
               > lax.broadcasted_iota(jnp.int32, (tm, tm), 1)).astype(BF16)
    inside = jnp.dot(earlier, sel.astype(BF16), preferred_element_type=F32)
    cnt = jnp.sum(sel, axis=0, keepdims=True)
    run = jnp.ceil(cnt / SLOT_ALIGN) * SLOT_ALIGN
    lower_e = (lax.broadcasted_iota(jnp.int32, (N_EXPERTS, N_EXPERTS), 0)
               < lax.broadcasted_iota(jnp.int32, (N_EXPERTS, N_EXPERTS), 1)).astype(BF16)
    start = jnp.dot(jnp.broadcast_to(run, (8, N_EXPERTS)).astype(BF16), lower_e,
                    preferred_element_type=F32)[0:1]
    slot = start + inside
    ws = [jnp.exp(t - tops[0]) for t in tops]
    den = sum(ws)
    mlane = lax.broadcasted_iota(jnp.int32, (tm, META_LANES), 1)
    meta = jnp.zeros((tm, META_LANES), F32)
    for k in range(TOP_K):
        meta = jnp.where(mlane == k, jnp.sum(jnp.where(picks[k], slot, 0.0), axis=-1, keepdims=True), meta)
        meta = jnp.where(mlane == TOP_K + k, ws[k] / den, meta)
    meta_ref[...] = meta
    cntb_ref[...] = cnt
    run_ref[...] += run
    cnt_ref[...] = run_ref[...]


def _router(x, gain, scale, shift, w_router, b_router, layer, n_p, t_s):
    n, d = x.shape
    tm = ROUTE_BLOCK
    e = w_router.shape[-1]
    group = _group_of_tile(tm, n_p, t_s)
    return pl.pallas_call(
        _router_kernel,
        grid=(n // tm,),
        in_specs=[pl.BlockSpec((tm, d), lambda i: (i, 0)),
                  pl.BlockSpec((1, d), lambda i: (0, 0)),
                  pl.BlockSpec((None, 1, d), lambda i: (group(i), 0, 0)),
                  pl.BlockSpec((None, 1, d), lambda i: (group(i), 0, 0)),
                  pl.BlockSpec((None, d, e), lambda i: (layer, 0, 0)),
                  pl.BlockSpec((None, 1, e), lambda i: (layer, 0, 0))],
        out_specs=[pl.BlockSpec((tm, d), lambda i: (i, 0)),
                   pl.BlockSpec((tm, META_LANES), lambda i: (i, 0)),
                   pl.BlockSpec((None, 1, e), lambda i: (i, 0, 0)),
                   pl.BlockSpec((1, e), lambda i: (0, 0))],
        out_shape=[jax.ShapeDtypeStruct((n, d), BF16),
                   jax.ShapeDtypeStruct((n, META_LANES), F32),
                   jax.ShapeDtypeStruct((n // tm, 1, e), F32),
                   jax.ShapeDtypeStruct((1, e), F32)],
        scratch_shapes=[pltpu.VMEM((1, e), F32)],
        compiler_params=_params(("arbitrary",)),
        name="router",
    )(x, gain.reshape(1, d), scale.reshape(N_GROUPS, 1, d), shift.reshape(N_GROUPS, 1, d),
      w_router, b_router.reshape(-1, 1, e))


def _route_plan(cnt_blk, cnt_tot, n_tiles, min_tiles):
    cb = cnt_blk[:, 0, :].astype(jnp.int32)
    run = (cb + SLOT_ALIGN - 1) // SLOT_ALIGN * SLOT_ALIGN
    counts = cnt_tot[0].astype(jnp.int32)
    padded = (counts + ROUTE_TILE - 1) // ROUTE_TILE * ROUTE_TILE
    ends = jnp.cumsum(padded)
    offs = ends - padded
    gstart = offs[None, :] + jnp.cumsum(run, axis=0) - run
    lstart = jnp.cumsum(run, axis=1) - run
    tile_start = jnp.arange(n_tiles, dtype=jnp.int32) * ROUTE_TILE
    n_valid = ends[-1] // ROUTE_TILE
    te = jnp.minimum(jnp.sum(ends[None, :] <= tile_start[:, None], axis=1), N_EXPERTS - 1)
    te = jnp.where(tile_start < ends[-1], te, te[jnp.maximum(n_valid - 1, 0)]).astype(jnp.int32)
    rows = jnp.clip((offs + counts)[te] - tile_start, 0, ROUTE_TILE).astype(jnp.int32)
    ragged = jnp.where(counts % ROUTE_TILE != 0, ends // ROUTE_TILE - 1, -1)
    tail = jnp.arange(min_tiles, n_tiles, dtype=jnp.int32)
    fill = jnp.concatenate([ragged, jnp.where(tail >= n_valid, tail, -1)]).astype(jnp.int32)
    runs = (gstart.reshape(-1).astype(jnp.int32), lstart.reshape(-1).astype(jnp.int32),
            (run // SLOT_ALIGN).reshape(-1).astype(jnp.int32),
            (jnp.sum(run, axis=1) // SLOT_ALIGN).astype(jnp.int32))
    return runs, te, rows, n_valid.reshape(1).astype(jnp.int32), fill


RUN_PIECE = 4
WAIT_PIECE = 8


def _start_runs(blk, gstart_ref, lstart_ref, nch_ref, copy):
    big = RUN_PIECE * SLOT_ALIGN

    def per_expert(e, carry):
        idx = blk * N_EXPERTS + e
        g0, l0, n = gstart_ref[idx], lstart_ref[idx], nch_ref[idx]
        n_big = n // RUN_PIECE

        def large(i, c2):
            off = pl.multiple_of(i * big, big)
            copy(pl.multiple_of(l0 + off, SLOT_ALIGN), pl.multiple_of(g0 + off, SLOT_ALIGN), big).start()
            return c2

        def small(i, c2):
            off = pl.multiple_of(n_big * big + i * SLOT_ALIGN, SLOT_ALIGN)
            copy(pl.multiple_of(l0 + off, SLOT_ALIGN), pl.multiple_of(g0 + off, SLOT_ALIGN), SLOT_ALIGN).start()
            return c2

        lax.fori_loop(0, n_big, large, 0)
        lax.fori_loop(0, n - n_big * RUN_PIECE, small, 0)
        return carry

    lax.fori_loop(0, N_EXPERTS, per_expert, 0)


def _wait_runs(n_chunks, copy):
    n_big = n_chunks // WAIT_PIECE

    def large(i, carry):
        copy(0, 0, WAIT_PIECE * SLOT_ALIGN).wait()
        return carry

    def small(i, carry):
        copy(0, 0, SLOT_ALIGN).wait()
        return carry

    lax.fori_loop(0, n_big, large, 0)
    lax.fori_loop(0, n_chunks - n_big * WAIT_PIECE, small, 0)


def _dispatch_kernel(fill_ref, gstart_ref, lstart_ref, nch_ref, ntot_ref, meta_ref, h_ref, xs_ref,
                     loc_ref, zero_ref, sem, zsem):
    blk = pl.program_id(0)

    @pl.when(blk == 0)
    def _():
        zero_ref[...] = jnp.zeros_like(zero_ref)

        def fill_copy(j):
            row0 = pl.multiple_of(fill_ref[j] * ROUTE_TILE, ROUTE_TILE)
            return pltpu.make_async_copy(zero_ref, xs_ref.at[pl.ds(row0, ROUTE_TILE), :], zsem)

        def start(j, carry):
            @pl.when(fill_ref[j] >= 0)
            def _():
                fill_copy(j).start()
            return carry

        def wait(j, carry):
            @pl.when(fill_ref[j] >= 0)
            def _():
                fill_copy(j).wait()
            return carry

        lax.fori_loop(0, fill_ref.shape[0], start, 0)
        lax.fori_loop(0, fill_ref.shape[0], wait, 0)

    buf = blk % 2
    last = pl.num_programs(0) - 1

    def copier(b):
        def copy(l0, g0, rows):
            return pltpu.make_async_copy(loc_ref.at[b, pl.ds(l0, rows), :], xs_ref.at[pl.ds(g0, rows), :],
                                         sem.at[b])
        return copy

    @pl.when(blk >= 2)
    def _():
        _wait_runs(ntot_ref[blk - 2], copier(buf))

    hb = h_ref[...]
    tm = hb.shape[0]
    slots_t = meta_ref[...].T[0:TOP_K]
    rows = 256
    for c0 in range(0, LOCAL_SLOTS, rows):
        slot = (c0 + lax.broadcasted_iota(jnp.int32, (rows, tm), 0)).astype(F32)
        onehot = sum(jnp.where(slot == slots_t[k:k + 1], 1.0, 0.0) for k in range(TOP_K))
        loc_ref[buf, c0:c0 + rows, :] = jnp.dot(onehot.astype(BF16), hb,
                                                preferred_element_type=F32).astype(loc_ref.dtype)

    _start_runs(blk, gstart_ref, lstart_ref, nch_ref, copier(buf))

    @pl.when(blk == last)
    def _():
        @pl.when(blk >= 1)
        def _():
            _wait_runs(ntot_ref[blk - 1], copier(1 - buf))
        _wait_runs(ntot_ref[blk], copier(buf))


def _dispatch(h, meta, runs, fill, n_slots):
    n, d = h.shape
    tm = ROUTE_BLOCK
    grid_spec = pltpu.PrefetchScalarGridSpec(
        num_scalar_prefetch=5,
        grid=(n // tm,),
        in_specs=[pl.BlockSpec((tm, META_LANES), lambda i, *_: (i, 0)),
                  pl.BlockSpec((tm, d), lambda i, *_: (i, 0))],
        out_specs=pl.BlockSpec(memory_space=pl.ANY),
        scratch_shapes=[pltpu.VMEM((2, LOCAL_SLOTS, d), F32), pltpu.VMEM((ROUTE_TILE, d), F32),
                        pltpu.SemaphoreType.DMA((2,)), pltpu.SemaphoreType.DMA(())])
    return pl.pallas_call(
        _dispatch_kernel,
        grid_spec=grid_spec,
        out_shape=jax.ShapeDtypeStruct((n_slots, d), F32),
        compiler_params=_params(("arbitrary",), VMEM_LIMIT),
        name="moe_dispatch",
    )(fill, *runs, meta, h)


def _experts_kernel(te_ref, rows_ref, nv_ref, x_ref, wgu_ref, bgu_ref, wd_ref, bd_ref, y_ref,
                    wgub_ref, wdb_ref, *, d_ff):
    i = pl.program_id(0)
    valid = i < nv_ref[0]
    fresh = jnp.logical_or(i == 0, te_ref[i] != te_ref[jnp.maximum(i - 1, 0)])

    @pl.when(jnp.logical_and(valid, fresh))
    def _():
        wgub_ref[...] = wgu_ref[...].astype(BF16)
        wdb_ref[...] = wd_ref[...].astype(BF16)

    @pl.when(valid)
    def _():
        row = lax.broadcasted_iota(jnp.int32, x_ref.shape, 0)
        x = jnp.where(row < rows_ref[i], x_ref[...], 0.0).astype(BF16)
        acc = None
        for c0 in range(0, d_ff, FF_CHUNK):
            g_cols, u_cols = slice(c0, c0 + FF_CHUNK), slice(d_ff + c0, d_ff + c0 + FF_CHUNK)
            gate = jnp.dot(x, wgub_ref[:, g_cols], preferred_element_type=F32) + bgu_ref[:, g_cols]
            up = jnp.dot(x, wgub_ref[:, u_cols], preferred_element_type=F32) + bgu_ref[:, u_cols]
            gate = jnp.minimum(gate, SWIGLU_LIMIT)
            up = jnp.clip(up, -SWIGLU_LIMIT, SWIGLU_LIMIT)
            act = (up + 1) * gate * jax.nn.sigmoid(SWIGLU_ALPHA * gate)
            part = jnp.dot(act.astype(BF16), wdb_ref[c0:c0 + FF_CHUNK, :], preferred_element_type=F32)
            acc = part if acc is None else acc + part
        y_ref[...] = acc + bd_ref[...]

    @pl.when(jnp.logical_not(valid))
    def _():
        y_ref[...] = jnp.zeros_like(y_ref)


def _experts(xs, te, rows, n_valid, w_gu, b_gu, w_down, b_down, layer):
    n_slots, d = xs.shape
    _, e, _, two_ff = w_gu.shape
    last = lambda i, nv: jnp.minimum(i, nv[0] - 1)
    grid_spec = pltpu.PrefetchScalarGridSpec(
        num_scalar_prefetch=3,
        grid=(n_slots // ROUTE_TILE,),
        in_specs=[pl.BlockSpec((ROUTE_TILE, d), lambda i, te, rw, nv: (last(i, nv), 0)),
                  pl.BlockSpec((None, None, d, two_ff), lambda i, te, rw, nv: (layer, te[i], 0, 0)),
                  pl.BlockSpec((None, None, 1, two_ff), lambda i, te, rw, nv: (layer, te[i], 0, 0)),
                  pl.BlockSpec((None, None, two_ff // 2, d), lambda i, te, rw, nv: (layer, te[i], 0, 0)),
                  pl.BlockSpec((None, None, 1, d), lambda i, te, rw, nv: (layer, te[i], 0, 0))],
        out_specs=pl.BlockSpec((ROUTE_TILE, d), lambda i, te, rw, nv: (i, 0)),
        scratch_shapes=[pltpu.VMEM((d, two_ff), BF16), pltpu.VMEM((two_ff // 2, d), BF16)])
    return pl.pallas_call(
        functools.partial(_experts_kernel, d_ff=two_ff // 2),
        grid_spec=grid_spec,
        out_shape=jax.ShapeDtypeStruct((n_slots, d), F32),
        compiler_params=_params(("arbitrary",), VMEM_LIMIT),
        name="moe_experts",
    )(te, rows, n_valid, xs, w_gu, b_gu.reshape(b_gu.shape[0], e, 1, two_ff), w_down,
      b_down.reshape(b_down.shape[0], e, 1, d))


def _combine_kernel(gstart_ref, lstart_ref, nch_ref, ntot_ref, meta_ref, x_ref, gate_ref, y_ref, *refs,
                    split_tiles):
    *o_refs, loc_ref, sem = refs
    blk = pl.program_id(0)

    buf = blk % 2

    def copier(b):
        def copy(l0, g0, rows):
            return pltpu.make_async_copy(y_ref.at[pl.ds(g0, rows), :], loc_ref.at[b, pl.ds(l0, rows), :],
                                         sem.at[b])
        return copy

    @pl.when(blk == 0)
    def _():
        loc_ref[...] = jnp.zeros_like(loc_ref)
        _start_runs(blk, gstart_ref, lstart_ref, nch_ref, copier(buf))

    @pl.when(blk + 1 < pl.num_programs(0))
    def _():
        _start_runs(blk + 1, gstart_ref, lstart_ref, nch_ref, copier(1 - buf))

    _wait_runs(ntot_ref[blk], copier(buf))

    meta = meta_ref[...]
    tm = meta.shape[0]
    cols = 256
    acc = jnp.zeros(x_ref.shape, F32)
    for c0 in range(0, LOCAL_SLOTS, cols):
        slot = (c0 + lax.broadcasted_iota(jnp.int32, (tm, cols), 1)).astype(F32)
        wts = sum(jnp.where(slot == meta[:, k:k + 1], meta[:, TOP_K + k:TOP_K + k + 1], 0.0)
                  for k in range(TOP_K))
        acc = acc + jnp.dot(wts.astype(BF16), loc_ref[buf, c0:c0 + cols, :].astype(BF16),
                            preferred_element_type=F32)
    out = x_ref[...] + gate_ref[...] * acc
    if split_tiles is None:
        o_refs[0][...] = out
    else:
        @pl.when(blk < split_tiles)
        def _():
            o_refs[0][...] = out

        @pl.when(blk >= split_tiles)
        def _():
            o_refs[1][...] = out


def _combine(y, meta, runs, x, gate, n_p, t_s, split=False):
    n, d = x.shape
    tm = ROUTE_BLOCK
    group = _group_of_tile(tm, n_p, t_s)
    st = n_p // tm
    if split:
        out_specs = [pl.BlockSpec((tm, d), lambda i, *_: (jnp.minimum(i, st - 1), 0)),
                     pl.BlockSpec((tm, d), lambda i, *_: (jnp.maximum(i - st, 0), 0))]
        out_shape = [jax.ShapeDtypeStruct((n_p, d), F32), jax.ShapeDtypeStruct((n - n_p, d), F32)]
    else:
        out_specs = pl.BlockSpec((tm, d), lambda i, *_: (i, 0))
        out_shape = jax.ShapeDtypeStruct((n, d), F32)
    grid_spec = pltpu.PrefetchScalarGridSpec(
        num_scalar_prefetch=4,
        grid=(n // tm,),
        in_specs=[pl.BlockSpec((tm, META_LANES), lambda i, *_: (i, 0)),
                  pl.BlockSpec((tm, d), lambda i, *_: (i, 0)),
                  pl.BlockSpec((None, 1, d), lambda i, *_: (group(i), 0, 0)),
                  pl.BlockSpec(memory_space=pl.ANY)],
        out_specs=out_specs,
        scratch_shapes=[pltpu.VMEM((2, LOCAL_SLOTS, d), F32), pltpu.SemaphoreType.DMA((2,))])
    return pl.pallas_call(
        functools.partial(_combine_kernel, split_tiles=st if split else None),
        grid_spec=grid_spec,
        out_shape=out_shape,
        compiler_params=_params(("arbitrary",), VMEM_LIMIT),
        name="moe_combine",
    )(*runs, meta, x, gate.reshape(N_GROUPS, 1, d), y)


def _moe(x, gain, scale, shift, gate, w_router, b_router, w_gu, b_gu, w_down, b_down, layer, n_p, t_s,
         split=False):
    n = x.shape[0]
    assert LOCAL_SLOTS >= ROUTE_BLOCK * TOP_K + N_EXPERTS * (SLOT_ALIGN - 1) and n % ROUTE_BLOCK == 0
    min_tiles = n * TOP_K // ROUTE_TILE
    max_slots = n * TOP_K + (n // ROUTE_BLOCK) * N_EXPERTS * (SLOT_ALIGN - 1) + N_EXPERTS * (ROUTE_TILE - 1)
    n_tiles = -(-max_slots // ROUTE_TILE)
    h, meta, cnt_blk, cnt_tot = _router(x, gain, scale, shift, w_router, b_router, layer, n_p, t_s)
    runs, te, rows, n_valid, fill = _route_plan(cnt_blk, cnt_tot, n_tiles, min_tiles)
    xs = _dispatch(h, meta, runs, fill, n_tiles * ROUTE_TILE)
    y = _experts(xs, te, rows, n_valid, w_gu, b_gu, w_down, b_down, layer)
    return _combine(y, meta, runs, x, gate, n_p, t_s, split)


def kernel(x_prompt, x_sample, c, cache_a_k, cache_a_v, state_b_fwd, state_b_bwd, cache_c_ckv,
           cache_c_krope, c_ctx, w_mod, b_mod, norm_mix, norm_ffn, e_w_in, e_w_out, e_a_qnorm,
           e_a_knorm, e_a_sink, e_b_conv, e_b_alog, e_b_dtbias, e_b_onorm, o_w_in, o_q_lora_norm,
           o_kv_lora_norm, o_w_uq, o_w_ukv, o_qnorm, o_knorm, o_w_out, moe_w_router, moe_b_router,
           moe_w_gu, moe_b_gu, moe_w_down, moe_b_down):
    bp, tp, d = x_prompt.shape
    bs, ts, _ = x_sample.shape
    depth = w_mod.shape[0]
    n_p, n_s = bp * tp, bs * ts
    n = n_p + n_s
    assert bs + 1 <= N_GROUPS and ts % 512 == 0 and n_p % ts == 0

    x = (x_prompt.reshape(n_p, d), x_sample.reshape(n_s, d))
    cond = jnp.concatenate([c_ctx[None], c, jnp.zeros((N_GROUPS - 1 - bs, d), F32)], axis=0)
    mod = _adaln(cond, w_mod, b_mod)

    new_a_k, new_a_v, new_b_fwd, new_b_bwd, new_c_ckv, new_c_krope = [], [], [], [], [], []
    for layer in range(depth):
        sh1, sc1, g1, sh2, sc2, g2 = (mod[layer, j] for j in range(6))
        h = _modulate(x, norm_mix[layer], sc1, sh1, n_p, ts)
        i = layer // 2
        if layer % 2 == 0:
            main_w = (A_HEADS + 2 * A_KV_HEADS) * A_HEAD_DIM + 4 * B_HEADS * B_DK
            proj = _matmul((h,), e_w_in, i, 0, main_w // 2, main_w, name="even_in_proj")
            tail = _matmul((h,), e_w_in[i][None, :, main_w:], 0, 0, 4 * B_HEADS, 4 * B_HEADS,
                           name="even_gate_proj")
            oa_p, kn_p = _attn_a_prompt(proj, bp, tp, e_a_sink[i], e_a_qnorm[i], e_a_knorm[i])
            oa_s = _attn_a_sample(proj, n_p, bs, ts, cache_a_k[:, i], cache_a_v[:, i],
                                  e_a_sink[i], e_a_qnorm[i], e_a_knorm[i])
            zeros = jnp.zeros((bp, B_HEADS, B_DK, B_DV), F32)
            ob_p, s_f, s_b = _deltanet(proj, tail, 0, bp, tp, e_b_conv[i], e_b_alog[i],
                                       e_b_dtbias[i], e_b_onorm[i], zeros, zeros)
            ob_s, _, _ = _deltanet(proj, tail, n_p, bs, ts, e_b_conv[i], e_b_alog[i],
                                   e_b_dtbias[i], e_b_onorm[i], state_b_fwd[:, i], state_b_bwd[:, i])
            x = (_matmul_residual([(oa_p, oa_s), (ob_p, ob_s)], e_w_out, i, x, g1, n_p, ts),)
            kw = A_KV_HEADS * A_HEAD_DIM
            new_a_k.append(kn_p.reshape(bp, tp, A_KV_HEADS, A_HEAD_DIM).transpose(0, 2, 1, 3))
            v_p = proj[:n_p, A_HEADS * A_HEAD_DIM + kw:A_HEADS * A_HEAD_DIM + 2 * kw]
            new_a_v.append(v_p.reshape(bp, tp, A_KV_HEADS, A_HEAD_DIM).transpose(0, 2, 1, 3))
            new_b_fwd.append(s_f)
            new_b_bwd.append(s_b)
        else:
            p1 = _matmul((h,), o_w_in, i, 0, o_w_in.shape[-1], o_w_in.shape[-1], name="odd_in_proj")
            cq, ckv, ckv_b = _lora_norm(p1, o_q_lora_norm[i], o_kv_lora_norm[i])
            q = _matmul((cq,), o_w_uq, i, 0, o_w_uq.shape[-1] // 2, o_w_uq.shape[-1], name="odd_uq")
            ckv_ctx = cache_c_ckv[:, i].reshape(-1, C_KV_LORA).astype(BF16)
            kv = _matmul((ckv_b, ckv_ctx), o_w_ukv, i, 0, o_w_ukv.shape[-1] // 2, o_w_ukv.shape[-1],
                         name="odd_ukv")
            o_p = _attn_c_prompt(q, kv, p1, bp, tp, o_qnorm[i], o_knorm[i])
            o_s = _attn_c_sample(q, kv, p1, n_p, bs, ts, cache_c_krope[:, i], o_qnorm[i], o_knorm[i])
            x = (_matmul_residual([(o_p, o_s)], o_w_out, i, x, g1, n_p, ts),)
            new_c_ckv.append(ckv[:n_p].reshape(bp, tp, C_KV_LORA))
            new_c_krope.append(p1[:n_p, C_Q_LORA + C_KV_LORA:].reshape(bp, tp, C_ROPE))
        x = _moe(x[0], norm_ffn[layer], sc2, sh2, g2, moe_w_router, moe_b_router, moe_w_gu, moe_b_gu,
                 moe_w_down, moe_b_down, layer, n_p, ts, split=layer == depth - 1)
        x = tuple(x) if layer == depth - 1 else (x,)

    return (x[0].reshape(bp, tp, d), x[1].reshape(bs, ts, d),
            jnp.stack(new_a_k, axis=1), jnp.stack(new_a_v, axis=1),
            jnp.stack(new_b_fwd, axis=1), jnp.stack(new_b_bwd, axis=1),
            jnp.stack(new_c_ckv, axis=1), jnp.stack(new_c_krope, axis=1))
```

```python
import functools

import numpy as np
import jax
import jax.numpy as jnp
from jax import lax
from jax.experimental import pallas as pl
from jax.experimental.pallas import tpu as pltpu

F32 = jnp.float32
BF16 = jnp.bfloat16

EPS = 1e-6
NEG_INF = -1e30
ROPE_BASE = 10000.0
GRID_W = 64
N_GROUPS = 8

A_HEADS, A_KV_HEADS, A_GROUP, A_HEAD_DIM, A_WINDOW, A_BLOCK = 8, 2, 4, 64, 128, 128
B_HEADS, B_DK, B_DV, B_CHUNK = 8, 64, 64, 64
C_HEADS, C_NOPE, C_ROPE, C_V, C_Q_LORA, C_KV_LORA = 16, 64, 32, 64, 384, 256
C_QK = C_NOPE + C_ROPE
N_EXPERTS, TOP_K = 32, 4
SWIGLU_LIMIT, SWIGLU_ALPHA = 7.0, 1.702

VMEM_LIMIT = 56 * 1024 * 1024


def _params(sem, vmem=None):
    return pltpu.CompilerParams(dimension_semantics=sem, vmem_limit_bytes=vmem)


def _bdot(a, b):
    return jnp.dot(a.astype(BF16), b.astype(BF16), preferred_element_type=F32)


def _bdot_nt(a, b):
    return lax.dot_general(a.astype(BF16), b.astype(BF16), (((1,), (1,)), ((), ())),
                           preferred_element_type=F32)


def _dot3(a, b):
    a_hi, b_hi = a.astype(BF16), b.astype(BF16)
    a_lo = (a - a_hi.astype(F32)).astype(BF16)
    b_lo = (b - b_hi.astype(F32)).astype(BF16)
    dot = functools.partial(jnp.dot, preferred_element_type=F32)
    return dot(a_hi, b_hi) + (dot(a_hi, b_lo) + dot(a_lo, b_hi))


def _rms(x, gain):
    return x * lax.rsqrt(jnp.mean(x * x, axis=-1, keepdims=True) + EPS) * gain


def _silu(x):
    return x * jax.nn.sigmoid(x)


def _group_of_tile(tm, n_p, t_s):
    def group(i):
        r = i * tm
        return jnp.where(r < n_p, 0, 1 + (r - n_p) // t_s)
    return group


def _adaln_kernel(cond_ref, w_ref, b_ref, o_ref):
    o_ref[...] = _bdot(_silu(cond_ref[...]), w_ref[...]) + b_ref[...]


def _adaln(cond, w_mod, b_mod):
    depth, d, _ = w_mod.shape
    return pl.pallas_call(
        _adaln_kernel,
        grid=(depth, 6),
        in_specs=[pl.BlockSpec((N_GROUPS, d), lambda l, j: (0, 0)),
                  pl.BlockSpec((None, d, d), lambda l, j: (l, 0, j)),
                  pl.BlockSpec((None, 1, d), lambda l, j: (l, 0, j))],
        out_specs=pl.BlockSpec((None, None, N_GROUPS, d), lambda l, j: (l, j, 0, 0)),
        out_shape=jax.ShapeDtypeStruct((depth, 6, N_GROUPS, d), F32),
        compiler_params=_params(("arbitrary", "arbitrary")),
        name="adaln",
    )(cond, w_mod, b_mod.reshape(depth, 1, 6 * d))


def _row_specs(arrays, tm, pos):
    specs, ends, start = [], [], 0
    for arr in arrays:
        nt = arr.shape[0] // tm
        specs.append(pl.BlockSpec((tm, arr.shape[1]),
                                  lambda *ids, s=start, nt=nt: (jnp.clip(ids[pos] - s, 0, nt - 1), 0)))
        start += nt
        ends.append(start)
    return specs, tuple(ends)


def _pick_rows(refs, ends, i):
    x = refs[-1][...]
    for ref, end in zip(reversed(refs[:-1]), reversed(ends[:-1])):
        x = jnp.where(i < end, ref[...], x)
    return x


def _modulate_kernel(*refs, ends):
    n = len(ends)
    g_ref, sc_ref, sh_ref, o_ref = refs[n:]
    y = _rms(_pick_rows(refs[:n], ends, pl.program_id(0)), g_ref[...])
    o_ref[...] = (y * (1 + sc_ref[...]) + sh_ref[...]).astype(o_ref.dtype)


def _modulate(xs, gain, scale, shift, n_p, t_s, tm=512):
    n, d = sum(x.shape[0] for x in xs), xs[0].shape[1]
    group = _group_of_tile(tm, n_p, t_s)
    specs, ends = _row_specs(xs, tm, 0)
    return pl.pallas_call(
        functools.partial(_modulate_kernel, ends=ends),
        grid=(n // tm,),
        in_specs=specs + [pl.BlockSpec((1, d), lambda i: (0, 0)),
                          pl.BlockSpec((None, 1, d), lambda i: (group(i), 0, 0)),
                          pl.BlockSpec((None, 1, d), lambda i: (group(i), 0, 0))],
        out_specs=pl.BlockSpec((tm, d), lambda i: (i, 0)),
        out_shape=jax.ShapeDtypeStruct((n, d), BF16),
        compiler_params=_params(("arbitrary",)),
        name="modulate",
    )(*xs, gain.reshape(1, d), scale.reshape(N_GROUPS, 1, d), shift.reshape(N_GROUPS, 1, d))


def _mm_kernel(*refs, ends):
    n = len(ends)
    w_ref, o_ref, wb_ref = refs[n:]

    @pl.when(pl.program_id(1) == 0)
    def _():
        wb_ref[...] = w_ref[...].astype(BF16)
    x = _pick_rows(refs[:n], ends, pl.program_id(1))
    o_ref[...] = jnp.dot(x, wb_ref[...], preferred_element_type=F32).astype(o_ref.dtype)


def _matmul(xs, w3, layer, col0_blocks, tn, n_out, out_dtype=F32, tm=512, name="matmul"):
    n, k = sum(x.shape[0] for x in xs), xs[0].shape[1]
    specs, ends = _row_specs(xs, tm, 1)
    return pl.pallas_call(
        functools.partial(_mm_kernel, ends=ends),
        grid=(n_out // tn, n // tm),
        in_specs=specs + [pl.BlockSpec((None, k, tn), lambda j, i: (layer, 0, col0_blocks + j))],
        out_specs=pl.BlockSpec((tm, tn), lambda j, i: (i, j)),
        out_shape=jax.ShapeDtypeStruct((n, n_out), out_dtype),
        scratch_shapes=[pltpu.VMEM((k, tn), BF16)],
        compiler_params=_params(("arbitrary", "arbitrary"), VMEM_LIMIT),
        name=name,
    )(*xs, w3)


def _mm_res_kernel(*refs, col_ends, res_ends):
    i = pl.program_id(0)
    pos = 0
    cols = []
    for ends in col_ends:
        cols.append(refs[pos:pos + len(ends)])
        pos += len(ends)
    res_refs = refs[pos:pos + len(res_ends)]
    w_ref, gate_ref, o_ref, wb_ref = refs[pos + len(res_ends):]

    @pl.when(i == 0)
    def _():
        wb_ref[...] = w_ref[...].astype(BF16)
    y, k0 = None, 0
    for col_refs, ends in zip(cols, col_ends):
        x = _pick_rows(col_refs, ends, i)
        part = jnp.dot(x, wb_ref[k0:k0 + x.shape[1], :], preferred_element_type=F32)
        y = part if y is None else y + part
        k0 += x.shape[1]
    o_ref[...] = _pick_rows(res_refs, res_ends, i) + gate_ref[...] * y


def _matmul_residual(mix_cols, w3, layer, res, gate, n_p, t_s, tm=512):
    n, d = sum(r.shape[0] for r in res), res[0].shape[1]
    k = sum(col[0].shape[1] for col in mix_cols)
    group = _group_of_tile(tm, n_p, t_s)
    specs, col_ends = [], []
    for col in mix_cols:
        s, e = _row_specs(col, tm, 0)
        specs += s
        col_ends.append(e)
    res_specs, res_ends = _row_specs(res, tm, 0)
    return pl.pallas_call(
        functools.partial(_mm_res_kernel, col_ends=tuple(col_ends), res_ends=res_ends),
        grid=(n // tm,),
        in_specs=specs + res_specs + [pl.BlockSpec((None, k, d), lambda i: (layer, 0, 0)),
                                      pl.BlockSpec((None, 1, d), lambda i: (group(i), 0, 0))],
        out_specs=pl.BlockSpec((tm, d), lambda i: (i, 0)),
        out_shape=jax.ShapeDtypeStruct((n, d), F32),
        scratch_shapes=[pltpu.VMEM((k, d), BF16)],
        compiler_params=_params(("arbitrary",), VMEM_LIMIT),
        name="out_proj_residual",
    )(*[piece for col in mix_cols for piece in col], *res, w3, gate.reshape(N_GROUPS, 1, d))


def _rope_tables(t_len, d):
    half, quarter = d // 2, d // 4
    pos = np.arange(t_len)
    row, col = pos // GRID_W, pos % GRID_W
    inv = ROPE_BASE ** (-np.arange(quarter, dtype=np.float64) / quarter)
    ang_r = row[:, None] * inv[None, :]
    ang_c = col[:, None] * inv[None, :]
    cos = np.concatenate([np.cos(ang_r), np.cos(ang_r), np.cos(ang_c), np.cos(ang_c)], axis=1)
    sin = np.concatenate([-np.sin(ang_r), np.sin(ang_r), -np.sin(ang_c), np.sin(ang_c)], axis=1)
    return jnp.asarray(cos, F32), jnp.asarray(sin, F32)


def _swap_pairs(x):
    q = x.shape[-1] // 4
    return jnp.concatenate([x[:, q:2 * q], x[:, :q], x[:, 3 * q:], x[:, 2 * q:3 * q]], axis=-1)


def _rope(x, cos, sin):
    return x * cos + _swap_pairs(x) * sin


def _softmax_attend(parts, sink):
    m = parts[0][0].max(axis=-1, keepdims=True)
    for s, _ in parts[1:]:
        m = jnp.maximum(m, s.max(axis=-1, keepdims=True))
    if sink is not None:
        m = jnp.maximum(m, sink)
    den = jnp.exp(sink - m) if sink is not None else 0.0
    acc = None
    for s, v in parts:
        p = jnp.exp(s - m)
        den = den + p.sum(axis=-1, keepdims=True)
        o = _bdot(p, v)
        acc = o if acc is None else acc + o
    return acc / den


def _group_sinks(sink_ref, hk, rows):
    head = lax.broadcasted_iota(jnp.int32, (A_GROUP * rows, 1), 0) // rows
    col = jnp.full((A_GROUP * rows, 1), sink_ref[hk * A_GROUP], F32)
    for g in range(1, A_GROUP):
        col = jnp.where(head == g, sink_ref[hk * A_GROUP + g], col)
    return col


def _attn_a_prompt_kernel(sink_ref, q_ref, kv_ref, gq_ref, gk_ref, o_ref, kn_ref):
    scale = A_HEAD_DIM ** -0.5
    q = q_ref[...]
    kv = kv_ref[...]
    t = q.shape[0]
    outs = []
    kns = []
    for hk in range(A_KV_HEADS):
        k = _rms(kv[:, hk * A_HEAD_DIM:(hk + 1) * A_HEAD_DIM], gk_ref[...])
        v = kv[:, (A_KV_HEADS + hk) * A_HEAD_DIM:(A_KV_HEADS + hk + 1) * A_HEAD_DIM]
        kns.append(k)
        qg = jnp.concatenate([_rms(q[:, h * A_HEAD_DIM:(h + 1) * A_HEAD_DIM], gq_ref[...])
                              for h in range(hk * A_GROUP, (hk + 1) * A_GROUP)], axis=0)
        s = _bdot_nt(qg, k) * scale
        o = _softmax_attend([(s, v)], _group_sinks(sink_ref, hk, t))
        outs += [o[g * t:(g + 1) * t] for g in range(A_GROUP)]
    o_ref[...] = jnp.concatenate(outs, axis=-1).astype(o_ref.dtype)
    kn_ref[...] = jnp.concatenate(kns, axis=-1)


def _attn_a_prompt(proj, n_seq, t, sink, gq, gk):
    qw = A_HEADS * A_HEAD_DIM
    kvw = 2 * A_KV_HEADS * A_HEAD_DIM
    return pl.pallas_call(
        _attn_a_prompt_kernel,
        grid=(n_seq,),
        in_specs=[pl.BlockSpec(memory_space=pltpu.SMEM),
                  pl.BlockSpec((t, qw), lambda b: (b, 0)),
                  pl.BlockSpec((t, kvw), lambda b: (b, qw // kvw)),
                  pl.BlockSpec((1, A_HEAD_DIM), lambda b: (0, 0)),
                  pl.BlockSpec((1, A_HEAD_DIM), lambda b: (0, 0))],
        out_specs=[pl.BlockSpec((t, qw), lambda b: (b, 0)),
                   pl.BlockSpec((t, A_KV_HEADS * A_HEAD_DIM), lambda b: (b, 0))],
        out_shape=[jax.ShapeDtypeStruct((n_seq * t, qw), BF16),
                   jax.ShapeDtypeStruct((n_seq * t, A_KV_HEADS * A_HEAD_DIM), F32)],
        compiler_params=_params(("arbitrary",)),
        name="attn_a_context",
    )(sink, proj, proj, gq.reshape(1, -1), gk.reshape(1, -1))


def _attn_a_sample_kernel(sink_ref, q_ref, kv_ref, kc_ref, vc_ref, gq_ref, gk_ref, cos_ref, sin_ref,
                          o_ref, *, t):
    scale = A_HEAD_DIM ** -0.5
    i = pl.program_id(1)
    win = 3 * A_BLOCK
    q0 = pl.multiple_of(i * A_BLOCK, A_BLOCK)
    k0 = pl.multiple_of(jnp.clip((i - 1) * A_BLOCK, 0, t - win), A_BLOCK)
    q = q_ref[...]
    kv = kv_ref[pl.ds(k0, win), :]
    cq, sq = cos_ref[pl.ds(q0, A_BLOCK), :], sin_ref[pl.ds(q0, A_BLOCK), :]
    ck, sk = cos_ref[pl.ds(k0, win), :], sin_ref[pl.ds(k0, win), :]
    rows = A_GROUP * A_BLOCK
    qpos = q0 + lax.broadcasted_iota(jnp.int32, (rows, win), 0) % A_BLOCK
    kpos = k0 + lax.broadcasted_iota(jnp.int32, (rows, win), 1)
    mask = jnp.abs(qpos - kpos) <= A_WINDOW
    outs = []
    for hk in range(A_KV_HEADS):
        k = _rope(_rms(kv[:, hk * A_HEAD_DIM:(hk + 1) * A_HEAD_DIM], gk_ref[...]), ck, sk)
        v = kv[:, (A_KV_HEADS + hk) * A_HEAD_DIM:(A_KV_HEADS + hk + 1) * A_HEAD_DIM]
        qg = jnp.concatenate([_rope(_rms(q[:, h * A_HEAD_DIM:(h + 1) * A_HEAD_DIM], gq_ref[...]), cq, sq)
                              for h in range(hk * A_GROUP, (hk + 1) * A_GROUP)], axis=0)
        s1 = jnp.where(mask, _bdot_nt(qg, k) * scale, NEG_INF)
        s2 = _bdot_nt(qg, kc_ref[hk]) * scale
        o = _softmax_attend([(s1, v), (s2, vc_ref[hk])], _group_sinks(sink_ref, hk, A_BLOCK))
        outs += [o[g * A_BLOCK:(g + 1) * A_BLOCK] for g in range(A_GROUP)]
    o_ref[...] = jnp.concatenate(outs, axis=-1).astype(o_ref.dtype)


def _attn_a_sample(proj, row0, n_seq, t, k_ctx, v_ctx, sink, gq, gk):
    qw = A_HEADS * A_HEAD_DIM
    kvw = 2 * A_KV_HEADS * A_HEAD_DIM
    nqb = t // A_BLOCK
    cos, sin = _rope_tables(t, A_HEAD_DIM)
    past = k_ctx.shape[2]
    return pl.pallas_call(
        functools.partial(_attn_a_sample_kernel, t=t),
        grid=(n_seq, nqb),
        in_specs=[pl.BlockSpec(memory_space=pltpu.SMEM),
                  pl.BlockSpec((A_BLOCK, qw), lambda b, i: (row0 // A_BLOCK + b * nqb + i, 0)),
                  pl.BlockSpec((t, kvw), lambda b, i: (row0 // t + b, qw // kvw)),
                  pl.BlockSpec((None, A_KV_HEADS, past, A_HEAD_DIM), lambda b, i: (b, 0, 0, 0)),
                  pl.BlockSpec((None, A_KV_HEADS, past, A_HEAD_DIM), lambda b, i: (b, 0, 0, 0)),
                  pl.BlockSpec((1, A_HEAD_DIM), lambda b, i: (0, 0)),
                  pl.BlockSpec((1, A_HEAD_DIM), lambda b, i: (0, 0)),
                  pl.BlockSpec((t, A_HEAD_DIM), lambda b, i: (0, 0)),
                  pl.BlockSpec((t, A_HEAD_DIM), lambda b, i: (0, 0))],
        out_specs=pl.BlockSpec((A_BLOCK, qw), lambda b, i: (b * nqb + i, 0)),
        out_shape=jax.ShapeDtypeStruct((n_seq * t, qw), BF16),
        compiler_params=_params(("arbitrary", "arbitrary")),
        name="attn_a_latent",
    )(sink, proj, proj, k_ctx, v_ctx, gq.reshape(1, -1), gk.reshape(1, -1), cos, sin)


def _per_head_lanes(x, fn):
    lane = lax.broadcasted_iota(jnp.int32, x.shape, 1)
    lo = fn(x[:, :B_DK])
    hi = fn(x[:, B_DK:])
    return jnp.where(lane < B_DK, lo, hi)


def _conv_silu(x, w):
    t = x.shape[0]
    row = lax.broadcasted_iota(jnp.int32, x.shape, 0)
    prev = jnp.where(row == 0, 0.0, pltpu.roll(x, 1, 0))
    nxt = jnp.where(row == t - 1, 0.0, pltpu.roll(x, t - 1, 0))
    return _silu(prev * w[0:1, :] + x * w[1:2, :] + nxt * w[2:3, :])


M_SAME, M_TRI, M_TRI_T, M_STRICT, M_BDIAG = range(5)


def _delta_masks(r):
    c = B_CHUNK
    ii = lax.broadcasted_iota(jnp.int32, (r, r), 0)
    jj = lax.broadcasted_iota(jnp.int32, (r, r), 1)
    same = (ii // c) == (jj // c)
    ahead = jnp.where(ii < r // 2, ii - jj, jj - ii)
    tri = jnp.where(same, ahead, -1) >= 0
    tri_t = jnp.where(same, ahead, 1) <= 0
    strict = jnp.where(same, ahead, -1) > 0
    bdiag = (ii // 16) == (jj // 16)
    return [x.astype(F32) for x in (same, tri, tri_t, strict, bdiag)]


def _delta_prepare(q, k, v, g_col, g_row, beta, mask_ref):
    c = B_CHUNK
    r = q.shape[0]
    dot = functools.partial(jnp.dot, preferred_element_type=F32)
    gc_col = jnp.sum(mask_ref[M_TRI] * g_row, axis=1, keepdims=True)
    gc_row = jnp.sum(mask_ref[M_TRI_T] * g_col, axis=0, keepdims=True)
    g_tot = jnp.sum(mask_ref[M_SAME] * g_row, axis=1, keepdims=True)
    ex = jnp.exp((gc_col - gc_row) * mask_ref[M_TRI])
    kb = k * beta
    qk = _bdot_nt(jnp.concatenate([kb, q], axis=0), k)
    m = qk[:r] * (ex * mask_ref[M_STRICT])
    aqk = qk[r:] * (ex * mask_ref[M_TRI])
    dg = m * mask_ref[M_BDIAG]
    off = m - dg
    n1 = -dg
    n1b = n1.astype(BF16)
    n2 = dot(n1b, n1b)
    n2b = n2.astype(BF16)
    t = dot(jnp.concatenate([n1b, n2b], axis=0), n2b)
    xs = n1 + n2 + t[:r]
    n4 = t[r:]
    n4b = n4.astype(BF16)
    t = dot(jnp.concatenate([xs.astype(BF16), n4b], axis=0), n4b)
    xs = xs + n4 + t[:r]
    n8 = t[r:]
    xs = xs + n8 + dot(xs.astype(BF16), n8.astype(BF16))
    xsb = xs.astype(BF16)
    f = -(off + dot(xsb, off.astype(BF16)))
    fb = f.astype(BF16)
    t = dot(fb, jnp.concatenate([xsb, fb], axis=1))
    ys = xs + f + t[:, :r]
    f2 = t[:, r:]
    ts = ys + f2 + dot(f2.astype(BF16), ys.astype(BF16))
    egc = jnp.exp(gc_col)
    rhs = jnp.concatenate([v * beta, kb * egc], axis=-1)
    uw = rhs + _bdot(ts, rhs)

    def block_diag(x):
        return jnp.concatenate([x] * (r // c), axis=-1) * mask_ref[M_SAME]

    kd = k * jnp.exp(g_tot - gc_col)
    kd_t = jnp.concatenate([kd, jnp.zeros_like(kd)], axis=-1).T[:c]
    kd_t = jnp.concatenate([kd_t] * (r // c), axis=0) * mask_ref[M_SAME]
    e_tot = jnp.broadcast_to(jnp.exp(g_tot), v.shape)
    return uw[:, :B_DV], e_tot, block_diag(uw[:, B_DV:]), aqk, block_diag(q * egc), kd_t


def _deltanet_kernel(alog_ref, dtb_ref, q_ref, k_ref, v_ref, z_ref, cwq_ref, cwk_ref, cwv_ref,
                     tail_ref, tailt_ref, onorm_ref, s0f_ref, s0b_ref,
                     o_ref, sf_ref, sb_ref,
                     qc_ref, kc_ref, vc_ref, oacc_ref, u_ref, et_ref, w_ref, aqk_ref, qg_ref, kdt_ref,
                     mask_ref, *, n_chunks):
    hp = pl.program_id(1)
    c = B_CHUNK

    @pl.when(jnp.logical_and(pl.program_id(0) == 0, hp == 0))
    def _():
        for i, x in enumerate(_delta_masks(4 * c)):
            mask_ref[i] = x

    def l2n(x):
        ss = _per_head_lanes(x * x, lambda a: jnp.sum(a, axis=-1, keepdims=True))
        return x * lax.rsqrt(ss + EPS)

    qc_ref[...] = l2n(_conv_silu(q_ref[...], cwq_ref[...])) * (B_DK ** -0.5)
    kc_ref[...] = l2n(_conv_silu(k_ref[...], cwk_ref[...]))
    vc_ref[...] = _conv_silu(v_ref[...], cwv_ref[...])
    oacc_ref[...] = jnp.zeros_like(oacc_ref)

    lane32 = lax.broadcasted_iota(jnp.int32, (c, 4 * B_HEADS), 1)

    def gates(chunk, d, head):
        tail = tail_ref[pl.ds(pl.multiple_of(chunk * c, c), c), :]
        ia = 2 * d * B_HEADS + head
        ib = ia + B_HEADS
        a_col = jnp.sum(jnp.where(lane32 == ia, tail, 0.0), axis=1, keepdims=True)
        b_col = jnp.sum(jnp.where(lane32 == ib, tail, 0.0), axis=1, keepdims=True)
        a_row = tailt_ref[chunk, pl.ds(ia, 1), :]
        na = -jnp.exp(alog_ref[d, head])
        bias = dtb_ref[d, head]
        g_col = na * jax.nn.softplus(a_col + bias)
        g_row = na * jax.nn.softplus(a_row + bias)
        return g_col, g_row, jax.nn.sigmoid(b_col)

    def stacked(ref, chunks):
        parts = []
        for chunk in chunks:
            x = ref[pl.ds(pl.multiple_of(chunk * c, c), c), :]
            parts += [x[:, :B_DK], x[:, B_DK:]]
        return jnp.concatenate(parts, axis=0)

    def prepare(j, carry):
        chunks = (j, n_chunks - 1 - j)
        gs = [gates(chunks[d], d, 2 * hp + hh) for d in range(2) for hh in range(2)]
        g_col = jnp.concatenate([g[0] for g in gs], axis=0)
        g_row = jnp.concatenate([g[1] for g in gs], axis=1)
        beta = jnp.concatenate([g[2] for g in gs], axis=0)
        outs = _delta_prepare(stacked(qc_ref, chunks), stacked(kc_ref, chunks), stacked(vc_ref, chunks),
                              g_col, g_row, beta, mask_ref)
        for ref, x in zip((u_ref, et_ref, w_ref, aqk_ref, qg_ref, kdt_ref), outs):
            ref[j] = x.astype(ref.dtype)
        return carry

    lax.fori_loop(0, n_chunks, prepare, 0, unroll=4)

    def scan(j, s):
        sb = s.astype(BF16)
        delta = u_ref[j] - jnp.dot(w_ref[j], sb, preferred_element_type=F32)
        db = delta.astype(BF16)
        o = (jnp.dot(qg_ref[j], sb, preferred_element_type=F32)
             + jnp.dot(aqk_ref[j], db, preferred_element_type=F32))
        for d, chunk in enumerate((j, n_chunks - 1 - j)):
            rows = pl.ds(pl.multiple_of(chunk * c, c), c)
            oacc_ref[rows, :] += jnp.concatenate([o[2 * d * c:(2 * d + 1) * c],
                                                  o[(2 * d + 1) * c:(2 * d + 2) * c]], axis=-1)
        return s * et_ref[j] + jnp.dot(kdt_ref[j], db, preferred_element_type=F32)

    init = jnp.concatenate([s0f_ref[0], s0f_ref[1], s0b_ref[0], s0b_ref[1]], axis=0)
    fin = lax.fori_loop(0, n_chunks, scan, init)
    sf_ref[0], sf_ref[1], sb_ref[0], sb_ref[1] = (fin[i * B_DK:(i + 1) * B_DK] for i in range(4))

    o = oacc_ref[...]
    ms = _per_head_lanes(o * o, lambda a: jnp.mean(a, axis=-1, keepdims=True))
    o_ref[...] = (o * lax.rsqrt(ms + EPS) * onorm_ref[...] * _silu(z_ref[...])).astype(o_ref.dtype)


def _deltanet(proj, tail, row0, n_seq, t, conv_w, a_log, dt_bias, o_norm, s0_f, s0_b):
    c = B_CHUNK
    n_chunks = t // c
    lw = 2 * B_DK
    col_q = (A_HEADS + 2 * A_KV_HEADS) * A_HEAD_DIM // lw
    nhp = B_HEADS // 2
    rows = tail[row0:row0 + n_seq * t]
    tail_t = rows.reshape(n_seq, n_chunks, c, 4 * B_HEADS).transpose(0, 1, 3, 2)
    onorm2 = jnp.concatenate([o_norm, o_norm]).reshape(1, lw)
    b0 = row0 // t
    seq_blk = lambda off: pl.BlockSpec((t, lw), lambda b, h: (b0 + b, col_q + off + h))
    cw_blk = lambda off: pl.BlockSpec((3, lw), lambda b, h: (0, off + h))
    st_blk = pl.BlockSpec((None, 2, B_DK, B_DV), lambda b, h: (b, h, 0, 0))
    return pl.pallas_call(
        functools.partial(_deltanet_kernel, n_chunks=n_chunks),
        grid=(n_seq, nhp),
        in_specs=[pl.BlockSpec(memory_space=pltpu.SMEM), pl.BlockSpec(memory_space=pltpu.SMEM),
                  seq_blk(0), seq_blk(nhp), seq_blk(2 * nhp), seq_blk(3 * nhp),
                  cw_blk(0), cw_blk(nhp), cw_blk(2 * nhp),
                  pl.BlockSpec((t, 4 * B_HEADS), lambda b, h: (b, 0)),
                  pl.BlockSpec((None, n_chunks, 4 * B_HEADS, c), lambda b, h: (b, 0, 0, 0)),
                  pl.BlockSpec((1, lw), lambda b, h: (0, 0)),
                  st_blk, st_blk],
        out_specs=[pl.BlockSpec((t, lw), lambda b, h: (b, h)), st_blk, st_blk],
        out_shape=[jax.ShapeDtypeStruct((n_seq * t, B_HEADS * B_DV), BF16),
                   jax.ShapeDtypeStruct((n_seq, B_HEADS, B_DK, B_DV), F32),
                   jax.ShapeDtypeStruct((n_seq, B_HEADS, B_DK, B_DV), F32)],
        scratch_shapes=[pltpu.VMEM((t, lw), F32)] * 4 + [pltpu.VMEM((n_chunks, 4 * c, B_DV), F32)] * 2
        + [pltpu.VMEM((n_chunks, 4 * c, 4 * c), BF16)] * 4 + [pltpu.VMEM((5, 4 * c, 4 * c), F32)],
        compiler_params=_params(("arbitrary", "arbitrary")),
        name="deltanet",
    )(a_log, dt_bias, proj, proj, proj, proj, conv_w, conv_w, conv_w, rows, tail_t, onorm2, s0_f, s0_b)


def _lora_norm_kernel(p_ref, gq_ref, gkv_ref, cq_ref, ckv_ref, ckvb_ref):
    p = p_ref[...]
    cq_ref[...] = _rms(p[:, :C_Q_LORA], gq_ref[...]).astype(cq_ref.dtype)
    ckv = _rms(p[:, C_Q_LORA:C_Q_LORA + C_KV_LORA], gkv_ref[...])
    ckv_ref[...] = ckv
    ckvb_ref[...] = ckv.astype(ckvb_ref.dtype)


def _lora_norm(p1, gq, gkv, tm=512):
    n, w = p1.shape
    return pl.pallas_call(
        _lora_norm_kernel,
        grid=(n // tm,),
        in_specs=[pl.BlockSpec((tm, w), lambda i: (i, 0)),
                  pl.BlockSpec((1, C_Q_LORA), lambda i: (0, 0)),
                  pl.BlockSpec((1, C_KV_LORA), lambda i: (0, 0))],
        out_specs=[pl.BlockSpec((tm, C_Q_LORA), lambda i: (i, 0)),
                   pl.BlockSpec((tm, C_KV_LORA), lambda i: (i, 0)),
                   pl.BlockSpec((tm, C_KV_LORA), lambda i: (i, 0))],
        out_shape=[jax.ShapeDtypeStruct((n, C_Q_LORA), BF16),
                   jax.ShapeDtypeStruct((n, C_KV_LORA), F32),
                   jax.ShapeDtypeStruct((n, C_KV_LORA), BF16)],
        compiler_params=_params(("arbitrary",)),
        name="lora_norm",
    )(p1, gq.reshape(1, -1), gkv.reshape(1, -1))


def _mla_head_k(kv, kr, kr_ss, h, gk):
    kn = kv[:, h * (C_NOPE + C_V):h * (C_NOPE + C_V) + C_NOPE]
    v = kv[:, h * (C_NOPE + C_V) + C_NOPE:(h + 1) * (C_NOPE + C_V)]
    rn = lax.rsqrt((jnp.sum(kn * kn, axis=-1, keepdims=True) + kr_ss) / C_QK + EPS)
    return kn * rn * gk[:, :C_NOPE], kr * rn, v


C_STACK = 4


def _place(x, i, n):
    t, w = x.shape
    parts = ([jnp.zeros((t, i * w), x.dtype)] if i else []) + [x]
    if i < n - 1:
        parts.append(jnp.zeros((t, (n - 1 - i) * w), x.dtype))
    return jnp.concatenate(parts, axis=-1)


def _by_block(cols, lane, w):
    out = cols[-1]
    for i in reversed(range(len(cols) - 1)):
        out = jnp.where(lane < (i + 1) * w, cols[i], out)
    return out


def _attn_c_prompt_kernel(q_ref, kv_ref, p_ref, gq_ref, gk_ref, o_ref):
    scale = C_QK ** -0.5
    n = C_STACK
    q = q_ref[...]
    kv = kv_ref[...]
    gk = gk_ref[...]
    t = q.shape[0]
    kr_raw = p_ref[...][:, C_Q_LORA + C_KV_LORA:]
    kr_ss = jnp.sum(kr_raw * kr_raw, axis=-1, keepdims=True)
    kr_g = kr_raw * gk[:, C_NOPE:]
    lane_q = lax.broadcasted_iota(jnp.int32, (t, n * C_QK), 1)
    lane_o = lax.broadcasted_iota(jnp.int32, (t, n * C_V), 1)
    outs = []
    for grp in range(C_HEADS // n):
        qs = q[:, grp * n * C_QK:(grp + 1) * n * C_QK]
        rn = [lax.rsqrt(jnp.mean(qs[:, i * C_QK:(i + 1) * C_QK] ** 2, axis=-1, keepdims=True) + EPS)
              for i in range(n)]
        qs = qs * _by_block(rn, lane_q, C_QK) * gq_ref[...]
        k_rows, v_rows = [], []
        for i in range(n):
            kn, kr, v = _mla_head_k(kv, kr_g, kr_ss, grp * n + i, gk)
            k_rows.append(_place(jnp.concatenate([kn, kr], axis=-1), i, n))
            v_rows.append(_place(v, i, n))
        s = _bdot_nt(qs, jnp.concatenate(k_rows, axis=0)) * scale
        ps, rden = [], []
        for i in range(n):
            si = s[:, i * t:(i + 1) * t]
            pi = jnp.exp(si - si.max(axis=-1, keepdims=True))
            ps.append(pi)
            rden.append(1.0 / pi.sum(axis=-1, keepdims=True))
        o = _bdot(jnp.concatenate(ps, axis=-1), jnp.concatenate(v_rows, axis=0))
        outs.append(o * _by_block(rden, lane_o, C_V))
    o_ref[...] = jnp.concatenate(outs, axis=-1).astype(o_ref.dtype)


def _attn_c_prompt(q, kv, p1, n_seq, t, gq, gk):
    return pl.pallas_call(
        _attn_c_prompt_kernel,
        grid=(n_seq,),
        in_specs=[pl.BlockSpec((t, q.shape[1]), lambda b: (b, 0)),
                  pl.BlockSpec((t, kv.shape[1]), lambda b: (b, 0)),
                  pl.BlockSpec((t, p1.shape[1]), lambda b: (b, 0)),
                  pl.BlockSpec((1, C_STACK * C_QK), lambda b: (0, 0)),
                  pl.BlockSpec((1, C_QK), lambda b: (0, 0))],
        out_specs=pl.BlockSpec((t, C_HEADS * C_V), lambda b: (b, 0)),
        out_shape=jax.ShapeDtypeStruct((n_seq * t, C_HEADS * C_V), BF16),
        compiler_params=_params(("arbitrary",), VMEM_LIMIT),
        name="attn_c_context",
    )(q, kv, p1, jnp.tile(gq, C_STACK).reshape(1, -1), gk.reshape(1, -1))


def _attn_c_sample_kernel(q_ref, kv_ref, p_ref, kvc_ref, krc_ref, gq_ref, gk_ref, cos_ref, sin_ref,
                          o_ref, *, tq):
    scale = C_QK ** -0.5
    i = pl.program_id(1)
    q0 = pl.multiple_of(i * tq, tq)
    q = q_ref[...]
    kv = kv_ref[...]
    kvc = kvc_ref[...]
    gq, gk = gq_ref[...], gk_ref[...]
    cq, sq = cos_ref[pl.ds(q0, tq), :], sin_ref[pl.ds(q0, tq), :]
    kr_raw = p_ref[...][:, C_Q_LORA + C_KV_LORA:]
    krc_raw = krc_ref[...]
    kr_all = jnp.concatenate([_rope(kr_raw * gk[:, C_NOPE:], cos_ref[...], sin_ref[...]),
                              krc_raw * gk[:, C_NOPE:]], axis=0)
    kr_ss = jnp.concatenate([jnp.sum(kr_raw * kr_raw, axis=-1, keepdims=True),
                             jnp.sum(krc_raw * krc_raw, axis=-1, keepdims=True)], axis=0)
    kr_pad = jnp.concatenate([jnp.zeros((kr_all.shape[0], C_NOPE), F32), kr_all], axis=-1)
    lane = lax.broadcasted_iota(jnp.int32, kr_pad.shape, 1)
    outs = []
    for h in range(C_HEADS):
        qh = q[:, h * C_QK:(h + 1) * C_QK]
        qh = qh * lax.rsqrt(jnp.mean(qh * qh, axis=-1, keepdims=True) + EPS) * gq
        qh = jnp.concatenate([qh[:, :C_NOPE], _rope(qh[:, C_NOPE:], cq, sq)], axis=-1)
        c0 = h * (C_NOPE + C_V)
        kn = jnp.concatenate([kv[:, c0:c0 + C_QK], kvc[:, c0:c0 + C_QK]], axis=0)
        v = jnp.concatenate([kv[:, c0 + C_NOPE:c0 + C_NOPE + C_V], kvc[:, c0 + C_NOPE:c0 + C_NOPE + C_V]],
                            axis=0)
        ss = jnp.sum(kn[:, :C_NOPE] * kn[:, :C_NOPE], axis=-1, keepdims=True) + kr_ss
        k = jnp.where(lane < C_NOPE, kn * gk, kr_pad) * lax.rsqrt(ss / C_QK + EPS)
        s = _bdot_nt(qh, k) * scale
        outs.append(_softmax_attend([(s, v)], None))
    o_ref[...] = jnp.concatenate(outs, axis=-1).astype(o_ref.dtype)


def _attn_c_sample(q, kv, p1, row0, n_seq, t, kr_ctx, gq, gk, tq=256):
    n = p1.shape[0]
    past = kr_ctx.shape[1]
    nq = t // tq
    cos, sin = _rope_tables(t, C_ROPE)
    return pl.pallas_call(
        functools.partial(_attn_c_sample_kernel, tq=tq),
        grid=(n_seq, nq),
        in_specs=[pl.BlockSpec((tq, q.shape[1]), lambda b, i: (row0 // tq + b * nq + i, 0)),
                  pl.BlockSpec((t, kv.shape[1]), lambda b, i: (row0 // t + b, 0)),
                  pl.BlockSpec((t, p1.shape[1]), lambda b, i: (row0 // t + b, 0)),
                  pl.BlockSpec((past, kv.shape[1]), lambda b, i: (n // past + b, 0)),
                  pl.BlockSpec((None, past, C_ROPE), lambda b, i: (b, 0, 0)),
                  pl.BlockSpec((1, C_QK), lambda b, i: (0, 0)),
                  pl.BlockSpec((1, C_QK), lambda b, i: (0, 0)),
                  pl.BlockSpec((t, C_ROPE), lambda b, i: (0, 0)),
                  pl.BlockSpec((t, C_ROPE), lambda b, i: (0, 0))],
        out_specs=pl.BlockSpec((tq, C_HEADS * C_V), lambda b, i: (b * nq + i, 0)),
        out_shape=jax.ShapeDtypeStruct((n_seq * t, C_HEADS * C_V), BF16),
        compiler_params=_params(("arbitrary", "arbitrary"), VMEM_LIMIT),
        name="attn_c_latent",
    )(q, kv, p1, kv, kr_ctx, gq.reshape(1, -1), gk.reshape(1, -1), cos, sin)


ROUTE_TILE = 512
ROUTE_BLOCK = 512
SLOT_ALIGN = 8
LOCAL_SLOTS = 2304
META_LANES = 128
FF_CHUNK = 256


def _router_kernel(x_ref, g_ref, sc_ref, sh_ref, wr_ref, br_ref, h_ref, meta_ref, cntb_ref, cnt_ref,
                   run_ref):
    @pl.when(pl.program_id(0) == 0)
    def _():
        run_ref[...] = jnp.zeros_like(run_ref)

    h = _rms(x_ref[...], g_ref[...]) * (1 + sc_ref[...]) + sh_ref[...]
    h_ref[...] = h.astype(h_ref.dtype)
    tm = h.shape[0]
    logits = _dot3(h, wr_ref[...]) + br_ref[...]
    lane = lax.broadcasted_iota(jnp.int32, logits.shape, 1)
    work = logits
    picks, tops = [], []
    for _ in range(TOP_K):
        m = work.max(axis=-1, keepdims=True)
        first = jnp.min(jnp.where(work == m, lane, N_EXPERTS), axis=-1, keepdims=True)
        pick = lane == first
        picks.append(pick)
        tops.append(m)
        work = jnp.where(pick, -jnp.inf, work)
    sel = sum(p.astype(F32) for p in picks)
    earlier = (lax.broadcasted_iota(jnp.int32, (tm, tm), 0)
               > lax.broadcasted_iota(jnp.int32, (tm, tm), 1)).astype(BF16)
    inside = jnp.dot(earlier, sel.astype(BF16), preferred_element_type=F32)
    cnt = jnp.sum(sel, axis=0, keepdims=True)
    run = jnp.ceil(cnt / SLOT_ALIGN) * SLOT_ALIGN
    lower_e = (lax.broadcasted_iota(jnp.int32, (N_EXPERTS, N_EXPERTS), 0)
               < lax.broadcasted_iota(jnp.int32, (N_EXPERTS, N_EXPERTS), 1)).astype(BF16)
    start = jnp.dot(jnp.broadcast_to(run, (8, N_EXPERTS)).astype(BF16), lower_e,
                    preferred_element_type=F32)[0:1]
    slot = start + inside
    ws = [jnp.exp(t - tops[0]) for t in tops]
    den = sum(ws)
    mlane = lax.broadcasted_iota(jnp.int32, (tm, META_LANES), 1)
    meta = jnp.zeros((tm, META_LANES), F32)
    for k in range(TOP_K):
        meta = jnp.where(mlane == k, jnp.sum(jnp.where(picks[k], slot, 0.0), axis=-1, keepdims=True), meta)
        meta = jnp.where(mlane == TOP_K + k, ws[k] / den, meta)
    meta_ref[...] = meta
    cntb_ref[...] = cnt
    run_ref[...] += run
    cnt_ref[...] = run_ref[...]


def _router(x, gain, scale, shift, w_router, b_router, layer, n_p, t_s):
    n, d = x.shape
    tm = ROUTE_BLOCK
    e = w_router.shape[-1]
    group = _group_of_tile(tm, n_p, t_s)
    return pl.pallas_call(
        _router_kernel,
        grid=(n // tm,),
        in_specs=[pl.BlockSpec((tm, d), lambda i: (i, 0)),
                  pl.BlockSpec((1, d), lambda i: (0, 0)),
                  pl.BlockSpec((None, 1, d), lambda i: (group(i), 0, 0)),
                  pl.BlockSpec((None, 1, d), lambda i: (group(i), 0, 0)),
                  pl.BlockSpec((None, d, e), lambda i: (layer, 0, 0)),
                  pl.BlockSpec((None, 1, e), lambda i: (layer, 0, 0))],
        out_specs=[pl.BlockSpec((tm, d), lambda i: (i, 0)),
                   pl.BlockSpec((tm, META_LANES), lambda i: (i, 0)),
                   pl.BlockSpec((None, 1, e), lambda i: (i, 0, 0)),
                   pl.BlockSpec((1, e), lambda i: (0, 0))],
        out_shape=[jax.ShapeDtypeStruct((n, d), BF16),
                   jax.ShapeDtypeStruct((n, META_LANES), F32),
                   jax.ShapeDtypeStruct((n // tm, 1, e), F32),
                   jax.ShapeDtypeStruct((1, e), F32)],
        scratch_shapes=[pltpu.VMEM((1, e), F32)],
        compiler_params=_params(("arbitrary",)),
        name="router",
    )(x, gain.reshape(1, d), scale.reshape(N_GROUPS, 1, d), shift.reshape(N_GROUPS, 1, d),
      w_router, b_router.reshape(-1, 1, e))


def _route_plan(cnt_blk, cnt_tot, n_tiles, min_tiles):
    cb = cnt_blk[:, 0, :].astype(jnp.int32)
    run = (cb + SLOT_ALIGN - 1) // SLOT_ALIGN * SLOT_ALIGN
    counts = cnt_tot[0].astype(jnp.int32)
    padded = (counts + ROUTE_TILE - 1) // ROUTE_TILE * ROUTE_TILE
    ends = jnp.cumsum(padded)
    offs = ends - padded
    gstart = offs[None, :] + jnp.cumsum(run, axis=0) - run
    lstart = jnp.cumsum(run, axis=1) - run
    tile_start = jnp.arange(n_tiles, dtype=jnp.int32) * ROUTE_TILE
    n_valid = ends[-1] // ROUTE_TILE
    te = jnp.minimum(jnp.sum(ends[None, :] <= tile_start[:, None], axis=1), N_EXPERTS - 1)
    te = jnp.where(tile_start < ends[-1], te, te[jnp.maximum(n_valid - 1, 0)]).astype(jnp.int32)
    rows = jnp.clip((offs + counts)[te] - tile_start, 0, ROUTE_TILE).astype(jnp.int32)
    ragged = jnp.where(counts % ROUTE_TILE != 0, ends // ROUTE_TILE - 1, -1)
    tail = jnp.arange(min_tiles, n_tiles, dtype=jnp.int32)
    fill = jnp.concatenate([ragged, jnp.where(tail >= n_valid, tail, -1)]).astype(jnp.int32)
    runs = (gstart.reshape(-1).astype(jnp.int32), lstart.reshape(-1).astype(jnp.int32),
            (run // SLOT_ALIGN).reshape(-1).astype(jnp.int32),
            (jnp.sum(run, axis=1) // SLOT_ALIGN).astype(jnp.int32))
    return runs, te, rows, n_valid.reshape(1).astype(jnp.int32), fill


RUN_PIECE = 4
WAIT_PIECE = 8


def _start_runs(blk, gstart_ref, lstart_ref, nch_ref, copy):
    big = RUN_PIECE * SLOT_ALIGN

    def per_expert(e, carry):
        idx = blk * N_EXPERTS + e
        g0, l0, n = gstart_ref[idx], lstart_ref[idx], nch_ref[idx]
        n_big = n // RUN_PIECE

        def large(i, c2):
            off = pl.multiple_of(i * big, big)
            copy(pl.multiple_of(l0 + off, SLOT_ALIGN), pl.multiple_of(g0 + off, SLOT_ALIGN), big).start()
            return c2

        def small(i, c2):
            off = pl.multiple_of(n_big * big + i * SLOT_ALIGN, SLOT_ALIGN)
            copy(pl.multiple_of(l0 + off, SLOT_ALIGN), pl.multiple_of(g0 + off, SLOT_ALIGN), SLOT_ALIGN).start()
            return c2

        lax.fori_loop(0, n_big, large, 0)
        lax.fori_loop(0, n - n_big * RUN_PIECE, small, 0)
        return carry

    lax.fori_loop(0, N_EXPERTS, per_expert, 0)


def _wait_runs(n_chunks, copy):
    n_big = n_chunks // WAIT_PIECE

    def large(i, carry):
        copy(0, 0, WAIT_PIECE * SLOT_ALIGN).wait()
        return carry

    def small(i, carry):
        copy(0, 0, SLOT_ALIGN).wait()
        return carry

    lax.fori_loop(0, n_big, large, 0)
    lax.fori_loop(0, n_chunks - n_big * WAIT_PIECE, small, 0)


def _dispatch_kernel(fill_ref, gstart_ref, lstart_ref, nch_ref, ntot_ref, meta_ref, h_ref, xs_ref,
                     loc_ref, zero_ref, sem, zsem):
    blk = pl.program_id(0)

    @pl.when(blk == 0)
    def _():
        zero_ref[...] = jnp.zeros_like(zero_ref)

        def fill_copy(j):
            row0 = pl.multiple_of(fill_ref[j] * ROUTE_TILE, ROUTE_TILE)
            return pltpu.make_async_copy(zero_ref, xs_ref.at[pl.ds(row0, ROUTE_TILE), :], zsem)

        def start(j, carry):
            @pl.when(fill_ref[j] >= 0)
            def _():
                fill_copy(j).start()
            return carry

        def wait(j, carry):
            @pl.when(fill_ref[j] >= 0)
            def _():
                fill_copy(j).wait()
            return carry

        lax.fori_loop(0, fill_ref.shape[0], start, 0)
        lax.fori_loop(0, fill_ref.shape[0], wait, 0)

    buf = blk % 2
    last = pl.num_programs(0) - 1

    def copier(b):
        def copy(l0, g0, rows):
            return pltpu.make_async_copy(loc_ref.at[b, pl.ds(l0, rows), :], xs_ref.at[pl.ds(g0, rows), :],
                                         sem.at[b])
        return copy

    @pl.when(blk >= 2)
    def _():
        _wait_runs(ntot_ref[blk - 2], copier(buf))

    hb = h_ref[...]
    tm = hb.shape[0]
    slots_t = meta_ref[...].T[0:TOP_K]
    rows = 256
    for c0 in range(0, LOCAL_SLOTS, rows):
        slot = (c0 + lax.broadcasted_iota(jnp.int32, (rows, tm), 0)).astype(F32)
        onehot = sum(jnp.where(slot == slots_t[k:k + 1], 1.0, 0.0) for k in range(TOP_K))
        loc_ref[buf, c0:c0 + rows, :] = jnp.dot(onehot.astype(BF16), hb,
                                                preferred_element_type=F32).astype(loc_ref.dtype)

    _start_runs(blk, gstart_ref, lstart_ref, nch_ref, copier(buf))

    @pl.when(blk == last)
    def _():
        @pl.when(blk >= 1)
        def _():
            _wait_runs(ntot_ref[blk - 1], copier(1 - buf))
        _wait_runs(ntot_ref[blk], copier(buf))


def _dispatch(h, meta, runs, fill, n_slots):
    n, d = h.shape
    tm = ROUTE_BLOCK
    grid_spec = pltpu.PrefetchScalarGridSpec(
        num_scalar_prefetch=5,
        grid=(n // tm,),
        in_specs=[pl.BlockSpec((tm, META_LANES), lambda i, *_: (i, 0)),
                  pl.BlockSpec((tm, d), lambda i, *_: (i, 0))],
        out_specs=pl.BlockSpec(memory_space=pl.ANY),
        scratch_shapes=[pltpu.VMEM((2, LOCAL_SLOTS, d), F32), pltpu.VMEM((ROUTE_TILE, d), F32),
                        pltpu.SemaphoreType.DMA((2,)), pltpu.SemaphoreType.DMA(())])
    return pl.pallas_call(
        _dispatch_kernel,
        grid_spec=grid_spec,
        out_shape=jax.ShapeDtypeStruct((n_slots, d), F32),
        compiler_params=_params(("arbitrary",), VMEM_LIMIT),
        name="moe_dispatch",
    )(fill, *runs, meta, h)


def _experts_kernel(te_ref, rows_ref, nv_ref, x_ref, wgu_ref, bgu_ref, wd_ref, bd_ref, y_ref,
                    wgub_ref, wdb_ref, *, d_ff):
    i = pl.program_id(0)
    valid = i < nv_ref[0]
    fresh = jnp.logical_or(i == 0, te_ref[i] != te_ref[jnp.maximum(i - 1, 0)])

    @pl.when(jnp.logical_and(valid, fresh))
    def _():
        wgub_ref[...] = wgu_ref[...].astype(BF16)
        wdb_ref[...] = wd_ref[...].astype(BF16)

    @pl.when(valid)
    def _():
        row = lax.broadcasted_iota(jnp.int32, x_ref.shape, 0)
        x = jnp.where(row < rows_ref[i], x_ref[...], 0.0).astype(BF16)
        acc = None
        for c0 in range(0, d_ff, FF_CHUNK):
            g_cols, u_cols = slice(c0, c0 + FF_CHUNK), slice(d_ff + c0, d_ff + c0 + FF_CHUNK)
            gate = jnp.dot(x, wgub_ref[:, g_cols], preferred_element_type=F32) + bgu_ref[:, g_cols]
            up = jnp.dot(x, wgub_ref[:, u_cols], preferred_element_type=F32) + bgu_ref[:, u_cols]
            gate = jnp.minimum(gate, SWIGLU_LIMIT)
            up = jnp.clip(up, -SWIGLU_LIMIT, SWIGLU_LIMIT)
            act = (up + 1) * gate * jax.nn.sigmoid(SWIGLU_ALPHA * gate)
            part = jnp.dot(act.astype(BF16), wdb_ref[c0:c0 + FF_CHUNK, :], preferred_element_type=F32)
            acc = part if acc is None else acc + part
        y_ref[...] = acc + bd_ref[...]

    @pl.when(jnp.logical_not(valid))
    def _():
        y_ref[...] = jnp.zeros_like(y_ref)


def _experts(xs, te, rows, n_valid, w_gu, b_gu, w_down, b_down, layer):
    n_slots, d = xs.shape
    _, e, _, two_ff = w_gu.shape
    last = lambda i, nv: jnp.minimum(i, nv[0] - 1)
    grid_spec = pltpu.PrefetchScalarGridSpec(
        num_scalar_prefetch=3,
        grid=(n_slots // ROUTE_TILE,),
        in_specs=[pl.BlockSpec((ROUTE_TILE, d), lambda i, te, rw, nv: (last(i, nv), 0)),
                  pl.BlockSpec((None, None, d, two_ff), lambda i, te, rw, nv: (layer, te[i], 0, 0)),
                  pl.BlockSpec((None, None, 1, two_ff), lambda i, te, rw, nv: (layer, te[i], 0, 0)),
                  pl.BlockSpec((None, None, two_ff // 2, d), lambda i, te, rw, nv: (layer, te[i], 0, 0)),
                  pl.BlockSpec((None, None, 1, d), lambda i, te, rw, nv: (layer, te[i], 0, 0))],
        out_specs=pl.BlockSpec((ROUTE_TILE, d), lambda i, te, rw, nv: (i, 0)),
        scratch_shapes=[pltpu.VMEM((d, two_ff), BF16), pltpu.VMEM((two_ff // 2, d), BF16)])
    return pl.pallas_call(
        functools.partial(_experts_kernel, d_ff=two_ff // 2),
        grid_spec=grid_spec,
        out_shape=jax.ShapeDtypeStruct((n_slots, d), F32),
        compiler_params=_params(("arbitrary",), VMEM_LIMIT),
        name="moe_experts",
    )(te, rows, n_valid, xs, w_gu, b_gu.reshape(b_gu.shape[0], e, 1, two_ff), w_down,
      b_down.reshape(b_down.shape[0], e, 1, d))


def _combine_kernel(gstart_ref, lstart_ref, nch_ref, ntot_ref, meta_ref, x_ref, gate_ref, y_ref, *refs,
                    split_tiles):
    *o_refs, loc_ref, sem = refs
    blk = pl.program_id(0)

    buf = blk % 2

    def copier(b):
        def copy(l0, g0, rows):
            return pltpu.make_async_copy(y_ref.at[pl.ds(g0, rows), :], loc_ref.at[b, pl.ds(l0, rows), :],
                                         sem.at[b])
        return copy

    @pl.when(blk == 0)
    def _():
        loc_ref[...] = jnp.zeros_like(loc_ref)
        _start_runs(blk, gstart_ref, lstart_ref, nch_ref, copier(buf))

    @pl.when(blk + 1 < pl.num_programs(0))
    def _():
        _start_runs(blk + 1, gstart_ref, lstart_ref, nch_ref, copier(1 - buf))

    _wait_runs(ntot_ref[blk], copier(buf))

    meta = meta_ref[...]
    tm = meta.shape[0]
    cols = 256
    acc = jnp.zeros(x_ref.shape, F32)
    for c0 in range(0, LOCAL_SLOTS, cols):
        slot = (c0 + lax.broadcasted_iota(jnp.int32, (tm, cols), 1)).astype(F32)
        wts = sum(jnp.where(slot == meta[:, k:k + 1], meta[:, TOP_K + k:TOP_K + k + 1], 0.0)
                  for k in range(TOP_K))
        acc = acc + jnp.dot(wts.astype(BF16), loc_ref[buf, c0:c0 + cols, :].astype(BF16),
                            preferred_element_type=F32)
    out = x_ref[...] + gate_ref[...] * acc
    if split_tiles is None:
        o_refs[0][...] = out
    else:
        @pl.when(blk < split_tiles)
        def _():
            o_refs[0][...] = out

        @pl.when(blk >= split_tiles)
        def _():
            o_refs[1][...] = out


def _combine(y, meta, runs, x, gate, n_p, t_s, split=False):
    n, d = x.shape
    tm = ROUTE_BLOCK
    group = _group_of_tile(tm, n_p, t_s)
    st = n_p // tm
    if split:
        out_specs = [pl.BlockSpec((tm, d), lambda i, *_: (jnp.minimum(i, st - 1), 0)),
                     pl.BlockSpec((tm, d), lambda i, *_: (jnp.maximum(i - st, 0), 0))]
        out_shape = [jax.ShapeDtypeStruct((n_p, d), F32), jax.ShapeDtypeStruct((n - n_p, d), F32)]
    else:
        out_specs = pl.BlockSpec((tm, d), lambda i, *_: (i, 0))
        out_shape = jax.ShapeDtypeStruct((n, d), F32)
    grid_spec = pltpu.PrefetchScalarGridSpec(
        num_scalar_prefetch=4,
        grid=(n // tm,),
        in_specs=[pl.BlockSpec((tm, META_LANES), lambda i, *_: (i, 0)),
                  pl.BlockSpec((tm, d), lambda i, *_: (i, 0)),
                  pl.BlockSpec((None, 1, d), lambda i, *_: (group(i), 0, 0)),
                  pl.BlockSpec(memory_space=pl.ANY)],
        out_specs=out_specs,
        scratch_shapes=[pltpu.VMEM((2, LOCAL_SLOTS, d), F32), pltpu.SemaphoreType.DMA((2,))])
    return pl.pallas_call(
        functools.partial(_combine_kernel, split_tiles=st if split else None),
        grid_spec=grid_spec,
        out_shape=out_shape,
        compiler_params=_params(("arbitrary",), VMEM_LIMIT),
        name="moe_combine",
    )(*runs, meta, x, gate.reshape(N_GROUPS, 1, d), y)


def _moe(x, gain, scale, shift, gate, w_router, b_router, w_gu, b_gu, w_down, b_down, layer, n_p, t_s,
         split=False):
    n = x.shape[0]
    assert LOCAL_SLOTS >= ROUTE_BLOCK * TOP_K + N_EXPERTS * (SLOT_ALIGN - 1) and n % ROUTE_BLOCK == 0
    min_tiles = n * TOP_K // ROUTE_TILE
    max_slots = n * TOP_K + (n // ROUTE_BLOCK) * N_EXPERTS * (SLOT_ALIGN - 1) + N_EXPERTS * (ROUTE_TILE - 1)
    n_tiles = -(-max_slots // ROUTE_TILE)
    h, meta, cnt_blk, cnt_tot = _router(x, gain, scale, shift, w_router, b_router, layer, n_p, t_s)
    runs, te, rows, n_valid, fill = _route_plan(cnt_blk, cnt_tot, n_tiles, min_tiles)
    xs = _dispatch(h, meta, runs, fill, n_tiles * ROUTE_TILE)
    y = _experts(xs, te, rows, n_valid, w_gu, b_gu, w_down, b_down, layer)
    return _combine(y, meta, runs, x, gate, n_p, t_s, split)


def kernel(x_prompt, x_sample, c, cache_a_k, cache_a_v, state_b_fwd, state_b_bwd, cache_c_ckv,
           cache_c_krope, c_ctx, w_mod, b_mod, norm_mix, norm_ffn, e_w_in, e_w_out, e_a_qnorm,
           e_a_knorm, e_a_sink, e_b_conv, e_b_alog, e_b_dtbias, e_b_onorm, o_w_in, o_q_lora_norm,
           o_kv_lora_norm, o_w_uq, o_w_ukv, o_qnorm, o_knorm, o_w_out, moe_w_router, moe_b_router,
           moe_w_gu, moe_b_gu, moe_w_down, moe_b_down):
    bp, tp, d = x_prompt.shape
    bs, ts, _ = x_sample.shape
    depth = w_mod.shape[0]
    n_p, n_s = bp * tp, bs * ts
    n = n_p + n_s
    assert bs + 1 <= N_GROUPS and ts % 512 == 0 and n_p % ts == 0

    x = (x_prompt.reshape(n_p, d), x_sample.reshape(n_s, d))
    cond = jnp.concatenate([c_ctx[None], c, jnp.zeros((N_GROUPS - 1 - bs, d), F32)], axis=0)
    mod = _adaln(cond, w_mod, b_mod)

    new_a_k, new_a_v, new_b_fwd, new_b_bwd, new_c_ckv, new_c_krope = [], [], [], [], [], []
    for layer in range(depth):
        sh1, sc1, g1, sh2, sc2, g2 = (mod[layer, j] for j in range(6))
        h = _modulate(x, norm_mix[layer], sc1, sh1, n_p, ts)
        i = layer // 2
        if layer % 2 == 0:
            main_w = (A_HEADS + 2 * A_KV_HEADS) * A_HEAD_DIM + 4 * B_HEADS * B_DK
            proj = _matmul((h,), e_w_in, i, 0, main_w // 2, main_w, name="even_in_proj")
            tail = _matmul((h,), e_w_in[i][None, :, main_w:], 0, 0, 4 * B_HEADS, 4 * B_HEADS,
                           name="even_gate_proj")
            oa_p, kn_p = _attn_a_prompt(proj, bp, tp, e_a_sink[i], e_a_qnorm[i], e_a_knorm[i])
            oa_s = _attn_a_sample(proj, n_p, bs, ts, cache_a_k[:, i], cache_a_v[:, i],
                                  e_a_sink[i], e_a_qnorm[i], e_a_knorm[i])
            zeros = jnp.zeros((bp, B_HEADS, B_DK, B_DV), F32)
            ob_p, s_f, s_b = _deltanet(proj, tail, 0, bp, tp, e_b_conv[i], e_b_alog[i],
                                       e_b_dtbias[i], e_b_onorm[i], zeros, zeros)
            ob_s, _, _ = _deltanet(proj, tail, n_p, bs, ts, e_b_conv[i], e_b_alog[i],
                                   e_b_dtbias[i], e_b_onorm[i], state_b_fwd[:, i], state_b_bwd[:, i])
            x = (_matmul_residual([(oa_p, oa_s), (ob_p, ob_s)], e_w_out, i, x, g1, n_p, ts),)
            kw = A_KV_HEADS * A_HEAD_DIM
            new_a_k.append(kn_p.reshape(bp, tp, A_KV_HEADS, A_HEAD_DIM).transpose(0, 2, 1, 3))
            v_p = proj[:n_p, A_HEADS * A_HEAD_DIM + kw:A_HEADS * A_HEAD_DIM + 2 * kw]
            new_a_v.append(v_p.reshape(bp, tp, A_KV_HEADS, A_HEAD_DIM).transpose(0, 2, 1, 3))
            new_b_fwd.append(s_f)
            new_b_bwd.append(s_b)
        else:
            p1 = _matmul((h,), o_w_in, i, 0, o_w_in.shape[-1], o_w_in.shape[-1], name="odd_in_proj")
            cq, ckv, ckv_b = _lora_norm(p1, o_q_lora_norm[i], o_kv_lora_norm[i])
            q = _matmul((cq,), o_w_uq, i, 0, o_w_uq.shape[-1] // 2, o_w_uq.shape[-1], name="odd_uq")
            ckv_ctx = cache_c_ckv[:, i].reshape(-1, C_KV_LORA).astype(BF16)
            kv = _matmul((ckv_b, ckv_ctx), o_w_ukv, i, 0, o_w_ukv.shape[-1] // 2, o_w_ukv.shape[-1],
                         name="odd_ukv")
            o_p = _attn_c_prompt(q, kv, p1, bp, tp, o_qnorm[i], o_knorm[i])
            o_s = _attn_c_sample(q, kv, p1, n_p, bs, ts, cache_c_krope[:, i], o_qnorm[i], o_knorm[i])
            x = (_matmul_residual([(o_p, o_s)], o_w_out, i, x, g1, n_p, ts),)
            new_c_ckv.append(ckv[:n_p].reshape(bp, tp, C_KV_LORA))
            new_c_krope.append(p1[:n_p, C_Q_LORA + C_KV_LORA:].reshape(bp, tp, C_ROPE))
        x = _moe(x[0], norm_ffn[layer], sc2, sh2, g2, moe_w_router, moe_b_router, moe_w_gu, moe_b_gu,
                 moe_w_down, moe_b_down, layer, n_p, ts, split=layer == depth - 1)
        x = tuple(x) if layer == depth - 1 else (x,)

    return (x[0].reshape(bp, tp, d), x[1].reshape(bs, ts, d),
            jnp.stack(new_a_k, axis=1), jnp.stack(new_a_v, axis=1),
            jnp.stack(new_b_fwd, axis=1), jnp.stack(new_b_bwd, axis=1),
            jnp.stack(new_c_ckv, axis=1), jnp.stack(new_c_krope, axis=1))
```

```python
import functools

import numpy as np
import jax
import jax.numpy as jnp
from jax import lax
from jax.experimental import pallas as pl
from jax.experimental.pallas import tpu as pltpu

F32 = jnp.float32
BF16 = jnp.bfloat16

EPS = 1e-6
NEG_INF = -1e30
ROPE_BASE = 10000.0
GRID_W = 64
N_GROUPS = 8

A_HEADS, A_KV_HEADS, A_GROUP, A_HEAD_DIM, A_WINDOW, A_BLOCK = 8, 2, 4, 64, 128, 128
B_HEADS, B_DK, B_DV, B_CHUNK = 8, 64, 64, 64
C_HEADS, C_NOPE, C_ROPE, C_V, C_Q_LORA, C_KV_LORA = 16, 64, 32, 64, 384, 256
C_QK = C_NOPE + C_ROPE
N_EXPERTS, TOP_K = 32, 4
SWIGLU_LIMIT, SWIGLU_ALPHA = 7.0, 1.702

VMEM_LIMIT = 56 * 1024 * 1024


def _params(sem, vmem=None):
    return pltpu.CompilerParams(dimension_semantics=sem, vmem_limit_bytes=vmem)


def _bdot(a, b):
    return jnp.dot(a.astype(BF16), b.astype(BF16), preferred_element_type=F32)


def _bdot_nt(a, b):
    return lax.dot_general(a.astype(BF16), b.astype(BF16), (((1,), (1,)), ((), ())),
                           preferred_element_type=F32)


def _dot3(a, b):
    a_hi, b_hi = a.astype(BF16), b.astype(BF16)
    a_lo = (a - a_hi.astype(F32)).astype(BF16)
    b_lo = (b - b_hi.astype(F32)).astype(BF16)
    dot = functools.partial(jnp.dot, preferred_element_type=F32)
    return dot(a_hi, b_hi) + (dot(a_hi, b_lo) + dot(a_lo, b_hi))


def _rms(x, gain):
    return x * lax.rsqrt(jnp.mean(x * x, axis=-1, keepdims=True) + EPS) * gain


def _silu(x):
    return x * jax.nn.sigmoid(x)


def _group_of_tile(tm, n_p, t_s):
    def group(i):
        r = i * tm
        return jnp.where(r < n_p, 0, 1 + (r - n_p) // t_s)
    return group


def _adaln_kernel(cond_ref, w_ref, b_ref, o_ref):
    o_ref[...] = _bdot(_silu(cond_ref[...]), w_ref[...]) + b_ref[...]


def _adaln(cond, w_mod, b_mod):
    depth, d, _ = w_mod.shape
    return pl.pallas_call(
        _adaln_kernel,
        grid=(depth, 6),
        in_specs=[pl.BlockSpec((N_GROUPS, d), lambda l, j: (0, 0)),
                  pl.BlockSpec((None, d, d), lambda l, j: (l, 0, j)),
                  pl.BlockSpec((None, 1, d), lambda l, j: (l, 0, j))],
        out_specs=pl.BlockSpec((None, None, N_GROUPS, d), lambda l, j: (l, j, 0, 0)),
        out_shape=jax.ShapeDtypeStruct((depth, 6, N_GROUPS, d), F32),
        compiler_params=_params(("arbitrary", "arbitrary")),
        name="adaln",
    )(cond, w_mod, b_mod.reshape(depth, 1, 6 * d))


def _row_specs(arrays, tm, pos):
    specs, ends, start = [], [], 0
    for arr in arrays:
        nt = arr.shape[0] // tm
        specs.append(pl.BlockSpec((tm, arr.shape[1]),
                                  lambda *ids, s=start, nt=nt: (jnp.clip(ids[pos] - s, 0, nt - 1), 0)))
        start += nt
        ends.append(start)
    return specs, tuple(ends)


def _pick_rows(refs, ends, i):
    x = refs[-1][...]
    for ref, end in zip(reversed(refs[:-1]), reversed(ends[:-1])):
        x = jnp.where(i < end, ref[...], x)
    return x


def _modulate_kernel(*refs, ends):
    n = len(ends)
    g_ref, sc_ref, sh_ref, o_ref = refs[n:]
    y = _rms(_pick_rows(refs[:n], ends, pl.program_id(0)), g_ref[...])
    o_ref[...] = (y * (1 + sc_ref[...]) + sh_ref[...]).astype(o_ref.dtype)


def _modulate(xs, gain, scale, shift, n_p, t_s, tm=512):
    n, d = sum(x.shape[0] for x in xs), xs[0].shape[1]
    group = _group_of_tile(tm, n_p, t_s)
    specs, ends = _row_specs(xs, tm, 0)
    return pl.pallas_call(
        functools.partial(_modulate_kernel, ends=ends),
        grid=(n // tm,),
        in_specs=specs + [pl.BlockSpec((1, d), lambda i: (0, 0)),
                          pl.BlockSpec((None, 1, d), lambda i: (group(i), 0, 0)),
                          pl.BlockSpec((None, 1, d), lambda i: (group(i), 0, 0))],
        out_specs=pl.BlockSpec((tm, d), lambda i: (i, 0)),
        out_shape=jax.ShapeDtypeStruct((n, d), BF16),
        compiler_params=_params(("arbitrary",)),
        name="modulate",
    )(*xs, gain.reshape(1, d), scale.reshape(N_GROUPS, 1, d), shift.reshape(N_GROUPS, 1, d))


def _mm_kernel(*refs, ends):
    n = len(ends)
    w_ref, o_ref, wb_ref = refs[n:]

    @pl.when(pl.program_id(1) == 0)
    def _():
        wb_ref[...] = w_ref[...].astype(BF16)
    x = _pick_rows(refs[:n], ends, pl.program_id(1))
    o_ref[...] = jnp.dot(x, wb_ref[...], preferred_element_type=F32).astype(o_ref.dtype)


def _matmul(xs, w3, layer, col0_blocks, tn, n_out, out_dtype=F32, tm=512, name="matmul"):
    n, k = sum(x.shape[0] for x in xs), xs[0].shape[1]
    specs, ends = _row_specs(xs, tm, 1)
    return pl.pallas_call(
        functools.partial(_mm_kernel, ends=ends),
        grid=(n_out // tn, n // tm),
        in_specs=specs + [pl.BlockSpec((None, k, tn), lambda j, i: (layer, 0, col0_blocks + j))],
        out_specs=pl.BlockSpec((tm, tn), lambda j, i: (i, j)),
        out_shape=jax.ShapeDtypeStruct((n, n_out), out_dtype),
        scratch_shapes=[pltpu.VMEM((k, tn), BF16)],
        compiler_params=_params(("arbitrary", "arbitrary"), VMEM_LIMIT),
        name=name,
    )(*xs, w3)


def _mm_res_kernel(*refs, col_ends, res_ends):
    i = pl.program_id(0)
    pos = 0
    cols = []
    for ends in col_ends:
        cols.append(refs[pos:pos + len(ends)])
        pos += len(ends)
    res_refs = refs[pos:pos + len(res_ends)]
    w_ref, gate_ref, o_ref, wb_ref = refs[pos + len(res_ends):]

    @pl.when(i == 0)
    def _():
        wb_ref[...] = w_ref[...].astype(BF16)
    y, k0 = None, 0
    for col_refs, ends in zip(cols, col_ends):
        x = _pick_rows(col_refs, ends, i)
        part = jnp.dot(x, wb_ref[k0:k0 + x.shape[1], :], preferred_element_type=F32)
        y = part if y is None else y + part
        k0 += x.shape[1]
    o_ref[...] = _pick_rows(res_refs, res_ends, i) + gate_ref[...] * y


def _matmul_residual(mix_cols, w3, layer, res, gate, n_p, t_s, tm=512):
    n, d = sum(r.shape[0] for r in res), res[0].shape[1]
    k = sum(col[0].shape[1] for col in mix_cols)
    group = _group_of_tile(tm, n_p, t_s)
    specs, col_ends = [], []
    for col in mix_cols:
        s, e = _row_specs(col, tm, 0)
        specs += s
        col_ends.append(e)
    res_specs, res_ends = _row_specs(res, tm, 0)
    return pl.pallas_call(
        functools.partial(_mm_res_kernel, col_ends=tuple(col_ends), res_ends=res_ends),
        grid=(n // tm,),
        in_specs=specs + res_specs + [pl.BlockSpec((None, k, d), lambda i: (layer, 0, 0)),
                                      pl.BlockSpec((None, 1, d), lambda i: (group(i), 0, 0))],
        out_specs=pl.BlockSpec((tm, d), lambda i: (i, 0)),
        out_shape=jax.ShapeDtypeStruct((n, d), F32),
        scratch_shapes=[pltpu.VMEM((k, d), BF16)],
        compiler_params=_params(("arbitrary",), VMEM_LIMIT),
        name="out_proj_residual",
    )(*[piece for col in mix_cols for piece in col], *res, w3, gate.reshape(N_GROUPS, 1, d))


def _rope_tables(t_len, d):
    half, quarter = d // 2, d // 4
    pos = np.arange(t_len)
    row, col = pos // GRID_W, pos % GRID_W
    inv = ROPE_BASE ** (-np.arange(quarter, dtype=np.float64) / quarter)
    ang_r = row[:, None] * inv[None, :]
    ang_c = col[:, None] * inv[None, :]
    cos = np.concatenate([np.cos(ang_r), np.cos(ang_r), np.cos(ang_c), np.cos(ang_c)], axis=1)
    sin = np.concatenate([-np.sin(ang_r), np.sin(ang_r), -np.sin(ang_c), np.sin(ang_c)], axis=1)
    return jnp.asarray(cos, F32), jnp.asarray(sin, F32)


def _swap_pairs(x):
    q = x.shape[-1] // 4
    return jnp.concatenate([x[:, q:2 * q], x[:, :q], x[:, 3 * q:], x[:, 2 * q:3 * q]], axis=-1)


def _rope(x, cos, sin):
    return x * cos + _swap_pairs(x) * sin


def _softmax_attend(parts, sink):
    m = parts[0][0].max(axis=-1, keepdims=True)
    for s, _ in parts[1:]:
        m = jnp.maximum(m, s.max(axis=-1, keepdims=True))
    if sink is not None:
        m = jnp.maximum(m, sink)
    den = jnp.exp(sink - m) if sink is not None else 0.0
    acc = None
    for s, v in parts:
        p = jnp.exp(s - m)
        den = den + p.sum(axis=-1, keepdims=True)
        o = _bdot(p, v)
        acc = o if acc is None else acc + o
    return acc / den


def _group_sinks(sink_ref, hk, rows):
    head = lax.broadcasted_iota(jnp.int32, (A_GROUP * rows, 1), 0) // rows
    col = jnp.full((A_GROUP * rows, 1), sink_ref[hk * A_GROUP], F32)
    for g in range(1, A_GROUP):
        col = jnp.where(head == g, sink_ref[hk * A_GROUP + g], col)
    return col


def _attn_a_prompt_kernel(sink_ref, q_ref, kv_ref, gq_ref, gk_ref, o_ref, kn_ref):
    scale = A_HEAD_DIM ** -0.5
    q = q_ref[...].astype(F32)
    kv = kv_ref[...].astype(F32)
    t = q.shape[0]
    outs = []
    kns = []
    for hk in range(A_KV_HEADS):
        k = _rms(kv[:, hk * A_HEAD_DIM:(hk + 1) * A_HEAD_DIM], gk_ref[...])
        v = kv[:, (A_KV_HEADS + hk) * A_HEAD_DIM:(A_KV_HEADS + hk + 1) * A_HEAD_DIM]
        kns.append(k)
        qg = jnp.concatenate([_rms(q[:, h * A_HEAD_DIM:(h + 1) * A_HEAD_DIM], gq_ref[...])
                              for h in range(hk * A_GROUP, (hk + 1) * A_GROUP)], axis=0)
        s = _bdot_nt(qg, k) * scale
        o = _softmax_attend([(s, v)], _group_sinks(sink_ref, hk, t))
        outs += [o[g * t:(g + 1) * t] for g in range(A_GROUP)]
    o_ref[...] = jnp.concatenate(outs, axis=-1).astype(o_ref.dtype)
    kn_ref[...] = jnp.concatenate(kns, axis=-1)


def _attn_a_prompt(proj, n_seq, t, sink, gq, gk):
    qw = A_HEADS * A_HEAD_DIM
    kvw = 2 * A_KV_HEADS * A_HEAD_DIM
    return pl.pallas_call(
        _attn_a_prompt_kernel,
        grid=(n_seq,),
        in_specs=[pl.BlockSpec(memory_space=pltpu.SMEM),
                  pl.BlockSpec((t, qw), lambda b: (b, 0)),
                  pl.BlockSpec((t, kvw), lambda b: (b, qw // kvw)),
                  pl.BlockSpec((1, A_HEAD_DIM), lambda b: (0, 0)),
                  pl.BlockSpec((1, A_HEAD_DIM), lambda b: (0, 0))],
        out_specs=[pl.BlockSpec((t, qw), lambda b: (b, 0)),
                   pl.BlockSpec((t, A_KV_HEADS * A_HEAD_DIM), lambda b: (b, 0))],
        out_shape=[jax.ShapeDtypeStruct((n_seq * t, qw), BF16),
                   jax.ShapeDtypeStruct((n_seq * t, A_KV_HEADS * A_HEAD_DIM), F32)],
        compiler_params=_params(("arbitrary",)),
        name="attn_a_context",
    )(sink, proj, proj, gq.reshape(1, -1), gk.reshape(1, -1))


def _attn_a_sample_kernel(sink_ref, q_ref, kv_ref, kc_ref, vc_ref, gq_ref, gk_ref, cos_ref, sin_ref,
                          o_ref, *, t):
    scale = A_HEAD_DIM ** -0.5
    i = pl.program_id(1)
    win = 3 * A_BLOCK
    q0 = pl.multiple_of(i * A_BLOCK, A_BLOCK)
    k0 = pl.multiple_of(jnp.clip((i - 1) * A_BLOCK, 0, t - win), A_BLOCK)
    q = q_ref[...].astype(F32)
    kv = kv_ref[pl.ds(k0, win), :].astype(F32)
    cq, sq = cos_ref[pl.ds(q0, A_BLOCK), :], sin_ref[pl.ds(q0, A_BLOCK), :]
    ck, sk = cos_ref[pl.ds(k0, win), :], sin_ref[pl.ds(k0, win), :]
    rows = A_GROUP * A_BLOCK
    qpos = q0 + lax.broadcasted_iota(jnp.int32, (rows, win), 0) % A_BLOCK
    kpos = k0 + lax.broadcasted_iota(jnp.int32, (rows, win), 1)
    mask = jnp.abs(qpos - kpos) <= A_WINDOW
    outs = []
    for hk in range(A_KV_HEADS):
        k = _rope(_rms(kv[:, hk * A_HEAD_DIM:(hk + 1) * A_HEAD_DIM], gk_ref[...]), ck, sk)
        v = kv[:, (A_KV_HEADS + hk) * A_HEAD_DIM:(A_KV_HEADS + hk + 1) * A_HEAD_DIM]
        qg = jnp.concatenate([_rope(_rms(q[:, h * A_HEAD_DIM:(h + 1) * A_HEAD_DIM], gq_ref[...]), cq, sq)
                              for h in range(hk * A_GROUP, (hk + 1) * A_GROUP)], axis=0)
        s1 = jnp.where(mask, _bdot_nt(qg, k) * scale, NEG_INF)
        s2 = _bdot_nt(qg, kc_ref[hk]) * scale
        o = _softmax_attend([(s1, v), (s2, vc_ref[hk])], _group_sinks(sink_ref, hk, A_BLOCK))
        outs += [o[g * A_BLOCK:(g + 1) * A_BLOCK] for g in range(A_GROUP)]
    o_ref[...] = jnp.concatenate(outs, axis=-1).astype(o_ref.dtype)


def _attn_a_sample(proj, row0, n_seq, t, k_ctx, v_ctx, sink, gq, gk):
    qw = A_HEADS * A_HEAD_DIM
    kvw = 2 * A_KV_HEADS * A_HEAD_DIM
    nqb = t // A_BLOCK
    cos, sin = _rope_tables(t, A_HEAD_DIM)
    past = k_ctx.shape[2]
    return pl.pallas_call(
        functools.partial(_attn_a_sample_kernel, t=t),
        grid=(n_seq, nqb),
        in_specs=[pl.BlockSpec(memory_space=pltpu.SMEM),
                  pl.BlockSpec((A_BLOCK, qw), lambda b, i: (row0 // A_BLOCK + b * nqb + i, 0)),
                  pl.BlockSpec((t, kvw), lambda b, i: (row0 // t + b, qw // kvw)),
                  pl.BlockSpec((None, A_KV_HEADS, past, A_HEAD_DIM), lambda b, i: (b, 0, 0, 0)),
                  pl.BlockSpec((None, A_KV_HEADS, past, A_HEAD_DIM), lambda b, i: (b, 0, 0, 0)),
                  pl.BlockSpec((1, A_HEAD_DIM), lambda b, i: (0, 0)),
                  pl.BlockSpec((1, A_HEAD_DIM), lambda b, i: (0, 0)),
                  pl.BlockSpec((t, A_HEAD_DIM), lambda b, i: (0, 0)),
                  pl.BlockSpec((t, A_HEAD_DIM), lambda b, i: (0, 0))],
        out_specs=pl.BlockSpec((A_BLOCK, qw), lambda b, i: (b * nqb + i, 0)),
        out_shape=jax.ShapeDtypeStruct((n_seq * t, qw), BF16),
        compiler_params=_params(("arbitrary", "arbitrary")),
        name="attn_a_latent",
    )(sink, proj, proj, k_ctx, v_ctx, gq.reshape(1, -1), gk.reshape(1, -1), cos, sin)


def _per_head_lanes(x, fn):
    lane = lax.broadcasted_iota(jnp.int32, x.shape, 1)
    lo = fn(x[:, :B_DK])
    hi = fn(x[:, B_DK:])
    return jnp.where(lane < B_DK, lo, hi)


def _conv_silu(x, w):
    t = x.shape[0]
    row = lax.broadcasted_iota(jnp.int32, x.shape, 0)
    prev = jnp.where(row == 0, 0.0, pltpu.roll(x, 1, 0))
    nxt = jnp.where(row == t - 1, 0.0, pltpu.roll(x, t - 1, 0))
    return _silu(prev * w[0:1, :] + x * w[1:2, :] + nxt * w[2:3, :])


M_SAME, M_TRI, M_TRI_T, M_STRICT, M_BDIAG = range(5)


def _delta_masks(r):
    c = B_CHUNK
    ii = lax.broadcasted_iota(jnp.int32, (r, r), 0)
    jj = lax.broadcasted_iota(jnp.int32, (r, r), 1)
    same = (ii // c) == (jj // c)
    ahead = jnp.where(ii < r // 2, ii - jj, jj - ii)
    tri = jnp.where(same, ahead, -1) >= 0
    tri_t = jnp.where(same, ahead, 1) <= 0
    strict = jnp.where(same, ahead, -1) > 0
    bdiag = (ii // 16) == (jj // 16)
    return [x.astype(F32) for x in (same, tri, tri_t, strict, bdiag)]


def _delta_prepare(q, k, v, g_col, g_row, beta, mask_ref):
    c = B_CHUNK
    r = q.shape[0]
    dot = functools.partial(jnp.dot, preferred_element_type=F32)
    gc_col = jnp.sum(mask_ref[M_TRI] * g_row, axis=1, keepdims=True)
    gc_row = jnp.sum(mask_ref[M_TRI_T] * g_col, axis=0, keepdims=True)
    g_tot = jnp.sum(mask_ref[M_SAME] * g_row, axis=1, keepdims=True)
    ex = jnp.exp((gc_col - gc_row) * mask_ref[M_TRI])
    kb = k * beta
    qk = _bdot_nt(jnp.concatenate([kb, q], axis=0), k)
    m = qk[:r] * (ex * mask_ref[M_STRICT])
    aqk = qk[r:] * (ex * mask_ref[M_TRI])
    dg = m * mask_ref[M_BDIAG]
    off = m - dg
    n1 = -dg
    n1b = n1.astype(BF16)
    n2 = dot(n1b, n1b)
    n2b = n2.astype(BF16)
    t = dot(jnp.concatenate([n1b, n2b], axis=0), n2b)
    xs = n1 + n2 + t[:r]
    n4 = t[r:]
    n4b = n4.astype(BF16)
    t = dot(jnp.concatenate([xs.astype(BF16), n4b], axis=0), n4b)
    xs = xs + n4 + t[:r]
    n8 = t[r:]
    xs = xs + n8 + dot(xs.astype(BF16), n8.astype(BF16))
    xsb = xs.astype(BF16)
    f = -(off + dot(xsb, off.astype(BF16)))
    fb = f.astype(BF16)
    t = dot(fb, jnp.concatenate([xsb, fb], axis=1))
    ys = xs + f + t[:, :r]
    f2 = t[:, r:]
    ts = ys + f2 + dot(f2.astype(BF16), ys.astype(BF16))
    egc = jnp.exp(gc_col)
    rhs = jnp.concatenate([v * beta, kb * egc], axis=-1)
    uw = rhs + _bdot(ts, rhs)

    def block_diag(x):
        return jnp.concatenate([x] * (r // c), axis=-1) * mask_ref[M_SAME]

    kd = k * jnp.exp(g_tot - gc_col)
    kd_t = jnp.concatenate([kd, jnp.zeros_like(kd)], axis=-1).T[:c]
    kd_t = jnp.concatenate([kd_t] * (r // c), axis=0) * mask_ref[M_SAME]
    e_tot = jnp.broadcast_to(jnp.exp(g_tot), v.shape)
    return uw[:, :B_DV], e_tot, block_diag(uw[:, B_DV:]), aqk, block_diag(q * egc), kd_t


def _deltanet_kernel(alog_ref, dtb_ref, q_ref, k_ref, v_ref, z_ref, cwq_ref, cwk_ref, cwv_ref,
                     tail_ref, tailt_ref, onorm_ref, s0f_ref, s0b_ref,
                     o_ref, sf_ref, sb_ref,
                     qc_ref, kc_ref, vc_ref, oacc_ref, u_ref, et_ref, w_ref, aqk_ref, qg_ref, kdt_ref,
                     mask_ref, *, n_chunks):
    hp = pl.program_id(1)
    c = B_CHUNK

    @pl.when(jnp.logical_and(pl.program_id(0) == 0, hp == 0))
    def _():
        for i, x in enumerate(_delta_masks(4 * c)):
            mask_ref[i] = x

    def l2n(x):
        ss = _per_head_lanes(x * x, lambda a: jnp.sum(a, axis=-1, keepdims=True))
        return x * lax.rsqrt(ss + EPS)

    qc_ref[...] = l2n(_conv_silu(q_ref[...].astype(F32), cwq_ref[...])) * (B_DK ** -0.5)
    kc_ref[...] = l2n(_conv_silu(k_ref[...].astype(F32), cwk_ref[...]))
    vc_ref[...] = _conv_silu(v_ref[...].astype(F32), cwv_ref[...])
    oacc_ref[...] = jnp.zeros_like(oacc_ref)

    lane32 = lax.broadcasted_iota(jnp.int32, (c, 4 * B_HEADS), 1)

    def gates(chunk, d, head):
        tail = tail_ref[pl.ds(pl.multiple_of(chunk * c, c), c), :]
        ia = 2 * d * B_HEADS + head
        ib = ia + B_HEADS
        a_col = jnp.sum(jnp.where(lane32 == ia, tail, 0.0), axis=1, keepdims=True)
        b_col = jnp.sum(jnp.where(lane32 == ib, tail, 0.0), axis=1, keepdims=True)
        a_row = tailt_ref[chunk, pl.ds(ia, 1), :]
        na = -jnp.exp(alog_ref[d, head])
        bias = dtb_ref[d, head]
        g_col = na * jax.nn.softplus(a_col + bias)
        g_row = na * jax.nn.softplus(a_row + bias)
        return g_col, g_row, jax.nn.sigmoid(b_col)

    def stacked(ref, chunks):
        parts = []
        for chunk in chunks:
            x = ref[pl.ds(pl.multiple_of(chunk * c, c), c), :]
            parts += [x[:, :B_DK], x[:, B_DK:]]
        return jnp.concatenate(parts, axis=0)

    def prepare(j, carry):
        chunks = (j, n_chunks - 1 - j)
        gs = [gates(chunks[d], d, 2 * hp + hh) for d in range(2) for hh in range(2)]
        g_col = jnp.concatenate([g[0] for g in gs], axis=0)
        g_row = jnp.concatenate([g[1] for g in gs], axis=1)
        beta = jnp.concatenate([g[2] for g in gs], axis=0)
        outs = _delta_prepare(stacked(qc_ref, chunks), stacked(kc_ref, chunks), stacked(vc_ref, chunks),
                              g_col, g_row, beta, mask_ref)
        for ref, x in zip((u_ref, et_ref, w_ref, aqk_ref, qg_ref, kdt_ref), outs):
            ref[j] = x.astype(ref.dtype)
        return carry

    lax.fori_loop(0, n_chunks, prepare, 0, unroll=4)

    def scan(j, s):
        sb = s.astype(BF16)
        delta = u_ref[j] - jnp.dot(w_ref[j], sb, preferred_element_type=F32)
        db = delta.astype(BF16)
        o = (jnp.dot(qg_ref[j], sb, preferred_element_type=F32)
             + jnp.dot(aqk_ref[j], db, preferred_element_type=F32))
        for d, chunk in enumerate((j, n_chunks - 1 - j)):
            rows = pl.ds(pl.multiple_of(chunk * c, c), c)
            oacc_ref[rows, :] += jnp.concatenate([o[2 * d * c:(2 * d + 1) * c],
                                                  o[(2 * d + 1) * c:(2 * d + 2) * c]], axis=-1)
        return s * et_ref[j] + jnp.dot(kdt_ref[j], db, preferred_element_type=F32)

    init = jnp.concatenate([s0f_ref[0], s0f_ref[1], s0b_ref[0], s0b_ref[1]], axis=0)
    fin = lax.fori_loop(0, n_chunks, scan, init)
    sf_ref[0], sf_ref[1], sb_ref[0], sb_ref[1] = (fin[i * B_DK:(i + 1) * B_DK] for i in range(4))

    o = oacc_ref[...]
    ms = _per_head_lanes(o * o, lambda a: jnp.mean(a, axis=-1, keepdims=True))
    o_ref[...] = (o * lax.rsqrt(ms + EPS) * onorm_ref[...] * _silu(z_ref[...].astype(F32))).astype(o_ref.dtype)


def _deltanet(proj, tail, row0, n_seq, t, conv_w, a_log, dt_bias, o_norm, s0_f, s0_b):
    c = B_CHUNK
    n_chunks = t // c
    lw = 2 * B_DK
    col_q = (A_HEADS + 2 * A_KV_HEADS) * A_HEAD_DIM // lw
    nhp = B_HEADS // 2
    rows = tail[row0:row0 + n_seq * t]
    tail_t = rows.reshape(n_seq, n_chunks, c, 4 * B_HEADS).transpose(0, 1, 3, 2)
    onorm2 = jnp.concatenate([o_norm, o_norm]).reshape(1, lw)
    b0 = row0 // t
    seq_blk = lambda off: pl.BlockSpec((t, lw), lambda b, h: (b0 + b, col_q + off + h))
    cw_blk = lambda off: pl.BlockSpec((3, lw), lambda b, h: (0, off + h))
    st_blk = pl.BlockSpec((None, 2, B_DK, B_DV), lambda b, h: (b, h, 0, 0))
    return pl.pallas_call(
        functools.partial(_deltanet_kernel, n_chunks=n_chunks),
        grid=(n_seq, nhp),
        in_specs=[pl.BlockSpec(memory_space=pltpu.SMEM), pl.BlockSpec(memory_space=pltpu.SMEM),
                  seq_blk(0), seq_blk(nhp), seq_blk(2 * nhp), seq_blk(3 * nhp),
                  cw_blk(0), cw_blk(nhp), cw_blk(2 * nhp),
                  pl.BlockSpec((t, 4 * B_HEADS), lambda b, h: (b, 0)),
                  pl.BlockSpec((None, n_chunks, 4 * B_HEADS, c), lambda b, h: (b, 0, 0, 0)),
                  pl.BlockSpec((1, lw), lambda b, h: (0, 0)),
                  st_blk, st_blk],
        out_specs=[pl.BlockSpec((t, lw), lambda b, h: (b, h)), st_blk, st_blk],
        out_shape=[jax.ShapeDtypeStruct((n_seq * t, B_HEADS * B_DV), BF16),
                   jax.ShapeDtypeStruct((n_seq, B_HEADS, B_DK, B_DV), F32),
                   jax.ShapeDtypeStruct((n_seq, B_HEADS, B_DK, B_DV), F32)],
        scratch_shapes=[pltpu.VMEM((t, lw), F32)] * 4 + [pltpu.VMEM((n_chunks, 4 * c, B_DV), F32)] * 2
        + [pltpu.VMEM((n_chunks, 4 * c, 4 * c), BF16)] * 4 + [pltpu.VMEM((5, 4 * c, 4 * c), F32)],
        compiler_params=_params(("arbitrary", "arbitrary")),
        name="deltanet",
    )(a_log, dt_bias, proj, proj, proj, proj, conv_w, conv_w, conv_w, rows, tail_t, onorm2, s0_f, s0_b)


def _lora_norm_kernel(p_ref, gq_ref, gkv_ref, cq_ref, ckv_ref, ckvb_ref):
    p = p_ref[...]
    cq_ref[...] = _rms(p[:, :C_Q_LORA], gq_ref[...]).astype(cq_ref.dtype)
    ckv = _rms(p[:, C_Q_LORA:C_Q_LORA + C_KV_LORA], gkv_ref[...])
    ckv_ref[...] = ckv
    ckvb_ref[...] = ckv.astype(ckvb_ref.dtype)


def _lora_norm(p1, gq, gkv, tm=512):
    n, w = p1.shape
    return pl.pallas_call(
        _lora_norm_kernel,
        grid=(n // tm,),
        in_specs=[pl.BlockSpec((tm, w), lambda i: (i, 0)),
                  pl.BlockSpec((1, C_Q_LORA), lambda i: (0, 0)),
                  pl.BlockSpec((1, C_KV_LORA), lambda i: (0, 0))],
        out_specs=[pl.BlockSpec((tm, C_Q_LORA), lambda i: (i, 0)),
                   pl.BlockSpec((tm, C_KV_LORA), lambda i: (i, 0)),
                   pl.BlockSpec((tm, C_KV_LORA), lambda i: (i, 0))],
        out_shape=[jax.ShapeDtypeStruct((n, C_Q_LORA), BF16),
                   jax.ShapeDtypeStruct((n, C_KV_LORA), F32),
                   jax.ShapeDtypeStruct((n, C_KV_LORA), BF16)],
        compiler_params=_params(("arbitrary",)),
        name="lora_norm",
    )(p1, gq.reshape(1, -1), gkv.reshape(1, -1))


def _mla_head_k(kv, kr, kr_ss, h, gk):
    kn = kv[:, h * (C_NOPE + C_V):h * (C_NOPE + C_V) + C_NOPE]
    v = kv[:, h * (C_NOPE + C_V) + C_NOPE:(h + 1) * (C_NOPE + C_V)]
    rn = lax.rsqrt((jnp.sum(kn * kn, axis=-1, keepdims=True) + kr_ss) / C_QK + EPS)
    return kn * rn * gk[:, :C_NOPE], kr * rn, v


C_STACK = 4


def _place(x, i, n):
    t, w = x.shape
    parts = ([jnp.zeros((t, i * w), x.dtype)] if i else []) + [x]
    if i < n - 1:
        parts.append(jnp.zeros((t, (n - 1 - i) * w), x.dtype))
    return jnp.concatenate(parts, axis=-1)


def _by_block(cols, lane, w):
    out = cols[-1]
    for i in reversed(range(len(cols) - 1)):
        out = jnp.where(lane < (i + 1) * w, cols[i], out)
    return out


def _attn_c_prompt_kernel(q_ref, kv_ref, p_ref, gq_ref, gk_ref, o_ref):
    scale = C_QK ** -0.5
    n = C_STACK
    q = q_ref[...].astype(F32)
    kv = kv_ref[...].astype(F32)
    gk = gk_ref[...]
    t = q.shape[0]
    kr_raw = p_ref[...][:, C_Q_LORA + C_KV_LORA:]
    kr_ss = jnp.sum(kr_raw * kr_raw, axis=-1, keepdims=True)
    kr_g = kr_raw * gk[:, C_NOPE:]
    lane_q = lax.broadcasted_iota(jnp.int32, (t, n * C_QK), 1)
    lane_o = lax.broadcasted_iota(jnp.int32, (t, n * C_V), 1)
    outs = []
    for grp in range(C_HEADS // n):
        qs = q[:, grp * n * C_QK:(grp + 1) * n * C_QK]
        rn = [lax.rsqrt(jnp.mean(qs[:, i * C_QK:(i + 1) * C_QK] ** 2, axis=-1, keepdims=True) + EPS)
              for i in range(n)]
        qs = qs * _by_block(rn, lane_q, C_QK) * gq_ref[...]
        k_rows, v_rows = [], []
        for i in range(n):
            kn, kr, v = _mla_head_k(kv, kr_g, kr_ss, grp * n + i, gk)
            k_rows.append(_place(jnp.concatenate([kn, kr], axis=-1), i, n))
            v_rows.append(_place(v, i, n))
        s = _bdot_nt(qs, jnp.concatenate(k_rows, axis=0)) * scale
        ps, rden = [], []
        for i in range(n):
            si = s[:, i * t:(i + 1) * t]
            pi = jnp.exp(si - si.max(axis=-1, keepdims=True))
            ps.append(pi)
            rden.append(1.0 / pi.sum(axis=-1, keepdims=True))
        o = _bdot(jnp.concatenate(ps, axis=-1), jnp.concatenate(v_rows, axis=0))
        outs.append(o * _by_block(rden, lane_o, C_V))
    o_ref[...] = jnp.concatenate(outs, axis=-1).astype(o_ref.dtype)


def _attn_c_prompt(q, kv, p1, n_seq, t, gq, gk):
    return pl.pallas_call(
        _attn_c_prompt_kernel,
        grid=(n_seq,),
        in_specs=[pl.BlockSpec((t, q.shape[1]), lambda b: (b, 0)),
                  pl.BlockSpec((t, kv.shape[1]), lambda b: (b, 0)),
                  pl.BlockSpec((t, p1.shape[1]), lambda b: (b, 0)),
                  pl.BlockSpec((1, C_STACK * C_QK), lambda b: (0, 0)),
                  pl.BlockSpec((1, C_QK), lambda b: (0, 0))],
        out_specs=pl.BlockSpec((t, C_HEADS * C_V), lambda b: (b, 0)),
        out_shape=jax.ShapeDtypeStruct((n_seq * t, C_HEADS * C_V), BF16),
        compiler_params=_params(("arbitrary",), VMEM_LIMIT),
        name="attn_c_context",
    )(q, kv, p1, jnp.tile(gq, C_STACK).reshape(1, -1), gk.reshape(1, -1))


def _attn_c_sample_kernel(q_ref, kv_ref, p_ref, kvc_ref, krc_ref, gq_ref, gk_ref, cos_ref, sin_ref,
                          o_ref, *, tq):
    scale = C_QK ** -0.5
    i = pl.program_id(1)
    q0 = pl.multiple_of(i * tq, tq)
    q = q_ref[...].astype(F32)
    kv = kv_ref[...].astype(F32)
    kvc = kvc_ref[...].astype(F32)
    gq, gk = gq_ref[...], gk_ref[...]
    cq, sq = cos_ref[pl.ds(q0, tq), :], sin_ref[pl.ds(q0, tq), :]
    kr_raw = p_ref[...][:, C_Q_LORA + C_KV_LORA:]
    krc_raw = krc_ref[...]
    kr_all = jnp.concatenate([_rope(kr_raw * gk[:, C_NOPE:], cos_ref[...], sin_ref[...]),
                              krc_raw * gk[:, C_NOPE:]], axis=0)
    kr_ss = jnp.concatenate([jnp.sum(kr_raw * kr_raw, axis=-1, keepdims=True),
                             jnp.sum(krc_raw * krc_raw, axis=-1, keepdims=True)], axis=0)
    kr_pad = jnp.concatenate([jnp.zeros((kr_all.shape[0], C_NOPE), F32), kr_all], axis=-1)
    lane = lax.broadcasted_iota(jnp.int32, kr_pad.shape, 1)
    outs = []
    for h in range(C_HEADS):
        qh = q[:, h * C_QK:(h + 1) * C_QK]
        qh = qh * lax.rsqrt(jnp.mean(qh * qh, axis=-1, keepdims=True) + EPS) * gq
        qh = jnp.concatenate([qh[:, :C_NOPE], _rope(qh[:, C_NOPE:], cq, sq)], axis=-1)
        c0 = h * (C_NOPE + C_V)
        kn = jnp.concatenate([kv[:, c0:c0 + C_QK], kvc[:, c0:c0 + C_QK]], axis=0)
        v = jnp.concatenate([kv[:, c0 + C_NOPE:c0 + C_NOPE + C_V], kvc[:, c0 + C_NOPE:c0 + C_NOPE + C_V]],
                            axis=0)
        ss = jnp.sum(kn[:, :C_NOPE] * kn[:, :C_NOPE], axis=-1, keepdims=True) + kr_ss
        k = jnp.where(lane < C_NOPE, kn * gk, kr_pad) * lax.rsqrt(ss / C_QK + EPS)
        s = _bdot_nt(qh, k) * scale
        outs.append(_softmax_attend([(s, v)], None))
    o_ref[...] = jnp.concatenate(outs, axis=-1).astype(o_ref.dtype)


def _attn_c_sample(q, kv, p1, row0, n_seq, t, kr_ctx, gq, gk, tq=256):
    n = p1.shape[0]
    past = kr_ctx.shape[1]
    nq = t // tq
    cos, sin = _rope_tables(t, C_ROPE)
    return pl.pallas_call(
        functools.partial(_attn_c_sample_kernel, tq=tq),
        grid=(n_seq, nq),
        in_specs=[pl.BlockSpec((tq, q.shape[1]), lambda b, i: (row0 // tq + b * nq + i, 0)),
                  pl.BlockSpec((t, kv.shape[1]), lambda b, i: (row0 // t + b, 0)),
                  pl.BlockSpec((t, p1.shape[1]), lambda b, i: (row0 // t + b, 0)),
                  pl.BlockSpec((past, kv.shape[1]), lambda b, i: (n // past + b, 0)),
                  pl.BlockSpec((None, past, C_ROPE), lambda b, i: (b, 0, 0)),
                  pl.BlockSpec((1, C_QK), lambda b, i: (0, 0)),
                  pl.BlockSpec((1, C_QK), lambda b, i: (0, 0)),
                  pl.BlockSpec((t, C_ROPE), lambda b, i: (0, 0)),
                  pl.BlockSpec((t, C_ROPE), lambda b, i: (0, 0))],
        out_specs=pl.BlockSpec((tq, C_HEADS * C_V), lambda b, i: (b * nq + i, 0)),
        out_shape=jax.ShapeDtypeStruct((n_seq * t, C_HEADS * C_V), BF16),
        compiler_params=_params(("arbitrary", "arbitrary"), VMEM_LIMIT),
        name="attn_c_latent",
    )(q, kv, p1, kv, kr_ctx, gq.reshape(1, -1), gk.reshape(1, -1), cos, sin)


ROUTE_TILE = 512
ROUTE_BLOCK = 512
SLOT_ALIGN = 8
LOCAL_SLOTS = 2304
META_LANES = 128
FF_CHUNK = 256


def _router_kernel(x_ref, g_ref, sc_ref, sh_ref, wr_ref, br_ref, h_ref, meta_ref, cntb_ref, cnt_ref,
                   run_ref):
    @pl.when(pl.program_id(0) == 0)
    def _():
        run_ref[...] = jnp.zeros_like(run_ref)

    h = _rms(x_ref[...], g_ref[...]) * (1 + sc_ref[...]) + sh_ref[...]
    h_ref[...] = h.astype(h_ref.dtype)
    tm = h.shape[0]
    logits = _dot3(h, wr_ref[...]) + br_ref[...]
    lane = lax.broadcasted_iota(jnp.int32, logits.shape, 1)
    work = logits
    picks, tops = [], []
    for _ in range(TOP_K):
        m = work.max(axis=-1, keepdims=True)
        first = jnp.min(jnp.where(work == m, lane, N_EXPERTS), axis=-1, keepdims=True)
        pick = lane == first
        picks.append(pick)
        tops.append(m)
        work = jnp.where(pick, -jnp.inf, work)
    sel = sum(p.astype(F32) for p in picks)
    earlier = (lax.broadcasted_iota(jnp.int32, (tm, tm), 0)
               > lax.broadcasted_iota(jnp.int32, (tm, tm), 1)).astype(BF16)
    inside = jnp.dot(earlier, sel.astype(BF16), preferred_element_type=F32)
    cnt = jnp.sum(sel, axis=0, keepdims=True)
    run = jnp.ceil(cnt / SLOT_ALIGN) * SLOT_ALIGN
    lower_e = (lax.broadcasted_iota(jnp.int32, (N_EXPERTS, N_EXPERTS), 0)
               < lax.broadcasted_iota(jnp.int32, (N_EXPERTS, N_EXPERTS), 1)).astype(BF16)
    start = jnp.dot(jnp.broadcast_to(run, (8, N_EXPERTS)).astype(BF16), lower_e,
                    preferred_element_type=F32)[0:1]
    slot = start + inside
    ws = [jnp.exp(t - tops[0]) for t in tops]
    den = sum(ws)
    mlane = lax.broadcasted_iota(jnp.int32, (tm, META_LANES), 1)
    meta = jnp.zeros((tm, META_LANES), F32)
    for k in range(TOP_K):
        meta = jnp.where(mlane == k, jnp.sum(jnp.where(picks[k], slot, 0.0), axis=-1, keepdims=True), meta)
        meta = jnp.where(mlane == TOP_K + k, ws[k] / den, meta)
    meta_ref[...] = meta
    cntb_ref[...] = cnt
    run_ref[...] += run
    cnt_ref[...] = run_ref[...]


def _router(x, gain, scale, shift, w_router, b_router, layer, n_p, t_s):
    n, d = x.shape
    tm = ROUTE_BLOCK
    e = w_router.shape[-1]
    group = _group_of_tile(tm, n_p, t_s)
    return pl.pallas_call(
        _router_kernel,
        grid=(n // tm,),
        in_specs=[pl.BlockSpec((tm, d), lambda i: (i, 0)),
                  pl.BlockSpec((1, d), lambda i: (0, 0)),
                  pl.BlockSpec((None, 1, d), lambda i: (group(i), 0, 0)),
                  pl.BlockSpec((None, 1, d), lambda i: (group(i), 0, 0)),
                  pl.BlockSpec((None, d, e), lambda i: (layer, 0, 0)),
                  pl.BlockSpec((None, 1, e), lambda i: (layer, 0, 0))],
        out_specs=[pl.BlockSpec((tm, d), lambda i: (i, 0)),
                   pl.BlockSpec((tm, META_LANES), lambda i: (i, 0)),
                   pl.BlockSpec((None, 1, e), lambda i: (i, 0, 0)),
                   pl.BlockSpec((1, e), lambda i: (0, 0))],
        out_shape=[jax.ShapeDtypeStruct((n, d), BF16),
                   jax.ShapeDtypeStruct((n, META_LANES), F32),
                   jax.ShapeDtypeStruct((n // tm, 1, e), F32),
                   jax.ShapeDtypeStruct((1, e), F32)],
        scratch_shapes=[pltpu.VMEM((1, e), F32)],
        compiler_params=_params(("arbitrary",)),
        name="router",
    )(x, gain.reshape(1, d), scale.reshape(N_GROUPS, 1, d), shift.reshape(N_GROUPS, 1, d),
      w_router, b_router.reshape(-1, 1, e))


def _route_plan(cnt_blk, cnt_tot, n_tiles, min_tiles):
    cb = cnt_blk[:, 0, :].astype(jnp.int32)
    run = (cb + SLOT_ALIGN - 1) // SLOT_ALIGN * SLOT_ALIGN
    counts = cnt_tot[0].astype(jnp.int32)
    padded = (counts + ROUTE_TILE - 1) // ROUTE_TILE * ROUTE_TILE
    ends = jnp.cumsum(padded)
    offs = ends - padded
    gstart = offs[None, :] + jnp.cumsum(run, axis=0) - run
    lstart = jnp.cumsum(run, axis=1) - run
    tile_start = jnp.arange(n_tiles, dtype=jnp.int32) * ROUTE_TILE
    n_valid = ends[-1] // ROUTE_TILE
    te = jnp.minimum(jnp.sum(ends[None, :] <= tile_start[:, None], axis=1), N_EXPERTS - 1)
    te = jnp.where(tile_start < ends[-1], te, te[jnp.maximum(n_valid - 1, 0)]).astype(jnp.int32)
    rows = jnp.clip((offs + counts)[te] - tile_start, 0, ROUTE_TILE).astype(jnp.int32)
    ragged = jnp.where(counts % ROUTE_TILE != 0, ends // ROUTE_TILE - 1, -1)
    tail = jnp.arange(min_tiles, n_tiles, dtype=jnp.int32)
    fill = jnp.concatenate([ragged, jnp.where(tail >= n_valid, tail, -1)]).astype(jnp.int32)
    runs = (gstart.reshape(-1).astype(jnp.int32), lstart.reshape(-1).astype(jnp.int32),
            (run // SLOT_ALIGN).reshape(-1).astype(jnp.int32),
            (jnp.sum(run, axis=1) // SLOT_ALIGN).astype(jnp.int32))
    return runs, te, rows, n_valid.reshape(1).astype(jnp.int32), fill


RUN_PIECE = 4
WAIT_PIECE = 8


def _start_runs(blk, gstart_ref, lstart_ref, nch_ref, copy):
    big = RUN_PIECE * SLOT_ALIGN

    def per_expert(e, carry):
        idx = blk * N_EXPERTS + e
        g0, l0, n = gstart_ref[idx], lstart_ref[idx], nch_ref[idx]
        n_big = n // RUN_PIECE

        def large(i, c2):
            off = pl.multiple_of(i * big, big)
            copy(pl.multiple_of(l0 + off, SLOT_ALIGN), pl.multiple_of(g0 + off, SLOT_ALIGN), big).start()
            return c2

        def small(i, c2):
            off = pl.multiple_of(n_big * big + i * SLOT_ALIGN, SLOT_ALIGN)
            copy(pl.multiple_of(l0 + off, SLOT_ALIGN), pl.multiple_of(g0 + off, SLOT_ALIGN), SLOT_ALIGN).start()
            return c2

        lax.fori_loop(0, n_big, large, 0)
        lax.fori_loop(0, n - n_big * RUN_PIECE, small, 0)
        return carry

    lax.fori_loop(0, N_EXPERTS, per_expert, 0)


def _wait_runs(n_chunks, copy):
    n_big = n_chunks // WAIT_PIECE

    def large(i, carry):
        copy(0, 0, WAIT_PIECE * SLOT_ALIGN).wait()
        return carry

    def small(i, carry):
        copy(0, 0, SLOT_ALIGN).wait()
        return carry

    lax.fori_loop(0, n_big, large, 0)
    lax.fori_loop(0, n_chunks - n_big * WAIT_PIECE, small, 0)


def _dispatch_kernel(fill_ref, gstart_ref, lstart_ref, nch_ref, ntot_ref, meta_ref, h_ref, xs_ref,
                     loc_ref, zero_ref, sem, zsem):
    blk = pl.program_id(0)

    @pl.when(blk == 0)
    def _():
        zero_ref[...] = jnp.zeros_like(zero_ref)

        def fill_copy(j):
            row0 = pl.multiple_of(fill_ref[j] * ROUTE_TILE, ROUTE_TILE)
            return pltpu.make_async_copy(zero_ref, xs_ref.at[pl.ds(row0, ROUTE_TILE), :], zsem)

        def start(j, carry):
            @pl.when(fill_ref[j] >= 0)
            def _():
                fill_copy(j).start()
            return carry

        def wait(j, carry):
            @pl.when(fill_ref[j] >= 0)
            def _():
                fill_copy(j).wait()
            return carry

        lax.fori_loop(0, fill_ref.shape[0], start, 0)
        lax.fori_loop(0, fill_ref.shape[0], wait, 0)

    buf = blk % 2
    last = pl.num_programs(0) - 1

    def copier(b):
        def copy(l0, g0, rows):
            return pltpu.make_async_copy(loc_ref.at[b, pl.ds(l0, rows), :], xs_ref.at[pl.ds(g0, rows), :],
                                         sem.at[b])
        return copy

    @pl.when(blk >= 2)
    def _():
        _wait_runs(ntot_ref[blk - 2], copier(buf))

    hb = h_ref[...]
    tm = hb.shape[0]
    slots_t = meta_ref[...].T[0:TOP_K]
    rows = 256
    for c0 in range(0, LOCAL_SLOTS, rows):
        slot = (c0 + lax.broadcasted_iota(jnp.int32, (rows, tm), 0)).astype(F32)
        onehot = sum(jnp.where(slot == slots_t[k:k + 1], 1.0, 0.0) for k in range(TOP_K))
        loc_ref[buf, c0:c0 + rows, :] = jnp.dot(onehot.astype(BF16), hb,
                                                preferred_element_type=F32).astype(loc_ref.dtype)

    _start_runs(blk, gstart_ref, lstart_ref, nch_ref, copier(buf))

    @pl.when(blk == last)
    def _():
        @pl.when(blk >= 1)
        def _():
            _wait_runs(ntot_ref[blk - 1], copier(1 - buf))
        _wait_runs(ntot_ref[blk], copier(buf))


def _dispatch(h, meta, runs, fill, n_slots):
    n, d = h.shape
    tm = ROUTE_BLOCK
    grid_spec = pltpu.PrefetchScalarGridSpec(
        num_scalar_prefetch=5,
        grid=(n // tm,),
        in_specs=[pl.BlockSpec((tm, META_LANES), lambda i, *_: (i, 0)),
                  pl.BlockSpec((tm, d), lambda i, *_: (i, 0))],
        out_specs=pl.BlockSpec(memory_space=pl.ANY),
        scratch_shapes=[pltpu.VMEM((2, LOCAL_SLOTS, d), F32), pltpu.VMEM((ROUTE_TILE, d), F32),
                        pltpu.SemaphoreType.DMA((2,)), pltpu.SemaphoreType.DMA(())])
    return pl.pallas_call(
        _dispatch_kernel,
        grid_spec=grid_spec,
        out_shape=jax.ShapeDtypeStruct((n_slots, d), F32),
        compiler_params=_params(("arbitrary",), VMEM_LIMIT),
        name="moe_dispatch",
    )(fill, *runs, meta, h)


def _experts_kernel(te_ref, rows_ref, nv_ref, x_ref, wgu_ref, bgu_ref, wd_ref, bd_ref, y_ref,
                    wgub_ref, wdb_ref, *, d_ff):
    i = pl.program_id(0)
    valid = i < nv_ref[0]
    fresh = jnp.logical_or(i == 0, te_ref[i] != te_ref[jnp.maximum(i - 1, 0)])

    @pl.when(jnp.logical_and(valid, fresh))
    def _():
        wgub_ref[...] = wgu_ref[...].astype(BF16)
        wdb_ref[...] = wd_ref[...].astype(BF16)

    @pl.when(valid)
    def _():
        row = lax.broadcasted_iota(jnp.int32, x_ref.shape, 0)
        x = jnp.where(row < rows_ref[i], x_ref[...], 0.0).astype(BF16)
        acc = None
        for c0 in range(0, d_ff, FF_CHUNK):
            g_cols, u_cols = slice(c0, c0 + FF_CHUNK), slice(d_ff + c0, d_ff + c0 + FF_CHUNK)
            gate = jnp.dot(x, wgub_ref[:, g_cols], preferred_element_type=F32) + bgu_ref[:, g_cols]
            up = jnp.dot(x, wgub_ref[:, u_cols], preferred_element_type=F32) + bgu_ref[:, u_cols]
            gate = jnp.minimum(gate, SWIGLU_LIMIT)
            up = jnp.clip(up, -SWIGLU_LIMIT, SWIGLU_LIMIT)
            act = (up + 1) * gate * jax.nn.sigmoid(SWIGLU_ALPHA * gate)
            part = jnp.dot(act.astype(BF16), wdb_ref[c0:c0 + FF_CHUNK, :], preferred_element_type=F32)
            acc = part if acc is None else acc + part
        y_ref[...] = acc + bd_ref[...]

    @pl.when(jnp.logical_not(valid))
    def _():
        y_ref[...] = jnp.zeros_like(y_ref)


def _experts(xs, te, rows, n_valid, w_gu, b_gu, w_down, b_down, layer):
    n_slots, d = xs.shape
    _, e, _, two_ff = w_gu.shape
    last = lambda i, nv: jnp.minimum(i, nv[0] - 1)
    grid_spec = pltpu.PrefetchScalarGridSpec(
        num_scalar_prefetch=3,
        grid=(n_slots // ROUTE_TILE,),
        in_specs=[pl.BlockSpec((ROUTE_TILE, d), lambda i, te, rw, nv: (last(i, nv), 0)),
                  pl.BlockSpec((None, None, d, two_ff), lambda i, te, rw, nv: (layer, te[i], 0, 0)),
                  pl.BlockSpec((None, None, 1, two_ff), lambda i, te, rw, nv: (layer, te[i], 0, 0)),
                  pl.BlockSpec((None, None, two_ff // 2, d), lambda i, te, rw, nv: (layer, te[i], 0, 0)),
                  pl.BlockSpec((None, None, 1, d), lambda i, te, rw, nv: (layer, te[i], 0, 0))],
        out_specs=pl.BlockSpec((ROUTE_TILE, d), lambda i, te, rw, nv: (i, 0)),
        scratch_shapes=[pltpu.VMEM((d, two_ff), BF16), pltpu.VMEM((two_ff // 2, d), BF16)])
    return pl.pallas_call(
        functools.partial(_experts_kernel, d_ff=two_ff // 2),
        grid_spec=grid_spec,
        out_shape=jax.ShapeDtypeStruct((n_slots, d), F32),
        compiler_params=_params(("arbitrary",), VMEM_LIMIT),
        name="moe_experts",
    )(te, rows, n_valid, xs, w_gu, b_gu.reshape(b_gu.shape[0], e, 1, two_ff), w_down,
      b_down.reshape(b_down.shape[0], e, 1, d))


def _combine_kernel(gstart_ref, lstart_ref, nch_ref, ntot_ref, meta_ref, x_ref, gate_ref, y_ref, *refs,
                    split_tiles):
    *o_refs, loc_ref, sem = refs
    blk = pl.program_id(0)

    buf = blk % 2

    def copier(b):
        def copy(l0, g0, rows):
            return pltpu.make_async_copy(y_ref.at[pl.ds(g0, rows), :], loc_ref.at[b, pl.ds(l0, rows), :],
                                         sem.at[b])
        return copy

    @pl.when(blk == 0)
    def _():
        loc_ref[...] = jnp.zeros_like(loc_ref)
        _start_runs(blk, gstart_ref, lstart_ref, nch_ref, copier(buf))

    @pl.when(blk + 1 < pl.num_programs(0))
    def _():
        _start_runs(blk + 1, gstart_ref, lstart_ref, nch_ref, copier(1 - buf))

    _wait_runs(ntot_ref[blk], copier(buf))

    meta = meta_ref[...]
    tm = meta.shape[0]
    cols = 256
    acc = jnp.zeros(x_ref.shape, F32)
    for c0 in range(0, LOCAL_SLOTS, cols):
        slot = (c0 + lax.broadcasted_iota(jnp.int32, (tm, cols), 1)).astype(F32)
        wts = sum(jnp.where(slot == meta[:, k:k + 1], meta[:, TOP_K + k:TOP_K + k + 1], 0.0)
                  for k in range(TOP_K))
        acc = acc + jnp.dot(wts.astype(BF16), loc_ref[buf, c0:c0 + cols, :].astype(BF16),
                            preferred_element_type=F32)
    out = x_ref[...] + gate_ref[...] * acc
    if split_tiles is None:
        o_refs[0][...] = out
    else:
        @pl.when(blk < split_tiles)
        def _():
            o_refs[0][...] = out

        @pl.when(blk >= split_tiles)
        def _():
            o_refs[1][...] = out


def _combine(y, meta, runs, x, gate, n_p, t_s, split=False):
    n, d = x.shape
    tm = ROUTE_BLOCK
    group = _group_of_tile(tm, n_p, t_s)
    st = n_p // tm
    if split:
        out_specs = [pl.BlockSpec((tm, d), lambda i, *_: (jnp.minimum(i, st - 1), 0)),
                     pl.BlockSpec((tm, d), lambda i, *_: (jnp.maximum(i - st, 0), 0))]
        out_shape = [jax.ShapeDtypeStruct((n_p, d), F32), jax.ShapeDtypeStruct((n - n_p, d), F32)]
    else:
        out_specs = pl.BlockSpec((tm, d), lambda i, *_: (i, 0))
        out_shape = jax.ShapeDtypeStruct((n, d), F32)
    grid_spec = pltpu.PrefetchScalarGridSpec(
        num_scalar_prefetch=4,
        grid=(n // tm,),
        in_specs=[pl.BlockSpec((tm, META_LANES), lambda i, *_: (i, 0)),
                  pl.BlockSpec((tm, d), lambda i, *_: (i, 0)),
                  pl.BlockSpec((None, 1, d), lambda i, *_: (group(i), 0, 0)),
                  pl.BlockSpec(memory_space=pl.ANY)],
        out_specs=out_specs,
        scratch_shapes=[pltpu.VMEM((2, LOCAL_SLOTS, d), F32), pltpu.SemaphoreType.DMA((2,))])
    return pl.pallas_call(
        functools.partial(_combine_kernel, split_tiles=st if split else None),
        grid_spec=grid_spec,
        out_shape=out_shape,
        compiler_params=_params(("arbitrary",), VMEM_LIMIT),
        name="moe_combine",
    )(*runs, meta, x, gate.reshape(N_GROUPS, 1, d), y)


def _moe(x, gain, scale, shift, gate, w_router, b_router, w_gu, b_gu, w_down, b_down, layer, n_p, t_s,
         split=False):
    n = x.shape[0]
    assert LOCAL_SLOTS >= ROUTE_BLOCK * TOP_K + N_EXPERTS * (SLOT_ALIGN - 1) and n % ROUTE_BLOCK == 0
    min_tiles = n * TOP_K // ROUTE_TILE
    max_slots = n * TOP_K + (n // ROUTE_BLOCK) * N_EXPERTS * (SLOT_ALIGN - 1) + N_EXPERTS * (ROUTE_TILE - 1)
    n_tiles = -(-max_slots // ROUTE_TILE)
    h, meta, cnt_blk, cnt_tot = _router(x, gain, scale, shift, w_router, b_router, layer, n_p, t_s)
    runs, te, rows, n_valid, fill = _route_plan(cnt_blk, cnt_tot, n_tiles, min_tiles)
    xs = _dispatch(h, meta, runs, fill, n_tiles * ROUTE_TILE)
    y = _experts(xs, te, rows, n_valid, w_gu, b_gu, w_down, b_down, layer)
    return _combine(y, meta, runs, x, gate, n_p, t_s, split)


def kernel(x_prompt, x_sample, c, cache_a_k, cache_a_v, state_b_fwd, state_b_bwd, cache_c_ckv,
           cache_c_krope, c_ctx, w_mod, b_mod, norm_mix, norm_ffn, e_w_in, e_w_out, e_a_qnorm,
           e_a_knorm, e_a_sink, e_b_conv, e_b_alog, e_b_dtbias, e_b_onorm, o_w_in, o_q_lora_norm,
           o_kv_lora_norm, o_w_uq, o_w_ukv, o_qnorm, o_knorm, o_w_out, moe_w_router, moe_b_router,
           moe_w_gu, moe_b_gu, moe_w_down, moe_b_down):
    bp, tp, d = x_prompt.shape
    bs, ts, _ = x_sample.shape
    depth = w_mod.shape[0]
    n_p, n_s = bp * tp, bs * ts
    n = n_p + n_s
    assert bs + 1 <= N_GROUPS and ts % 512 == 0 and n_p % ts == 0

    x = (x_prompt.reshape(n_p, d), x_sample.reshape(n_s, d))
    cond = jnp.concatenate([c_ctx[None], c, jnp.zeros((N_GROUPS - 1 - bs, d), F32)], axis=0)
    mod = _adaln(cond, w_mod, b_mod)

    new_a_k, new_a_v, new_b_fwd, new_b_bwd, new_c_ckv, new_c_krope = [], [], [], [], [], []
    for layer in range(depth):
        sh1, sc1, g1, sh2, sc2, g2 = (mod[layer, j] for j in range(6))
        h = _modulate(x, norm_mix[layer], sc1, sh1, n_p, ts)
        i = layer // 2
        if layer % 2 == 0:
            main_w = (A_HEADS + 2 * A_KV_HEADS) * A_HEAD_DIM + 4 * B_HEADS * B_DK
            proj = _matmul((h,), e_w_in, i, 0, main_w // 2, main_w, out_dtype=BF16, name="even_in_proj")
            tail = _matmul((h,), e_w_in[i][None, :, main_w:], 0, 0, 4 * B_HEADS, 4 * B_HEADS,
                           name="even_gate_proj")
            oa_p, kn_p = _attn_a_prompt(proj, bp, tp, e_a_sink[i], e_a_qnorm[i], e_a_knorm[i])
            oa_s = _attn_a_sample(proj, n_p, bs, ts, cache_a_k[:, i], cache_a_v[:, i],
                                  e_a_sink[i], e_a_qnorm[i], e_a_knorm[i])
            zeros = jnp.zeros((bp, B_HEADS, B_DK, B_DV), F32)
            ob_p, s_f, s_b = _deltanet(proj, tail, 0, bp, tp, e_b_conv[i], e_b_alog[i],
                                       e_b_dtbias[i], e_b_onorm[i], zeros, zeros)
            ob_s, _, _ = _deltanet(proj, tail, n_p, bs, ts, e_b_conv[i], e_b_alog[i],
                                   e_b_dtbias[i], e_b_onorm[i], state_b_fwd[:, i], state_b_bwd[:, i])
            x = (_matmul_residual([(oa_p, oa_s), (ob_p, ob_s)], e_w_out, i, x, g1, n_p, ts),)
            kw = A_KV_HEADS * A_HEAD_DIM
            new_a_k.append(kn_p.reshape(bp, tp, A_KV_HEADS, A_HEAD_DIM).transpose(0, 2, 1, 3))
            v_p = proj[:n_p, A_HEADS * A_HEAD_DIM + kw:A_HEADS * A_HEAD_DIM + 2 * kw].astype(F32)
            new_a_v.append(v_p.reshape(bp, tp, A_KV_HEADS, A_HEAD_DIM).transpose(0, 2, 1, 3))
            new_b_fwd.append(s_f)
            new_b_bwd.append(s_b)
        else:
            p1 = _matmul((h,), o_w_in, i, 0, o_w_in.shape[-1], o_w_in.shape[-1], name="odd_in_proj")
            cq, ckv, ckv_b = _lora_norm(p1, o_q_lora_norm[i], o_kv_lora_norm[i])
            q = _matmul((cq,), o_w_uq, i, 0, o_w_uq.shape[-1] // 2, o_w_uq.shape[-1], out_dtype=BF16,
                        name="odd_uq")
            ckv_ctx = cache_c_ckv[:, i].reshape(-1, C_KV_LORA).astype(BF16)
            kv = _matmul((ckv_b, ckv_ctx), o_w_ukv, i, 0, o_w_ukv.shape[-1] // 2, o_w_ukv.shape[-1],
                         out_dtype=BF16, name="odd_ukv")
            o_p = _attn_c_prompt(q, kv, p1, bp, tp, o_qnorm[i], o_knorm[i])
            o_s = _attn_c_sample(q, kv, p1, n_p, bs, ts, cache_c_krope[:, i], o_qnorm[i], o_knorm[i])
            x = (_matmul_residual([(o_p, o_s)], o_w_out, i, x, g1, n_p, ts),)
            new_c_ckv.append(ckv[:n_p].reshape(bp, tp, C_KV_LORA))
            new_c_krope.append(p1[:n_p, C_Q_LORA + C_KV_LORA:].reshape(bp, tp, C_ROPE))
        x = _moe(x[0], norm_ffn[layer], sc2, sh2, g2, moe_w_router, moe_b_router, moe_w_gu, moe_b_gu,
                 moe_w_down, moe_b_down, layer, n_p, ts, split=layer == depth - 1)
        x = tuple(x) if layer == depth - 1 else (x,)

    return (x[0].reshape(bp, tp, d), x[1].reshape(bs, ts, d),
            jnp.stack(new_a_k, axis=1), jnp.stack(new_a_v, axis=1),
            jnp.stack(new_b_fwd, axis=1), jnp.stack(new_b_bwd, axis=1),
            jnp.stack(new_c_ckv, axis=1), jnp.stack(new_c_krope, axis=1))
```

```python
import functools

import numpy as np
import jax
import jax.numpy as jnp
from jax import lax
from jax.experimental import pallas as pl
from jax.experimental.pallas import tpu as pltpu

F32 = jnp.float32
BF16 = jnp.bfloat16

EPS = 1e-6
NEG_INF = -1e30
ROPE_BASE = 10000.0
GRID_W = 64
N_GROUPS = 8

A_HEADS, A_KV_HEADS, A_GROUP, A_HEAD_DIM, A_WINDOW, A_BLOCK = 8, 2, 4, 64, 128, 128
B_HEADS, B_DK, B_DV, B_CHUNK = 8, 64, 64, 64
C_HEADS, C_NOPE, C_ROPE, C_V, C_Q_LORA, C_KV_LORA = 16, 64, 32, 64, 384, 256
C_QK = C_NOPE + C_ROPE
N_EXPERTS, TOP_K = 32, 4
SWIGLU_LIMIT, SWIGLU_ALPHA = 7.0, 1.702

VMEM_LIMIT = 56 * 1024 * 1024


def _params(sem, vmem=None):
    return pltpu.CompilerParams(dimension_semantics=sem, vmem_limit_bytes=vmem)


def _bdot(a, b):
    return jnp.dot(a.astype(BF16), b.astype(BF16), preferred_element_type=F32)


def _bdot_nt(a, b):
    return lax.dot_general(a.astype(BF16), b.astype(BF16), (((1,), (1,)), ((), ())),
                           preferred_element_type=F32)


def _dot3(a, b):
    a_hi, b_hi = a.astype(BF16), b.astype(BF16)
    a_lo = (a - a_hi.astype(F32)).astype(BF16)
    b_lo = (b - b_hi.astype(F32)).astype(BF16)
    dot = functools.partial(jnp.dot, preferred_element_type=F32)
    return dot(a_hi, b_hi) + (dot(a_hi, b_lo) + dot(a_lo, b_hi))


def _rms(x, gain):
    return x * lax.rsqrt(jnp.mean(x * x, axis=-1, keepdims=True) + EPS) * gain


def _silu(x):
    return x * jax.nn.sigmoid(x)


def _group_of_tile(tm, n_p, t_s):
    def group(i):
        r = i * tm
        return jnp.where(r < n_p, 0, 1 + (r - n_p) // t_s)
    return group


def _adaln_kernel(cond_ref, w_ref, b_ref, o_ref):
    o_ref[...] = _bdot(_silu(cond_ref[...]), w_ref[...]) + b_ref[...]


def _adaln(cond, w_mod, b_mod):
    depth, d, _ = w_mod.shape
    return pl.pallas_call(
        _adaln_kernel,
        grid=(depth, 6),
        in_specs=[pl.BlockSpec((N_GROUPS, d), lambda l, j: (0, 0)),
                  pl.BlockSpec((None, d, d), lambda l, j: (l, 0, j)),
                  pl.BlockSpec((None, 1, d), lambda l, j: (l, 0, j))],
        out_specs=pl.BlockSpec((None, None, N_GROUPS, d), lambda l, j: (l, j, 0, 0)),
        out_shape=jax.ShapeDtypeStruct((depth, 6, N_GROUPS, d), F32),
        compiler_params=_params(("arbitrary", "arbitrary")),
        name="adaln",
    )(cond, w_mod, b_mod.reshape(depth, 1, 6 * d))


def _row_specs(arrays, tm, pos):
    specs, ends, start = [], [], 0
    for arr in arrays:
        nt = arr.shape[0] // tm
        specs.append(pl.BlockSpec((tm, arr.shape[1]),
                                  lambda *ids, s=start, nt=nt: (jnp.clip(ids[pos] - s, 0, nt - 1), 0)))
        start += nt
        ends.append(start)
    return specs, tuple(ends)


def _pick_rows(refs, ends, i):
    x = refs[-1][...]
    for ref, end in zip(reversed(refs[:-1]), reversed(ends[:-1])):
        x = jnp.where(i < end, ref[...], x)
    return x


def _modulate_kernel(*refs, ends):
    n = len(ends)
    g_ref, sc_ref, sh_ref, o_ref = refs[n:]
    y = _rms(_pick_rows(refs[:n], ends, pl.program_id(0)), g_ref[...])
    o_ref[...] = (y * (1 + sc_ref[...]) + sh_ref[...]).astype(o_ref.dtype)


def _modulate(xs, gain, scale, shift, n_p, t_s, tm=1024):
    n, d = sum(x.shape[0] for x in xs), xs[0].shape[1]
    group = _group_of_tile(tm, n_p, t_s)
    specs, ends = _row_specs(xs, tm, 0)
    return pl.pallas_call(
        functools.partial(_modulate_kernel, ends=ends),
        grid=(n // tm,),
        in_specs=specs + [pl.BlockSpec((1, d), lambda i: (0, 0)),
                          pl.BlockSpec((None, 1, d), lambda i: (group(i), 0, 0)),
                          pl.BlockSpec((None, 1, d), lambda i: (group(i), 0, 0))],
        out_specs=pl.BlockSpec((tm, d), lambda i: (i, 0)),
        out_shape=jax.ShapeDtypeStruct((n, d), BF16),
        compiler_params=_params(("arbitrary",)),
        name="modulate",
    )(*xs, gain.reshape(1, d), scale.reshape(N_GROUPS, 1, d), shift.reshape(N_GROUPS, 1, d))


def _mm_kernel(*refs, ends):
    n = len(ends)
    w_ref, o_ref, wb_ref = refs[n:]

    @pl.when(pl.program_id(1) == 0)
    def _():
        wb_ref[...] = w_ref[...].astype(BF16)
    x = _pick_rows(refs[:n], ends, pl.program_id(1))
    o_ref[...] = jnp.dot(x, wb_ref[...], preferred_element_type=F32).astype(o_ref.dtype)


def _matmul(xs, w3, layer, col0_blocks, tn, n_out, out_dtype=F32, tm=1024, name="matmul"):
    n, k = sum(x.shape[0] for x in xs), xs[0].shape[1]
    specs, ends = _row_specs(xs, tm, 1)
    return pl.pallas_call(
        functools.partial(_mm_kernel, ends=ends),
        grid=(n_out // tn, n // tm),
        in_specs=specs + [pl.BlockSpec((None, k, tn), lambda j, i: (layer, 0, col0_blocks + j))],
        out_specs=pl.BlockSpec((tm, tn), lambda j, i: (i, j)),
        out_shape=jax.ShapeDtypeStruct((n, n_out), out_dtype),
        scratch_shapes=[pltpu.VMEM((k, tn), BF16)],
        compiler_params=_params(("arbitrary", "arbitrary"), VMEM_LIMIT),
        name=name,
    )(*xs, w3)


def _mm_res_kernel(*refs, col_ends, res_ends):
    i = pl.program_id(0)
    pos = 0
    cols = []
    for ends in col_ends:
        cols.append(refs[pos:pos + len(ends)])
        pos += len(ends)
    res_refs = refs[pos:pos + len(res_ends)]
    w_ref, gate_ref, o_ref, wb_ref = refs[pos + len(res_ends):]

    @pl.when(i == 0)
    def _():
        wb_ref[...] = w_ref[...].astype(BF16)
    y, k0 = None, 0
    for col_refs, ends in zip(cols, col_ends):
        x = _pick_rows(col_refs, ends, i)
        part = jnp.dot(x, wb_ref[k0:k0 + x.shape[1], :], preferred_element_type=F32)
        y = part if y is None else y + part
        k0 += x.shape[1]
    o_ref[...] = _pick_rows(res_refs, res_ends, i) + gate_ref[...] * y


def _matmul_residual(mix_cols, w3, layer, res, gate, n_p, t_s, tm=1024):
    n, d = sum(r.shape[0] for r in res), res[0].shape[1]
    k = sum(col[0].shape[1] for col in mix_cols)
    group = _group_of_tile(tm, n_p, t_s)
    specs, col_ends = [], []
    for col in mix_cols:
        s, e = _row_specs(col, tm, 0)
        specs += s
        col_ends.append(e)
    res_specs, res_ends = _row_specs(res, tm, 0)
    return pl.pallas_call(
        functools.partial(_mm_res_kernel, col_ends=tuple(col_ends), res_ends=res_ends),
        grid=(n // tm,),
        in_specs=specs + res_specs + [pl.BlockSpec((None, k, d), lambda i: (layer, 0, 0)),
                                      pl.BlockSpec((None, 1, d), lambda i: (group(i), 0, 0))],
        out_specs=pl.BlockSpec((tm, d), lambda i: (i, 0)),
        out_shape=jax.ShapeDtypeStruct((n, d), F32),
        scratch_shapes=[pltpu.VMEM((k, d), BF16)],
        compiler_params=_params(("arbitrary",), VMEM_LIMIT),
        name="out_proj_residual",
    )(*[piece for col in mix_cols for piece in col], *res, w3, gate.reshape(N_GROUPS, 1, d))


def _rope_tables(t_len, d):
    half, quarter = d // 2, d // 4
    pos = np.arange(t_len)
    row, col = pos // GRID_W, pos % GRID_W
    inv = ROPE_BASE ** (-np.arange(quarter, dtype=np.float64) / quarter)
    ang_r = row[:, None] * inv[None, :]
    ang_c = col[:, None] * inv[None, :]
    cos = np.concatenate([np.cos(ang_r), np.cos(ang_r), np.cos(ang_c), np.cos(ang_c)], axis=1)
    sin = np.concatenate([-np.sin(ang_r), np.sin(ang_r), -np.sin(ang_c), np.sin(ang_c)], axis=1)
    return jnp.asarray(cos, F32), jnp.asarray(sin, F32)


def _swap_pairs(x):
    q = x.shape[-1] // 4
    return jnp.concatenate([x[:, q:2 * q], x[:, :q], x[:, 3 * q:], x[:, 2 * q:3 * q]], axis=-1)


def _rope(x, cos, sin):
    return x * cos + _swap_pairs(x) * sin


def _softmax_attend(parts, sink):
    m = parts[0][0].max(axis=-1, keepdims=True)
    for s, _ in parts[1:]:
        m = jnp.maximum(m, s.max(axis=-1, keepdims=True))
    if sink is not None:
        m = jnp.maximum(m, sink)
    den = jnp.exp(sink - m) if sink is not None else 0.0
    acc = None
    for s, v in parts:
        p = jnp.exp(s - m)
        den = den + p.sum(axis=-1, keepdims=True)
        o = _bdot(p, v)
        acc = o if acc is None else acc + o
    return acc / den


def _group_sinks(sink_ref, hk, rows):
    head = lax.broadcasted_iota(jnp.int32, (A_GROUP * rows, 1), 0) // rows
    col = jnp.full((A_GROUP * rows, 1), sink_ref[hk * A_GROUP], F32)
    for g in range(1, A_GROUP):
        col = jnp.where(head == g, sink_ref[hk * A_GROUP + g], col)
    return col


def _attn_a_prompt_kernel(sink_ref, q_ref, kv_ref, gq_ref, gk_ref, o_ref, kn_ref):
    scale = A_HEAD_DIM ** -0.5
    q = q_ref[...].astype(F32)
    kv = kv_ref[...].astype(F32)
    t = q.shape[0]
    outs = []
    kns = []
    for hk in range(A_KV_HEADS):
        k = _rms(kv[:, hk * A_HEAD_DIM:(hk + 1) * A_HEAD_DIM], gk_ref[...])
        v = kv[:, (A_KV_HEADS + hk) * A_HEAD_DIM:(A_KV_HEADS + hk + 1) * A_HEAD_DIM]
        kns.append(k)
        qg = jnp.concatenate([_rms(q[:, h * A_HEAD_DIM:(h + 1) * A_HEAD_DIM], gq_ref[...])
                              for h in range(hk * A_GROUP, (hk + 1) * A_GROUP)], axis=0)
        s = _bdot_nt(qg, k) * scale
        o = _softmax_attend([(s, v)], _group_sinks(sink_ref, hk, t))
        outs += [o[g * t:(g + 1) * t] for g in range(A_GROUP)]
    o_ref[...] = jnp.concatenate(outs, axis=-1).astype(o_ref.dtype)
    kn_ref[...] = jnp.concatenate(kns, axis=-1)


def _attn_a_prompt(proj, n_seq, t, sink, gq, gk):
    qw = A_HEADS * A_HEAD_DIM
    kvw = 2 * A_KV_HEADS * A_HEAD_DIM
    return pl.pallas_call(
        _attn_a_prompt_kernel,
        grid=(n_seq,),
        in_specs=[pl.BlockSpec(memory_space=pltpu.SMEM),
                  pl.BlockSpec((t, qw), lambda b: (b, 0)),
                  pl.BlockSpec((t, kvw), lambda b: (b, qw // kvw)),
                  pl.BlockSpec((1, A_HEAD_DIM), lambda b: (0, 0)),
                  pl.BlockSpec((1, A_HEAD_DIM), lambda b: (0, 0))],
        out_specs=[pl.BlockSpec((t, qw), lambda b: (b, 0)),
                   pl.BlockSpec((t, A_KV_HEADS * A_HEAD_DIM), lambda b: (b, 0))],
        out_shape=[jax.ShapeDtypeStruct((n_seq * t, qw), BF16),
                   jax.ShapeDtypeStruct((n_seq * t, A_KV_HEADS * A_HEAD_DIM), F32)],
        compiler_params=_params(("arbitrary",)),
        name="attn_a_context",
    )(sink, proj, proj, gq.reshape(1, -1), gk.reshape(1, -1))


def _attn_a_sample_kernel(sink_ref, q_ref, kv_ref, kc_ref, vc_ref, gq_ref, gk_ref, cos_ref, sin_ref,
                          o_ref, *, t):
    scale = A_HEAD_DIM ** -0.5
    i = pl.program_id(1)
    win = 3 * A_BLOCK
    q0 = pl.multiple_of(i * A_BLOCK, A_BLOCK)
    k0 = pl.multiple_of(jnp.clip((i - 1) * A_BLOCK, 0, t - win), A_BLOCK)
    q = q_ref[...].astype(F32)
    kv = kv_ref[pl.ds(k0, win), :].astype(F32)
    cq, sq = cos_ref[pl.ds(q0, A_BLOCK), :], sin_ref[pl.ds(q0, A_BLOCK), :]
    ck, sk = cos_ref[pl.ds(k0, win), :], sin_ref[pl.ds(k0, win), :]
    rows = A_GROUP * A_BLOCK
    qpos = q0 + lax.broadcasted_iota(jnp.int32, (rows, win), 0) % A_BLOCK
    kpos = k0 + lax.broadcasted_iota(jnp.int32, (rows, win), 1)
    mask = jnp.abs(qpos - kpos) <= A_WINDOW
    outs = []
    for hk in range(A_KV_HEADS):
        k = _rope(_rms(kv[:, hk * A_HEAD_DIM:(hk + 1) * A_HEAD_DIM], gk_ref[...]), ck, sk)
        v = kv[:, (A_KV_HEADS + hk) * A_HEAD_DIM:(A_KV_HEADS + hk + 1) * A_HEAD_DIM]
        qg = jnp.concatenate([_rope(_rms(q[:, h * A_HEAD_DIM:(h + 1) * A_HEAD_DIM], gq_ref[...]), cq, sq)
                              for h in range(hk * A_GROUP, (hk + 1) * A_GROUP)], axis=0)
        s1 = jnp.where(mask, _bdot_nt(qg, k) * scale, NEG_INF)
        s2 = _bdot_nt(qg, kc_ref[hk]) * scale
        o = _softmax_attend([(s1, v), (s2, vc_ref[hk])], _group_sinks(sink_ref, hk, A_BLOCK))
        outs += [o[g * A_BLOCK:(g + 1) * A_BLOCK] for g in range(A_GROUP)]
    o_ref[...] = jnp.concatenate(outs, axis=-1).astype(o_ref.dtype)


def _attn_a_sample(proj, row0, n_seq, t, k_ctx, v_ctx, sink, gq, gk):
    qw = A_HEADS * A_HEAD_DIM
    kvw = 2 * A_KV_HEADS * A_HEAD_DIM
    nqb = t // A_BLOCK
    cos, sin = _rope_tables(t, A_HEAD_DIM)
    past = k_ctx.shape[2]
    return pl.pallas_call(
        functools.partial(_attn_a_sample_kernel, t=t),
        grid=(n_seq, nqb),
        in_specs=[pl.BlockSpec(memory_space=pltpu.SMEM),
                  pl.BlockSpec((A_BLOCK, qw), lambda b, i: (row0 // A_BLOCK + b * nqb + i, 0)),
                  pl.BlockSpec((t, kvw), lambda b, i: (row0 // t + b, qw // kvw)),
                  pl.BlockSpec((None, A_KV_HEADS, past, A_HEAD_DIM), lambda b, i: (b, 0, 0, 0)),
                  pl.BlockSpec((None, A_KV_HEADS, past, A_HEAD_DIM), lambda b, i: (b, 0, 0, 0)),
                  pl.BlockSpec((1, A_HEAD_DIM), lambda b, i: (0, 0)),
                  pl.BlockSpec((1, A_HEAD_DIM), lambda b, i: (0, 0)),
                  pl.BlockSpec((t, A_HEAD_DIM), lambda b, i: (0, 0)),
                  pl.BlockSpec((t, A_HEAD_DIM), lambda b, i: (0, 0))],
        out_specs=pl.BlockSpec((A_BLOCK, qw), lambda b, i: (b * nqb + i, 0)),
        out_shape=jax.ShapeDtypeStruct((n_seq * t, qw), BF16),
        compiler_params=_params(("arbitrary", "arbitrary")),
        name="attn_a_latent",
    )(sink, proj, proj, k_ctx, v_ctx, gq.reshape(1, -1), gk.reshape(1, -1), cos, sin)


def _per_head_lanes(x, fn):
    lane = lax.broadcasted_iota(jnp.int32, x.shape, 1)
    lo = fn(x[:, :B_DK])
    hi = fn(x[:, B_DK:])
    return jnp.where(lane < B_DK, lo, hi)


def _conv_silu(x, w):
    t = x.shape[0]
    row = lax.broadcasted_iota(jnp.int32, x.shape, 0)
    prev = jnp.where(row == 0, 0.0, pltpu.roll(x, 1, 0))
    nxt = jnp.where(row == t - 1, 0.0, pltpu.roll(x, t - 1, 0))
    return _silu(prev * w[0:1, :] + x * w[1:2, :] + nxt * w[2:3, :])


M_SAME, M_TRI, M_TRI_T, M_STRICT, M_BDIAG = range(5)


def _delta_masks(r):
    c = B_CHUNK
    ii = lax.broadcasted_iota(jnp.int32, (r, r), 0)
    jj = lax.broadcasted_iota(jnp.int32, (r, r), 1)
    same = (ii // c) == (jj // c)
    ahead = jnp.where(ii < r // 2, ii - jj, jj - ii)
    tri = jnp.where(same, ahead, -1) >= 0
    tri_t = jnp.where(same, ahead, 1) <= 0
    strict = jnp.where(same, ahead, -1) > 0
    bdiag = (ii // 16) == (jj // 16)
    return [x.astype(F32) for x in (same, tri, tri_t, strict, bdiag)]


def _delta_prepare(q, k, v, g_col, g_row, beta, mask_ref):
    c = B_CHUNK
    r = q.shape[0]
    dot = functools.partial(jnp.dot, preferred_element_type=F32)
    gc_col = jnp.sum(mask_ref[M_TRI] * g_row, axis=1, keepdims=True)
    gc_row = jnp.sum(mask_ref[M_TRI_T] * g_col, axis=0, keepdims=True)
    g_tot = jnp.sum(mask_ref[M_SAME] * g_row, axis=1, keepdims=True)
    ex = jnp.exp((gc_col - gc_row) * mask_ref[M_TRI])
    kb = k * beta
    qk = _bdot_nt(jnp.concatenate([kb, q], axis=0), k)
    m = qk[:r] * (ex * mask_ref[M_STRICT])
    aqk = qk[r:] * (ex * mask_ref[M_TRI])
    dg = m * mask_ref[M_BDIAG]
    off = m - dg
    n1 = -dg
    n1b = n1.astype(BF16)
    n2 = dot(n1b, n1b)
    n2b = n2.astype(BF16)
    t = dot(jnp.concatenate([n1b, n2b], axis=0), n2b)
    xs = n1 + n2 + t[:r]
    n4 = t[r:]
    n4b = n4.astype(BF16)
    t = dot(jnp.concatenate([xs.astype(BF16), n4b], axis=0), n4b)
    xs = xs + n4 + t[:r]
    n8 = t[r:]
    xs = xs + n8 + dot(xs.astype(BF16), n8.astype(BF16))
    xsb = xs.astype(BF16)
    f = -(off + dot(xsb, off.astype(BF16)))
    fb = f.astype(BF16)
    t = dot(fb, jnp.concatenate([xsb, fb], axis=1))
    ys = xs + f + t[:, :r]
    f2 = t[:, r:]
    ts = ys + f2 + dot(f2.astype(BF16), ys.astype(BF16))
    egc = jnp.exp(gc_col)
    rhs = jnp.concatenate([v * beta, kb * egc], axis=-1)
    uw = rhs + _bdot(ts, rhs)

    def block_diag(x):
        return jnp.concatenate([x] * (r // c), axis=-1) * mask_ref[M_SAME]

    kd = k * jnp.exp(g_tot - gc_col)
    kd_t = jnp.concatenate([kd, jnp.zeros_like(kd)], axis=-1).T[:c]
    kd_t = jnp.concatenate([kd_t] * (r // c), axis=0) * mask_ref[M_SAME]
    e_tot = jnp.broadcast_to(jnp.exp(g_tot), v.shape)
    return uw[:, :B_DV], e_tot, block_diag(uw[:, B_DV:]), aqk, block_diag(q * egc), kd_t


def _deltanet_kernel(alog_ref, dtb_ref, q_ref, k_ref, v_ref, z_ref, cwq_ref, cwk_ref, cwv_ref,
                     tail_ref, tailt_ref, onorm_ref, s0f_ref, s0b_ref,
                     o_ref, sf_ref, sb_ref,
                     qc_ref, kc_ref, vc_ref, oacc_ref, u_ref, et_ref, w_ref, aqk_ref, qg_ref, kdt_ref,
                     mask_ref, *, n_chunks):
    hp = pl.program_id(1)
    c = B_CHUNK

    @pl.when(jnp.logical_and(pl.program_id(0) == 0, hp == 0))
    def _():
        for i, x in enumerate(_delta_masks(4 * c)):
            mask_ref[i] = x

    def l2n(x):
        ss = _per_head_lanes(x * x, lambda a: jnp.sum(a, axis=-1, keepdims=True))
        return x * lax.rsqrt(ss + EPS)

    qc_ref[...] = l2n(_conv_silu(q_ref[...].astype(F32), cwq_ref[...])) * (B_DK ** -0.5)
    kc_ref[...] = l2n(_conv_silu(k_ref[...].astype(F32), cwk_ref[...]))
    vc_ref[...] = _conv_silu(v_ref[...].astype(F32), cwv_ref[...])
    oacc_ref[...] = jnp.zeros_like(oacc_ref)

    lane32 = lax.broadcasted_iota(jnp.int32, (c, 4 * B_HEADS), 1)

    def gates(chunk, d, head):
        tail = tail_ref[pl.ds(pl.multiple_of(chunk * c, c), c), :]
        ia = 2 * d * B_HEADS + head
        ib = ia + B_HEADS
        a_col = jnp.sum(jnp.where(lane32 == ia, tail, 0.0), axis=1, keepdims=True)
        b_col = jnp.sum(jnp.where(lane32 == ib, tail, 0.0), axis=1, keepdims=True)
        a_row = tailt_ref[chunk, pl.ds(ia, 1), :]
        na = -jnp.exp(alog_ref[d, head])
        bias = dtb_ref[d, head]
        g_col = na * jax.nn.softplus(a_col + bias)
        g_row = na * jax.nn.softplus(a_row + bias)
        return g_col, g_row, jax.nn.sigmoid(b_col)

    def stacked(ref, chunks):
        parts = []
        for chunk in chunks:
            x = ref[pl.ds(pl.multiple_of(chunk * c, c), c), :]
            parts += [x[:, :B_DK], x[:, B_DK:]]
        return jnp.concatenate(parts, axis=0)

    def prepare(j, carry):
        chunks = (j, n_chunks - 1 - j)
        gs = [gates(chunks[d], d, 2 * hp + hh) for d in range(2) for hh in range(2)]
        g_col = jnp.concatenate([g[0] for g in gs], axis=0)
        g_row = jnp.concatenate([g[1] for g in gs], axis=1)
        beta = jnp.concatenate([g[2] for g in gs], axis=0)
        outs = _delta_prepare(stacked(qc_ref, chunks), stacked(kc_ref, chunks), stacked(vc_ref, chunks),
                              g_col, g_row, beta, mask_ref)
        for ref, x in zip((u_ref, et_ref, w_ref, aqk_ref, qg_ref, kdt_ref), outs):
            ref[j] = x.astype(ref.dtype)
        return carry

    lax.fori_loop(0, n_chunks, prepare, 0, unroll=4)

    def scan(j, s):
        sb = s.astype(BF16)
        delta = u_ref[j] - jnp.dot(w_ref[j], sb, preferred_element_type=F32)
        db = delta.astype(BF16)
        o = (jnp.dot(qg_ref[j], sb, preferred_element_type=F32)
             + jnp.dot(aqk_ref[j], db, preferred_element_type=F32))
        for d, chunk in enumerate((j, n_chunks - 1 - j)):
            rows = pl.ds(pl.multiple_of(chunk * c, c), c)
            oacc_ref[rows, :] += jnp.concatenate([o[2 * d * c:(2 * d + 1) * c],
                                                  o[(2 * d + 1) * c:(2 * d + 2) * c]], axis=-1)
        return s * et_ref[j] + jnp.dot(kdt_ref[j], db, preferred_element_type=F32)

    init = jnp.concatenate([s0f_ref[0], s0f_ref[1], s0b_ref[0], s0b_ref[1]], axis=0)
    fin = lax.fori_loop(0, n_chunks, scan, init)
    sf_ref[0], sf_ref[1], sb_ref[0], sb_ref[1] = (fin[i * B_DK:(i + 1) * B_DK] for i in range(4))

    o = oacc_ref[...]
    ms = _per_head_lanes(o * o, lambda a: jnp.mean(a, axis=-1, keepdims=True))
    o_ref[...] = (o * lax.rsqrt(ms + EPS) * onorm_ref[...] * _silu(z_ref[...].astype(F32))).astype(o_ref.dtype)


def _deltanet(proj, tail, row0, n_seq, t, conv_w, a_log, dt_bias, o_norm, s0_f, s0_b):
    c = B_CHUNK
    n_chunks = t // c
    lw = 2 * B_DK
    col_q = (A_HEADS + 2 * A_KV_HEADS) * A_HEAD_DIM // lw
    nhp = B_HEADS // 2
    rows = tail[row0:row0 + n_seq * t]
    tail_t = rows.reshape(n_seq, n_chunks, c, 4 * B_HEADS).transpose(0, 1, 3, 2)
    onorm2 = jnp.concatenate([o_norm, o_norm]).reshape(1, lw)
    b0 = row0 // t
    seq_blk = lambda off: pl.BlockSpec((t, lw), lambda b, h: (b0 + b, col_q + off + h))
    cw_blk = lambda off: pl.BlockSpec((3, lw), lambda b, h: (0, off + h))
    st_blk = pl.BlockSpec((None, 2, B_DK, B_DV), lambda b, h: (b, h, 0, 0))
    return pl.pallas_call(
        functools.partial(_deltanet_kernel, n_chunks=n_chunks),
        grid=(n_seq, nhp),
        in_specs=[pl.BlockSpec(memory_space=pltpu.SMEM), pl.BlockSpec(memory_space=pltpu.SMEM),
                  seq_blk(0), seq_blk(nhp), seq_blk(2 * nhp), seq_blk(3 * nhp),
                  cw_blk(0), cw_blk(nhp), cw_blk(2 * nhp),
                  pl.BlockSpec((t, 4 * B_HEADS), lambda b, h: (b, 0)),
                  pl.BlockSpec((None, n_chunks, 4 * B_HEADS, c), lambda b, h: (b, 0, 0, 0)),
                  pl.BlockSpec((1, lw), lambda b, h: (0, 0)),
                  st_blk, st_blk],
        out_specs=[pl.BlockSpec((t, lw), lambda b, h: (b, h)), st_blk, st_blk],
        out_shape=[jax.ShapeDtypeStruct((n_seq * t, B_HEADS * B_DV), BF16),
                   jax.ShapeDtypeStruct((n_seq, B_HEADS, B_DK, B_DV), F32),
                   jax.ShapeDtypeStruct((n_seq, B_HEADS, B_DK, B_DV), F32)],
        scratch_shapes=[pltpu.VMEM((t, lw), F32)] * 4 + [pltpu.VMEM((n_chunks, 4 * c, B_DV), F32)] * 2
        + [pltpu.VMEM((n_chunks, 4 * c, 4 * c), BF16)] * 4 + [pltpu.VMEM((5, 4 * c, 4 * c), F32)],
        compiler_params=_params(("arbitrary", "arbitrary")),
        name="deltanet",
    )(a_log, dt_bias, proj, proj, proj, proj, conv_w, conv_w, conv_w, rows, tail_t, onorm2, s0_f, s0_b)


def _lora_norm_kernel(p_ref, gq_ref, gkv_ref, cq_ref, ckv_ref, ckvb_ref):
    p = p_ref[...]
    cq_ref[...] = _rms(p[:, :C_Q_LORA], gq_ref[...]).astype(cq_ref.dtype)
    ckv = _rms(p[:, C_Q_LORA:C_Q_LORA + C_KV_LORA], gkv_ref[...])
    ckv_ref[...] = ckv
    ckvb_ref[...] = ckv.astype(ckvb_ref.dtype)


def _lora_norm(p1, gq, gkv, tm=1024):
    n, w = p1.shape
    return pl.pallas_call(
        _lora_norm_kernel,
        grid=(n // tm,),
        in_specs=[pl.BlockSpec((tm, w), lambda i: (i, 0)),
                  pl.BlockSpec((1, C_Q_LORA), lambda i: (0, 0)),
                  pl.BlockSpec((1, C_KV_LORA), lambda i: (0, 0))],
        out_specs=[pl.BlockSpec((tm, C_Q_LORA), lambda i: (i, 0)),
                   pl.BlockSpec((tm, C_KV_LORA), lambda i: (i, 0)),
                   pl.BlockSpec((tm, C_KV_LORA), lambda i: (i, 0))],
        out_shape=[jax.ShapeDtypeStruct((n, C_Q_LORA), BF16),
                   jax.ShapeDtypeStruct((n, C_KV_LORA), F32),
                   jax.ShapeDtypeStruct((n, C_KV_LORA), BF16)],
        compiler_params=_params(("arbitrary",)),
        name="lora_norm",
    )(p1, gq.reshape(1, -1), gkv.reshape(1, -1))


def _mla_head_k(kv, kr, kr_ss, h, gk):
    kn = kv[:, h * (C_NOPE + C_V):h * (C_NOPE + C_V) + C_NOPE]
    v = kv[:, h * (C_NOPE + C_V) + C_NOPE:(h + 1) * (C_NOPE + C_V)]
    rn = lax.rsqrt((jnp.sum(kn * kn, axis=-1, keepdims=True) + kr_ss) / C_QK + EPS)
    return kn * rn * gk[:, :C_NOPE], kr * rn, v


C_STACK = 4


def _place(x, i, n):
    t, w = x.shape
    parts = ([jnp.zeros((t, i * w), x.dtype)] if i else []) + [x]
    if i < n - 1:
        parts.append(jnp.zeros((t, (n - 1 - i) * w), x.dtype))
    return jnp.concatenate(parts, axis=-1)


def _by_block(cols, lane, w):
    out = cols[-1]
    for i in reversed(range(len(cols) - 1)):
        out = jnp.where(lane < (i + 1) * w, cols[i], out)
    return out


def _attn_c_prompt_kernel(q_ref, kv_ref, p_ref, gq_ref, gk_ref, o_ref):
    scale = C_QK ** -0.5
    n = C_STACK
    q = q_ref[...].astype(F32)
    kv = kv_ref[...].astype(F32)
    gk = gk_ref[...]
    t = q.shape[0]
    kr_raw = p_ref[...][:, C_Q_LORA + C_KV_LORA:]
    kr_ss = jnp.sum(kr_raw * kr_raw, axis=-1, keepdims=True)
    kr_g = kr_raw * gk[:, C_NOPE:]
    lane_q = lax.broadcasted_iota(jnp.int32, (t, n * C_QK), 1)
    lane_o = lax.broadcasted_iota(jnp.int32, (t, n * C_V), 1)
    outs = []
    for grp in range(C_HEADS // n):
        qs = q[:, grp * n * C_QK:(grp + 1) * n * C_QK]
        rn = [lax.rsqrt(jnp.mean(qs[:, i * C_QK:(i + 1) * C_QK] ** 2, axis=-1, keepdims=True) + EPS)
              for i in range(n)]
        qs = qs * _by_block(rn, lane_q, C_QK) * gq_ref[...]
        k_rows, v_rows = [], []
        for i in range(n):
            kn, kr, v = _mla_head_k(kv, kr_g, kr_ss, grp * n + i, gk)
            k_rows.append(_place(jnp.concatenate([kn, kr], axis=-1), i, n))
            v_rows.append(_place(v, i, n))
        s = _bdot_nt(qs, jnp.concatenate(k_rows, axis=0)) * scale
        ps, rden = [], []
        for i in range(n):
            si = s[:, i * t:(i + 1) * t]
            pi = jnp.exp(si - si.max(axis=-1, keepdims=True))
            ps.append(pi)
            rden.append(1.0 / pi.sum(axis=-1, keepdims=True))
        o = _bdot(jnp.concatenate(ps, axis=-1), jnp.concatenate(v_rows, axis=0))
        outs.append(o * _by_block(rden, lane_o, C_V))
    o_ref[...] = jnp.concatenate(outs, axis=-1).astype(o_ref.dtype)


def _attn_c_prompt(q, kv, p1, n_seq, t, gq, gk):
    return pl.pallas_call(
        _attn_c_prompt_kernel,
        grid=(n_seq,),
        in_specs=[pl.BlockSpec((t, q.shape[1]), lambda b: (b, 0)),
                  pl.BlockSpec((t, kv.shape[1]), lambda b: (b, 0)),
                  pl.BlockSpec((t, p1.shape[1]), lambda b: (b, 0)),
                  pl.BlockSpec((1, C_STACK * C_QK), lambda b: (0, 0)),
                  pl.BlockSpec((1, C_QK), lambda b: (0, 0))],
        out_specs=pl.BlockSpec((t, C_HEADS * C_V), lambda b: (b, 0)),
        out_shape=jax.ShapeDtypeStruct((n_seq * t, C_HEADS * C_V), BF16),
        compiler_params=_params(("arbitrary",), VMEM_LIMIT),
        name="attn_c_context",
    )(q, kv, p1, jnp.tile(gq, C_STACK).reshape(1, -1), gk.reshape(1, -1))


def _attn_c_sample_kernel(q_ref, kv_ref, p_ref, kvc_ref, krc_ref, gq_ref, gk_ref, cos_ref, sin_ref,
                          o_ref, *, tq):
    scale = C_QK ** -0.5
    i = pl.program_id(1)
    q0 = pl.multiple_of(i * tq, tq)
    q = q_ref[...].astype(F32)
    kv = kv_ref[...].astype(F32)
    kvc = kvc_ref[...].astype(F32)
    gq, gk = gq_ref[...], gk_ref[...]
    cq, sq = cos_ref[pl.ds(q0, tq), :], sin_ref[pl.ds(q0, tq), :]
    kr_raw = p_ref[...][:, C_Q_LORA + C_KV_LORA:]
    krc_raw = krc_ref[...]
    kr_all = jnp.concatenate([_rope(kr_raw * gk[:, C_NOPE:], cos_ref[...], sin_ref[...]),
                              krc_raw * gk[:, C_NOPE:]], axis=0)
    kr_ss = jnp.concatenate([jnp.sum(kr_raw * kr_raw, axis=-1, keepdims=True),
                             jnp.sum(krc_raw * krc_raw, axis=-1, keepdims=True)], axis=0)
    kr_pad = jnp.concatenate([jnp.zeros((kr_all.shape[0], C_NOPE), F32), kr_all], axis=-1)
    lane = lax.broadcasted_iota(jnp.int32, kr_pad.shape, 1)
    outs = []
    for h in range(C_HEADS):
        qh = q[:, h * C_QK:(h + 1) * C_QK]
        qh = qh * lax.rsqrt(jnp.mean(qh * qh, axis=-1, keepdims=True) + EPS) * gq
        qh = jnp.concatenate([qh[:, :C_NOPE], _rope(qh[:, C_NOPE:], cq, sq)], axis=-1)
        c0 = h * (C_NOPE + C_V)
        kn = jnp.concatenate([kv[:, c0:c0 + C_QK], kvc[:, c0:c0 + C_QK]], axis=0)
        v = jnp.concatenate([kv[:, c0 + C_NOPE:c0 + C_NOPE + C_V], kvc[:, c0 + C_NOPE:c0 + C_NOPE + C_V]],
                            axis=0)
        ss = jnp.sum(kn[:, :C_NOPE] * kn[:, :C_NOPE], axis=-1, keepdims=True) + kr_ss
        k = jnp.where(lane < C_NOPE, kn * gk, kr_pad) * lax.rsqrt(ss / C_QK + EPS)
        s = _bdot_nt(qh, k) * scale
        outs.append(_softmax_attend([(s, v)], None))
    o_ref[...] = jnp.concatenate(outs, axis=-1).astype(o_ref.dtype)


def _attn_c_sample(q, kv, p1, row0, n_seq, t, kr_ctx, gq, gk, tq=256):
    n = p1.shape[0]
    past = kr_ctx.shape[1]
    nq = t // tq
    cos, sin = _rope_tables(t, C_ROPE)
    return pl.pallas_call(
        functools.partial(_attn_c_sample_kernel, tq=tq),
        grid=(n_seq, nq),
        in_specs=[pl.BlockSpec((tq, q.shape[1]), lambda b, i: (row0 // tq + b * nq + i, 0)),
                  pl.BlockSpec((t, kv.shape[1]), lambda b, i: (row0 // t + b, 0)),
                  pl.BlockSpec((t, p1.shape[1]), lambda b, i: (row0 // t + b, 0)),
                  pl.BlockSpec((past, kv.shape[1]), lambda b, i: (n // past + b, 0)),
                  pl.BlockSpec((None, past, C_ROPE), lambda b, i: (b, 0, 0)),
                  pl.BlockSpec((1, C_QK), lambda b, i: (0, 0)),
                  pl.BlockSpec((1, C_QK), lambda b, i: (0, 0)),
                  pl.BlockSpec((t, C_ROPE), lambda b, i: (0, 0)),
                  pl.BlockSpec((t, C_ROPE), lambda b, i: (0, 0))],
        out_specs=pl.BlockSpec((tq, C_HEADS * C_V), lambda b, i: (b * nq + i, 0)),
        out_shape=jax.ShapeDtypeStruct((n_seq * t, C_HEADS * C_V), BF16),
        compiler_params=_params(("arbitrary", "arbitrary"), VMEM_LIMIT),
        name="attn_c_latent",
    )(q, kv, p1, kv, kr_ctx, gq.reshape(1, -1), gk.reshape(1, -1), cos, sin)


ROUTE_TILE = 512
ROUTE_BLOCK = 512
SLOT_ALIGN = 8
LOCAL_SLOTS = 2304
META_LANES = 128
FF_CHUNK = 256


def _router_kernel(x_ref, g_ref, sc_ref, sh_ref, wr_ref, br_ref, h_ref, meta_ref, cntb_ref, cnt_ref,
                   run_ref):
    @pl.when(pl.program_id(0) == 0)
    def _():
        run_ref[...] = jnp.zeros_like(run_ref)

    h = _rms(x_ref[...], g_ref[...]) * (1 + sc_ref[...]) + sh_ref[...]
    h_ref[...] = h.astype(h_ref.dtype)
    tm = h.shape[0]
    logits = _dot3(h, wr_ref[...]) + br_ref[...]
    lane = lax.broadcasted_iota(jnp.int32, logits.shape, 1)
    work = logits
    picks, tops = [], []
    for _ in range(TOP_K):
        m = work.max(axis=-1, keepdims=True)
        first = jnp.min(jnp.where(work == m, lane, N_EXPERTS), axis=-1, keepdims=True)
        pick = lane == first
        picks.append(pick)
        tops.append(m)
        work = jnp.where(pick, -jnp.inf, work)
    sel = sum(p.astype(F32) for p in picks)
    earlier = (lax.broadcasted_iota(jnp.int32, (tm, tm), 0)
               > lax.broadcasted_iota(jnp.int32, (tm, tm), 1)).astype(BF16)
    inside = jnp.dot(earlier, sel.astype(BF16), preferred_element_type=F32)
    cnt = jnp.sum(sel, axis=0, keepdims=True)
    run = jnp.ceil(cnt / SLOT_ALIGN) * SLOT_ALIGN
    lower_e = (lax.broadcasted_iota(jnp.int32, (N_EXPERTS, N_EXPERTS), 0)
               < lax.broadcasted_iota(jnp.int32, (N_EXPERTS, N_EXPERTS), 1)).astype(BF16)
    start = jnp.dot(jnp.broadcast_to(run, (8, N_EXPERTS)).astype(BF16), lower_e,
                    preferred_element_type=F32)[0:1]
    slot = start + inside
    ws = [jnp.exp(t - tops[0]) for t in tops]
    den = sum(ws)
    mlane = lax.broadcasted_iota(jnp.int32, (tm, META_LANES), 1)
    meta = jnp.zeros((tm, META_LANES), F32)
    for k in range(TOP_K):
        meta = jnp.where(mlane == k, jnp.sum(jnp.where(picks[k], slot, 0.0), axis=-1, keepdims=True), meta)
        meta = jnp.where(mlane == TOP_K + k, ws[k] / den, meta)
    meta_ref[...] = meta
    cntb_ref[...] = cnt
    run_ref[...] += run
    cnt_ref[...] = run_ref[...]


def _router(x, gain, scale, shift, w_router, b_router, layer, n_p, t_s):
    n, d = x.shape
    tm = ROUTE_BLOCK
    e = w_router.shape[-1]
    group = _group_of_tile(tm, n_p, t_s)
    return pl.pallas_call(
        _router_kernel,
        grid=(n // tm,),
        in_specs=[pl.BlockSpec((tm, d), lambda i: (i, 0)),
                  pl.BlockSpec((1, d), lambda i: (0, 0)),
                  pl.BlockSpec((None, 1, d), lambda i: (group(i), 0, 0)),
                  pl.BlockSpec((None, 1, d), lambda i: (group(i), 0, 0)),
                  pl.BlockSpec((None, d, e), lambda i: (layer, 0, 0)),
                  pl.BlockSpec((None, 1, e), lambda i: (layer, 0, 0))],
        out_specs=[pl.BlockSpec((tm, d), lambda i: (i, 0)),
                   pl.BlockSpec((tm, META_LANES), lambda i: (i, 0)),
                   pl.BlockSpec((None, 1, e), lambda i: (i, 0, 0)),
                   pl.BlockSpec((1, e), lambda i: (0, 0))],
        out_shape=[jax.ShapeDtypeStruct((n, d), BF16),
                   jax.ShapeDtypeStruct((n, META_LANES), F32),
                   jax.ShapeDtypeStruct((n // tm, 1, e), F32),
                   jax.ShapeDtypeStruct((1, e), F32)],
        scratch_shapes=[pltpu.VMEM((1, e), F32)],
        compiler_params=_params(("arbitrary",)),
        name="router",
    )(x, gain.reshape(1, d), scale.reshape(N_GROUPS, 1, d), shift.reshape(N_GROUPS, 1, d),
      w_router, b_router.reshape(-1, 1, e))


def _route_plan(cnt_blk, cnt_tot, n_tiles, min_tiles):
    cb = cnt_blk[:, 0, :].astype(jnp.int32)
    run = (cb + SLOT_ALIGN - 1) // SLOT_ALIGN * SLOT_ALIGN
    counts = cnt_tot[0].astype(jnp.int32)
    padded = (counts + ROUTE_TILE - 1) // ROUTE_TILE * ROUTE_TILE
    ends = jnp.cumsum(padded)
    offs = ends - padded
    gstart = offs[None, :] + jnp.cumsum(run, axis=0) - run
    lstart = jnp.cumsum(run, axis=1) - run
    tile_start = jnp.arange(n_tiles, dtype=jnp.int32) * ROUTE_TILE
    n_valid = ends[-1] // ROUTE_TILE
    te = jnp.minimum(jnp.sum(ends[None, :] <= tile_start[:, None], axis=1), N_EXPERTS - 1)
    te = jnp.where(tile_start < ends[-1], te, te[jnp.maximum(n_valid - 1, 0)]).astype(jnp.int32)
    rows = jnp.clip((offs + counts)[te] - tile_start, 0, ROUTE_TILE).astype(jnp.int32)
    ragged = jnp.where(counts % ROUTE_TILE != 0, ends // ROUTE_TILE - 1, -1)
    tail = jnp.arange(min_tiles, n_tiles, dtype=jnp.int32)
    fill = jnp.concatenate([ragged, jnp.where(tail >= n_valid, tail, -1)]).astype(jnp.int32)
    runs = (gstart.reshape(-1).astype(jnp.int32), lstart.reshape(-1).astype(jnp.int32),
            (run // SLOT_ALIGN).reshape(-1).astype(jnp.int32),
            (jnp.sum(run, axis=1) // SLOT_ALIGN).astype(jnp.int32))
    return runs, te, rows, n_valid.reshape(1).astype(jnp.int32), fill


RUN_PIECE = 4
WAIT_PIECE = 8


def _start_runs(blk, gstart_ref, lstart_ref, nch_ref, copy):
    big = RUN_PIECE * SLOT_ALIGN

    def per_expert(e, carry):
        idx = blk * N_EXPERTS + e
        g0, l0, n = gstart_ref[idx], lstart_ref[idx], nch_ref[idx]
        n_big = n // RUN_PIECE

        def large(i, c2):
            off = pl.multiple_of(i * big, big)
            copy(pl.multiple_of(l0 + off, SLOT_ALIGN), pl.multiple_of(g0 + off, SLOT_ALIGN), big).start()
            return c2

        def small(i, c2):
            off = pl.multiple_of(n_big * big + i * SLOT_ALIGN, SLOT_ALIGN)
            copy(pl.multiple_of(l0 + off, SLOT_ALIGN), pl.multiple_of(g0 + off, SLOT_ALIGN), SLOT_ALIGN).start()
            return c2

        lax.fori_loop(0, n_big, large, 0)
        lax.fori_loop(0, n - n_big * RUN_PIECE, small, 0)
        return carry

    lax.fori_loop(0, N_EXPERTS, per_expert, 0)


def _wait_runs(n_chunks, copy):
    n_big = n_chunks // WAIT_PIECE

    def large(i, carry):
        copy(0, 0, WAIT_PIECE * SLOT_ALIGN).wait()
        return carry

    def small(i, carry):
        copy(0, 0, SLOT_ALIGN).wait()
        return carry

    lax.fori_loop(0, n_big, large, 0)
    lax.fori_loop(0, n_chunks - n_big * WAIT_PIECE, small, 0)


def _dispatch_kernel(fill_ref, gstart_ref, lstart_ref, nch_ref, ntot_ref, meta_ref, h_ref, xs_ref,
                     loc_ref, zero_ref, sem, zsem):
    blk = pl.program_id(0)

    @pl.when(blk == 0)
    def _():
        zero_ref[...] = jnp.zeros_like(zero_ref)

        def fill_copy(j):
            row0 = pl.multiple_of(fill_ref[j] * ROUTE_TILE, ROUTE_TILE)
            return pltpu.make_async_copy(zero_ref, xs_ref.at[pl.ds(row0, ROUTE_TILE), :], zsem)

        def start(j, carry):
            @pl.when(fill_ref[j] >= 0)
            def _():
                fill_copy(j).start()
            return carry

        def wait(j, carry):
            @pl.when(fill_ref[j] >= 0)
            def _():
                fill_copy(j).wait()
            return carry

        lax.fori_loop(0, fill_ref.shape[0], start, 0)
        lax.fori_loop(0, fill_ref.shape[0], wait, 0)

    buf = blk % 2
    last = pl.num_programs(0) - 1

    def copier(b):
        def copy(l0, g0, rows):
            return pltpu.make_async_copy(loc_ref.at[b, pl.ds(l0, rows), :], xs_ref.at[pl.ds(g0, rows), :],
                                         sem.at[b])
        return copy

    @pl.when(blk >= 2)
    def _():
        _wait_runs(ntot_ref[blk - 2], copier(buf))

    hb = h_ref[...]
    tm = hb.shape[0]
    slots_t = meta_ref[...].T[0:TOP_K]
    rows = 256
    for c0 in range(0, LOCAL_SLOTS, rows):
        slot = (c0 + lax.broadcasted_iota(jnp.int32, (rows, tm), 0)).astype(F32)
        onehot = sum(jnp.where(slot == slots_t[k:k + 1], 1.0, 0.0) for k in range(TOP_K))
        loc_ref[buf, c0:c0 + rows, :] = jnp.dot(onehot.astype(BF16), hb,
                                                preferred_element_type=F32).astype(loc_ref.dtype)

    _start_runs(blk, gstart_ref, lstart_ref, nch_ref, copier(buf))

    @pl.when(blk == last)
    def _():
        @pl.when(blk >= 1)
        def _():
            _wait_runs(ntot_ref[blk - 1], copier(1 - buf))
        _wait_runs(ntot_ref[blk], copier(buf))


def _dispatch(h, meta, runs, fill, n_slots):
    n, d = h.shape
    tm = ROUTE_BLOCK
    grid_spec = pltpu.PrefetchScalarGridSpec(
        num_scalar_prefetch=5,
        grid=(n // tm,),
        in_specs=[pl.BlockSpec((tm, META_LANES), lambda i, *_: (i, 0)),
                  pl.BlockSpec((tm, d), lambda i, *_: (i, 0))],
        out_specs=pl.BlockSpec(memory_space=pl.ANY),
        scratch_shapes=[pltpu.VMEM((2, LOCAL_SLOTS, d), F32), pltpu.VMEM((ROUTE_TILE, d), F32),
                        pltpu.SemaphoreType.DMA((2,)), pltpu.SemaphoreType.DMA(())])
    return pl.pallas_call(
        _dispatch_kernel,
        grid_spec=grid_spec,
        out_shape=jax.ShapeDtypeStruct((n_slots, d), F32),
        compiler_params=_params(("arbitrary",), VMEM_LIMIT),
        name="moe_dispatch",
    )(fill, *runs, meta, h)


def _experts_kernel(te_ref, rows_ref, nv_ref, x_ref, wgu_ref, bgu_ref, wd_ref, bd_ref, y_ref,
                    wgub_ref, wdb_ref, *, d_ff):
    i = pl.program_id(0)
    valid = i < nv_ref[0]
    fresh = jnp.logical_or(i == 0, te_ref[i] != te_ref[jnp.maximum(i - 1, 0)])

    @pl.when(jnp.logical_and(valid, fresh))
    def _():
        wgub_ref[...] = wgu_ref[...].astype(BF16)
        wdb_ref[...] = wd_ref[...].astype(BF16)

    @pl.when(valid)
    def _():
        row = lax.broadcasted_iota(jnp.int32, x_ref.shape, 0)
        x = jnp.where(row < rows_ref[i], x_ref[...], 0.0).astype(BF16)
        acc = None
        for c0 in range(0, d_ff, FF_CHUNK):
            g_cols, u_cols = slice(c0, c0 + FF_CHUNK), slice(d_ff + c0, d_ff + c0 + FF_CHUNK)
            gate = jnp.dot(x, wgub_ref[:, g_cols], preferred_element_type=F32) + bgu_ref[:, g_cols]
            up = jnp.dot(x, wgub_ref[:, u_cols], preferred_element_type=F32) + bgu_ref[:, u_cols]
            gate = jnp.minimum(gate, SWIGLU_LIMIT)
            up = jnp.clip(up, -SWIGLU_LIMIT, SWIGLU_LIMIT)
            act = (up + 1) * gate * jax.nn.sigmoid(SWIGLU_ALPHA * gate)
            part = jnp.dot(act.astype(BF16), wdb_ref[c0:c0 + FF_CHUNK, :], preferred_element_type=F32)
            acc = part if acc is None else acc + part
        y_ref[...] = acc + bd_ref[...]

    @pl.when(jnp.logical_not(valid))
    def _():
        y_ref[...] = jnp.zeros_like(y_ref)


def _experts(xs, te, rows, n_valid, w_gu, b_gu, w_down, b_down, layer):
    n_slots, d = xs.shape
    _, e, _, two_ff = w_gu.shape
    last = lambda i, nv: jnp.minimum(i, nv[0] - 1)
    grid_spec = pltpu.PrefetchScalarGridSpec(
        num_scalar_prefetch=3,
        grid=(n_slots // ROUTE_TILE,),
        in_specs=[pl.BlockSpec((ROUTE_TILE, d), lambda i, te, rw, nv: (last(i, nv), 0)),
                  pl.BlockSpec((None, None, d, two_ff), lambda i, te, rw, nv: (layer, te[i], 0, 0)),
                  pl.BlockSpec((None, None, 1, two_ff), lambda i, te, rw, nv: (layer, te[i], 0, 0)),
                  pl.BlockSpec((None, None, two_ff // 2, d), lambda i, te, rw, nv: (layer, te[i], 0, 0)),
                  pl.BlockSpec((None, None, 1, d), lambda i, te, rw, nv: (layer, te[i], 0, 0))],
        out_specs=pl.BlockSpec((ROUTE_TILE, d), lambda i, te, rw, nv: (i, 0)),
        scratch_shapes=[pltpu.VMEM((d, two_ff), BF16), pltpu.VMEM((two_ff // 2, d), BF16)])
    return pl.pallas_call(
        functools.partial(_experts_kernel, d_ff=two_ff // 2),
        grid_spec=grid_spec,
        out_shape=jax.ShapeDtypeStruct((n_slots, d), F32),
        compiler_params=_params(("arbitrary",), VMEM_LIMIT),
        name="moe_experts",
    )(te, rows, n_valid, xs, w_gu, b_gu.reshape(b_gu.shape[0], e, 1, two_ff), w_down,
      b_down.reshape(b_down.shape[0], e, 1, d))


def _combine_kernel(gstart_ref, lstart_ref, nch_ref, ntot_ref, meta_ref, x_ref, gate_ref, y_ref, *refs,
                    split_tiles):
    *o_refs, loc_ref, sem = refs
    blk = pl.program_id(0)

    buf = blk % 2

    def copier(b):
        def copy(l0, g0, rows):
            return pltpu.make_async_copy(y_ref.at[pl.ds(g0, rows), :], loc_ref.at[b, pl.ds(l0, rows), :],
                                         sem.at[b])
        return copy

    @pl.when(blk == 0)
    def _():
        loc_ref[...] = jnp.zeros_like(loc_ref)
        _start_runs(blk, gstart_ref, lstart_ref, nch_ref, copier(buf))

    @pl.when(blk + 1 < pl.num_programs(0))
    def _():
        _start_runs(blk + 1, gstart_ref, lstart_ref, nch_ref, copier(1 - buf))

    _wait_runs(ntot_ref[blk], copier(buf))

    meta = meta_ref[...]
    tm = meta.shape[0]
    cols = 256
    acc = jnp.zeros(x_ref.shape, F32)
    for c0 in range(0, LOCAL_SLOTS, cols):
        slot = (c0 + lax.broadcasted_iota(jnp.int32, (tm, cols), 1)).astype(F32)
        wts = sum(jnp.where(slot == meta[:, k:k + 1], meta[:, TOP_K + k:TOP_K + k + 1], 0.0)
                  for k in range(TOP_K))
        acc = acc + jnp.dot(wts.astype(BF16), loc_ref[buf, c0:c0 + cols, :].astype(BF16),
                            preferred_element_type=F32)
    out = x_ref[...] + gate_ref[...] * acc
    if split_tiles is None:
        o_refs[0][...] = out
    else:
        @pl.when(blk < split_tiles)
        def _():
            o_refs[0][...] = out

        @pl.when(blk >= split_tiles)
        def _():
            o_refs[1][...] = out


def _combine(y, meta, runs, x, gate, n_p, t_s, split=False):
    n, d = x.shape
    tm = ROUTE_BLOCK
    group = _group_of_tile(tm, n_p, t_s)
    st = n_p // tm
    if split:
        out_specs = [pl.BlockSpec((tm, d), lambda i, *_: (jnp.minimum(i, st - 1), 0)),
                     pl.BlockSpec((tm, d), lambda i, *_: (jnp.maximum(i - st, 0), 0))]
        out_shape = [jax.ShapeDtypeStruct((n_p, d), F32), jax.ShapeDtypeStruct((n - n_p, d), F32)]
    else:
        out_specs = pl.BlockSpec((tm, d), lambda i, *_: (i, 0))
        out_shape = jax.ShapeDtypeStruct((n, d), F32)
    grid_spec = pltpu.PrefetchScalarGridSpec(
        num_scalar_prefetch=4,
        grid=(n // tm,),
        in_specs=[pl.BlockSpec((tm, META_LANES), lambda i, *_: (i, 0)),
                  pl.BlockSpec((tm, d), lambda i, *_: (i, 0)),
                  pl.BlockSpec((None, 1, d), lambda i, *_: (group(i), 0, 0)),
                  pl.BlockSpec(memory_space=pl.ANY)],
        out_specs=out_specs,
        scratch_shapes=[pltpu.VMEM((2, LOCAL_SLOTS, d), F32), pltpu.SemaphoreType.DMA((2,))])
    return pl.pallas_call(
        functools.partial(_combine_kernel, split_tiles=st if split else None),
        grid_spec=grid_spec,
        out_shape=out_shape,
        compiler_params=_params(("arbitrary",), VMEM_LIMIT),
        name="moe_combine",
    )(*runs, meta, x, gate.reshape(N_GROUPS, 1, d), y)


def _moe(x, gain, scale, shift, gate, w_router, b_router, w_gu, b_gu, w_down, b_down, layer, n_p, t_s,
         split=False):
    n = x.shape[0]
    assert LOCAL_SLOTS >= ROUTE_BLOCK * TOP_K + N_EXPERTS * (SLOT_ALIGN - 1) and n % ROUTE_BLOCK == 0
    min_tiles = n * TOP_K // ROUTE_TILE
    max_slots = n * TOP_K + (n // ROUTE_BLOCK) * N_EXPERTS * (SLOT_ALIGN - 1) + N_EXPERTS * (ROUTE_TILE - 1)
    n_tiles = -(-max_slots // ROUTE_TILE)
    h, meta, cnt_blk, cnt_tot = _router(x, gain, scale, shift, w_router, b_router, layer, n_p, t_s)
    runs, te, rows, n_valid, fill = _route_plan(cnt_blk, cnt_tot, n_tiles, min_tiles)
    xs = _dispatch(h, meta, runs, fill, n_tiles * ROUTE_TILE)
    y = _experts(xs, te, rows, n_valid, w_gu, b_gu, w_down, b_down, layer)
    return _combine(y, meta, runs, x, gate, n_p, t_s, split)


def kernel(x_prompt, x_sample, c, cache_a_k, cache_a_v, state_b_fwd, state_b_bwd, cache_c_ckv,
           cache_c_krope, c_ctx, w_mod, b_mod, norm_mix, norm_ffn, e_w_in, e_w_out, e_a_qnorm,
           e_a_knorm, e_a_sink, e_b_conv, e_b_alog, e_b_dtbias, e_b_onorm, o_w_in, o_q_lora_norm,
           o_kv_lora_norm, o_w_uq, o_w_ukv, o_qnorm, o_knorm, o_w_out, moe_w_router, moe_b_router,
           moe_w_gu, moe_b_gu, moe_w_down, moe_b_down):
    bp, tp, d = x_prompt.shape
    bs, ts, _ = x_sample.shape
    depth = w_mod.shape[0]
    n_p, n_s = bp * tp, bs * ts
    n = n_p + n_s
    assert bs + 1 <= N_GROUPS and ts % 1024 == 0 and n_p % ts == 0

    x = (x_prompt.reshape(n_p, d), x_sample.reshape(n_s, d))
    cond = jnp.concatenate([c_ctx[None], c, jnp.zeros((N_GROUPS - 1 - bs, d), F32)], axis=0)
    mod = _adaln(cond, w_mod, b_mod)

    new_a_k, new_a_v, new_b_fwd, new_b_bwd, new_c_ckv, new_c_krope = [], [], [], [], [], []
    for layer in range(depth):
        sh1, sc1, g1, sh2, sc2, g2 = (mod[layer, j] for j in range(6))
        h = _modulate(x, norm_mix[layer], sc1, sh1, n_p, ts)
        i = layer // 2
        if layer % 2 == 0:
            main_w = (A_HEADS + 2 * A_KV_HEADS) * A_HEAD_DIM + 4 * B_HEADS * B_DK
            proj = _matmul((h,), e_w_in, i, 0, main_w // 2, main_w, out_dtype=BF16, name="even_in_proj")
            tail = _matmul((h,), e_w_in[i][None, :, main_w:], 0, 0, 4 * B_HEADS, 4 * B_HEADS,
                           name="even_gate_proj")
            oa_p, kn_p = _attn_a_prompt(proj, bp, tp, e_a_sink[i], e_a_qnorm[i], e_a_knorm[i])
            oa_s = _attn_a_sample(proj, n_p, bs, ts, cache_a_k[:, i], cache_a_v[:, i],
                                  e_a_sink[i], e_a_qnorm[i], e_a_knorm[i])
            zeros = jnp.zeros((bp, B_HEADS, B_DK, B_DV), F32)
            ob_p, s_f, s_b = _deltanet(proj, tail, 0, bp, tp, e_b_conv[i], e_b_alog[i],
                                       e_b_dtbias[i], e_b_onorm[i], zeros, zeros)
            ob_s, _, _ = _deltanet(proj, tail, n_p, bs, ts, e_b_conv[i], e_b_alog[i],
                                   e_b_dtbias[i], e_b_onorm[i], state_b_fwd[:, i], state_b_bwd[:, i])
            x = (_matmul_residual([(oa_p, oa_s), (ob_p, ob_s)], e_w_out, i, x, g1, n_p, ts),)
            kw = A_KV_HEADS * A_HEAD_DIM
            new_a_k.append(kn_p.reshape(bp, tp, A_KV_HEADS, A_HEAD_DIM).transpose(0, 2, 1, 3))
            v_p = proj[:n_p, A_HEADS * A_HEAD_DIM + kw:A_HEADS * A_HEAD_DIM + 2 * kw].astype(F32)
            new_a_v.append(v_p.reshape(bp, tp, A_KV_HEADS, A_HEAD_DIM).transpose(0, 2, 1, 3))
            new_b_fwd.append(s_f)
            new_b_bwd.append(s_b)
        else:
            p1 = _matmul((h,), o_w_in, i, 0, o_w_in.shape[-1], o_w_in.shape[-1], name="odd_in_proj")
            cq, ckv, ckv_b = _lora_norm(p1, o_q_lora_norm[i], o_kv_lora_norm[i])
            q = _matmul((cq,), o_w_uq, i, 0, o_w_uq.shape[-1] // 2, o_w_uq.shape[-1], out_dtype=BF16,
                        name="odd_uq")
            ckv_ctx = cache_c_ckv[:, i].reshape(-1, C_KV_LORA).astype(BF16)
            kv = _matmul((ckv_b, ckv_ctx), o_w_ukv, i, 0, o_w_ukv.shape[-1] // 2, o_w_ukv.shape[-1],
                         out_dtype=BF16, name="odd_ukv")
            o_p = _attn_c_prompt(q, kv, p1, bp, tp, o_qnorm[i], o_knorm[i])
            o_s = _attn_c_sample(q, kv, p1, n_p, bs, ts, cache_c_krope[:, i], o_qnorm[i], o_knorm[i])
            x = (_matmul_residual([(o_p, o_s)], o_w_out, i, x, g1, n_p, ts),)
            new_c_ckv.append(ckv[:n_p].reshape(bp, tp, C_KV_LORA))
            new_c_krope.append(p1[:n_p, C_Q_LORA + C_KV_LORA:].reshape(bp, tp, C_ROPE))
        x = _moe(x[0], norm_ffn[layer], sc2, sh2, g2, moe_w_router, moe_b_router, moe_w_gu, moe_b_gu,
                 moe_w_down, moe_b_down, layer, n_p, ts, split=layer == depth - 1)
        x = tuple(x) if layer == depth - 1 else (x,)

    return (x[0].reshape(bp, tp, d), x[1].reshape(bs, ts, d),
            jnp.stack(new_a_k, axis=1), jnp.stack(new_a_v, axis=1),
            jnp.stack(new_b_fwd, axis=1), jnp.stack(new_b_bwd, axis=1),
            jnp.stack(new_c_ckv, axis=1), jnp.stack(new_c_krope, axis=1))
```
